```python
import math
import jax
import jax.numpy as jnp
from jax import lax
import numpy as np

D_MODEL = 1024
BATCH = 4
SEQ = 4096
DEPTH = 4

F32 = jnp.float32
N_BRANCH = 4
ROPE_THETA = 500000.0
NORM_EPS = 1e-6
Q_BLOCK = 128
NEG_INF = -1e30

GDN_HEADS = 4
GDN_DK = 64
GDN_DV = 64
GDN_CONV = 4
GDN_CHUNK = 64
MLA_HEADS = 4
MLA_Q_LORA = 256
MLA_KV_LORA = 128
MLA_NOPE = 64
MLA_ROPE = 32
MLA_DV = 64
RWKV_HEADS = 4
RWKV_DH = 64
RWKV_DECAY_LORA = 64
RWKV_AAA_LORA = 64
RWKV_GATE_LORA = 128
RWKV_GN_EPS = 64e-5
DSA_HEADS = 4
DSA_DH = 64
IDX_HEADS = 8
IDX_DH = 64
IDX_TOPK_MAX = 256
D_FF = 4 * D_MODEL

GDN_QKV = 2 * GDN_HEADS * GDN_DK + GDN_HEADS * GDN_DV
GDN_SPLITS = (GDN_QKV, GDN_HEADS, GDN_HEADS, GDN_HEADS * GDN_DV)
MLA_SPLITS = (MLA_Q_LORA, MLA_KV_LORA, MLA_ROPE)
RWKV_W = RWKV_HEADS * RWKV_DH
RWKV_SPLITS = (RWKV_W, RWKV_W, RWKV_W, RWKV_DECAY_LORA, RWKV_AAA_LORA, RWKV_GATE_LORA)
DSA_W = DSA_HEADS * DSA_DH
DSA_SPLITS = (IDX_HEADS * IDX_DH, IDX_DH, IDX_HEADS, DSA_W, DSA_W, DSA_W)
GATE_SPLITS = (D_MODEL,) * N_BRANCH
SLAB_SPLITS = (sum(GDN_SPLITS), sum(MLA_SPLITS), sum(RWKV_SPLITS), sum(DSA_SPLITS), sum(GATE_SPLITS))
N_IN = sum(SLAB_SPLITS)
GDN_W = GDN_HEADS * GDN_DV
MLA_W = MLA_HEADS * MLA_DV
RWKV_IN = sum(RWKV_SPLITS)

kernel_name = "hybrid_gdn_mla_rwkv7_dsa_trunk"


def _split(t, sizes):
    return jnp.split(t, np.cumsum(sizes)[:-1].tolist(), axis=-1)


def rms_norm(x, g, eps=NORM_EPS):
    x32 = x.astype(F32)
    y = x32 * lax.rsqrt(jnp.mean(x32 * x32, axis=-1, keepdims=True) + eps)
    return (y * g.astype(F32)).astype(x.dtype)


def l2_normalize(x, eps=1e-6):
    x32 = x.astype(F32)
    return x32 * lax.rsqrt(jnp.sum(x32 * x32, axis=-1, keepdims=True) + eps)


def rotary_tables(positions, rot_dims):
    inv_freq = ROPE_THETA ** (-jnp.arange(0, rot_dims, 2, dtype=F32) / rot_dims)
    ang = positions.astype(F32)[..., None] * inv_freq
    return jnp.cos(ang), jnp.sin(ang)


def apply_rotary(x, cos, sin):
    half = cos.shape[-1]
    if x.ndim == 4:
        cos, sin = cos[:, :, None, :], sin[:, :, None, :]
    x1 = x[..., :half].astype(F32)
    x2 = x[..., half:2 * half].astype(F32)
    rot = jnp.concatenate([x1 * cos - x2 * sin, x2 * cos + x1 * sin], axis=-1).astype(x.dtype)
    return jnp.concatenate([rot, x[..., 2 * half:]], axis=-1)


def causal_depthwise_conv(x, w):
    k, c = w.shape
    return lax.conv_general_dilated(x, w[:, None, :].astype(x.dtype), window_strides=(1,),
                                    padding=[(k - 1, 0)], dimension_numbers=('NWC', 'WIO', 'NWC'),
                                    feature_group_count=c)


def token_shift(x):
    return jnp.pad(x, ((0, 0), (1, 0), (0, 0)))[:, :-1]


def gated_delta_rule_chunked(q, k, v, g, beta):
    b, t, h, dk = q.shape
    dv = v.shape[-1]
    c = GDN_CHUNK
    n = t // c

    def to_chunks(a):
        a = a.reshape((b, n, c, h) + a.shape[3:])
        return jnp.moveaxis(a, (1, 3), (0, 2))

    q = to_chunks(q * dk ** -0.5)
    k = to_chunks(k)
    v = to_chunks(v)
    g = to_chunks(g)
    beta = to_chunks(beta)
    gc = jnp.cumsum(g, axis=-1)
    incl = jnp.tril(jnp.ones((c, c), bool))
    strict = jnp.tril(jnp.ones((c, c), bool), -1)
    decay = jnp.exp(jnp.where(incl, gc[..., :, None] - gc[..., None, :], -jnp.inf))
    k_beta = k * beta[..., None]
    v_beta = v * beta[..., None]
    m = jnp.where(strict, jnp.einsum('nbhcd,nbhsd->nbhcs', k_beta, k) * decay, 0.0)
    eye = jnp.eye(c, dtype=m.dtype)
    t_inv = lax.linalg.triangular_solve(eye + m, jnp.broadcast_to(eye, m.shape),
                                        left_side=True, lower=True)
    u = t_inv @ v_beta
    w = t_inv @ (k_beta * jnp.exp(gc)[..., None])
    a_intra = jnp.einsum('nbhcd,nbhsd->nbhcs', q, k) * decay
    q_dec = q * jnp.exp(gc)[..., None]
    k_dec = k * jnp.exp(gc[..., -1:] - gc)[..., None]
    g_last = jnp.exp(gc[..., -1])

    def step(state, xs):
        q_d, k_d, u_c, w_c, a_c, gl = xs
        v_new = u_c - w_c @ state
        o = q_d @ state + a_c @ v_new
        state = state * gl[..., None, None] + jnp.swapaxes(k_d, -1, -2) @ v_new
        return state, o

    s0 = jnp.zeros((b, h, dk, dv), F32)
    _, o = lax.scan(step, s0, (q_dec, k_dec, u, w, a_intra, g_last))
    return jnp.moveaxis(o, (0, 2), (1, 3)).reshape(b, t, h, dv)


def gdn_mixer(p, conv_w, a_log, dt_bias, norm_g):
    b, s, _ = p.shape
    qkv, a, beta_logit, z = _split(p, GDN_SPLITS)
    qkv = jax.nn.silu(causal_depthwise_conv(qkv, conv_w)).astype(F32)
    q, k, v = _split(qkv, (GDN_HEADS * GDN_DK, GDN_HEADS * GDN_DK, GDN_HEADS * GDN_DV))
    q = l2_normalize(q.reshape(b, s, GDN_HEADS, GDN_DK))
    k = l2_normalize(k.reshape(b, s, GDN_HEADS, GDN_DK))
    v = v.reshape(b, s, GDN_HEADS, GDN_DV)
    beta = jax.nn.sigmoid(beta_logit.astype(F32))
    g = -jnp.exp(a_log.astype(F32)) * jax.nn.softplus(a.astype(F32) + dt_bias.astype(F32))
    o = gated_delta_rule_chunked(q, k, v, g, beta)
    o = rms_norm(o, norm_g) * jax.nn.silu(z.astype(F32).reshape(b, s, GDN_HEADS, GDN_DV))
    return o.reshape(b, s, GDN_W).astype(p.dtype)


def causal_block_attention(q, k, v, scale):
    b, s, h, _ = q.shape
    key_pos = jnp.arange(s)

    def block(i):
        start = i * Q_BLOCK
        qb = lax.dynamic_slice_in_dim(q, start, Q_BLOCK, axis=1)
        sc = jnp.einsum('bqhd,bkhd->bhqk', qb, k, preferred_element_type=F32) * scale
        q_pos = start + jnp.arange(Q_BLOCK)
        sc = jnp.where(key_pos[None, :] <= q_pos[:, None], sc, NEG_INF)
        pr = jax.nn.softmax(sc, axis=-1)
        return jnp.einsum('bhqk,bkhd->bqhd', pr.astype(v.dtype), v)

    o = lax.map(block, jnp.arange(s // Q_BLOCK))
    return jnp.moveaxis(o, 0, 1).reshape(b, s, h, v.shape[-1])


def mla_mixer(p, rope, q_norm_g, kv_norm_g, w_uq, w_ukv):
    b, s, _ = p.shape
    c_q, c_kv, k_rope = _split(p, MLA_SPLITS)
    q = (rms_norm(c_q, q_norm_g) @ w_uq).reshape(b, s, MLA_HEADS, MLA_NOPE + MLA_ROPE)
    q = jnp.concatenate([q[..., :MLA_NOPE], apply_rotary(q[..., MLA_NOPE:], *rope)], axis=-1)
    kv = (rms_norm(c_kv, kv_norm_g) @ w_ukv).reshape(b, s, MLA_HEADS, MLA_NOPE + MLA_DV)
    k_nope, v = kv[..., :MLA_NOPE], kv[..., MLA_NOPE:]
    k_rope = apply_rotary(k_rope, *rope)
    k = jnp.concatenate([k_nope, jnp.broadcast_to(k_rope[:, :, None, :], (b, s, MLA_HEADS, MLA_ROPE))], axis=-1)
    o = causal_block_attention(q, k, v, (MLA_NOPE + MLA_ROPE) ** -0.5)
    return o.reshape(b, s, MLA_W)


def rwkv7_scan(r, w, k, v, kk, a):
    b, t, h, n = r.shape

    def step(state, xs):
        r_t, w_t, k_t, v_t, kk_t, a_t = xs
        sa = jnp.einsum('bhvk,bhk->bhv', state, -kk_t)
        state = (state * w_t[:, :, None, :] + sa[..., None] * (kk_t * a_t)[:, :, None, :]
                 + v_t[..., None] * k_t[:, :, None, :])
        return state, jnp.einsum('bhvk,bhk->bhv', state, r_t)

    xs = tuple(jnp.moveaxis(z, 1, 0) for z in (r, w, k, v, kk, a))
    _, o = lax.scan(step, jnp.zeros((b, h, n, n), F32), xs)
    return jnp.moveaxis(o, 0, 1)


def rwkv7_mixer(p, mu, w0, w_w2, a0, w_a2, w_g2, k_k, k_a, r_k, ln_w, ln_b):
    b, s, _ = p.shape
    p = p + (token_shift(p) - p) * mu
    r, k, v, xw, xa, xg = _split(p.astype(F32), RWKV_SPLITS)
    w_log = -jax.nn.softplus(-(w0.astype(F32) + jnp.tanh(xw) @ w_w2.astype(F32))) - 0.5
    decay = jnp.exp(-jnp.exp(w_log))
    a = jax.nn.sigmoid(a0.astype(F32) + xa @ w_a2.astype(F32))
    g = jax.nn.sigmoid(xg) @ w_g2.astype(F32)
    heads = lambda z: z.reshape(b, s, RWKV_HEADS, RWKV_DH)
    kk = l2_normalize(heads(k * k_k.astype(F32)))
    k = k * (1.0 + (a - 1.0) * k_a.astype(F32))
    r, k, v, a, decay = heads(r), heads(k), heads(v), heads(a), heads(decay)
    o = rwkv7_scan(r, decay, k, v, kk, a)
    mean = jnp.mean(o, axis=-1, keepdims=True)
    var = jnp.mean(jnp.square(o - mean), axis=-1, keepdims=True)
    o = ((o - mean) * lax.rsqrt(var + RWKV_GN_EPS)).reshape(b, s, RWKV_W) * ln_w.astype(F32) + ln_b.astype(F32)
    bonus = jnp.sum(r * k * r_k.astype(F32), axis=-1, keepdims=True) * v
    o = (o + bonus.reshape(b, s, RWKV_W)) * g
    return o.astype(p.dtype)


def dsa_mixer(p, rope_idx, rope_att):
    b, s, _ = p.shape
    iq, ik, iw, q, k, v = _split(p, DSA_SPLITS)
    iq = apply_rotary(iq.reshape(b, s, IDX_HEADS, IDX_DH), *rope_idx)
    ik = apply_rotary(ik, *rope_idx)
    iw = iw.astype(F32) * (IDX_HEADS ** -0.5 * IDX_DH ** -0.5)
    q = apply_rotary(q.reshape(b, s, DSA_HEADS, DSA_DH), *rope_att)
    k = apply_rotary(k.reshape(b, s, DSA_HEADS, DSA_DH), *rope_att)
    v = v.reshape(b, s, DSA_HEADS, DSA_DH)
    n_sel = min(IDX_TOPK_MAX, s // 4)
    key_pos = jnp.arange(s)
    gather = jax.vmap(lambda tb, ib: tb[ib])

    def block(i):
        start = i * Q_BLOCK
        q_pos = start + jnp.arange(Q_BLOCK)
        iqb = lax.dynamic_slice_in_dim(iq, start, Q_BLOCK, axis=1)
        iwb = lax.dynamic_slice_in_dim(iw, start, Q_BLOCK, axis=1)
        qb = lax.dynamic_slice_in_dim(q, start, Q_BLOCK, axis=1)
        logits = jnp.einsum('bqhd,bsd->bqhs', iqb, ik, preferred_element_type=F32)
        score = jnp.einsum('bqhs,bqh->bqs', jax.nn.relu(logits), iwb)
        score = jnp.where(key_pos[None, None, :] <= q_pos[None, :, None], score, -jnp.inf)
        _, sel = lax.top_k(score, n_sel)
        valid = sel <= q_pos[None, :, None]
        k_sel = gather(k, sel)
        v_sel = gather(v, sel)
        sc = jnp.einsum('bqhd,bqkhd->bqhk', qb, k_sel, preferred_element_type=F32) * DSA_DH ** -0.5
        sc = jnp.where(valid[:, :, None, :], sc, NEG_INF)
        pr = jax.nn.softmax(sc, axis=-1)
        return jnp.einsum('bqhk,bqkhd->bqhd', pr.astype(v.dtype), v_sel)

    o = lax.map(block, jnp.arange(s // Q_BLOCK))
    return jnp.moveaxis(o, 0, 1).reshape(b, s, DSA_W)


def setup_inputs(seed: int = 0) -> dict:
    key = jax.random.key(seed)
    ks = iter(jax.random.split(key, 48))
    nrm = lambda shape, scale: scale * jax.random.normal(next(ks), shape, F32)
    gain = lambda shape: 1.0 + 0.02 * jax.random.normal(next(ks), shape, F32)
    unif = lambda shape, lo, hi: jax.random.uniform(next(ks), shape, F32, lo, hi)
    L = DEPTH
    x = jax.random.normal(next(ks), (BATCH, SEQ, D_MODEL), F32)
    positions = (jax.random.randint(next(ks), (BATCH, 1), 0, 1024, dtype=jnp.int32)
                 + jnp.arange(SEQ, dtype=jnp.int32)[None, :])
    dt = jnp.exp(unif((L, GDN_HEADS), math.log(1e-3), math.log(1e-1)))
    return {
        "x": x,
        "positions": positions,
        "mix_norm_g": gain((L, D_MODEL)),
        "w_in": nrm((L, D_MODEL, N_IN), D_MODEL ** -0.5),
        "gdn_conv_w": nrm((L, GDN_CONV, GDN_QKV), GDN_CONV ** -0.5),
        "gdn_a_log": jnp.log(unif((L, GDN_HEADS), 1.0, 16.0)),
        "gdn_dt_bias": dt + jnp.log(-jnp.expm1(-dt)),
        "gdn_norm_g": gain((L, GDN_DV)),
        "mla_q_norm_g": gain((L, MLA_Q_LORA)),
        "mla_kv_norm_g": gain((L, MLA_KV_LORA)),
        "mla_w_uq": nrm((L, MLA_Q_LORA, MLA_HEADS * (MLA_NOPE + MLA_ROPE)), MLA_Q_LORA ** -0.5),
        "mla_w_ukv": nrm((L, MLA_KV_LORA, MLA_HEADS * (MLA_NOPE + MLA_DV)), MLA_KV_LORA ** -0.5),
        "rwkv_mu": unif((L, RWKV_IN), 0.0, 1.0),
        "rwkv_w0": unif((L, RWKV_W), -6.0, -1.0),
        "rwkv_w_w2": nrm((L, RWKV_DECAY_LORA, RWKV_W), 0.5 * RWKV_DECAY_LORA ** -0.5),
        "rwkv_a0": nrm((L, RWKV_W), 0.1),
        "rwkv_w_a2": nrm((L, RWKV_AAA_LORA, RWKV_W), 0.5 * RWKV_AAA_LORA ** -0.5),
        "rwkv_w_g2": nrm((L, RWKV_GATE_LORA, RWKV_W), RWKV_GATE_LORA ** -0.5),
        "rwkv_k_k": 0.85 + nrm((L, RWKV_W), 0.02),
        "rwkv_k_a": 1.0 + nrm((L, RWKV_W), 0.02),
        "rwkv_r_k": nrm((L, RWKV_HEADS, RWKV_DH), 0.1),
        "rwkv_ln_w": gain((L, RWKV_W)),
        "rwkv_ln_b": nrm((L, RWKV_W), 0.02),
        "w_out_gdn": nrm((L, GDN_W, D_MODEL), GDN_W ** -0.5),
        "w_out_mla": nrm((L, MLA_W, D_MODEL), MLA_W ** -0.5),
        "w_out_rwkv": nrm((L, RWKV_W, D_MODEL), RWKV_W ** -0.5),
        "w_out_dsa": nrm((L, DSA_W, D_MODEL), DSA_W ** -0.5),
        "w_o": nrm((L, D_MODEL, D_MODEL), D_MODEL ** -0.5),
        "mlp_norm_g": gain((L, D_MODEL)),
        "w_up": nrm((L, D_MODEL, D_FF), D_MODEL ** -0.5),
        "w_down": nrm((L, D_FF, D_MODEL), D_FF ** -0.5),
        "final_norm_g": gain((D_MODEL,)),
    }


def reference(x, positions, mix_norm_g, w_in, gdn_conv_w, gdn_a_log, gdn_dt_bias, gdn_norm_g,
              mla_q_norm_g, mla_kv_norm_g, mla_w_uq, mla_w_ukv,
              rwkv_mu, rwkv_w0, rwkv_w_w2, rwkv_a0, rwkv_w_a2, rwkv_w_g2, rwkv_k_k, rwkv_k_a,
              rwkv_r_k, rwkv_ln_w, rwkv_ln_b,
              w_out_gdn, w_out_mla, w_out_rwkv, w_out_dsa, w_o,
              mlp_norm_g, w_up, w_down, final_norm_g):
    rope_mla = rotary_tables(positions, MLA_ROPE)
    rope_idx = rotary_tables(positions, IDX_DH // 4)
    rope_att = rotary_tables(positions, DSA_DH // 4)
    for l in range(DEPTH):
        h = rms_norm(x, mix_norm_g[l])
        p_gdn, p_mla, p_rwkv, p_dsa, p_gate = _split(h @ w_in[l], SLAB_SPLITS)
        y_a = gdn_mixer(p_gdn, gdn_conv_w[l], gdn_a_log[l], gdn_dt_bias[l], gdn_norm_g[l]) @ w_out_gdn[l]
        y_b = mla_mixer(p_mla, rope_mla, mla_q_norm_g[l], mla_kv_norm_g[l], mla_w_uq[l], mla_w_ukv[l]) @ w_out_mla[l]
        y_c = rwkv7_mixer(p_rwkv, rwkv_mu[l], rwkv_w0[l], rwkv_w_w2[l], rwkv_a0[l], rwkv_w_a2[l],
                          rwkv_w_g2[l], rwkv_k_k[l], rwkv_k_a[l], rwkv_r_k[l], rwkv_ln_w[l],
                          rwkv_ln_b[l]) @ w_out_rwkv[l]
        y_d = dsa_mixer(p_dsa, rope_idx, rope_att) @ w_out_dsa[l]
        g_a, g_b, g_c, g_d = _split(jax.nn.sigmoid(p_gate.astype(F32)).astype(x.dtype), GATE_SPLITS)
        merged = g_a * y_a + g_b * y_b + g_c * y_c + g_d * y_d
        x = x + merged @ w_o[l]
        hm = rms_norm(x, mlp_norm_g[l])
        x = x + jnp.square(jax.nn.relu(hm @ w_up[l])) @ w_down[l]
    return rms_norm(x, final_norm_g)
```

```python
import functools
import math

import numpy as np
import jax
import jax.numpy as jnp
from jax import lax
from jax.experimental import pallas as pl
from jax.experimental.pallas import tpu as pltpu

F32 = jnp.float32
BF16 = jnp.bfloat16
I32 = jnp.int32

D_MODEL = 1024
DEPTH = 4
ROPE_THETA = 500000.0
NORM_EPS = 1e-6
NEG_INF = -1e30
HEADS = 4
DH = 64
HW = HEADS * DH
GDN_CONV = 4
CHUNK = 64
MLA_Q_LORA = 256
MLA_KV_LORA = 128
MLA_NOPE = 64
MLA_ROPE = 32
MLA_QK = 128
RWKV_GN_EPS = 64e-5
IDX_HEADS = 8
IDX_TOPK_MAX = 256
D_FF = 4 * D_MODEL
INT_MIN = -2 ** 31

VMEM_LIMIT_BYTES = 56 * 1024 * 1024

OFF_RWKV = 0
OFF_GDN_Q = 1024
OFF_GDN_K = 1280
OFF_GDN_V = 1536
OFF_GDN_Z = 1792
OFF_MLA_CQ = 2048
OFF_MLA_CKV = 2304
OFF_MLA_KR = 2432
OFF_DSA_IQ = 2560
OFF_DSA_Q = 3072
OFF_DSA_K = 3328
OFF_DSA_V = 3584
OFF_GDN_AB = 3840
OFF_DSA_IKW = 3968
OFF_GATE = 4096
N_SLAB = 8192

NN = (((1,), (0,)), ((), ()))
NT = (((1,), (1,)), ((), ()))
TN = (((0,), (0,)), ((), ()))


def _cparams(n_axes):
    return pltpu.CompilerParams(dimension_semantics=("arbitrary",) * n_axes,
                                vmem_limit_bytes=VMEM_LIMIT_BYTES)


def _mm(a, b, dims=NN):
    return lax.dot_general(a.astype(BF16), b.astype(BF16), dims, preferred_element_type=F32)


def _split(a):
    hi = a.astype(BF16)
    lo = (a - hi.astype(F32)).astype(BF16)
    return hi, lo


def _mm3(a, b, dims=NN):
    ah, al = _split(a)
    bh, bl = _split(b)
    d = lambda x, y: lax.dot_general(x, y, dims, preferred_element_type=F32)
    return d(ah, bh) + (d(ah, bl) + d(al, bh))


def _mm_exact_lhs(m, x, dims=NN):
    mb = m.astype(BF16)
    x0 = x.astype(BF16)
    r1 = x - x0.astype(F32)
    x1 = r1.astype(BF16)
    x2 = (r1 - x1.astype(F32)).astype(BF16)
    d = lambda y: lax.dot_general(mb, y, dims, preferred_element_type=F32)
    return d(x0) + (d(x1) + d(x2))


def _mm_exact_rhs(x, m, dims=NN):
    mb = m.astype(BF16)
    x0 = x.astype(BF16)
    r1 = x - x0.astype(F32)
    x1 = r1.astype(BF16)
    x2 = (r1 - x1.astype(F32)).astype(BF16)
    d = lambda y: lax.dot_general(y, mb, dims, preferred_element_type=F32)
    return d(x0) + (d(x1) + d(x2))


def _iota2(shape, dim):
    return lax.broadcasted_iota(I32, shape, dim)


def _seg_ones(width, seg):
    sh = int(math.log2(seg))
    r = _iota2((width, width), 0) >> sh
    c = _iota2((width, width), 1) >> sh
    return jnp.where(r == c, 1.0, 0.0).astype(F32)


def _seg_sum(x, seg_ones):
    return _mm_exact_rhs(x, seg_ones)


def _softplus(x):
    return jnp.maximum(x, 0.0) + jnp.log1p(jnp.exp(-jnp.abs(x)))


def _sigmoid(x):
    return 1.0 / (1.0 + jnp.exp(-x))


def _silu(x):
    return x * _sigmoid(x)


def _tri_masks(n):
    r = _iota2((n, n), 0)
    c = _iota2((n, n), 1)
    return r, c


def _unit_lower_inverse(m, r, c):
    eye = jnp.where(r == c, 1.0, 0.0).astype(F32)
    same16 = (r >> 4) == (c >> 4)
    same32 = (r >> 5) == (c >> 5)
    md = jnp.where(same16, m, 0.0)
    t = eye - md
    p = md
    for _ in range(3):
        p = _mm3(p, p)
        t = t + _mm3(t, p)
    c32 = jnp.where(same32 & jnp.logical_not(same16), m, 0.0)
    t = t - _mm3(_mm3(t, c32), t)
    c64 = jnp.where(same32, 0.0, m)
    t = t - _mm3(_mm3(t, c64), t)
    return t


def _in_proj_kernel(x_ref, g_ref, w_ref, o_ref, h_ref):
    @pl.when(pl.program_id(1) == 0)
    def _():
        x = x_ref[...]
        ms = jnp.mean(x * x, axis=-1, keepdims=True)
        h_ref[...] = (x * lax.rsqrt(ms + NORM_EPS) * g_ref[...]).astype(BF16)

    o_ref[...] = jnp.dot(h_ref[...], w_ref[...], preferred_element_type=F32)


def _in_proj(x2d, g, w):
    t, d = x2d.shape
    n = w.shape[1]
    tm = min(1024, t)
    tn = 512
    return pl.pallas_call(
        _in_proj_kernel,
        grid=(t // tm, n // tn),
        in_specs=[pl.BlockSpec((tm, d), lambda i, j: (i, 0)),
                  pl.BlockSpec((1, d), lambda i, j: (0, 0)),
                  pl.BlockSpec((d, tn), lambda i, j: (0, j))],
        out_specs=pl.BlockSpec((tm, tn), lambda i, j: (i, j)),
        out_shape=jax.ShapeDtypeStruct((t, n), F32),
        scratch_shapes=[pltpu.VMEM((tm, d), BF16)],
        compiler_params=_cparams(2),
        name="in_proj",
    )(x2d, g, w)


def _gdn_kernel(q_ref, k_ref, v_ref, z_ref, ab_ref, abr_ref, cw_ref, pcol_ref, prow_ref, ng_ref,
                o_ref, xbuf_ref, y_ref, gb_ref, state_ref, *, tb):
    nchunk = tb // CHUNK

    @pl.when(pl.program_id(1) == 0)
    def _():
        xbuf_ref[0:8, :] = jnp.zeros((8, 3 * HW), F32)
        state_ref[...] = jnp.zeros_like(state_ref)

    xbuf_ref[8:8 + tb, 0:HW] = q_ref[...]
    xbuf_ref[8:8 + tb, HW:2 * HW] = k_ref[...]
    xbuf_ref[8:8 + tb, 2 * HW:3 * HW] = v_ref[...]
    cw = cw_ref[...]
    y = xbuf_ref[8:8 + tb, :] * cw[3:4, :]
    y = y + xbuf_ref[7:7 + tb, :] * cw[2:3, :]
    y = y + xbuf_ref[6:6 + tb, :] * cw[1:2, :]
    y = y + xbuf_ref[5:5 + tb, :] * cw[0:1, :]
    xbuf_ref[0:8, :] = xbuf_ref[tb:tb + 8, :]
    y = _silu(y)
    seg = _seg_ones(HW, DH)
    q = y[:, 0:HW]
    k = y[:, HW:2 * HW]
    y_ref[:, 0:HW] = q * lax.rsqrt(_seg_sum(q * q, seg) + 1e-6) * (DH ** -0.5)
    y_ref[:, HW:2 * HW] = k * lax.rsqrt(_seg_sum(k * k, seg) + 1e-6)
    y_ref[:, 2 * HW:3 * HW] = y[:, 2 * HW:3 * HW]

    ab = ab_ref[...]
    pcol = pcol_ref[...]
    g_col = -jnp.exp(pcol[0:1, :]) * _softplus(ab + pcol[1:2, :])
    gb_ref[:, 0:128] = g_col
    gb_ref[:, 128:256] = _sigmoid(ab)

    r, c = _tri_masks(CHUNK)
    incl = c <= r
    strict = c < r
    l_incl = jnp.where(incl, 1.0, 0.0).astype(F32)
    prow = prow_ref[...]

    def chunk_body(ci, carry):
        row0 = pl.multiple_of(ci * CHUNK, CHUNK)
        yc = y_ref[pl.ds(row0, CHUNK), :]
        gbc = gb_ref[pl.ds(row0, CHUNK), :]
        gcol_all = _mm_exact_lhs(l_incl, gbc[:, 0:128])
        abr = abr_ref[0, ci]
        g_row = -jnp.exp(prow[:, 0:1]) * _softplus(abr + prow[:, 1:2])
        grow_all = _mm_exact_rhs(g_row, l_incl, NT)
        outs = []
        for h in range(HEADS):
            sl = slice(h * DH, (h + 1) * DH)
            qh = yc[:, sl]
            kh = yc[:, HW + h * DH:HW + (h + 1) * DH]
            vh = yc[:, 2 * HW + h * DH:2 * HW + (h + 1) * DH]
            beta = gbc[:, 128 + 4 + h:128 + 5 + h]
            gcol = gcol_all[:, h:h + 1]
            grow = grow_all[h:h + 1, :]
            decay = jnp.where(incl, jnp.exp(jnp.where(incl, gcol - grow, 0.0)), 0.0)
            kb = kh * beta
            vb = vh * beta
            m = jnp.where(strict, _mm3(kb, kh, NT) * decay, 0.0)
            t_inv = _unit_lower_inverse(m, r, c)
            egc = jnp.exp(gcol)
            glast = gcol[CHUNK - 1:CHUNK, :]
            uw = _mm3(t_inv, jnp.concatenate([vb, kb * egc], axis=1))
            u = uw[:, 0:DH]
            w = uw[:, DH:2 * DH]
            a_intra = _mm3(qh, kh, NT) * decay
            s = state_ref[h]
            v_new = u - _mm3(w, s)
            o = _mm3(qh * egc, s) + _mm3(a_intra, v_new)
            state_ref[h] = s * jnp.exp(glast) + _mm3(kh * jnp.exp(glast - gcol), v_new, TN)
            outs.append(o)
        o_all = jnp.concatenate(outs, axis=1)
        ms = _seg_sum(o_all * o_all, seg) * (1.0 / DH)
        o_n = o_all * lax.rsqrt(ms + NORM_EPS) * ng_ref[...]
        o_ref[pl.ds(row0, CHUNK), :] = o_n * _silu(z_ref[pl.ds(row0, CHUNK), :])
        return carry

    lax.fori_loop(0, nchunk, chunk_body, 0)


def _gdn(slab, abr, conv_w, pcol, prow, norm_g, b, s):
    tb = min(256, s)
    nb = s // tb
    nchunk = tb // CHUNK
    col = lambda off, w: pl.BlockSpec((tb, w), lambda i, j, o=off // w: (i * nb + j, o))
    full = lambda shape: pl.BlockSpec(shape, lambda i, j: (0,) * len(shape))
    return pl.pallas_call(
        functools.partial(_gdn_kernel, tb=tb),
        grid=(b, nb),
        in_specs=[col(OFF_GDN_Q, HW), col(OFF_GDN_K, HW), col(OFF_GDN_V, HW), col(OFF_GDN_Z, HW),
                  col(OFF_GDN_AB, 128),
                  pl.BlockSpec((1, nchunk, 8, CHUNK), lambda i, j: (i, j, 0, 0)),
                  full((GDN_CONV, 3 * HW)), full((8, 128)), full((8, 128)), full((1, HW))],
        out_specs=pl.BlockSpec((tb, HW), lambda i, j: (i * nb + j, 0)),
        out_shape=jax.ShapeDtypeStruct((b * s, HW), F32),
        scratch_shapes=[pltpu.VMEM((tb + 8, 3 * HW), F32), pltpu.VMEM((tb, 3 * HW), F32),
                        pltpu.VMEM((tb, 256), F32), pltpu.VMEM((HEADS, DH, DH), F32)],
        compiler_params=_cparams(2),
        name="gdn",
    )(slab, slab, slab, slab, slab, abr, conv_w, pcol, prow, norm_g)


def _rwkv_kernel(p_ref, mu_ref, vec_ref, ww2_ref, wa2_ref, wg2_ref, o_ref,
                 xbuf_ref, f_ref, state_ref, *, tb):
    nchunk = tb // CHUNK

    @pl.when(pl.program_id(1) == 0)
    def _():
        xbuf_ref[0:8, :] = jnp.zeros((8, 4 * HW), F32)
        state_ref[...] = jnp.zeros_like(state_ref)

    p = p_ref[...]
    xbuf_ref[8:8 + tb, :] = p
    prev = xbuf_ref[7:7 + tb, :]
    xbuf_ref[0:8, :] = xbuf_ref[tb:tb + 8, :]
    pm = p + (prev - p) * mu_ref[...]
    vec = vec_ref[...]
    w0, a0, k_k, k_a = vec[0:1, :], vec[1:2, :], vec[2:3, :], vec[3:4, :]
    rr = pm[:, 0:HW]
    k = pm[:, HW:2 * HW]
    v = pm[:, 2 * HW:3 * HW]
    xw = pm[:, 768:832]
    xa = pm[:, 832:896]
    xg = pm[:, 896:1024]
    w_log = -_softplus(-(w0 + _mm3(jnp.tanh(xw), ww2_ref[...]))) - 0.5
    lw = -jnp.exp(w_log)
    a = _sigmoid(a0 + _mm3(xa, wa2_ref[...]))
    g = _mm3(_sigmoid(xg), wg2_ref[...])
    seg = _seg_ones(HW, DH)
    kk = k * k_k
    kk = kk * lax.rsqrt(_seg_sum(kk * kk, seg) + 1e-6)
    k2 = k * (1.0 + (a - 1.0) * k_a)
    f_ref[:, 0:HW] = rr
    f_ref[:, HW:2 * HW] = k2
    f_ref[:, 2 * HW:3 * HW] = v
    f_ref[:, 3 * HW:4 * HW] = kk
    f_ref[:, 4 * HW:5 * HW] = kk * a
    f_ref[:, 5 * HW:6 * HW] = lw
    f_ref[:, 6 * HW:7 * HW] = g

    r_i, c_i = _tri_masks(CHUNK)
    incl = c_i <= r_i
    strict = c_i < r_i
    l_incl = jnp.where(incl, 1.0, 0.0).astype(F32)
    ln_w, ln_b, r_k = vec[4:5, :], vec[5:6, :], vec[6:7, :]

    def chunk_body(ci, carry):
        row0 = pl.multiple_of(ci * CHUNK, CHUNK)
        fc = f_ref[pl.ds(row0, CHUNK), :]
        lw_c = fc[:, 5 * HW:6 * HW]
        lwc_all = _mm_exact_lhs(l_incl, lw_c)
        outs = []
        for h in range(HEADS):
            hs = lambda j: fc[:, j * HW + h * DH:j * HW + (h + 1) * DH]
            r_h, k_h, v_h, kk_h, kka_h, lw_h = hs(0), hs(1), hs(2), hs(3), hs(4), hs(5)
            lwc = lwc_all[:, h * DH:(h + 1) * DH]
            lwl = lwc[CHUNK - 1:CHUNK, :]
            e_neg = jnp.exp(-lwc)
            at = kk_h * jnp.exp(lwc - lw_h)
            bt = kka_h * e_neg
            kt = k_h * e_neg
            rt = r_h * jnp.exp(lwc)
            y2 = jnp.concatenate([at, rt], axis=0)
            x2 = jnp.concatenate([bt, kt], axis=0)
            gram = _mm3(y2, x2, NT)
            n_m = jnp.where(strict, gram[0:CHUNK, 0:CHUNK], 0.0)
            p_m = jnp.where(strict, gram[0:CHUNK, CHUNK:], 0.0)
            qb_m = jnp.where(incl, gram[CHUNK:, 0:CHUNK], 0.0)
            qk_m = jnp.where(incl, gram[CHUNK:, CHUNK:], 0.0)
            t_inv = _unit_lower_inverse(n_m, r_i, c_i)
            zt = state_ref[h]
            yz = _mm3(y2, zt, NT)
            u = _mm3(t_inv, yz[0:CHUNK, :] + _mm3(p_m, v_h))
            o = yz[CHUNK:, :] + _mm3(qk_m, v_h) - _mm3(qb_m, u)
            e_rel = jnp.exp(lwl - lwc)
            vu = jnp.concatenate([v_h, -u], axis=0)
            kb = jnp.concatenate([k_h * e_rel, kka_h * e_rel], axis=0)
            state_ref[h] = zt * jnp.exp(lwl) + _mm3(vu, kb, TN)
            outs.append(o)
        o_all = jnp.concatenate(outs, axis=1)
        mean = _seg_sum(o_all, seg) * (1.0 / DH)
        cen = o_all - mean
        var = _seg_sum(cen * cen, seg) * (1.0 / DH)
        o_n = cen * lax.rsqrt(var + RWKV_GN_EPS) * ln_w + ln_b
        bonus = _seg_sum(fc[:, 0:HW] * fc[:, HW:2 * HW] * r_k, seg) * fc[:, 2 * HW:3 * HW]
        o_ref[pl.ds(row0, CHUNK), :] = (o_n + bonus) * fc[:, 6 * HW:7 * HW]
        return carry

    lax.fori_loop(0, nchunk, chunk_body, 0)


def _rwkv(slab, mu, vec, ww2, wa2, wg2, b, s):
    tb = min(256, s)
    nb = s // tb
    full = lambda shape: pl.BlockSpec(shape, lambda i, j: (0,) * len(shape))
    return pl.pallas_call(
        functools.partial(_rwkv_kernel, tb=tb),
        grid=(b, nb),
        in_specs=[pl.BlockSpec((tb, 4 * HW), lambda i, j: (i * nb + j, OFF_RWKV // (4 * HW))),
                  full((1, 4 * HW)), full((8, HW)), full((64, HW)), full((64, HW)), full((128, HW))],
        out_specs=pl.BlockSpec((tb, HW), lambda i, j: (i * nb + j, 0)),
        out_shape=jax.ShapeDtypeStruct((b * s, HW), F32),
        scratch_shapes=[pltpu.VMEM((tb + 8, 4 * HW), F32), pltpu.VMEM((tb, 7 * HW), F32),
                        pltpu.VMEM((HEADS, DH, DH), F32)],
        compiler_params=_cparams(2),
        name="rwkv7",
    )(slab, mu, vec, ww2, wa2, wg2)


def _rope(x, cos_t, sin_lo, sin_hi, half):
    w = x.shape[-1]
    return x * cos_t + pltpu.roll(x, w - half, 1) * sin_lo + pltpu.roll(x, half, 1) * sin_hi


def _tile_lanes(t, n):
    return t if n == 1 else jnp.concatenate([t] * n, axis=1)


def _mla_prep_kernel(cq_ref, ckv_ref, kr_ref, tab_ref, qg_ref, kvg_ref, wq_ref, wk_ref, wv_ref,
                     q_ref, k_ref, v_ref):
    cq = cq_ref[...]
    hq = cq * lax.rsqrt(jnp.mean(cq * cq, axis=-1, keepdims=True) + NORM_EPS) * qg_ref[...]
    ckv = ckv_ref[...]
    hkv = ckv * lax.rsqrt(jnp.mean(ckv * ckv, axis=-1, keepdims=True) + NORM_EPS) * kvg_ref[...]
    tab = tab_ref[...]
    cos_t, sin_lo, sin_hi = tab[:, 0:128], tab[:, 128:256], tab[:, 256:384]
    q = _mm(hq, wq_ref[...])
    q = _rope(q, _tile_lanes(cos_t, HEADS), _tile_lanes(sin_lo, HEADS), _tile_lanes(sin_hi, HEADS),
              MLA_ROPE // 2)
    q_ref[...] = (q * ((MLA_NOPE + MLA_ROPE) ** -0.5)).astype(BF16)
    kr = _rope(kr_ref[...], cos_t, sin_lo, sin_hi, MLA_ROPE // 2)
    k = _mm(hkv, wk_ref[...]) + _tile_lanes(kr, HEADS)
    k_ref[...] = k.astype(BF16)
    v_ref[...] = _mm(hkv, wv_ref[...]).astype(BF16)


def _mla_prep(slab, tab, qg, kvg, wq, wk, wv):
    t = slab.shape[0]
    tm = min(512, t)
    full = lambda shape: pl.BlockSpec(shape, lambda i: (0,) * len(shape))
    return pl.pallas_call(
        _mla_prep_kernel,
        grid=(t // tm,),
        in_specs=[pl.BlockSpec((tm, 256), lambda i: (i, OFF_MLA_CQ // 256)),
                  pl.BlockSpec((tm, 128), lambda i: (i, OFF_MLA_CKV // 128)),
                  pl.BlockSpec((tm, 128), lambda i: (i, OFF_MLA_KR // 128)),
                  pl.BlockSpec((tm, 384), lambda i: (i, 0)),
                  full((1, 256)), full((1, 128)), full((256, 512)), full((128, 512)), full((128, 256))],
        out_specs=[pl.BlockSpec((tm, 512), lambda i: (i, 0)), pl.BlockSpec((tm, 512), lambda i: (i, 0)),
                   pl.BlockSpec((tm, 256), lambda i: (i, 0))],
        out_shape=[jax.ShapeDtypeStruct((t, 512), BF16), jax.ShapeDtypeStruct((t, 512), BF16),
                   jax.ShapeDtypeStruct((t, 256), BF16)],
        compiler_params=_cparams(1),
        name="mla_prep",
    )(slab, slab, slab, tab, qg, kvg, wq, wk, wv)


def _flash_kernel(q_ref, k_ref, v_ref, o_ref, m_ref, l_ref, acc_ref, *, tq, tk):
    qi = pl.program_id(1)
    ki = pl.program_id(2)

    @pl.when(ki == 0)
    def _():
        m_ref[...] = jnp.full_like(m_ref, NEG_INF)
        l_ref[...] = jnp.zeros_like(l_ref)
        acc_ref[...] = jnp.zeros_like(acc_ref)

    @pl.when(ki * tk <= qi * tq + (tq - 1))
    def _():
        q_pos = qi * tq + _iota2((tq, tk), 0)
        k_pos = ki * tk + _iota2((tq, tk), 1)
        causal = k_pos <= q_pos
        v_all = v_ref[...]
        for h in range(HEADS):
            q = q_ref[:, h * MLA_QK:(h + 1) * MLA_QK]
            k = k_ref[:, h * MLA_QK:(h + 1) * MLA_QK]
            s = lax.dot_general(q, k, NT, preferred_element_type=F32)
            s = jnp.where(causal, s, NEG_INF)
            m_old = m_ref[h]
            m_new = jnp.maximum(m_old, jnp.max(s, axis=-1, keepdims=True))
            alpha = jnp.exp(m_old - m_new)
            p = jnp.exp(s - m_new)
            l_ref[h] = alpha * l_ref[h] + jnp.sum(p, axis=-1, keepdims=True)
            pv = jnp.dot(p.astype(BF16), v_all[:, h * DH:(h + 1) * DH], preferred_element_type=F32)
            acc_ref[h] = alpha * acc_ref[h] + pv
            m_ref[h] = m_new

    @pl.when(ki == pl.num_programs(2) - 1)
    def _():
        o_ref[...] = jnp.concatenate([acc_ref[h] / l_ref[h] for h in range(HEADS)], axis=1)


def _flash(q, k, v, b, s):
    tq = min(512, s)
    tk = tq
    nq = s // tq
    nk = s // tk
    kv_idx = lambda i, j, kk: (i * nk + jnp.minimum(kk, (j * tq + tq - 1) // tk), 0)
    return pl.pallas_call(
        functools.partial(_flash_kernel, tq=tq, tk=tk),
        grid=(b, nq, nk),
        in_specs=[pl.BlockSpec((tq, 512), lambda i, j, kk: (i * nq + j, 0)),
                  pl.BlockSpec((tk, 512), kv_idx),
                  pl.BlockSpec((tk, 256), kv_idx)],
        out_specs=pl.BlockSpec((tq, 256), lambda i, j, kk: (i * nq + j, 0)),
        out_shape=jax.ShapeDtypeStruct((b * s, 256), F32),
        scratch_shapes=[pltpu.VMEM((HEADS, tq, 1), F32), pltpu.VMEM((HEADS, tq, 1), F32),
                        pltpu.VMEM((HEADS, tq, DH), F32)],
        compiler_params=_cparams(3),
        name="mla_flash",
    )(q, k, v)


def _dsa_key_prep_kernel(ikw_ref, k_ref, tab_ref, ik_ref, ko_ref):
    tab = tab_ref[...]
    cos_t, sin_lo, sin_hi = tab[:, 0:128], tab[:, 128:256], tab[:, 256:384]
    half = DH // 8
    ik = _rope(ikw_ref[...], cos_t, sin_lo, sin_hi, half)
    ik_ref[...] = ik[:, 0:DH].astype(BF16)
    kr = _rope(k_ref[...], _tile_lanes(cos_t, 2), _tile_lanes(sin_lo, 2), _tile_lanes(sin_hi, 2), half)
    for h in range(HEADS):
        ko_ref[h] = kr[:, h * DH:(h + 1) * DH].astype(BF16)


def _dsa_key_prep(slab, tab):
    t = slab.shape[0]
    tm = min(512, t)
    return pl.pallas_call(
        _dsa_key_prep_kernel,
        grid=(t // tm,),
        in_specs=[pl.BlockSpec((tm, 128), lambda i: (i, OFF_DSA_IKW // 128)),
                  pl.BlockSpec((tm, 256), lambda i: (i, OFF_DSA_K // 256)),
                  pl.BlockSpec((tm, 384), lambda i: (i, 0))],
        out_specs=[pl.BlockSpec((tm, DH), lambda i: (i, 0)),
                   pl.BlockSpec((HEADS, tm, DH), lambda i: (0, i, 0))],
        out_shape=[jax.ShapeDtypeStruct((t, DH), BF16), jax.ShapeDtypeStruct((HEADS, t, DH), BF16)],
        compiler_params=_cparams(1),
        name="dsa_key_prep",
    )(slab, slab, tab)


def _rope_rows(x, cos8, sin8, nheads):
    parts = []
    for h in range(nheads):
        x1 = x[h * DH:h * DH + 8, :]
        x2 = x[h * DH + 8:h * DH + 16, :]
        parts.append(x1 * cos8 - x2 * sin8)
        parts.append(x2 * cos8 + x1 * sin8)
        parts.append(x[h * DH + 16:(h + 1) * DH, :])
    return jnp.concatenate(parts, axis=0)


def _dsa_kernel(iqt_ref, iwt_ref, qt_ref, cst_ref, ik_ref, k_ref, vt_ref, o_ref,
                keys_ref, *, tq, tk, n_sel, idx_bits):
    qi = pl.program_id(1)
    nkt = (qi * tq + tq - 1) // tk + 1
    cs = cst_ref[0]
    cos8, sin8 = cs[0:8, :], cs[8:16, :]
    iqt = _rope_rows(iqt_ref[0], cos8, sin8, IDX_HEADS).astype(BF16)
    iw = iwt_ref[0] * (IDX_HEADS ** -0.5 * DH ** -0.5)
    q_pos = qi * tq + _iota2((tk, tq), 1)

    def score_body(kt, carry):
        row0 = pl.multiple_of(kt * tk, tk)
        ik = ik_ref[pl.ds(row0, tk), :]
        sc = jnp.zeros((tk, tq), F32)
        for h in range(IDX_HEADS):
            lg = lax.dot_general(ik, iqt[h * DH:(h + 1) * DH, :], NN, preferred_element_type=F32)
            sc = sc + jnp.maximum(lg, 0.0) * iw[h:h + 1, :]
        bits = lax.bitcast_convert_type(sc, I32)
        key = bits ^ ((bits >> 31) & 0x7FFFFFFF)
        key = jnp.where(sc == 0.0, 0, key)
        k_pos = row0 + _iota2((tk, tq), 0)
        keys_ref[pl.ds(row0, tk), :] = jnp.where(k_pos <= q_pos, key, INT_MIN)
        return carry

    lax.fori_loop(0, nkt, score_body, 0)

    def count(pred_fn):
        def body(kt, acc):
            row0 = pl.multiple_of(kt * tk, tk)
            kv = keys_ref[pl.ds(row0, tk), :]
            k_idx = row0 + _iota2((tk, tq), 0)
            return acc + jnp.sum(jnp.where(pred_fn(kv, k_idx), 1.0, 0.0), axis=0, keepdims=True)
        return lax.fori_loop(0, nkt, body, jnp.zeros((1, tq), F32))

    n_sel_f = float(n_sel)

    def bit_body(it, thr):
        bit = jnp.maximum(31 - it, 0)
        cand = jnp.where(it == 0, jnp.zeros_like(thr), thr | jnp.left_shift(jnp.int32(1), bit))
        cnt = count(lambda kv, k_idx: kv >= cand)
        return jnp.where(cnt >= n_sel_f, cand, thr)

    thr = lax.fori_loop(0, 32, bit_body, jnp.full((1, tq), INT_MIN, I32))
    cnt_gt = count(lambda kv, k_idx: kv > thr)
    cnt_eq = count(lambda kv, k_idx: kv == thr)
    need = n_sel_f - cnt_gt

    def idx_body(it, j):
        cand = j + jnp.left_shift(jnp.int32(1), idx_bits - 1 - it)
        cnt = count(lambda kv, k_idx: (kv == thr) & (k_idx < cand))
        return jnp.where(cnt < need, cand, j)

    excess = jnp.max(jnp.where(cnt_eq > need, 1.0, 0.0)) > 0.0
    j_cut = lax.cond(excess,
                     lambda: lax.fori_loop(0, idx_bits, idx_body, jnp.zeros((1, tq), I32)),
                     lambda: jnp.full((1, tq), 2 ** 30, I32))

    qt = (_rope_rows(qt_ref[0], cos8, sin8, HEADS) * (DH ** -0.5)).astype(BF16)
    for h in range(HEADS):
        q_h = qt[h * DH:(h + 1) * DH, :]

        def att_body(kt, carry):
            m_old, l_old, acc = carry
            row0 = pl.multiple_of(kt * tk, tk)
            kv = keys_ref[pl.ds(row0, tk), :]
            k_idx = row0 + _iota2((tk, tq), 0)
            sel = ((kv > thr) | ((kv == thr) & (k_idx <= j_cut))) & (kv != INT_MIN)
            s = lax.dot_general(k_ref[h, pl.ds(row0, tk), :], q_h, NN,
                                preferred_element_type=F32)
            s = jnp.where(sel, s, NEG_INF)
            m_new = jnp.maximum(m_old, jnp.max(s, axis=0, keepdims=True))
            alpha = jnp.exp(m_old - m_new)
            p = jnp.where(sel, jnp.exp(s - m_new), 0.0)
            l_new = alpha * l_old + jnp.sum(p, axis=0, keepdims=True)
            pv = lax.dot_general(vt_ref[0, kt, h * DH:(h + 1) * DH, :], p.astype(BF16), NN,
                                 preferred_element_type=F32)
            return m_new, l_new, alpha * acc + pv

        m0 = jnp.full((1, tq), NEG_INF, F32)
        l0 = jnp.zeros((1, tq), F32)
        a0 = jnp.zeros((DH, tq), F32)
        _, l_f, acc_f = lax.fori_loop(0, nkt, att_body, (m0, l0, a0))
        o_ref[0, h * DH:(h + 1) * DH, :] = acc_f / l_f


DSA_TQ = 256
DSA_TK = 512


def _dsa(iqt, iwt, qt, cst, ik, kr, vt, b, s, n_sel):
    tq = min(DSA_TQ, s)
    tk = min(DSA_TK, s)
    nq = s // tq
    return pl.pallas_call(
        functools.partial(_dsa_kernel, tq=tq, tk=tk, n_sel=n_sel, idx_bits=int(math.log2(s))),
        grid=(b, nq),
        in_specs=[pl.BlockSpec((1, IDX_HEADS * DH, tq), lambda i, j: (i, 0, j)),
                  pl.BlockSpec((1, 8, tq), lambda i, j: (i, 0, j)),
                  pl.BlockSpec((1, HW, tq), lambda i, j: (i, 0, j)),
                  pl.BlockSpec((1, 16, tq), lambda i, j: (i, 0, j)),
                  pl.BlockSpec((s, DH), lambda i, j: (i, 0)),
                  pl.BlockSpec((HEADS, s, DH), lambda i, j: (0, i, 0)),
                  pl.BlockSpec((1, s // tk, HW, tk), lambda i, j: (i, 0, 0, 0))],
        out_specs=pl.BlockSpec((1, HW, tq), lambda i, j: (i, 0, j)),
        out_shape=jax.ShapeDtypeStruct((b, HW, s), F32),
        scratch_shapes=[pltpu.VMEM((s, tq), I32)],
        compiler_params=_cparams(2),
        name="dsa",
    )(iqt, iwt, qt, cst, ik, kr, vt)


def _merge_kernel(x_ref, oa_ref, ob_ref, oc_ref, od_ref, ga_ref, gb_ref, gc_ref, gd_ref,
                  wa_ref, wb_ref, wc_ref, wd_ref, wo_ref, o_ref):
    def branch(o_r, g_r, w_r):
        y = jnp.dot(o_r[...].astype(BF16), w_r[...], preferred_element_type=F32)
        return _sigmoid(g_r[...]) * y

    merged = (branch(oa_ref, ga_ref, wa_ref) + branch(ob_ref, gb_ref, wb_ref)
              + branch(oc_ref, gc_ref, wc_ref) + branch(od_ref, gd_ref, wd_ref))
    o_ref[...] = x_ref[...] + jnp.dot(merged.astype(BF16), wo_ref[...], preferred_element_type=F32)


def _merge(x2d, oa, ob, oc, od, slab, wa, wb, wc, wd, wo):
    t, d = x2d.shape
    tm = min(512, t)
    row = lambda w: pl.BlockSpec((tm, w), lambda i: (i, 0))
    gate = lambda n: pl.BlockSpec((tm, d), lambda i, o=OFF_GATE // d + n: (i, o))
    full = lambda shape: pl.BlockSpec(shape, lambda i: (0,) * len(shape))
    return pl.pallas_call(
        _merge_kernel,
        grid=(t // tm,),
        in_specs=[row(d), row(HW), row(HW), row(HW), row(HW), gate(0), gate(1), gate(2), gate(3),
                  full((HW, d)), full((HW, d)), full((HW, d)), full((HW, d)), full((d, d))],
        out_specs=row(d),
        out_shape=jax.ShapeDtypeStruct((t, d), F32),
        compiler_params=_cparams(1),
        name="merge",
    )(x2d, oa, ob, oc, od, slab, slab, slab, slab, wa, wb, wc, wd, wo)


def _mlp_kernel(x_ref, g_ref, wu_ref, wd_ref, fg_ref, o_ref, h_ref, acc_ref, *, final_norm):
    fi = pl.program_id(1)

    @pl.when(fi == 0)
    def _():
        x = x_ref[...]
        ms = jnp.mean(x * x, axis=-1, keepdims=True)
        h_ref[...] = (x * lax.rsqrt(ms + NORM_EPS) * g_ref[...]).astype(BF16)
        acc_ref[...] = jnp.zeros_like(acc_ref)

    u = jnp.dot(h_ref[...], wu_ref[...], preferred_element_type=F32)
    u = jnp.square(jnp.maximum(u, 0.0))
    acc_ref[...] += jnp.dot(u.astype(BF16), wd_ref[...], preferred_element_type=F32)

    @pl.when(fi == pl.num_programs(1) - 1)
    def _():
        y = x_ref[...] + acc_ref[...]
        if final_norm:
            ms = jnp.mean(y * y, axis=-1, keepdims=True)
            y = y * lax.rsqrt(ms + NORM_EPS) * fg_ref[...]
        o_ref[...] = y


def _mlp(x2d, g, wu, wd, fg, final_norm):
    t, d = x2d.shape
    f = wu.shape[1]
    tm = min(1024, t)
    tf = 512
    return pl.pallas_call(
        functools.partial(_mlp_kernel, final_norm=final_norm),
        grid=(t // tm, f // tf),
        in_specs=[pl.BlockSpec((tm, d), lambda i, j: (i, 0)),
                  pl.BlockSpec((1, d), lambda i, j: (0, 0)),
                  pl.BlockSpec((d, tf), lambda i, j: (0, j)),
                  pl.BlockSpec((tf, d), lambda i, j: (j, 0)),
                  pl.BlockSpec((1, d), lambda i, j: (0, 0))],
        out_specs=pl.BlockSpec((tm, d), lambda i, j: (i, 0)),
        out_shape=jax.ShapeDtypeStruct((t, d), F32),
        scratch_shapes=[pltpu.VMEM((tm, d), BF16), pltpu.VMEM((tm, d), F32)],
        compiler_params=_cparams(2),
        name="mlp",
    )(x2d, g, wu, wd, fg)


def _slab_weight(w_in, dtype=BF16):
    d = w_in.shape[0]
    z = lambda n: jnp.zeros((d, n), w_in.dtype)
    gdn, mla, rwkv, dsa = 0, 1032, 1448, 2472
    gate = 3824
    cols = [
        w_in[:, rwkv:rwkv + 1024],
        w_in[:, gdn:gdn + 768],
        w_in[:, gdn + 776:gdn + 1032],
        w_in[:, mla:mla + 256],
        w_in[:, mla + 256:mla + 384],
        z(64), w_in[:, mla + 384:mla + 416], z(32),
        w_in[:, dsa:dsa + 512],
        w_in[:, dsa + 584:dsa + 1352],
        w_in[:, gdn + 768:gdn + 776], z(120),
        w_in[:, dsa + 512:dsa + 584], z(56),
        w_in[:, gate:gate + 4096],
    ]
    return jnp.concatenate(cols, axis=1).astype(dtype)


def _mla_weights(w_uq, w_ukv):
    wq = w_uq.reshape(MLA_Q_LORA, HEADS, MLA_NOPE + MLA_ROPE)
    wq = jnp.pad(wq, ((0, 0), (0, 0), (0, MLA_QK - MLA_NOPE - MLA_ROPE))).reshape(MLA_Q_LORA, HEADS * MLA_QK)
    wkv = w_ukv.reshape(MLA_KV_LORA, HEADS, MLA_NOPE + DH)
    wk = jnp.pad(wkv[:, :, :MLA_NOPE], ((0, 0), (0, 0), (0, MLA_QK - MLA_NOPE))).reshape(MLA_KV_LORA, HEADS * MLA_QK)
    wv = wkv[:, :, MLA_NOPE:].reshape(MLA_KV_LORA, HW)
    return wq.astype(BF16), wk.astype(BF16), wv.astype(BF16)


def _rope_tables(positions, rot_dims, offset, period):
    half = rot_dims // 2
    inv_freq = ROPE_THETA ** (-jnp.arange(0, rot_dims, 2, dtype=F32) / rot_dims)
    ang = positions.astype(F32)[..., None] * inv_freq
    cos, sin = jnp.cos(ang), jnp.sin(ang)
    b, s, _ = cos.shape
    lead = jnp.zeros((b, s, offset), F32)
    tail = jnp.zeros((b, s, period - offset - rot_dims), F32)
    zh = jnp.zeros((b, s, half), F32)
    cos_t = jnp.concatenate([lead + 1.0, cos, cos, tail + 1.0], axis=-1)
    sin_lo = jnp.concatenate([lead, -sin, zh, tail], axis=-1)
    sin_hi = jnp.concatenate([lead, zh, sin, tail], axis=-1)
    tab = jnp.concatenate([cos_t, sin_lo, sin_hi], axis=-1)
    return tab.reshape(b * s, 3 * period), cos, sin


def _layer(x2d, b, s, prm, tabs, final_g, is_last):
    (mix_g, w_slab, conv_w, gdn_pcol, gdn_prow, gdn_ng, mla_qg, mla_kvg, wq, wk, wv,
     rwkv_mu, rwkv_vec, ww2, wa2, wg2, wo_gdn, wo_mla, wo_rwkv, wo_dsa, w_o, mlp_g, w_up, w_down) = prm
    mla_tab, dsa_tab, dsa_cst = tabs
    t = b * s
    slab = _in_proj(x2d, mix_g, w_slab)

    ab = slab[:, OFF_GDN_AB:OFF_GDN_AB + 8].reshape(b, s // CHUNK, CHUNK, 8)
    abr = jnp.swapaxes(ab, 2, 3)
    o_gdn = _gdn(slab, abr, conv_w, gdn_pcol, gdn_prow, gdn_ng, b, s)
    q, k, v = _mla_prep(slab, mla_tab, mla_qg, mla_kvg, wq, wk, wv)
    o_mla = _flash(q, k, v, b, s)
    o_rwkv = _rwkv(slab, rwkv_mu, rwkv_vec, ww2, wa2, wg2, b, s)
    ik, kr = _dsa_key_prep(slab, dsa_tab)
    tr = lambda off, w: jnp.swapaxes(slab[:, off:off + w].reshape(b, s, w), 1, 2)
    iqt = tr(OFF_DSA_IQ, IDX_HEADS * DH)
    iwt = tr(OFF_DSA_IKW + DH, 8)
    qt = tr(OFF_DSA_Q, HW)
    tk = min(DSA_TK, s)
    vt = slab[:, OFF_DSA_V:OFF_DSA_V + HW].astype(BF16).reshape(b, s // tk, tk, HW)
    vt = jnp.swapaxes(vt, 2, 3)
    n_sel = min(IDX_TOPK_MAX, s // 4)
    o_dsa_t = _dsa(iqt, iwt, qt, dsa_cst, ik, kr, vt, b, s, n_sel)
    o_dsa = jnp.swapaxes(o_dsa_t, 1, 2).reshape(t, HW)

    x2d = _merge(x2d, o_gdn, o_mla, o_rwkv, o_dsa, slab, wo_gdn, wo_mla, wo_rwkv, wo_dsa, w_o)
    return _mlp(x2d, mlp_g, w_up, w_down, final_g, is_last)


def kernel(x, positions, mix_norm_g, w_in, gdn_conv_w, gdn_a_log, gdn_dt_bias, gdn_norm_g, mla_q_norm_g, mla_kv_norm_g, mla_w_uq, mla_w_ukv, rwkv_mu, rwkv_w0, rwkv_w_w2, rwkv_a0, rwkv_w_a2, rwkv_w_g2, rwkv_k_k, rwkv_k_a, rwkv_r_k, rwkv_ln_w, rwkv_ln_b, w_out_gdn, w_out_mla, w_out_rwkv, w_out_dsa, w_o, mlp_norm_g, w_up, w_down, final_norm_g):
    b, s, d = x.shape
    depth = w_in.shape[0]
    mla_tab, _, _ = _rope_tables(positions, MLA_ROPE, MLA_NOPE, MLA_QK)
    dsa_tab, cos8, sin8 = _rope_tables(positions, DH // 4, 0, 128 // 2)
    dsa_tab = jnp.concatenate([jnp.concatenate([dsa_tab[:, i * 64:(i + 1) * 64]] * 2, axis=1) for i in range(3)], axis=1)
    dsa_cst = jnp.swapaxes(jnp.concatenate([cos8, sin8], axis=-1), 1, 2)
    tabs = (mla_tab, dsa_tab, dsa_cst)
    row = lambda a: a.reshape(1, -1).astype(F32)
    pad8 = lambda rows: jnp.concatenate(rows + [jnp.zeros((8 - len(rows), rows[0].shape[1]), F32)], axis=0)
    x2d = x.reshape(b * s, d)
    final_g = row(final_norm_g)
    for l in range(depth):
        lane4 = lambda a: jnp.pad(a.astype(F32), (0, 124)).reshape(1, 128)
        gdn_pcol = pad8([lane4(gdn_a_log[l]), lane4(gdn_dt_bias[l])])
        gdn_prow = jnp.pad(jnp.stack([gdn_a_log[l], gdn_dt_bias[l]], axis=1).astype(F32), ((0, 4), (0, 126)))
        wq, wk, wv = _mla_weights(mla_w_uq[l], mla_w_ukv[l])
        rwkv_vec = pad8([row(rwkv_w0[l]), row(rwkv_a0[l]), row(rwkv_k_k[l]), row(rwkv_k_a[l]),
                         row(rwkv_ln_w[l]), row(rwkv_ln_b[l]), row(rwkv_r_k[l])])
        prm = (row(mix_norm_g[l]), _slab_weight(w_in[l]), gdn_conv_w[l].astype(F32), gdn_pcol, gdn_prow,
               row(jnp.tile(gdn_norm_g[l], HEADS)), row(mla_q_norm_g[l]), row(mla_kv_norm_g[l]), wq, wk, wv,
               row(rwkv_mu[l]), rwkv_vec, rwkv_w_w2[l].astype(F32), rwkv_w_a2[l].astype(F32),
               rwkv_w_g2[l].astype(F32),
               w_out_gdn[l].astype(BF16), w_out_mla[l].astype(BF16), w_out_rwkv[l].astype(BF16),
               w_out_dsa[l].astype(BF16), w_o[l].astype(BF16), row(mlp_norm_g[l]),
               w_up[l].astype(BF16), w_down[l].astype(BF16))
        x2d = _layer(x2d, b, s, prm, tabs, final_g, l == depth - 1)
    return x2d.reshape(b, s, d)
```

```python
import functools
import math

import numpy as np
import jax
import jax.numpy as jnp
from jax import lax
from jax.experimental import pallas as pl
from jax.experimental.pallas import tpu as pltpu

F32 = jnp.float32
BF16 = jnp.bfloat16
I32 = jnp.int32

D_MODEL = 1024
DEPTH = 4
ROPE_THETA = 500000.0
NORM_EPS = 1e-6
NEG_INF = -1e30
HEADS = 4
DH = 64
HW = HEADS * DH
GDN_CONV = 4
CHUNK = 64
MLA_Q_LORA = 256
MLA_KV_LORA = 128
MLA_NOPE = 64
MLA_ROPE = 32
MLA_QK = 128
RWKV_GN_EPS = 64e-5
IDX_HEADS = 8
IDX_TOPK_MAX = 256
D_FF = 4 * D_MODEL
INT_MIN = -2 ** 31

VMEM_LIMIT_BYTES = 56 * 1024 * 1024

OFF_RWKV = 0
OFF_GDN_Q = 1024
OFF_GDN_K = 1280
OFF_GDN_V = 1536
OFF_GDN_Z = 1792
OFF_MLA_CQ = 2048
OFF_MLA_CKV = 2304
OFF_MLA_KR = 2432
OFF_DSA_IQ = 2560
OFF_DSA_Q = 3072
OFF_DSA_K = 3328
OFF_DSA_V = 3584
OFF_GDN_AB = 3840
OFF_DSA_IKW = 3968
OFF_GATE = 4096
N_SLAB = 8192

NN = (((1,), (0,)), ((), ()))
NT = (((1,), (1,)), ((), ()))
TN = (((0,), (0,)), ((), ()))
BNN = (((2,), (1,)), ((0,), (0,)))
BNT = (((2,), (2,)), ((0,), (0,)))
BTN = (((1,), (1,)), ((0,), (0,)))


def _cparams(n_axes):
    return pltpu.CompilerParams(dimension_semantics=("arbitrary",) * n_axes,
                                vmem_limit_bytes=VMEM_LIMIT_BYTES)


def _mm(a, b, dims=NN):
    return lax.dot_general(a.astype(BF16), b.astype(BF16), dims, preferred_element_type=F32)


def _split(a):
    hi = a.astype(BF16)
    lo = (a - hi.astype(F32)).astype(BF16)
    return hi, lo


def _mm3(a, b, dims=NN):
    ah, al = _split(a)
    bh, bl = _split(b)
    d = lambda x, y: lax.dot_general(x, y, dims, preferred_element_type=F32)
    return d(ah, bh) + (d(ah, bl) + d(al, bh))


def _mm_exact_lhs(m, x, dims=NN):
    mb = m.astype(BF16)
    x0 = x.astype(BF16)
    r1 = x - x0.astype(F32)
    x1 = r1.astype(BF16)
    x2 = (r1 - x1.astype(F32)).astype(BF16)
    d = lambda y: lax.dot_general(mb, y, dims, preferred_element_type=F32)
    return d(x0) + (d(x1) + d(x2))


def _mm_exact_rhs(x, m, dims=NN):
    mb = m.astype(BF16)
    x0 = x.astype(BF16)
    r1 = x - x0.astype(F32)
    x1 = r1.astype(BF16)
    x2 = (r1 - x1.astype(F32)).astype(BF16)
    d = lambda y: lax.dot_general(y, mb, dims, preferred_element_type=F32)
    return d(x0) + (d(x1) + d(x2))


def _iota2(shape, dim):
    return lax.broadcasted_iota(I32, shape, dim)


def _seg_ones(width, seg):
    sh = int(math.log2(seg))
    r = _iota2((width, width), 0) >> sh
    c = _iota2((width, width), 1) >> sh
    return jnp.where(r == c, 1.0, 0.0).astype(F32)


def _seg_sum(x, seg_ones):
    return _mm_exact_rhs(x, seg_ones)


def _softplus(x):
    return jnp.maximum(x, 0.0) + jnp.log1p(jnp.exp(-jnp.abs(x)))


def _sigmoid(x):
    return 1.0 / (1.0 + jnp.exp(-x))


def _silu(x):
    return x * _sigmoid(x)


def _tri_masks(n):
    r = _iota2((n, n), 0)
    c = _iota2((n, n), 1)
    return r, c


def _unit_lower_inverse(m, r, c, dims=NN):
    eye = jnp.where(r == c, 1.0, 0.0).astype(F32)
    same16 = (r >> 4) == (c >> 4)
    same32 = (r >> 5) == (c >> 5)
    md = jnp.where(same16, m, 0.0)
    t = eye - md
    p = md
    for _ in range(3):
        p = _mm3(p, p, dims)
        t = t + _mm3(t, p, dims)
    c32 = jnp.where(same32 & jnp.logical_not(same16), m, 0.0)
    t = t - _mm3(_mm3(t, c32, dims), t, dims)
    c64 = jnp.where(same32, 0.0, m)
    t = t - _mm3(_mm3(t, c64, dims), t, dims)
    return t


def _in_proj_kernel(x_ref, g_ref, w_ref, o_ref, h_ref):
    @pl.when(pl.program_id(1) == 0)
    def _():
        x = x_ref[...]
        ms = jnp.mean(x * x, axis=-1, keepdims=True)
        h_ref[...] = (x * lax.rsqrt(ms + NORM_EPS) * g_ref[...]).astype(BF16)

    o_ref[...] = jnp.dot(h_ref[...], w_ref[...], preferred_element_type=F32)


def _in_proj(x2d, g, w):
    t, d = x2d.shape
    n = w.shape[1]
    tm = min(1024, t)
    tn = 512
    return pl.pallas_call(
        _in_proj_kernel,
        grid=(t // tm, n // tn),
        in_specs=[pl.BlockSpec((tm, d), lambda i, j: (i, 0)),
                  pl.BlockSpec((1, d), lambda i, j: (0, 0)),
                  pl.BlockSpec((d, tn), lambda i, j: (0, j))],
        out_specs=pl.BlockSpec((tm, tn), lambda i, j: (i, j)),
        out_shape=jax.ShapeDtypeStruct((t, n), F32),
        scratch_shapes=[pltpu.VMEM((tm, d), BF16)],
        compiler_params=_cparams(2),
        name="in_proj",
    )(x2d, g, w)


def _to_groups(x, nchunk):
    parts = [x[:, h * DH:(h + 1) * DH].reshape(nchunk, 1, CHUNK, DH) for h in range(HEADS)]
    return jnp.concatenate(parts, axis=1).reshape(nchunk * HEADS, CHUNK, DH)


def _col_groups(x, lane0, nchunk):
    parts = [x[:, lane0 + h:lane0 + h + 1].reshape(nchunk, 1, CHUNK, 1) for h in range(HEADS)]
    return jnp.concatenate(parts, axis=1).reshape(nchunk * HEADS, CHUNK, 1)


def _gdn_kernel(q_ref, k_ref, v_ref, z_ref, ab_ref, abr_ref, cw_ref, pcol_ref, prow_ref, ng_ref,
                o_ref, xbuf_ref, state_ref, u_ref, w_ref, a_ref, qd_ref, kd_ref, egl_ref, oraw_ref,
                *, tb):
    nchunk = tb // CHUNK

    @pl.when(pl.program_id(1) == 0)
    def _():
        xbuf_ref[0:8, :] = jnp.zeros((8, 3 * HW), F32)
        state_ref[...] = jnp.zeros_like(state_ref)

    xbuf_ref[8:8 + tb, 0:HW] = q_ref[...]
    xbuf_ref[8:8 + tb, HW:2 * HW] = k_ref[...]
    xbuf_ref[8:8 + tb, 2 * HW:3 * HW] = v_ref[...]
    cw = cw_ref[...]
    y = xbuf_ref[8:8 + tb, :] * cw[3:4, :]
    y = y + xbuf_ref[7:7 + tb, :] * cw[2:3, :]
    y = y + xbuf_ref[6:6 + tb, :] * cw[1:2, :]
    y = y + xbuf_ref[5:5 + tb, :] * cw[0:1, :]
    xbuf_ref[0:8, :] = xbuf_ref[tb:tb + 8, :]
    y = _silu(y)
    seg = _seg_ones(HW, DH)
    q = y[:, 0:HW]
    k = y[:, HW:2 * HW]
    q = q * lax.rsqrt(_seg_sum(q * q, seg) + 1e-6) * (DH ** -0.5)
    k = k * lax.rsqrt(_seg_sum(k * k, seg) + 1e-6)
    v = y[:, 2 * HW:3 * HW]

    ab = ab_ref[...]
    pcol = pcol_ref[...]
    g_col = -jnp.exp(pcol[0:1, :]) * _softplus(ab + pcol[1:2, :])
    beta_col = _sigmoid(ab)

    r, c = _tri_masks(CHUNK)
    incl = c <= r
    strict = c < r
    l_incl = jnp.where(incl, 1.0, 0.0).astype(F32)

    rb = _iota2((tb, tb), 0)
    cb = _iota2((tb, tb), 1)
    l_blk = jnp.where(((rb >> 6) == (cb >> 6)) & (cb <= rb), 1.0, 0.0).astype(F32)
    gc_col = _mm_exact_lhs(l_blk, g_col)
    prow = prow_ref[...]
    abr = abr_ref[0].reshape(nchunk * 8, CHUNK)
    prow_t = jnp.concatenate([prow] * nchunk, axis=0)
    g_row = -jnp.exp(prow_t[:, 0:1]) * _softplus(abr + prow_t[:, 1:2])
    gc_row = _mm_exact_rhs(g_row, l_incl, NT).reshape(nchunk, 8, CHUNK)

    qg, kg, vg = _to_groups(q, nchunk), _to_groups(k, nchunk), _to_groups(v, nchunk)
    beta = _col_groups(beta_col, 4, nchunk)
    gcol = _col_groups(gc_col, 0, nchunk)
    grow = jnp.concatenate([gc_row[:, h:h + 1, :].reshape(nchunk, 1, 1, CHUNK) for h in range(HEADS)],
                           axis=1).reshape(nchunk * HEADS, 1, CHUNK)
    decay = jnp.where(incl, jnp.exp(jnp.where(incl, gcol - grow, 0.0)), 0.0)
    kb = kg * beta
    m = jnp.where(strict, _mm3(kb, kg, BNT) * decay, 0.0)
    t_inv = _unit_lower_inverse(m, r, c, BNN)
    egc = jnp.exp(gcol)
    glast = gcol[:, CHUNK - 1:CHUNK, :]
    uw = _mm3(t_inv, jnp.concatenate([vg * beta, kb * egc], axis=2), BNN)
    u_ref[...] = uw[:, :, 0:DH]
    w_ref[...] = uw[:, :, DH:2 * DH]
    a_ref[...] = _mm3(qg, kg, BNT) * decay
    qd_ref[...] = qg * egc
    kd_ref[...] = kg * jnp.exp(glast - gcol)
    egl_ref[...] = jnp.broadcast_to(jnp.exp(glast), (nchunk * HEADS, 1, DH))

    def chunk_body(ci, carry):
        sl = pl.ds(pl.multiple_of(ci * HEADS, HEADS), HEADS)
        s = state_ref[...]
        v_new = u_ref[sl] - _mm3(w_ref[sl], s, BNN)
        o = _mm3(qd_ref[sl], s, BNN) + _mm3(a_ref[sl], v_new, BNN)
        state_ref[...] = s * egl_ref[sl] + _mm3(kd_ref[sl], v_new, BTN)
        oraw_ref[pl.ds(pl.multiple_of(ci * CHUNK, CHUNK), CHUNK), :] = jnp.concatenate(
            [o[h] for h in range(HEADS)], axis=1)
        return carry

    lax.fori_loop(0, nchunk, chunk_body, 0)
    o_all = oraw_ref[...]
    ms = _seg_sum(o_all * o_all, seg) * (1.0 / DH)
    o_ref[...] = o_all * lax.rsqrt(ms + NORM_EPS) * ng_ref[...] * _silu(z_ref[...])


GDN_TB = 512


def _gdn(slab, abr, conv_w, pcol, prow, norm_g, b, s):
    tb = min(GDN_TB, s)
    nb = s // tb
    nchunk = tb // CHUNK
    ng = nchunk * HEADS
    grp = lambda w: pltpu.VMEM((ng, CHUNK, w), F32)
    col = lambda off, w: pl.BlockSpec((tb, w), lambda i, j, o=off // w: (i * nb + j, o))
    full = lambda shape: pl.BlockSpec(shape, lambda i, j: (0,) * len(shape))
    return pl.pallas_call(
        functools.partial(_gdn_kernel, tb=tb),
        grid=(b, nb),
        in_specs=[col(OFF_GDN_Q, HW), col(OFF_GDN_K, HW), col(OFF_GDN_V, HW), col(OFF_GDN_Z, HW),
                  col(OFF_GDN_AB, 128),
                  pl.BlockSpec((1, nchunk, 8, CHUNK), lambda i, j: (i, j, 0, 0)),
                  full((GDN_CONV, 3 * HW)), full((8, 128)), full((8, 128)), full((1, HW))],
        out_specs=pl.BlockSpec((tb, HW), lambda i, j: (i * nb + j, 0)),
        out_shape=jax.ShapeDtypeStruct((b * s, HW), F32),
        scratch_shapes=[pltpu.VMEM((tb + 8, 3 * HW), F32), pltpu.VMEM((HEADS, DH, DH), F32),
                        grp(DH), grp(DH), grp(CHUNK), grp(DH), grp(DH),
                        pltpu.VMEM((ng, 1, DH), F32), pltpu.VMEM((tb, HW), F32)],
        compiler_params=_cparams(2),
        name="gdn",
    )(slab, slab, slab, slab, slab, abr, conv_w, pcol, prow, norm_g)


def _rwkv_kernel(p_ref, mu_ref, vec_ref, ww2_ref, wa2_ref, wg2_ref, o_ref,
                 xbuf_ref, state_ref, ta_ref, tpv_ref, rt_ref, qkv_ref, qb_ref, bc_ref, ktv_ref,
                 el_ref, oraw_ref, *, tb):
    nchunk = tb // CHUNK

    @pl.when(pl.program_id(1) == 0)
    def _():
        xbuf_ref[0:8, :] = jnp.zeros((8, 4 * HW), F32)
        state_ref[...] = jnp.zeros_like(state_ref)

    p = p_ref[...]
    xbuf_ref[8:8 + tb, :] = p
    prev = xbuf_ref[7:7 + tb, :]
    xbuf_ref[0:8, :] = xbuf_ref[tb:tb + 8, :]
    pm = p + (prev - p) * mu_ref[...]
    vec = vec_ref[...]
    w0, a0, k_k, k_a = vec[0:1, :], vec[1:2, :], vec[2:3, :], vec[3:4, :]
    rr = pm[:, 0:HW]
    k = pm[:, HW:2 * HW]
    v = pm[:, 2 * HW:3 * HW]
    xw = pm[:, 768:832]
    xa = pm[:, 832:896]
    xg = pm[:, 896:1024]
    w_log = -_softplus(-(w0 + _mm3(jnp.tanh(xw), ww2_ref[...]))) - 0.5
    lw = -jnp.exp(w_log)
    a = _sigmoid(a0 + _mm3(xa, wa2_ref[...]))
    g = _mm3(_sigmoid(xg), wg2_ref[...])
    seg = _seg_ones(HW, DH)
    kk = k * k_k
    kk = kk * lax.rsqrt(_seg_sum(kk * kk, seg) + 1e-6)
    k2 = k * (1.0 + (a - 1.0) * k_a)
    kka = kk * a

    r_i, c_i = _tri_masks(CHUNK)
    incl = c_i <= r_i
    strict = c_i < r_i
    rb = _iota2((tb, tb), 0)
    cb = _iota2((tb, tb), 1)
    l_blk = jnp.where(((rb >> 6) == (cb >> 6)) & (cb <= rb), 1.0, 0.0).astype(F32)
    lwc_all = _mm_exact_lhs(l_blk, lw)

    rg, kg, vg = _to_groups(rr, nchunk), _to_groups(k2, nchunk), _to_groups(v, nchunk)
    kkg, kkag = _to_groups(kk, nchunk), _to_groups(kka, nchunk)
    lwg, lwc = _to_groups(lw, nchunk), _to_groups(lwc_all, nchunk)
    lwl = lwc[:, CHUNK - 1:CHUNK, :]
    e_neg = jnp.exp(-lwc)
    at = kkg * jnp.exp(lwc - lwg)
    rt = rg * jnp.exp(lwc)
    y2 = jnp.concatenate([at, rt], axis=1)
    x2 = jnp.concatenate([kkag * e_neg, kg * e_neg], axis=1)
    gram = _mm3(y2, x2, BNT)
    n_m = jnp.where(strict, gram[:, 0:CHUNK, 0:CHUNK], 0.0)
    p_m = jnp.where(strict, gram[:, 0:CHUNK, CHUNK:], 0.0)
    qb_m = jnp.where(incl, gram[:, CHUNK:, 0:CHUNK], 0.0)
    qk_m = jnp.where(incl, gram[:, CHUNK:, CHUNK:], 0.0)
    t_inv = _unit_lower_inverse(n_m, r_i, c_i, BNN)
    ta_ref[...] = _mm3(t_inv, at, BNN)
    tpv_ref[...] = _mm3(t_inv, _mm3(p_m, vg, BNN), BNN)
    rt_ref[...] = rt
    qkv_ref[...] = _mm3(qk_m, vg, BNN)
    qb_ref[...] = qb_m
    e_rel = jnp.exp(lwl - lwc)
    bc_ref[...] = kkag * e_rel
    ktv_ref[...] = _mm3(vg, kg * e_rel, BTN)
    el_ref[...] = jnp.exp(lwl)

    def chunk_body(ci, carry):
        sl = pl.ds(pl.multiple_of(ci * HEADS, HEADS), HEADS)
        zt = state_ref[...]
        u = _mm3(ta_ref[sl], zt, BNT) + tpv_ref[sl]
        o = _mm3(rt_ref[sl], zt, BNT) + qkv_ref[sl] - _mm3(qb_ref[sl], u, BNN)
        state_ref[...] = zt * el_ref[sl] + ktv_ref[sl] - _mm3(u, bc_ref[sl], BTN)
        oraw_ref[pl.ds(pl.multiple_of(ci * CHUNK, CHUNK), CHUNK), :] = jnp.concatenate(
            [o[h] for h in range(HEADS)], axis=1)
        return carry

    lax.fori_loop(0, nchunk, chunk_body, 0)
    ln_w, ln_b, r_k = vec[4:5, :], vec[5:6, :], vec[6:7, :]
    o_all = oraw_ref[...]
    mean = _seg_sum(o_all, seg) * (1.0 / DH)
    cen = o_all - mean
    var = _seg_sum(cen * cen, seg) * (1.0 / DH)
    o_n = cen * lax.rsqrt(var + RWKV_GN_EPS) * ln_w + ln_b
    bonus = _seg_sum(rr * k2 * r_k, seg) * v
    o_ref[...] = (o_n + bonus) * g


RWKV_TB = 512


def _rwkv(slab, mu, vec, ww2, wa2, wg2, b, s):
    tb = min(RWKV_TB, s)
    nb = s // tb
    ng = (tb // CHUNK) * HEADS
    grp = lambda: pltpu.VMEM((ng, CHUNK, DH), F32)
    full = lambda shape: pl.BlockSpec(shape, lambda i, j: (0,) * len(shape))
    return pl.pallas_call(
        functools.partial(_rwkv_kernel, tb=tb),
        grid=(b, nb),
        in_specs=[pl.BlockSpec((tb, 4 * HW), lambda i, j: (i * nb + j, OFF_RWKV // (4 * HW))),
                  full((1, 4 * HW)), full((8, HW)), full((64, HW)), full((64, HW)), full((128, HW))],
        out_specs=pl.BlockSpec((tb, HW), lambda i, j: (i * nb + j, 0)),
        out_shape=jax.ShapeDtypeStruct((b * s, HW), F32),
        scratch_shapes=[pltpu.VMEM((tb + 8, 4 * HW), F32), pltpu.VMEM((HEADS, DH, DH), F32),
                        grp(), grp(), grp(), grp(), grp(), grp(), grp(),
                        pltpu.VMEM((ng, 1, DH), F32), pltpu.VMEM((tb, HW), F32)],
        compiler_params=_cparams(2),
        name="rwkv7",
    )(slab, mu, vec, ww2, wa2, wg2)


def _rope(x, cos_t, sin_lo, sin_hi, half):
    w = x.shape[-1]
    return x * cos_t + pltpu.roll(x, w - half, 1) * sin_lo + pltpu.roll(x, half, 1) * sin_hi


def _tile_lanes(t, n):
    return t if n == 1 else jnp.concatenate([t] * n, axis=1)


def _mla_prep_kernel(cq_ref, ckv_ref, kr_ref, tab_ref, qg_ref, kvg_ref, wq_ref, wk_ref, wvt_ref,
                     q_ref, k_ref, vt_ref):
    cq = cq_ref[...]
    hq = cq * lax.rsqrt(jnp.mean(cq * cq, axis=-1, keepdims=True) + NORM_EPS) * qg_ref[...]
    ckv = ckv_ref[...]
    hkv = ckv * lax.rsqrt(jnp.mean(ckv * ckv, axis=-1, keepdims=True) + NORM_EPS) * kvg_ref[...]
    tab = tab_ref[...]
    cos_t, sin_lo, sin_hi = tab[:, 0:128], tab[:, 128:256], tab[:, 256:384]
    q = _mm(hq, wq_ref[...])
    q = _rope(q, _tile_lanes(cos_t, HEADS), _tile_lanes(sin_lo, HEADS), _tile_lanes(sin_hi, HEADS),
              MLA_ROPE // 2)
    q_ref[...] = (q * ((MLA_NOPE + MLA_ROPE) ** -0.5)).astype(BF16)
    kr = _rope(kr_ref[...], cos_t, sin_lo, sin_hi, MLA_ROPE // 2)
    k = _mm(hkv, wk_ref[...]) + _tile_lanes(kr, HEADS)
    k_ref[...] = k.astype(BF16)
    vt_ref[...] = lax.dot_general(wvt_ref[...], hkv.astype(BF16), NT,
                                  preferred_element_type=F32).astype(BF16)


def _mla_prep(slab, tab, qg, kvg, wq, wk, wvt):
    t = slab.shape[0]
    tm = min(512, t)
    full = lambda shape: pl.BlockSpec(shape, lambda i: (0,) * len(shape))
    return pl.pallas_call(
        _mla_prep_kernel,
        grid=(t // tm,),
        in_specs=[pl.BlockSpec((tm, 256), lambda i: (i, OFF_MLA_CQ // 256)),
                  pl.BlockSpec((tm, 128), lambda i: (i, OFF_MLA_CKV // 128)),
                  pl.BlockSpec((tm, 128), lambda i: (i, OFF_MLA_KR // 128)),
                  pl.BlockSpec((tm, 384), lambda i: (i, 0)),
                  full((1, 256)), full((1, 128)), full((256, 512)), full((128, 512)), full((256, 128))],
        out_specs=[pl.BlockSpec((tm, 512), lambda i: (i, 0)), pl.BlockSpec((tm, 512), lambda i: (i, 0)),
                   pl.BlockSpec((256, tm), lambda i: (0, i))],
        out_shape=[jax.ShapeDtypeStruct((t, 512), BF16), jax.ShapeDtypeStruct((t, 512), BF16),
                   jax.ShapeDtypeStruct((256, t), BF16)],
        compiler_params=_cparams(1),
        name="mla_prep",
    )(slab, slab, slab, tab, qg, kvg, wq, wk, wvt)


def _flash_kernel(q_ref, k_ref, vt_ref, o_ref, m_ref, l_ref, acc_ref, *, tq, tk):
    qi = pl.program_id(1)
    ki = pl.program_id(2)

    @pl.when(ki == 0)
    def _():
        m_ref[...] = jnp.full_like(m_ref, NEG_INF)
        l_ref[...] = jnp.zeros_like(l_ref)
        acc_ref[...] = jnp.zeros_like(acc_ref)

    def step(on_diagonal):
        if on_diagonal:
            causal = _iota2((tk, tq), 0) <= _iota2((tk, tq), 1)
        vt = vt_ref[...]
        scores = [lax.dot_general(k_ref[:, h * MLA_QK:(h + 1) * MLA_QK], q_ref[:, h * MLA_QK:(h + 1) * MLA_QK],
                                  NT, preferred_element_type=F32) for h in range(HEADS)]
        for h in range(HEADS):
            s = scores[h]
            if on_diagonal:
                s = jnp.where(causal, s, NEG_INF)
            m_old = m_ref[h]
            m_new = jnp.maximum(m_old, jnp.max(s, axis=0, keepdims=True))
            alpha = jnp.exp(m_old - m_new)
            p = jnp.exp(s - m_new)
            l_ref[h] = alpha * l_ref[h] + jnp.sum(p, axis=0, keepdims=True)
            pv = jnp.dot(vt[h * DH:(h + 1) * DH, :], p.astype(BF16), preferred_element_type=F32)
            acc_ref[h * DH:(h + 1) * DH, :] = alpha * acc_ref[h * DH:(h + 1) * DH, :] + pv
            m_ref[h] = m_new

    pl.when(ki < qi)(lambda: step(False))
    pl.when(ki == qi)(lambda: step(True))

    @pl.when(ki == pl.num_programs(2) - 1)
    def _():
        o_t = jnp.concatenate([acc_ref[h * DH:(h + 1) * DH, :] / l_ref[h] for h in range(HEADS)], axis=0)
        o_ref[...] = o_t.T


def _flash(q, k, vt, b, s):
    tq = min(512, s)
    tk = tq
    nq = s // tq
    nk = s // tk
    return pl.pallas_call(
        functools.partial(_flash_kernel, tq=tq, tk=tk),
        grid=(b, nq, nk),
        in_specs=[pl.BlockSpec((tq, 512), lambda i, j, kk: (i * nq + j, 0)),
                  pl.BlockSpec((tk, 512), lambda i, j, kk: (i * nk + jnp.minimum(kk, j), 0)),
                  pl.BlockSpec((256, tk), lambda i, j, kk: (0, i * nk + jnp.minimum(kk, j)))],
        out_specs=pl.BlockSpec((tq, 256), lambda i, j, kk: (i * nq + j, 0)),
        out_shape=jax.ShapeDtypeStruct((b * s, 256), F32),
        scratch_shapes=[pltpu.VMEM((HEADS, 1, tq), F32), pltpu.VMEM((HEADS, 1, tq), F32),
                        pltpu.VMEM((HW, tq), F32)],
        compiler_params=_cparams(3),
        name="mla_flash",
    )(q, k, vt)


DSA_TQ = 256
DSA_TK = 512


def _dsa_key_prep_kernel(ikw_ref, k_ref, v_ref, tab_ref, ik_ref, ko_ref, vt_ref):
    tab = tab_ref[...]
    cos_t, sin_lo, sin_hi = tab[:, 0:128], tab[:, 128:256], tab[:, 256:384]
    half = DH // 8
    ik = _rope(ikw_ref[...], cos_t, sin_lo, sin_hi, half)
    ik_ref[...] = ik[:, 0:DH].astype(BF16)
    kr = _rope(k_ref[...], _tile_lanes(cos_t, 2), _tile_lanes(sin_lo, 2), _tile_lanes(sin_hi, 2), half)
    for h in range(HEADS):
        ko_ref[h] = kr[:, h * DH:(h + 1) * DH].astype(BF16)
    vt_ref[0] = v_ref[...].T.astype(BF16)


def _dsa_key_prep(slab, tab):
    t = slab.shape[0]
    tm = min(DSA_TK, t)
    return pl.pallas_call(
        _dsa_key_prep_kernel,
        grid=(t // tm,),
        in_specs=[pl.BlockSpec((tm, 128), lambda i: (i, OFF_DSA_IKW // 128)),
                  pl.BlockSpec((tm, 256), lambda i: (i, OFF_DSA_K // 256)),
                  pl.BlockSpec((tm, 256), lambda i: (i, OFF_DSA_V // 256)),
                  pl.BlockSpec((tm, 384), lambda i: (i, 0))],
        out_specs=[pl.BlockSpec((tm, DH), lambda i: (i, 0)),
                   pl.BlockSpec((HEADS, tm, DH), lambda i: (0, i, 0)),
                   pl.BlockSpec((1, HW, tm), lambda i: (i, 0, 0))],
        out_shape=[jax.ShapeDtypeStruct((t, DH), BF16), jax.ShapeDtypeStruct((HEADS, t, DH), BF16),
                   jax.ShapeDtypeStruct((t // tm, HW, tm), BF16)],
        compiler_params=_cparams(1),
        name="dsa_key_prep",
    )(slab, slab, slab, tab)


def _rope_rows(x, cos8, sin8, nheads):
    parts = []
    for h in range(nheads):
        x1 = x[h * DH:h * DH + 8, :]
        x2 = x[h * DH + 8:h * DH + 16, :]
        parts.append(x1 * cos8 - x2 * sin8)
        parts.append(x2 * cos8 + x1 * sin8)
        parts.append(x[h * DH + 16:(h + 1) * DH, :])
    return jnp.concatenate(parts, axis=0)


def _dsa_kernel(iq_ref, ikw_ref, q_ref, cst_ref, ik_ref, k_ref, vt_ref, o_ref,
                keys_ref, bias_ref, acc_ref, s_ref, m_ref, l_ref, *, tq, tk, n_sel, idx_bits):
    qi = pl.program_id(1)
    nkt = (qi * tq + tq - 1) // tk + 1
    cs = cst_ref[0]
    cos8, sin8 = cs[0:8, :], cs[8:16, :]
    iqt = _rope_rows(iq_ref[...].T, cos8, sin8, IDX_HEADS).astype(BF16)
    iw = ikw_ref[...].T[DH:DH + 8, :] * (IDX_HEADS ** -0.5 * DH ** -0.5)
    q_pos = qi * tq + _iota2((tk, tq), 1)

    def score_body(kt, carry):
        row0 = pl.multiple_of(kt * tk, tk)
        ik = ik_ref[pl.ds(row0, tk), :]
        sc = jnp.zeros((tk, tq), F32)
        for h in range(IDX_HEADS):
            lg = lax.dot_general(ik, iqt[h * DH:(h + 1) * DH, :], NN, preferred_element_type=F32)
            sc = sc + jnp.maximum(lg, 0.0) * iw[h:h + 1, :]
        bits = lax.bitcast_convert_type(sc, I32)
        key = bits ^ ((bits >> 31) & 0x7FFFFFFF)
        key = jnp.where(sc == 0.0, 0, key)
        k_pos = row0 + _iota2((tk, tq), 0)
        keys_ref[pl.ds(row0, tk), :] = jnp.where(k_pos <= q_pos, key, INT_MIN)
        return carry

    lax.fori_loop(0, nkt, score_body, 0)

    def count(pred_fn):
        def body(kt, acc):
            row0 = pl.multiple_of(kt * tk, tk)
            kv = keys_ref[pl.ds(row0, tk), :]
            k_idx = row0 + _iota2((tk, tq), 0)
            ones = jnp.where(pred_fn(kv, k_idx), 1.0, 0.0)
            return acc + jnp.sum(ones.reshape(tk // 64, 64, tq), axis=0)
        part = lax.fori_loop(0, nkt, body, jnp.zeros((64, tq), F32))
        return jnp.sum(part, axis=0, keepdims=True)

    n_sel_f = float(n_sel)

    def bit_body(it, thr):
        bit = jnp.maximum(31 - it, 0)
        cand = jnp.where(it == 0, jnp.zeros_like(thr), thr | jnp.left_shift(jnp.int32(1), bit))
        cnt = count(lambda kv, k_idx: kv >= cand)
        return jnp.where(cnt >= n_sel_f, cand, thr)

    thr = lax.fori_loop(0, 32, bit_body, jnp.full((1, tq), INT_MIN, I32))
    cnt_gt = count(lambda kv, k_idx: kv > thr)
    cnt_eq = count(lambda kv, k_idx: kv == thr)
    need = n_sel_f - cnt_gt

    def idx_body(it, j):
        cand = j + jnp.left_shift(jnp.int32(1), idx_bits - 1 - it)
        cnt = count(lambda kv, k_idx: (kv == thr) & (k_idx < cand))
        return jnp.where(cnt < need, cand, j)

    excess = jnp.max(jnp.where(cnt_eq > need, 1.0, 0.0)) > 0.0
    j_cut = lax.cond(excess,
                     lambda: lax.fori_loop(0, idx_bits, idx_body, jnp.zeros((1, tq), I32)),
                     lambda: jnp.full((1, tq), 2 ** 30, I32))

    def bias_body(kt, carry):
        row0 = pl.multiple_of(kt * tk, tk)
        kv = keys_ref[pl.ds(row0, tk), :]
        k_idx = row0 + _iota2((tk, tq), 0)
        sel = ((kv > thr) | ((kv == thr) & (k_idx <= j_cut))) & (kv != INT_MIN)
        bias_ref[pl.ds(row0, tk), :] = jnp.where(sel, 0.0, NEG_INF)
        return carry

    lax.fori_loop(0, nkt, bias_body, 0)

    qt = (_rope_rows(q_ref[...].T, cos8, sin8, HEADS) * (DH ** -0.5)).astype(BF16)
    acc_ref[...] = jnp.zeros_like(acc_ref)

    def att_body(kt, carry):
        row0 = pl.multiple_of(kt * tk, tk)
        for h in range(HEADS):
            s_ref[h] = lax.dot_general(k_ref[h, pl.ds(row0, tk), :], qt[h * DH:(h + 1) * DH, :], NN,
                                       preferred_element_type=F32)

        @pl.when(kt >= 0)
        def _():
            bias = bias_ref[pl.ds(row0, tk), :]
            for h in range(HEADS):
                hs = slice(h * DH, (h + 1) * DH)
                s = s_ref[h] + bias
                m_old = m_ref[h]
                m_new = jnp.maximum(m_old, jnp.max(s, axis=0, keepdims=True))
                alpha = jnp.exp(m_old - m_new)
                p = jnp.exp(s - m_new)
                pv = lax.dot_general(vt_ref[kt, hs, :], p.astype(BF16), NN,
                                     preferred_element_type=F32)
                acc_ref[hs, :] = alpha * acc_ref[hs, :] + pv
                m_ref[h] = m_new
                l_ref[h] = alpha * l_ref[h] + jnp.sum(p, axis=0, keepdims=True)
        return carry

    m_ref[...] = jnp.full_like(m_ref, NEG_INF)
    l_ref[...] = jnp.zeros_like(l_ref)
    lax.fori_loop(0, nkt, att_body, 0)
    o_t = jnp.concatenate([acc_ref[h * DH:(h + 1) * DH, :] / l_ref[h] for h in range(HEADS)], axis=0)
    o_ref[...] = o_t.T


def _dsa(slab, cst, ik, kr, vt, b, s, n_sel):
    tq = min(DSA_TQ, s)
    tk = min(DSA_TK, s)
    nq = s // tq
    nkt = s // tk
    return pl.pallas_call(
        functools.partial(_dsa_kernel, tq=tq, tk=tk, n_sel=n_sel, idx_bits=int(math.log2(s))),
        grid=(b, nq),
        in_specs=[pl.BlockSpec((tq, IDX_HEADS * DH), lambda i, j: (i * nq + j, OFF_DSA_IQ // (IDX_HEADS * DH))),
                  pl.BlockSpec((tq, 128), lambda i, j: (i * nq + j, OFF_DSA_IKW // 128)),
                  pl.BlockSpec((tq, HW), lambda i, j: (i * nq + j, OFF_DSA_Q // HW)),
                  pl.BlockSpec((1, 16, tq), lambda i, j: (i, 0, j)),
                  pl.BlockSpec((s, DH), lambda i, j: (i, 0)),
                  pl.BlockSpec((HEADS, s, DH), lambda i, j: (0, i, 0)),
                  pl.BlockSpec((nkt, HW, tk), lambda i, j: (i, 0, 0))],
        out_specs=pl.BlockSpec((tq, HW), lambda i, j: (i * nq + j, 0)),
        out_shape=jax.ShapeDtypeStruct((b * s, HW), F32),
        scratch_shapes=[pltpu.VMEM((s, tq), I32), pltpu.VMEM((s, tq), F32), pltpu.VMEM((HW, tq), F32),
                        pltpu.VMEM((HEADS, tk, tq), F32), pltpu.VMEM((HEADS, 1, tq), F32),
                        pltpu.VMEM((HEADS, 1, tq), F32)],
        compiler_params=_cparams(2),
        name="dsa",
    )(slab, slab, slab, cst, ik, kr, vt)


def _merge_kernel(x_ref, oa_ref, ob_ref, oc_ref, od_ref, ga_ref, gb_ref, gc_ref, gd_ref,
                  wa_ref, wb_ref, wc_ref, wd_ref, wo_ref, o_ref):
    def branch(o_r, g_r, w_r):
        y = jnp.dot(o_r[...].astype(BF16), w_r[...], preferred_element_type=F32)
        return _sigmoid(g_r[...]) * y

    merged = (branch(oa_ref, ga_ref, wa_ref) + branch(ob_ref, gb_ref, wb_ref)
              + branch(oc_ref, gc_ref, wc_ref) + branch(od_ref, gd_ref, wd_ref))
    o_ref[...] = x_ref[...] + jnp.dot(merged.astype(BF16), wo_ref[...], preferred_element_type=F32)


def _merge(x2d, oa, ob, oc, od, slab, wa, wb, wc, wd, wo):
    t, d = x2d.shape
    tm = min(512, t)
    row = lambda w: pl.BlockSpec((tm, w), lambda i: (i, 0))
    gate = lambda n: pl.BlockSpec((tm, d), lambda i, o=OFF_GATE // d + n: (i, o))
    full = lambda shape: pl.BlockSpec(shape, lambda i: (0,) * len(shape))
    return pl.pallas_call(
        _merge_kernel,
        grid=(t // tm,),
        in_specs=[row(d), row(HW), row(HW), row(HW), row(HW), gate(0), gate(1), gate(2), gate(3),
                  full((HW, d)), full((HW, d)), full((HW, d)), full((HW, d)), full((d, d))],
        out_specs=row(d),
        out_shape=jax.ShapeDtypeStruct((t, d), F32),
        compiler_params=_cparams(1),
        name="merge",
    )(x2d, oa, ob, oc, od, slab, slab, slab, slab, wa, wb, wc, wd, wo)


def _mlp_kernel(x_ref, g_ref, wu_ref, wd_ref, fg_ref, o_ref, h_ref, acc_ref, *, final_norm):
    fi = pl.program_id(1)

    @pl.when(fi == 0)
    def _():
        x = x_ref[...]
        ms = jnp.mean(x * x, axis=-1, keepdims=True)
        h_ref[...] = (x * lax.rsqrt(ms + NORM_EPS) * g_ref[...]).astype(BF16)
        acc_ref[...] = jnp.zeros_like(acc_ref)

    u = jnp.dot(h_ref[...], wu_ref[...], preferred_element_type=F32)
    u = jnp.square(jnp.maximum(u, 0.0))
    acc_ref[...] += jnp.dot(u.astype(BF16), wd_ref[...], preferred_element_type=F32)

    @pl.when(fi == pl.num_programs(1) - 1)
    def _():
        y = x_ref[...] + acc_ref[...]
        if final_norm:
            ms = jnp.mean(y * y, axis=-1, keepdims=True)
            y = y * lax.rsqrt(ms + NORM_EPS) * fg_ref[...]
        o_ref[...] = y


def _mlp(x2d, g, wu, wd, fg, final_norm):
    t, d = x2d.shape
    f = wu.shape[1]
    tm = min(1024, t)
    tf = 512
    return pl.pallas_call(
        functools.partial(_mlp_kernel, final_norm=final_norm),
        grid=(t // tm, f // tf),
        in_specs=[pl.BlockSpec((tm, d), lambda i, j: (i, 0)),
                  pl.BlockSpec((1, d), lambda i, j: (0, 0)),
                  pl.BlockSpec((d, tf), lambda i, j: (0, j)),
                  pl.BlockSpec((tf, d), lambda i, j: (j, 0)),
                  pl.BlockSpec((1, d), lambda i, j: (0, 0))],
        out_specs=pl.BlockSpec((tm, d), lambda i, j: (i, 0)),
        out_shape=jax.ShapeDtypeStruct((t, d), F32),
        scratch_shapes=[pltpu.VMEM((tm, d), BF16), pltpu.VMEM((tm, d), F32)],
        compiler_params=_cparams(2),
        name="mlp",
    )(x2d, g, wu, wd, fg)


def _slab_weight(w_in, dtype=BF16):
    d = w_in.shape[0]
    z = lambda n: jnp.zeros((d, n), w_in.dtype)
    gdn, mla, rwkv, dsa = 0, 1032, 1448, 2472
    gate = 3824
    cols = [
        w_in[:, rwkv:rwkv + 1024],
        w_in[:, gdn:gdn + 768],
        w_in[:, gdn + 776:gdn + 1032],
        w_in[:, mla:mla + 256],
        w_in[:, mla + 256:mla + 384],
        z(64), w_in[:, mla + 384:mla + 416], z(32),
        w_in[:, dsa:dsa + 512],
        w_in[:, dsa + 584:dsa + 1352],
        w_in[:, gdn + 768:gdn + 776], z(120),
        w_in[:, dsa + 512:dsa + 584], z(56),
        w_in[:, gate:gate + 4096],
    ]
    return jnp.concatenate(cols, axis=1).astype(dtype)


def _mla_weights(w_uq, w_ukv):
    wq = w_uq.reshape(MLA_Q_LORA, HEADS, MLA_NOPE + MLA_ROPE)
    wq = jnp.pad(wq, ((0, 0), (0, 0), (0, MLA_QK - MLA_NOPE - MLA_ROPE))).reshape(MLA_Q_LORA, HEADS * MLA_QK)
    wkv = w_ukv.reshape(MLA_KV_LORA, HEADS, MLA_NOPE + DH)
    wk = jnp.pad(wkv[:, :, :MLA_NOPE], ((0, 0), (0, 0), (0, MLA_QK - MLA_NOPE))).reshape(MLA_KV_LORA, HEADS * MLA_QK)
    wvt = wkv[:, :, MLA_NOPE:].reshape(MLA_KV_LORA, HW).T
    return wq.astype(BF16), wk.astype(BF16), wvt.astype(BF16)


def _rope_tables(positions, rot_dims, offset, period):
    half = rot_dims // 2
    inv_freq = ROPE_THETA ** (-jnp.arange(0, rot_dims, 2, dtype=F32) / rot_dims)
    ang = positions.astype(F32)[..., None] * inv_freq
    cos, sin = jnp.cos(ang), jnp.sin(ang)
    b, s, _ = cos.shape
    lead = jnp.zeros((b, s, offset), F32)
    tail = jnp.zeros((b, s, period - offset - rot_dims), F32)
    zh = jnp.zeros((b, s, half), F32)
    cos_t = jnp.concatenate([lead + 1.0, cos, cos, tail + 1.0], axis=-1)
    sin_lo = jnp.concatenate([lead, -sin, zh, tail], axis=-1)
    sin_hi = jnp.concatenate([lead, zh, sin, tail], axis=-1)
    tab = jnp.concatenate([cos_t, sin_lo, sin_hi], axis=-1)
    return tab.reshape(b * s, 3 * period), cos, sin


def _layer(x2d, b, s, prm, tabs, final_g, is_last):
    (mix_g, w_slab, conv_w, gdn_pcol, gdn_prow, gdn_ng, mla_qg, mla_kvg, wq, wk, wv,
     rwkv_mu, rwkv_vec, ww2, wa2, wg2, wo_gdn, wo_mla, wo_rwkv, wo_dsa, w_o, mlp_g, w_up, w_down) = prm
    mla_tab, dsa_tab, dsa_cst = tabs
    t = b * s
    slab = _in_proj(x2d, mix_g, w_slab)

    ab = slab[:, OFF_GDN_AB:OFF_GDN_AB + 8].reshape(b, s // CHUNK, CHUNK, 8)
    abr = jnp.swapaxes(ab, 2, 3)
    o_gdn = _gdn(slab, abr, conv_w, gdn_pcol, gdn_prow, gdn_ng, b, s)
    q, k, v = _mla_prep(slab, mla_tab, mla_qg, mla_kvg, wq, wk, wv)
    o_mla = _flash(q, k, v, b, s)
    o_rwkv = _rwkv(slab, rwkv_mu, rwkv_vec, ww2, wa2, wg2, b, s)
    ik, kr, vt = _dsa_key_prep(slab, dsa_tab)
    o_dsa = _dsa(slab, dsa_cst, ik, kr, vt, b, s, min(IDX_TOPK_MAX, s // 4))

    x2d = _merge(x2d, o_gdn, o_mla, o_rwkv, o_dsa, slab, wo_gdn, wo_mla, wo_rwkv, wo_dsa, w_o)
    return _mlp(x2d, mlp_g, w_up, w_down, final_g, is_last)


def kernel(x, positions, mix_norm_g, w_in, gdn_conv_w, gdn_a_log, gdn_dt_bias, gdn_norm_g, mla_q_norm_g, mla_kv_norm_g, mla_w_uq, mla_w_ukv, rwkv_mu, rwkv_w0, rwkv_w_w2, rwkv_a0, rwkv_w_a2, rwkv_w_g2, rwkv_k_k, rwkv_k_a, rwkv_r_k, rwkv_ln_w, rwkv_ln_b, w_out_gdn, w_out_mla, w_out_rwkv, w_out_dsa, w_o, mlp_norm_g, w_up, w_down, final_norm_g):
    b, s, d = x.shape
    depth = w_in.shape[0]
    mla_tab, _, _ = _rope_tables(positions, MLA_ROPE, MLA_NOPE, MLA_QK)
    dsa_tab, cos8, sin8 = _rope_tables(positions, DH // 4, 0, 128 // 2)
    dsa_tab = jnp.concatenate([jnp.concatenate([dsa_tab[:, i * 64:(i + 1) * 64]] * 2, axis=1) for i in range(3)], axis=1)
    dsa_cst = jnp.swapaxes(jnp.concatenate([cos8, sin8], axis=-1), 1, 2)
    tabs = (mla_tab, dsa_tab, dsa_cst)
    row = lambda a: a.reshape(1, -1).astype(F32)
    pad8 = lambda rows: jnp.concatenate(rows + [jnp.zeros((8 - len(rows), rows[0].shape[1]), F32)], axis=0)
    x2d = x.reshape(b * s, d)
    final_g = row(final_norm_g)
    for l in range(depth):
        lane4 = lambda a: jnp.pad(a.astype(F32), (0, 124)).reshape(1, 128)
        gdn_pcol = pad8([lane4(gdn_a_log[l]), lane4(gdn_dt_bias[l])])
        gdn_prow = jnp.pad(jnp.stack([gdn_a_log[l], gdn_dt_bias[l]], axis=1).astype(F32), ((0, 4), (0, 126)))
        wq, wk, wv = _mla_weights(mla_w_uq[l], mla_w_ukv[l])
        rwkv_vec = pad8([row(rwkv_w0[l]), row(rwkv_a0[l]), row(rwkv_k_k[l]), row(rwkv_k_a[l]),
                         row(rwkv_ln_w[l]), row(rwkv_ln_b[l]), row(rwkv_r_k[l])])
        prm = (row(mix_norm_g[l]), _slab_weight(w_in[l]), gdn_conv_w[l].astype(F32), gdn_pcol, gdn_prow,
               row(jnp.tile(gdn_norm_g[l], HEADS)), row(mla_q_norm_g[l]), row(mla_kv_norm_g[l]), wq, wk, wv,
               row(rwkv_mu[l]), rwkv_vec, rwkv_w_w2[l].astype(F32), rwkv_w_a2[l].astype(F32),
               rwkv_w_g2[l].astype(F32),
               w_out_gdn[l].astype(BF16), w_out_mla[l].astype(BF16), w_out_rwkv[l].astype(BF16),
               w_out_dsa[l].astype(BF16), w_o[l].astype(BF16), row(mlp_norm_g[l]),
               w_up[l].astype(BF16), w_down[l].astype(BF16))
        x2d = _layer(x2d, b, s, prm, tabs, final_g, l == depth - 1)
    return x2d.reshape(b, s, d)
```

```python
import functools
import math

import numpy as np
import jax
import jax.numpy as jnp
from jax import lax
from jax.experimental import pallas as pl
from jax.experimental.pallas import tpu as pltpu

F32 = jnp.float32
BF16 = jnp.bfloat16
I32 = jnp.int32

D_MODEL = 1024
DEPTH = 4
ROPE_THETA = 500000.0
NORM_EPS = 1e-6
NEG_INF = -1e30
HEADS = 4
DH = 64
HW = HEADS * DH
GDN_CONV = 4
CHUNK = 64
MLA_Q_LORA = 256
MLA_KV_LORA = 128
MLA_NOPE = 64
MLA_ROPE = 32
MLA_QK = 128
RWKV_GN_EPS = 64e-5
IDX_HEADS = 8
IDX_TOPK_MAX = 256
D_FF = 4 * D_MODEL
INT_MIN = -2 ** 31

VMEM_LIMIT_BYTES = 56 * 1024 * 1024

OFF_RWKV = 0
OFF_GDN_Q = 1024
OFF_GDN_K = 1280
OFF_GDN_V = 1536
OFF_GDN_Z = 1792
OFF_MLA_CQ = 2048
OFF_MLA_CKV = 2304
OFF_MLA_KR = 2432
OFF_DSA_IQ = 2560
OFF_DSA_Q = 3072
OFF_DSA_K = 3328
OFF_DSA_V = 3584
OFF_GDN_AB = 3840
OFF_DSA_IKW = 3968
OFF_GATE = 4096
N_SLAB = 8192

NN = (((1,), (0,)), ((), ()))
NT = (((1,), (1,)), ((), ()))
TN = (((0,), (0,)), ((), ()))
BNN = (((2,), (1,)), ((0,), (0,)))
BNT = (((2,), (2,)), ((0,), (0,)))
BTN = (((1,), (1,)), ((0,), (0,)))


def _cparams(n_axes):
    return pltpu.CompilerParams(dimension_semantics=("arbitrary",) * n_axes,
                                vmem_limit_bytes=VMEM_LIMIT_BYTES)


def _mm(a, b, dims=NN):
    return lax.dot_general(a.astype(BF16), b.astype(BF16), dims, preferred_element_type=F32)


def _mm_exact_lhs(m, x, dims=NN):
    mb = m.astype(BF16)
    x0 = x.astype(BF16)
    r1 = x - x0.astype(F32)
    x1 = r1.astype(BF16)
    x2 = (r1 - x1.astype(F32)).astype(BF16)
    d = lambda y: lax.dot_general(mb, y, dims, preferred_element_type=F32)
    return d(x0) + (d(x1) + d(x2))


def _mm_exact_rhs(x, m, dims=NN):
    mb = m.astype(BF16)
    x0 = x.astype(BF16)
    r1 = x - x0.astype(F32)
    x1 = r1.astype(BF16)
    x2 = (r1 - x1.astype(F32)).astype(BF16)
    d = lambda y: lax.dot_general(y, mb, dims, preferred_element_type=F32)
    return d(x0) + (d(x1) + d(x2))


def _iota2(shape, dim):
    return lax.broadcasted_iota(I32, shape, dim)


def _seg_ones(width, seg):
    sh = int(math.log2(seg))
    r = _iota2((width, width), 0) >> sh
    c = _iota2((width, width), 1) >> sh
    return jnp.where(r == c, 1.0, 0.0).astype(F32)


def _seg_sum(x, seg_ones):
    return _mm_exact_rhs(x, seg_ones)


def _softplus(x):
    return jnp.maximum(x, 0.0) + jnp.log1p(jnp.exp(-jnp.abs(x)))


def _sigmoid(x):
    return 1.0 / (1.0 + jnp.exp(-x))


def _silu(x):
    return x * _sigmoid(x)


def _tri_masks(n):
    r = _iota2((n, n), 0)
    c = _iota2((n, n), 1)
    return r, c


def _unit_lower_inverse(m, r, c, dims=NN):
    eye = jnp.where(r == c, 1.0, 0.0).astype(F32)
    same16 = (r >> 4) == (c >> 4)
    same32 = (r >> 5) == (c >> 5)
    md = jnp.where(same16, m, 0.0)
    t = eye - md
    p = md
    for _ in range(3):
        p = _mm(p, p, dims)
        t = t + _mm(t, p, dims)
    c32 = jnp.where(same32 & jnp.logical_not(same16), m, 0.0)
    t = t - _mm(_mm(t, c32, dims), t, dims)
    c64 = jnp.where(same32, 0.0, m)
    t = t - _mm(_mm(t, c64, dims), t, dims)
    return t


IN_PROJ_TN = 512


def _in_proj_kernel(x_ref, g_ref, w_ref, o_ref):
    x = x_ref[...]
    ms = jnp.mean(x * x, axis=-1, keepdims=True)
    h = (x * lax.rsqrt(ms + NORM_EPS) * g_ref[...]).astype(BF16)
    for n0 in range(0, o_ref.shape[1], IN_PROJ_TN):
        o_ref[:, n0:n0 + IN_PROJ_TN] = jnp.dot(
            h, w_ref[:, n0:n0 + IN_PROJ_TN], preferred_element_type=F32).astype(o_ref.dtype)


def _in_proj(x2d, g, w, out_dtype):
    t, d = x2d.shape
    n = w.shape[1]
    tm = min(512, t)
    return pl.pallas_call(
        _in_proj_kernel,
        grid=(t // tm,),
        in_specs=[pl.BlockSpec((tm, d), lambda i: (i, 0)),
                  pl.BlockSpec((1, d), lambda i: (0, 0)),
                  pl.BlockSpec((d, n), lambda i: (0, 0))],
        out_specs=pl.BlockSpec((tm, n), lambda i: (i, 0)),
        out_shape=jax.ShapeDtypeStruct((t, n), out_dtype),
        compiler_params=_cparams(1),
        name="in_proj",
    )(x2d, g, w)


def _to_groups(x, nchunk):
    parts = [x[:, h * DH:(h + 1) * DH].reshape(nchunk, 1, CHUNK, DH) for h in range(HEADS)]
    return jnp.concatenate(parts, axis=1).reshape(nchunk * HEADS, CHUNK, DH)


def _col_groups(x, lane0, nchunk):
    parts = [x[:, lane0 + h:lane0 + h + 1].reshape(nchunk, 1, CHUNK, 1) for h in range(HEADS)]
    return jnp.concatenate(parts, axis=1).reshape(nchunk * HEADS, CHUNK, 1)


def _gdn_kernel(q_ref, k_ref, v_ref, z_ref, ab_ref, abr_ref, cw_ref, pcol_ref, prow_ref, ng_ref,
                o_ref, xbuf_ref, state_ref, u_ref, w_ref, a_ref, qd_ref, kd_ref, egl_ref, oraw_ref,
                *, tb):
    nchunk = tb // CHUNK

    @pl.when(pl.program_id(1) == 0)
    def _():
        xbuf_ref[0:8, :] = jnp.zeros((8, 3 * HW), F32)
        state_ref[...] = jnp.zeros_like(state_ref)

    xbuf_ref[8:8 + tb, 0:HW] = q_ref[...]
    xbuf_ref[8:8 + tb, HW:2 * HW] = k_ref[...]
    xbuf_ref[8:8 + tb, 2 * HW:3 * HW] = v_ref[...]
    cw = cw_ref[...]
    y = xbuf_ref[8:8 + tb, :] * cw[3:4, :]
    y = y + xbuf_ref[7:7 + tb, :] * cw[2:3, :]
    y = y + xbuf_ref[6:6 + tb, :] * cw[1:2, :]
    y = y + xbuf_ref[5:5 + tb, :] * cw[0:1, :]
    xbuf_ref[0:8, :] = xbuf_ref[tb:tb + 8, :]
    y = _silu(y)
    seg = _seg_ones(HW, DH)
    q = y[:, 0:HW]
    k = y[:, HW:2 * HW]
    q = q * lax.rsqrt(_seg_sum(q * q, seg) + 1e-6) * (DH ** -0.5)
    k = k * lax.rsqrt(_seg_sum(k * k, seg) + 1e-6)
    v = y[:, 2 * HW:3 * HW]

    ab = ab_ref[...]
    pcol = pcol_ref[...]
    g_col = -jnp.exp(pcol[0:1, :]) * _softplus(ab + pcol[1:2, :])
    beta_col = _sigmoid(ab)

    r, c = _tri_masks(CHUNK)
    incl = c <= r
    strict = c < r
    l_incl = jnp.where(incl, 1.0, 0.0).astype(F32)

    rb = _iota2((tb, tb), 0)
    cb = _iota2((tb, tb), 1)
    l_blk = jnp.where(((rb >> 6) == (cb >> 6)) & (cb <= rb), 1.0, 0.0).astype(F32)
    gc_col = _mm_exact_lhs(l_blk, g_col)
    prow = prow_ref[...]
    abr = abr_ref[0].reshape(nchunk * 8, CHUNK)
    prow_t = jnp.concatenate([prow] * nchunk, axis=0)
    g_row = -jnp.exp(prow_t[:, 0:1]) * _softplus(abr + prow_t[:, 1:2])
    gc_row = _mm_exact_rhs(g_row, l_incl, NT).reshape(nchunk, 8, CHUNK)

    qg, kg, vg = _to_groups(q, nchunk), _to_groups(k, nchunk), _to_groups(v, nchunk)
    beta = _col_groups(beta_col, 4, nchunk)
    gcol = _col_groups(gc_col, 0, nchunk)
    grow = jnp.concatenate([gc_row[:, h:h + 1, :].reshape(nchunk, 1, 1, CHUNK) for h in range(HEADS)],
                           axis=1).reshape(nchunk * HEADS, 1, CHUNK)
    decay = jnp.where(incl, jnp.exp(jnp.where(incl, gcol - grow, 0.0)), 0.0)
    kb = kg * beta
    m = jnp.where(strict, _mm(kb, kg, BNT) * decay, 0.0)
    t_inv = _unit_lower_inverse(m, r, c, BNN)
    egc = jnp.exp(gcol)
    glast = gcol[:, CHUNK - 1:CHUNK, :]
    uw = _mm(t_inv, jnp.concatenate([vg * beta, kb * egc], axis=2), BNN)
    u_ref[...] = uw[:, :, 0:DH]
    w_ref[...] = uw[:, :, DH:2 * DH]
    a_ref[...] = _mm(qg, kg, BNT) * decay
    qd_ref[...] = qg * egc
    kd_ref[...] = kg * jnp.exp(glast - gcol)
    egl_ref[...] = jnp.broadcast_to(jnp.exp(glast), (nchunk * HEADS, 1, DH))

    def chunk_body(ci, carry):
        sl = pl.ds(pl.multiple_of(ci * HEADS, HEADS), HEADS)
        s = state_ref[...]
        v_new = u_ref[sl] - _mm(w_ref[sl], s, BNN)
        o = _mm(qd_ref[sl], s, BNN) + _mm(a_ref[sl], v_new, BNN)
        state_ref[...] = s * egl_ref[sl] + _mm(kd_ref[sl], v_new, BTN)
        oraw_ref[pl.ds(pl.multiple_of(ci * CHUNK, CHUNK), CHUNK), :] = jnp.concatenate(
            [o[h] for h in range(HEADS)], axis=1)
        return carry

    lax.fori_loop(0, nchunk, chunk_body, 0)
    o_all = oraw_ref[...]
    ms = _seg_sum(o_all * o_all, seg) * (1.0 / DH)
    o_ref[...] = o_all * lax.rsqrt(ms + NORM_EPS) * ng_ref[...] * _silu(z_ref[...])


GDN_TB = 512


def _gdn(slab, abr, conv_w, pcol, prow, norm_g, b, s):
    tb = min(GDN_TB, s)
    nb = s // tb
    nchunk = tb // CHUNK
    ng = nchunk * HEADS
    grp = lambda w: pltpu.VMEM((ng, CHUNK, w), F32)
    col = lambda off, w: pl.BlockSpec((tb, w), lambda i, j, o=off // w: (i * nb + j, o))
    full = lambda shape: pl.BlockSpec(shape, lambda i, j: (0,) * len(shape))
    return pl.pallas_call(
        functools.partial(_gdn_kernel, tb=tb),
        grid=(b, nb),
        in_specs=[col(OFF_GDN_Q, HW), col(OFF_GDN_K, HW), col(OFF_GDN_V, HW), col(OFF_GDN_Z, HW),
                  col(OFF_GDN_AB, 128),
                  pl.BlockSpec((1, nchunk, 8, CHUNK), lambda i, j: (i, j, 0, 0)),
                  full((GDN_CONV, 3 * HW)), full((8, 128)), full((8, 128)), full((1, HW))],
        out_specs=pl.BlockSpec((tb, HW), lambda i, j: (i * nb + j, 0)),
        out_shape=jax.ShapeDtypeStruct((b * s, HW), F32),
        scratch_shapes=[pltpu.VMEM((tb + 8, 3 * HW), F32), pltpu.VMEM((HEADS, DH, DH), F32),
                        grp(DH), grp(DH), grp(CHUNK), grp(DH), grp(DH),
                        pltpu.VMEM((ng, 1, DH), F32), pltpu.VMEM((tb, HW), F32)],
        compiler_params=_cparams(2),
        name="gdn",
    )(slab, slab, slab, slab, slab, abr, conv_w, pcol, prow, norm_g)


def _rwkv_kernel(p_ref, mu_ref, vec_ref, ww2_ref, wa2_ref, wg2_ref, o_ref,
                 xbuf_ref, state_ref, ta_ref, tpv_ref, rt_ref, qkv_ref, qb_ref, bc_ref, ktv_ref,
                 el_ref, oraw_ref, *, tb):
    nchunk = tb // CHUNK

    @pl.when(pl.program_id(1) == 0)
    def _():
        xbuf_ref[0:8, :] = jnp.zeros((8, 4 * HW), F32)
        state_ref[...] = jnp.zeros_like(state_ref)

    p = p_ref[...]
    xbuf_ref[8:8 + tb, :] = p
    prev = xbuf_ref[7:7 + tb, :]
    xbuf_ref[0:8, :] = xbuf_ref[tb:tb + 8, :]
    pm = p + (prev - p) * mu_ref[...]
    vec = vec_ref[...]
    w0, a0, k_k, k_a = vec[0:1, :], vec[1:2, :], vec[2:3, :], vec[3:4, :]
    rr = pm[:, 0:HW]
    k = pm[:, HW:2 * HW]
    v = pm[:, 2 * HW:3 * HW]
    xw = pm[:, 768:832]
    xa = pm[:, 832:896]
    xg = pm[:, 896:1024]
    w_log = -_softplus(-(w0 + _mm(jnp.tanh(xw), ww2_ref[...]))) - 0.5
    lw = -jnp.exp(w_log)
    a = _sigmoid(a0 + _mm(xa, wa2_ref[...]))
    g = _mm(_sigmoid(xg), wg2_ref[...])
    seg = _seg_ones(HW, DH)
    kk = k * k_k
    kk = kk * lax.rsqrt(_seg_sum(kk * kk, seg) + 1e-6)
    k2 = k * (1.0 + (a - 1.0) * k_a)
    kka = kk * a

    r_i, c_i = _tri_masks(CHUNK)
    incl = c_i <= r_i
    strict = c_i < r_i
    rb = _iota2((tb, tb), 0)
    cb = _iota2((tb, tb), 1)
    l_blk = jnp.where(((rb >> 6) == (cb >> 6)) & (cb <= rb), 1.0, 0.0).astype(F32)
    lwc_all = _mm_exact_lhs(l_blk, lw)

    rg, kg, vg = _to_groups(rr, nchunk), _to_groups(k2, nchunk), _to_groups(v, nchunk)
    kkg, kkag = _to_groups(kk, nchunk), _to_groups(kka, nchunk)
    lwg, lwc = _to_groups(lw, nchunk), _to_groups(lwc_all, nchunk)
    lwl = lwc[:, CHUNK - 1:CHUNK, :]
    e_neg = jnp.exp(-lwc)
    at = kkg * jnp.exp(lwc - lwg)
    rt = rg * jnp.exp(lwc)
    y2 = jnp.concatenate([at, rt], axis=1)
    x2 = jnp.concatenate([kkag * e_neg, kg * e_neg], axis=1)
    gram = _mm(y2, x2, BNT)
    n_m = jnp.where(strict, gram[:, 0:CHUNK, 0:CHUNK], 0.0)
    p_m = jnp.where(strict, gram[:, 0:CHUNK, CHUNK:], 0.0)
    qb_m = jnp.where(incl, gram[:, CHUNK:, 0:CHUNK], 0.0)
    qk_m = jnp.where(incl, gram[:, CHUNK:, CHUNK:], 0.0)
    t_inv = _unit_lower_inverse(n_m, r_i, c_i, BNN)
    ta_ref[...] = _mm(t_inv, at, BNN)
    tpv_ref[...] = _mm(t_inv, _mm(p_m, vg, BNN), BNN)
    rt_ref[...] = rt
    qkv_ref[...] = _mm(qk_m, vg, BNN)
    qb_ref[...] = qb_m
    e_rel = jnp.exp(lwl - lwc)
    bc_ref[...] = kkag * e_rel
    ktv_ref[...] = _mm(vg, kg * e_rel, BTN)
    el_ref[...] = jnp.exp(lwl)

    def chunk_body(ci, carry):
        sl = pl.ds(pl.multiple_of(ci * HEADS, HEADS), HEADS)
        zt = state_ref[...]
        u = _mm(ta_ref[sl], zt, BNT) + tpv_ref[sl]
        o = _mm(rt_ref[sl], zt, BNT) + qkv_ref[sl] - _mm(qb_ref[sl], u, BNN)
        state_ref[...] = zt * el_ref[sl] + ktv_ref[sl] - _mm(u, bc_ref[sl], BTN)
        oraw_ref[pl.ds(pl.multiple_of(ci * CHUNK, CHUNK), CHUNK), :] = jnp.concatenate(
            [o[h] for h in range(HEADS)], axis=1)
        return carry

    lax.fori_loop(0, nchunk, chunk_body, 0)
    ln_w, ln_b, r_k = vec[4:5, :], vec[5:6, :], vec[6:7, :]
    o_all = oraw_ref[...]
    mean = _seg_sum(o_all, seg) * (1.0 / DH)
    cen = o_all - mean
    var = _seg_sum(cen * cen, seg) * (1.0 / DH)
    o_n = cen * lax.rsqrt(var + RWKV_GN_EPS) * ln_w + ln_b
    bonus = _seg_sum(rr * k2 * r_k, seg) * v
    o_ref[...] = (o_n + bonus) * g


RWKV_TB = 512


def _rwkv(slab, mu, vec, ww2, wa2, wg2, b, s):
    tb = min(RWKV_TB, s)
    nb = s // tb
    ng = (tb // CHUNK) * HEADS
    grp = lambda: pltpu.VMEM((ng, CHUNK, DH), F32)
    full = lambda shape: pl.BlockSpec(shape, lambda i, j: (0,) * len(shape))
    return pl.pallas_call(
        functools.partial(_rwkv_kernel, tb=tb),
        grid=(b, nb),
        in_specs=[pl.BlockSpec((tb, 4 * HW), lambda i, j: (i * nb + j, OFF_RWKV // (4 * HW))),
                  full((1, 4 * HW)), full((8, HW)), full((64, HW)), full((64, HW)), full((128, HW))],
        out_specs=pl.BlockSpec((tb, HW), lambda i, j: (i * nb + j, 0)),
        out_shape=jax.ShapeDtypeStruct((b * s, HW), F32),
        scratch_shapes=[pltpu.VMEM((tb + 8, 4 * HW), F32), pltpu.VMEM((HEADS, DH, DH), F32),
                        grp(), grp(), grp(), grp(), grp(), grp(), grp(),
                        pltpu.VMEM((ng, 1, DH), F32), pltpu.VMEM((tb, HW), F32)],
        compiler_params=_cparams(2),
        name="rwkv7",
    )(slab, mu, vec, ww2, wa2, wg2)


def _rope(x, cos_t, sin_lo, sin_hi, half):
    w = x.shape[-1]
    return x * cos_t + pltpu.roll(x, w - half, 1) * sin_lo + pltpu.roll(x, half, 1) * sin_hi


def _tile_lanes(t, n):
    return t if n == 1 else jnp.concatenate([t] * n, axis=1)


def _mla_prep_kernel(cq_ref, ckv_ref, kr_ref, tab_ref, qg_ref, kvg_ref, wq_ref, wk_ref, wvt_ref,
                     q_ref, k_ref, vt_ref):
    cq = cq_ref[...]
    hq = cq * lax.rsqrt(jnp.mean(cq * cq, axis=-1, keepdims=True) + NORM_EPS) * qg_ref[...]
    ckv = ckv_ref[...]
    hkv = ckv * lax.rsqrt(jnp.mean(ckv * ckv, axis=-1, keepdims=True) + NORM_EPS) * kvg_ref[...]
    tab = tab_ref[...]
    cos_t, sin_lo, sin_hi = tab[:, 0:128], tab[:, 128:256], tab[:, 256:384]
    q = _mm(hq, wq_ref[...])
    q = _rope(q, _tile_lanes(cos_t, HEADS), _tile_lanes(sin_lo, HEADS), _tile_lanes(sin_hi, HEADS),
              MLA_ROPE // 2)
    q_ref[...] = (q * ((MLA_NOPE + MLA_ROPE) ** -0.5)).astype(BF16)
    kr = _rope(kr_ref[...], cos_t, sin_lo, sin_hi, MLA_ROPE // 2)
    k = _mm(hkv, wk_ref[...]) + _tile_lanes(kr, HEADS)
    k_ref[...] = k.astype(BF16)
    vt_ref[...] = lax.dot_general(wvt_ref[...], hkv.astype(BF16), NT,
                                  preferred_element_type=F32).astype(BF16)


def _mla_prep(slab, tab, qg, kvg, wq, wk, wvt):
    t = slab.shape[0]
    tm = min(512, t)
    full = lambda shape: pl.BlockSpec(shape, lambda i: (0,) * len(shape))
    return pl.pallas_call(
        _mla_prep_kernel,
        grid=(t // tm,),
        in_specs=[pl.BlockSpec((tm, 256), lambda i: (i, OFF_MLA_CQ // 256)),
                  pl.BlockSpec((tm, 128), lambda i: (i, OFF_MLA_CKV // 128)),
                  pl.BlockSpec((tm, 128), lambda i: (i, OFF_MLA_KR // 128)),
                  pl.BlockSpec((tm, 384), lambda i: (i, 0)),
                  full((1, 256)), full((1, 128)), full((256, 512)), full((128, 512)), full((256, 128))],
        out_specs=[pl.BlockSpec((tm, 512), lambda i: (i, 0)), pl.BlockSpec((tm, 512), lambda i: (i, 0)),
                   pl.BlockSpec((256, tm), lambda i: (0, i))],
        out_shape=[jax.ShapeDtypeStruct((t, 512), BF16), jax.ShapeDtypeStruct((t, 512), BF16),
                   jax.ShapeDtypeStruct((256, t), BF16)],
        compiler_params=_cparams(1),
        name="mla_prep",
    )(slab, slab, slab, tab, qg, kvg, wq, wk, wvt)


def _flash_kernel(q_ref, k_ref, vt_ref, o_ref, m_ref, l_ref, acc_ref, *, tq, tk):
    qi = pl.program_id(1)
    ki = pl.program_id(2)

    @pl.when(ki == 0)
    def _():
        m_ref[...] = jnp.full_like(m_ref, NEG_INF)
        l_ref[...] = jnp.zeros_like(l_ref)
        acc_ref[...] = jnp.zeros_like(acc_ref)

    def step(on_diagonal):
        if on_diagonal:
            causal = _iota2((tk, tq), 0) <= _iota2((tk, tq), 1)
        vt = vt_ref[...]
        scores = [lax.dot_general(k_ref[:, h * MLA_QK:(h + 1) * MLA_QK], q_ref[:, h * MLA_QK:(h + 1) * MLA_QK],
                                  NT, preferred_element_type=F32) for h in range(HEADS)]
        for h in range(HEADS):
            s = scores[h]
            if on_diagonal:
                s = jnp.where(causal, s, NEG_INF)
            m_old = m_ref[h]
            m_new = jnp.maximum(m_old, jnp.max(s, axis=0, keepdims=True))
            alpha = jnp.exp(m_old - m_new)
            p = jnp.exp(s - m_new)
            l_ref[h] = alpha * l_ref[h] + jnp.sum(p, axis=0, keepdims=True)
            pv = jnp.dot(vt[h * DH:(h + 1) * DH, :], p.astype(BF16), preferred_element_type=F32)
            acc_ref[h * DH:(h + 1) * DH, :] = alpha * acc_ref[h * DH:(h + 1) * DH, :] + pv
            m_ref[h] = m_new

    pl.when(ki < qi)(lambda: step(False))
    pl.when(ki == qi)(lambda: step(True))

    @pl.when(ki == pl.num_programs(2) - 1)
    def _():
        o_t = jnp.concatenate([acc_ref[h * DH:(h + 1) * DH, :] / l_ref[h] for h in range(HEADS)], axis=0)
        o_ref[...] = o_t.T


def _flash(q, k, vt, b, s):
    tq = min(512, s)
    tk = tq
    nq = s // tq
    nk = s // tk
    return pl.pallas_call(
        functools.partial(_flash_kernel, tq=tq, tk=tk),
        grid=(b, nq, nk),
        in_specs=[pl.BlockSpec((tq, 512), lambda i, j, kk: (i * nq + j, 0)),
                  pl.BlockSpec((tk, 512), lambda i, j, kk: (i * nk + jnp.minimum(kk, j), 0)),
                  pl.BlockSpec((256, tk), lambda i, j, kk: (0, i * nk + jnp.minimum(kk, j)))],
        out_specs=pl.BlockSpec((tq, 256), lambda i, j, kk: (i * nq + j, 0)),
        out_shape=jax.ShapeDtypeStruct((b * s, 256), F32),
        scratch_shapes=[pltpu.VMEM((HEADS, 1, tq), F32), pltpu.VMEM((HEADS, 1, tq), F32),
                        pltpu.VMEM((HW, tq), F32)],
        compiler_params=_cparams(3),
        name="mla_flash",
    )(q, k, vt)


DSA_TQ = 256
DSA_TK = 512


def _dsa_key_prep_kernel(ikw_ref, k_ref, v_ref, tab_ref, ik_ref, ko_ref, vt_ref):
    tab = tab_ref[...]
    cos_t, sin_lo, sin_hi = tab[:, 0:128], tab[:, 128:256], tab[:, 256:384]
    half = DH // 8
    ik = _rope(ikw_ref[...], cos_t, sin_lo, sin_hi, half)
    ik_ref[...] = ik[:, 0:DH].astype(BF16)
    kr = _rope(k_ref[...], _tile_lanes(cos_t, 2), _tile_lanes(sin_lo, 2), _tile_lanes(sin_hi, 2), half)
    for h in range(HEADS):
        ko_ref[h] = kr[:, h * DH:(h + 1) * DH].astype(BF16)
    vt_ref[0] = v_ref[...].T.astype(BF16)


def _dsa_key_prep(slab, tab):
    t = slab.shape[0]
    tm = min(DSA_TK, t)
    return pl.pallas_call(
        _dsa_key_prep_kernel,
        grid=(t // tm,),
        in_specs=[pl.BlockSpec((tm, 128), lambda i: (i, OFF_DSA_IKW // 128)),
                  pl.BlockSpec((tm, 256), lambda i: (i, OFF_DSA_K // 256)),
                  pl.BlockSpec((tm, 256), lambda i: (i, OFF_DSA_V // 256)),
                  pl.BlockSpec((tm, 384), lambda i: (i, 0))],
        out_specs=[pl.BlockSpec((tm, DH), lambda i: (i, 0)),
                   pl.BlockSpec((HEADS, tm, DH), lambda i: (0, i, 0)),
                   pl.BlockSpec((1, HW, tm), lambda i: (i, 0, 0))],
        out_shape=[jax.ShapeDtypeStruct((t, DH), BF16), jax.ShapeDtypeStruct((HEADS, t, DH), BF16),
                   jax.ShapeDtypeStruct((t // tm, HW, tm), BF16)],
        compiler_params=_cparams(1),
        name="dsa_key_prep",
    )(slab, slab, slab, tab)


def _rope_rows(x, cos8, sin8, nheads):
    parts = []
    for h in range(nheads):
        x1 = x[h * DH:h * DH + 8, :]
        x2 = x[h * DH + 8:h * DH + 16, :]
        parts.append(x1 * cos8 - x2 * sin8)
        parts.append(x2 * cos8 + x1 * sin8)
        parts.append(x[h * DH + 16:(h + 1) * DH, :])
    return jnp.concatenate(parts, axis=0)


def _dsa_kernel(iq_ref, ikw_ref, q_ref, cst_ref, ik_ref, k_ref, vt_ref, o_ref,
                keys_ref, bias_ref, acc_ref, s_ref, m_ref, l_ref, *, tq, tk, n_sel, idx_bits):
    qi = pl.program_id(1)
    nkt = (qi * tq + tq - 1) // tk + 1
    cs = cst_ref[0]
    cos8, sin8 = cs[0:8, :], cs[8:16, :]
    iqt = _rope_rows(iq_ref[...].T, cos8, sin8, IDX_HEADS).astype(BF16)
    iw = ikw_ref[...].T[DH:DH + 8, :] * (IDX_HEADS ** -0.5 * DH ** -0.5)
    q_pos = qi * tq + _iota2((tk, tq), 1)

    def score_body(kt, carry):
        row0 = pl.multiple_of(kt * tk, tk)
        ik = ik_ref[pl.ds(row0, tk), :]
        sc = jnp.zeros((tk, tq), F32)
        for h in range(IDX_HEADS):
            lg = lax.dot_general(ik, iqt[h * DH:(h + 1) * DH, :], NN, preferred_element_type=F32)
            sc = sc + jnp.maximum(lg, 0.0) * iw[h:h + 1, :]
        bits = lax.bitcast_convert_type(sc, I32)
        key = bits ^ ((bits >> 31) & 0x7FFFFFFF)
        key = jnp.where(sc == 0.0, 0, key)
        k_pos = row0 + _iota2((tk, tq), 0)
        keys_ref[pl.ds(row0, tk), :] = jnp.where(k_pos <= q_pos, key, INT_MIN)
        return carry

    lax.fori_loop(0, nkt, score_body, 0)

    def count(*pred_fns):
        def body(kt, accs):
            row0 = pl.multiple_of(kt * tk, tk)
            kv = keys_ref[pl.ds(row0, tk), :]
            k_idx = row0 + _iota2((tk, tq), 0)
            return tuple(acc + jnp.sum(jnp.where(fn(kv, k_idx), 1.0, 0.0).reshape(tk // 64, 64, tq), axis=0)
                         for acc, fn in zip(accs, pred_fns))
        parts = lax.fori_loop(0, nkt, body, tuple(jnp.zeros((64, tq), F32) for _ in pred_fns))
        sums = tuple(jnp.sum(p, axis=0, keepdims=True) for p in parts)
        return sums[0] if len(sums) == 1 else sums

    n_sel_f = float(n_sel)

    def bit_body(st):
        it, thr, done, _ = st
        bit = jnp.maximum(31 - it, 0)
        cand = jnp.where(it == 0, jnp.zeros_like(thr), thr | jnp.left_shift(jnp.int32(1), bit))
        cnt = count(lambda kv, k_idx: kv >= cand)
        thr = jnp.where((done < 0.5) & (cnt >= n_sel_f), cand, thr)
        done = jnp.maximum(done, jnp.where(cnt == n_sel_f, 1.0, 0.0))
        return it + 1, thr, done, (jnp.min(done) > 0.5).astype(I32)

    _, thr, _, _ = lax.while_loop(
        lambda st: (st[0] < 32) & (st[3] == 0), bit_body,
        (jnp.int32(0), jnp.full((1, tq), INT_MIN, I32), jnp.zeros((1, tq), F32), jnp.int32(0)))
    cnt_gt, cnt_eq = count(lambda kv, k_idx: kv > thr, lambda kv, k_idx: kv == thr)
    need = n_sel_f - cnt_gt

    def idx_body(it, j):
        cand = j + jnp.left_shift(jnp.int32(1), idx_bits - 1 - it)
        cnt = count(lambda kv, k_idx: (kv == thr) & (k_idx < cand))
        return jnp.where(cnt < need, cand, j)

    excess = jnp.max(jnp.where(cnt_eq > need, 1.0, 0.0)) > 0.0
    j_cut = lax.cond(excess,
                     lambda: lax.fori_loop(0, idx_bits, idx_body, jnp.zeros((1, tq), I32)),
                     lambda: jnp.full((1, tq), 2 ** 30, I32))

    def bias_body(kt, carry):
        row0 = pl.multiple_of(kt * tk, tk)
        kv = keys_ref[pl.ds(row0, tk), :]
        k_idx = row0 + _iota2((tk, tq), 0)
        sel = ((kv > thr) | ((kv == thr) & (k_idx <= j_cut))) & (kv != INT_MIN)
        bias_ref[pl.ds(row0, tk), :] = jnp.where(sel, 0.0, NEG_INF)
        return carry

    lax.fori_loop(0, nkt, bias_body, 0)

    qt = (_rope_rows(q_ref[...].T, cos8, sin8, HEADS) * (DH ** -0.5)).astype(BF16)
    acc_ref[...] = jnp.zeros_like(acc_ref)

    def att_body(kt, carry):
        row0 = pl.multiple_of(kt * tk, tk)
        for h in range(HEADS):
            s_ref[h] = lax.dot_general(k_ref[h, pl.ds(row0, tk), :], qt[h * DH:(h + 1) * DH, :], NN,
                                       preferred_element_type=F32)

        @pl.when(kt >= 0)
        def _():
            bias = bias_ref[pl.ds(row0, tk), :]
            for h in range(HEADS):
                hs = slice(h * DH, (h + 1) * DH)
                s = s_ref[h] + bias
                m_old = m_ref[h]
                m_new = jnp.maximum(m_old, jnp.max(s, axis=0, keepdims=True))
                alpha = jnp.exp(m_old - m_new)
                p = jnp.exp(s - m_new)
                pv = lax.dot_general(vt_ref[kt, hs, :], p.astype(BF16), NN,
                                     preferred_element_type=F32)
                acc_ref[hs, :] = alpha * acc_ref[hs, :] + pv
                m_ref[h] = m_new
                l_ref[h] = alpha * l_ref[h] + jnp.sum(p, axis=0, keepdims=True)
        return carry

    m_ref[...] = jnp.full_like(m_ref, NEG_INF)
    l_ref[...] = jnp.zeros_like(l_ref)
    lax.fori_loop(0, nkt, att_body, 0)
    o_t = jnp.concatenate([acc_ref[h * DH:(h + 1) * DH, :] / l_ref[h] for h in range(HEADS)], axis=0)
    o_ref[...] = o_t.T


def _dsa(slab, cst, ik, kr, vt, b, s, n_sel):
    tq = min(DSA_TQ, s)
    tk = min(DSA_TK, s)
    nq = s // tq
    nkt = s // tk
    return pl.pallas_call(
        functools.partial(_dsa_kernel, tq=tq, tk=tk, n_sel=n_sel, idx_bits=int(math.log2(s))),
        grid=(b, nq),
        in_specs=[pl.BlockSpec((tq, IDX_HEADS * DH), lambda i, j: (i * nq + j, OFF_DSA_IQ // (IDX_HEADS * DH))),
                  pl.BlockSpec((tq, 128), lambda i, j: (i * nq + j, OFF_DSA_IKW // 128)),
                  pl.BlockSpec((tq, HW), lambda i, j: (i * nq + j, OFF_DSA_Q // HW)),
                  pl.BlockSpec((1, 16, tq), lambda i, j: (i, 0, j)),
                  pl.BlockSpec((s, DH), lambda i, j: (i, 0)),
                  pl.BlockSpec((HEADS, s, DH), lambda i, j: (0, i, 0)),
                  pl.BlockSpec((nkt, HW, tk), lambda i, j: (i, 0, 0))],
        out_specs=pl.BlockSpec((tq, HW), lambda i, j: (i * nq + j, 0)),
        out_shape=jax.ShapeDtypeStruct((b * s, HW), F32),
        scratch_shapes=[pltpu.VMEM((s, tq), I32), pltpu.VMEM((s, tq), F32), pltpu.VMEM((HW, tq), F32),
                        pltpu.VMEM((HEADS, tk, tq), F32), pltpu.VMEM((HEADS, 1, tq), F32),
                        pltpu.VMEM((HEADS, 1, tq), F32)],
        compiler_params=_cparams(2),
        name="dsa",
    )(slab, slab, slab, cst, ik, kr, vt)


def _merge_kernel(x_ref, oa_ref, ob_ref, oc_ref, od_ref, ga_ref, gb_ref, gc_ref, gd_ref,
                  wa_ref, wb_ref, wc_ref, wd_ref, wo_ref, o_ref):
    def branch(o_r, g_r, w_r):
        y = jnp.dot(o_r[...].astype(BF16), w_r[...], preferred_element_type=F32)
        return _sigmoid(g_r[...].astype(F32)) * y

    merged = (branch(oa_ref, ga_ref, wa_ref) + branch(ob_ref, gb_ref, wb_ref)
              + branch(oc_ref, gc_ref, wc_ref) + branch(od_ref, gd_ref, wd_ref))
    o_ref[...] = x_ref[...] + jnp.dot(merged.astype(BF16), wo_ref[...], preferred_element_type=F32)


def _merge(x2d, oa, ob, oc, od, gates, wa, wb, wc, wd, wo):
    t, d = x2d.shape
    tm = min(512, t)
    row = lambda w: pl.BlockSpec((tm, w), lambda i: (i, 0))
    gate = lambda n: pl.BlockSpec((tm, d), lambda i, o=n: (i, o))
    full = lambda shape: pl.BlockSpec(shape, lambda i: (0,) * len(shape))
    return pl.pallas_call(
        _merge_kernel,
        grid=(t // tm,),
        in_specs=[row(d), row(HW), row(HW), row(HW), row(HW), gate(0), gate(1), gate(2), gate(3),
                  full((HW, d)), full((HW, d)), full((HW, d)), full((HW, d)), full((d, d))],
        out_specs=row(d),
        out_shape=jax.ShapeDtypeStruct((t, d), F32),
        compiler_params=_cparams(1),
        name="merge",
    )(x2d, oa, ob, oc, od, gates, gates, gates, gates, wa, wb, wc, wd, wo)


def _mlp_kernel(x_ref, g_ref, wu_ref, wd_ref, fg_ref, o_ref, h_ref, acc_ref, *, final_norm):
    fi = pl.program_id(1)

    @pl.when(fi == 0)
    def _():
        x = x_ref[...]
        ms = jnp.mean(x * x, axis=-1, keepdims=True)
        h_ref[...] = (x * lax.rsqrt(ms + NORM_EPS) * g_ref[...]).astype(BF16)
        acc_ref[...] = jnp.zeros_like(acc_ref)

    u = jnp.dot(h_ref[...], wu_ref[...], preferred_element_type=F32)
    u = jnp.square(jnp.maximum(u, 0.0))
    acc_ref[...] += jnp.dot(u.astype(BF16), wd_ref[...], preferred_element_type=F32)

    @pl.when(fi == pl.num_programs(1) - 1)
    def _():
        y = x_ref[...] + acc_ref[...]
        if final_norm:
            ms = jnp.mean(y * y, axis=-1, keepdims=True)
            y = y * lax.rsqrt(ms + NORM_EPS) * fg_ref[...]
        o_ref[...] = y


def _mlp(x2d, g, wu, wd, fg, final_norm):
    t, d = x2d.shape
    f = wu.shape[1]
    tm = min(1024, t)
    tf = 512
    return pl.pallas_call(
        functools.partial(_mlp_kernel, final_norm=final_norm),
        grid=(t // tm, f // tf),
        in_specs=[pl.BlockSpec((tm, d), lambda i, j: (i, 0)),
                  pl.BlockSpec((1, d), lambda i, j: (0, 0)),
                  pl.BlockSpec((d, tf), lambda i, j: (0, j)),
                  pl.BlockSpec((tf, d), lambda i, j: (j, 0)),
                  pl.BlockSpec((1, d), lambda i, j: (0, 0))],
        out_specs=pl.BlockSpec((tm, d), lambda i, j: (i, 0)),
        out_shape=jax.ShapeDtypeStruct((t, d), F32),
        scratch_shapes=[pltpu.VMEM((tm, d), BF16), pltpu.VMEM((tm, d), F32)],
        compiler_params=_cparams(2),
        name="mlp",
    )(x2d, g, wu, wd, fg)


def _slab_weight(w_in, dtype=BF16):
    d = w_in.shape[0]
    z = lambda n: jnp.zeros((d, n), w_in.dtype)
    gdn, mla, rwkv, dsa = 0, 1032, 1448, 2472
    gate = 3824
    cols = [
        w_in[:, rwkv:rwkv + 1024],
        w_in[:, gdn:gdn + 768],
        w_in[:, gdn + 776:gdn + 1032],
        w_in[:, mla:mla + 256],
        w_in[:, mla + 256:mla + 384],
        z(64), w_in[:, mla + 384:mla + 416], z(32),
        w_in[:, dsa:dsa + 512],
        w_in[:, dsa + 584:dsa + 1352],
        w_in[:, gdn + 768:gdn + 776], z(120),
        w_in[:, dsa + 512:dsa + 584], z(56),
        w_in[:, gate:gate + 4096],
    ]
    return jnp.concatenate(cols, axis=1).astype(dtype)


def _mla_weights(w_uq, w_ukv):
    wq = w_uq.reshape(MLA_Q_LORA, HEADS, MLA_NOPE + MLA_ROPE)
    wq = jnp.pad(wq, ((0, 0), (0, 0), (0, MLA_QK - MLA_NOPE - MLA_ROPE))).reshape(MLA_Q_LORA, HEADS * MLA_QK)
    wkv = w_ukv.reshape(MLA_KV_LORA, HEADS, MLA_NOPE + DH)
    wk = jnp.pad(wkv[:, :, :MLA_NOPE], ((0, 0), (0, 0), (0, MLA_QK - MLA_NOPE))).reshape(MLA_KV_LORA, HEADS * MLA_QK)
    wvt = wkv[:, :, MLA_NOPE:].reshape(MLA_KV_LORA, HW).T
    return wq.astype(BF16), wk.astype(BF16), wvt.astype(BF16)


def _rope_tables(positions, rot_dims, offset, period):
    half = rot_dims // 2
    inv_freq = ROPE_THETA ** (-jnp.arange(0, rot_dims, 2, dtype=F32) / rot_dims)
    ang = positions.astype(F32)[..., None] * inv_freq
    cos, sin = jnp.cos(ang), jnp.sin(ang)
    b, s, _ = cos.shape
    lead = jnp.zeros((b, s, offset), F32)
    tail = jnp.zeros((b, s, period - offset - rot_dims), F32)
    zh = jnp.zeros((b, s, half), F32)
    cos_t = jnp.concatenate([lead + 1.0, cos, cos, tail + 1.0], axis=-1)
    sin_lo = jnp.concatenate([lead, -sin, zh, tail], axis=-1)
    sin_hi = jnp.concatenate([lead, zh, sin, tail], axis=-1)
    tab = jnp.concatenate([cos_t, sin_lo, sin_hi], axis=-1)
    return tab.reshape(b * s, 3 * period), cos, sin


def _layer(x2d, b, s, prm, tabs, final_g, is_last):
    (mix_g, w_slab, conv_w, gdn_pcol, gdn_prow, gdn_ng, mla_qg, mla_kvg, wq, wk, wv,
     rwkv_mu, rwkv_vec, ww2, wa2, wg2, wo_gdn, wo_mla, wo_rwkv, wo_dsa, w_o, mlp_g, w_up, w_down) = prm
    mla_tab, dsa_tab, dsa_cst = tabs
    t = b * s
    slab = _in_proj(x2d, mix_g, w_slab[:, :OFF_GATE], F32)
    gates = _in_proj(x2d, mix_g, w_slab[:, OFF_GATE:], BF16)

    ab = slab[:, OFF_GDN_AB:OFF_GDN_AB + 8].reshape(b, s // CHUNK, CHUNK, 8)
    abr = jnp.swapaxes(ab, 2, 3)
    o_gdn = _gdn(slab, abr, conv_w, gdn_pcol, gdn_prow, gdn_ng, b, s)
    q, k, v = _mla_prep(slab, mla_tab, mla_qg, mla_kvg, wq, wk, wv)
    o_mla = _flash(q, k, v, b, s)
    o_rwkv = _rwkv(slab, rwkv_mu, rwkv_vec, ww2, wa2, wg2, b, s)
    ik, kr, vt = _dsa_key_prep(slab, dsa_tab)
    o_dsa = _dsa(slab, dsa_cst, ik, kr, vt, b, s, min(IDX_TOPK_MAX, s // 4))

    x2d = _merge(x2d, o_gdn, o_mla, o_rwkv, o_dsa, gates, wo_gdn, wo_mla, wo_rwkv, wo_dsa, w_o)
    return _mlp(x2d, mlp_g, w_up, w_down, final_g, is_last)


def kernel(x, positions, mix_norm_g, w_in, gdn_conv_w, gdn_a_log, gdn_dt_bias, gdn_norm_g, mla_q_norm_g, mla_kv_norm_g, mla_w_uq, mla_w_ukv, rwkv_mu, rwkv_w0, rwkv_w_w2, rwkv_a0, rwkv_w_a2, rwkv_w_g2, rwkv_k_k, rwkv_k_a, rwkv_r_k, rwkv_ln_w, rwkv_ln_b, w_out_gdn, w_out_mla, w_out_rwkv, w_out_dsa, w_o, mlp_norm_g, w_up, w_down, final_norm_g):
    b, s, d = x.shape
    depth = w_in.shape[0]
    mla_tab, _, _ = _rope_tables(positions, MLA_ROPE, MLA_NOPE, MLA_QK)
    dsa_tab, cos8, sin8 = _rope_tables(positions, DH // 4, 0, 128 // 2)
    dsa_tab = jnp.concatenate([jnp.concatenate([dsa_tab[:, i * 64:(i + 1) * 64]] * 2, axis=1) for i in range(3)], axis=1)
    dsa_cst = jnp.swapaxes(jnp.concatenate([cos8, sin8], axis=-1), 1, 2)
    tabs = (mla_tab, dsa_tab, dsa_cst)
    row = lambda a: a.reshape(1, -1).astype(F32)
    pad8 = lambda rows: jnp.concatenate(rows + [jnp.zeros((8 - len(rows), rows[0].shape[1]), F32)], axis=0)
    x2d = x.reshape(b * s, d)
    final_g = row(final_norm_g)
    for l in range(depth):
        lane4 = lambda a: jnp.pad(a.astype(F32), (0, 124)).reshape(1, 128)
        gdn_pcol = pad8([lane4(gdn_a_log[l]), lane4(gdn_dt_bias[l])])
        gdn_prow = jnp.pad(jnp.stack([gdn_a_log[l], gdn_dt_bias[l]], axis=1).astype(F32), ((0, 4), (0, 126)))
        wq, wk, wv = _mla_weights(mla_w_uq[l], mla_w_ukv[l])
        rwkv_vec = pad8([row(rwkv_w0[l]), row(rwkv_a0[l]), row(rwkv_k_k[l]), row(rwkv_k_a[l]),
                         row(rwkv_ln_w[l]), row(rwkv_ln_b[l]), row(rwkv_r_k[l])])
        prm = (row(mix_norm_g[l]), _slab_weight(w_in[l]), gdn_conv_w[l].astype(F32), gdn_pcol, gdn_prow,
               row(jnp.tile(gdn_norm_g[l], HEADS)), row(mla_q_norm_g[l]), row(mla_kv_norm_g[l]), wq, wk, wv,
               row(rwkv_mu[l]), rwkv_vec, rwkv_w_w2[l].astype(F32), rwkv_w_a2[l].astype(F32),
               rwkv_w_g2[l].astype(F32),
               w_out_gdn[l].astype(BF16), w_out_mla[l].astype(BF16), w_out_rwkv[l].astype(BF16),
               w_out_dsa[l].astype(BF16), w_o[l].astype(BF16), row(mlp_norm_g[l]),
               w_up[l].astype(BF16), w_down[l].astype(BF16))
        x2d = _layer(x2d, b, s, prm, tabs, final_g, l == depth - 1)
    return x2d.reshape(b, s, d)
```

```python
import functools
import math

import numpy as np
import jax
import jax.numpy as jnp
from jax import lax
from jax.experimental import pallas as pl
from jax.experimental.pallas import tpu as pltpu

F32 = jnp.float32
BF16 = jnp.bfloat16
I32 = jnp.int32
I16 = jnp.int16

D_MODEL = 1024
DEPTH = 4
ROPE_THETA = 500000.0
NORM_EPS = 1e-6
NEG_INF = -1e30
HEADS = 4
DH = 64
HW = HEADS * DH
GDN_CONV = 4
CHUNK = 64
MLA_Q_LORA = 256
MLA_KV_LORA = 128
MLA_NOPE = 64
MLA_ROPE = 32
MLA_QK = 128
RWKV_GN_EPS = 64e-5
IDX_HEADS = 8
IDX_TOPK_MAX = 256
D_FF = 4 * D_MODEL
INT_MIN = -2 ** 31

VMEM_LIMIT_BYTES = 56 * 1024 * 1024

OFF_RWKV = 0
OFF_GDN_Q = 1024
OFF_GDN_K = 1280
OFF_GDN_V = 1536
OFF_GDN_Z = 1792
OFF_MLA_CQ = 2048
OFF_MLA_CKV = 2304
OFF_MLA_KR = 2432
OFF_DSA_IQ = 2560
OFF_DSA_Q = 3072
OFF_DSA_K = 3328
OFF_DSA_V = 3584
OFF_GDN_AB = 3840
OFF_DSA_IKW = 3968
OFF_GATE = 4096
N_SLAB = 8192

NN = (((1,), (0,)), ((), ()))
NT = (((1,), (1,)), ((), ()))
TN = (((0,), (0,)), ((), ()))
BNN = (((2,), (1,)), ((0,), (0,)))
BNT = (((2,), (2,)), ((0,), (0,)))
BTN = (((1,), (1,)), ((0,), (0,)))


def _cparams(n_axes):
    return pltpu.CompilerParams(dimension_semantics=("arbitrary",) * n_axes,
                                vmem_limit_bytes=VMEM_LIMIT_BYTES)


def _mm(a, b, dims=NN):
    return lax.dot_general(a.astype(BF16), b.astype(BF16), dims, preferred_element_type=F32)


def _mm_exact_lhs(m, x, dims=NN):
    mb = m.astype(BF16)
    x0 = x.astype(BF16)
    r1 = x - x0.astype(F32)
    x1 = r1.astype(BF16)
    x2 = (r1 - x1.astype(F32)).astype(BF16)
    d = lambda y: lax.dot_general(mb, y, dims, preferred_element_type=F32)
    return d(x0) + (d(x1) + d(x2))


def _mm_exact_rhs(x, m, dims=NN):
    mb = m.astype(BF16)
    x0 = x.astype(BF16)
    r1 = x - x0.astype(F32)
    x1 = r1.astype(BF16)
    x2 = (r1 - x1.astype(F32)).astype(BF16)
    d = lambda y: lax.dot_general(y, mb, dims, preferred_element_type=F32)
    return d(x0) + (d(x1) + d(x2))


def _iota2(shape, dim):
    return lax.broadcasted_iota(I32, shape, dim)


def _seg_ones(width, seg):
    sh = int(math.log2(seg))
    r = _iota2((width, width), 0) >> sh
    c = _iota2((width, width), 1) >> sh
    return jnp.where(r == c, 1.0, 0.0).astype(F32)


def _seg_sum(x, seg_ones):
    return _mm_exact_rhs(x, seg_ones)


def _softplus(x):
    return jnp.maximum(x, 0.0) + jnp.log1p(jnp.exp(-jnp.abs(x)))


def _sigmoid(x):
    return 1.0 / (1.0 + jnp.exp(-x))


def _silu(x):
    return x * _sigmoid(x)


def _tri_masks(n):
    r = _iota2((n, n), 0)
    c = _iota2((n, n), 1)
    return r, c


def _unit_lower_inverse(m, r, c, dims=NN):
    eye = jnp.where(r == c, 1.0, 0.0).astype(F32)
    same16 = (r >> 4) == (c >> 4)
    same32 = (r >> 5) == (c >> 5)
    md = jnp.where(same16, m, 0.0)
    t = eye - md
    p = md
    for _ in range(3):
        p = _mm(p, p, dims)
        t = t + _mm(t, p, dims)
    c32 = jnp.where(same32 & jnp.logical_not(same16), m, 0.0)
    t = t - _mm(_mm(t, c32, dims), t, dims)
    c64 = jnp.where(same32, 0.0, m)
    t = t - _mm(_mm(t, c64, dims), t, dims)
    return t


IN_PROJ_TN = 512


def _in_proj_kernel(x_ref, g_ref, w_ref, o_ref):
    x = x_ref[...]
    ms = jnp.mean(x * x, axis=-1, keepdims=True)
    h = (x * lax.rsqrt(ms + NORM_EPS) * g_ref[...]).astype(BF16)
    for n0 in range(0, o_ref.shape[1], IN_PROJ_TN):
        o_ref[:, n0:n0 + IN_PROJ_TN] = jnp.dot(
            h, w_ref[:, n0:n0 + IN_PROJ_TN], preferred_element_type=F32).astype(o_ref.dtype)


def _in_proj(x2d, g, w, out_dtype):
    t, d = x2d.shape
    n = w.shape[1]
    tm = min(512, t)
    return pl.pallas_call(
        _in_proj_kernel,
        grid=(t // tm,),
        in_specs=[pl.BlockSpec((tm, d), lambda i: (i, 0)),
                  pl.BlockSpec((1, d), lambda i: (0, 0)),
                  pl.BlockSpec((d, n), lambda i: (0, 0))],
        out_specs=pl.BlockSpec((tm, n), lambda i: (i, 0)),
        out_shape=jax.ShapeDtypeStruct((t, n), out_dtype),
        compiler_params=_cparams(1),
        name="in_proj",
    )(x2d, g, w)


def _to_groups(x, nchunk):
    parts = [x[:, h * DH:(h + 1) * DH].reshape(nchunk, 1, CHUNK, DH) for h in range(HEADS)]
    return jnp.concatenate(parts, axis=1).reshape(nchunk * HEADS, CHUNK, DH)


def _col_groups(x, lane0, nchunk):
    parts = [x[:, lane0 + h:lane0 + h + 1].reshape(nchunk, 1, CHUNK, 1) for h in range(HEADS)]
    return jnp.concatenate(parts, axis=1).reshape(nchunk * HEADS, CHUNK, 1)


def _gdn_kernel(q_ref, k_ref, v_ref, z_ref, ab_ref, abr_ref, cw_ref, pcol_ref, prow_ref, ng_ref,
                o_ref, xbuf_ref, state_ref, u_ref, w_ref, a_ref, qd_ref, kd_ref, egl_ref, oraw_ref,
                *, tb):
    nchunk = tb // CHUNK

    @pl.when(pl.program_id(1) == 0)
    def _():
        xbuf_ref[0:8, :] = jnp.zeros((8, 3 * HW), F32)
        state_ref[...] = jnp.zeros_like(state_ref)

    xbuf_ref[8:8 + tb, 0:HW] = q_ref[...]
    xbuf_ref[8:8 + tb, HW:2 * HW] = k_ref[...]
    xbuf_ref[8:8 + tb, 2 * HW:3 * HW] = v_ref[...]
    cw = cw_ref[...]
    y = xbuf_ref[8:8 + tb, :] * cw[3:4, :]
    y = y + xbuf_ref[7:7 + tb, :] * cw[2:3, :]
    y = y + xbuf_ref[6:6 + tb, :] * cw[1:2, :]
    y = y + xbuf_ref[5:5 + tb, :] * cw[0:1, :]
    xbuf_ref[0:8, :] = xbuf_ref[tb:tb + 8, :]
    y = _silu(y)
    seg = _seg_ones(HW, DH)
    q = y[:, 0:HW]
    k = y[:, HW:2 * HW]
    q = q * lax.rsqrt(_seg_sum(q * q, seg) + 1e-6) * (DH ** -0.5)
    k = k * lax.rsqrt(_seg_sum(k * k, seg) + 1e-6)
    v = y[:, 2 * HW:3 * HW]

    ab = ab_ref[...]
    pcol = pcol_ref[...]
    g_col = -jnp.exp(pcol[0:1, :]) * _softplus(ab + pcol[1:2, :])
    beta_col = _sigmoid(ab)

    r, c = _tri_masks(CHUNK)
    incl = c <= r
    strict = c < r
    l_incl = jnp.where(incl, 1.0, 0.0).astype(F32)

    rb = _iota2((tb, tb), 0)
    cb = _iota2((tb, tb), 1)
    l_blk = jnp.where(((rb >> 6) == (cb >> 6)) & (cb <= rb), 1.0, 0.0).astype(F32)
    gc_col = _mm_exact_lhs(l_blk, g_col)
    prow = prow_ref[...]
    abr = abr_ref[0].reshape(nchunk * 8, CHUNK)
    prow_t = jnp.concatenate([prow] * nchunk, axis=0)
    g_row = -jnp.exp(prow_t[:, 0:1]) * _softplus(abr + prow_t[:, 1:2])
    gc_row = _mm_exact_rhs(g_row, l_incl, NT).reshape(nchunk, 8, CHUNK)

    qg, kg, vg = _to_groups(q, nchunk), _to_groups(k, nchunk), _to_groups(v, nchunk)
    beta = _col_groups(beta_col, 4, nchunk)
    gcol = _col_groups(gc_col, 0, nchunk)
    grow = jnp.concatenate([gc_row[:, h:h + 1, :].reshape(nchunk, 1, 1, CHUNK) for h in range(HEADS)],
                           axis=1).reshape(nchunk * HEADS, 1, CHUNK)
    decay = jnp.where(incl, jnp.exp(jnp.where(incl, gcol - grow, 0.0)), 0.0)
    kb = kg * beta
    m = jnp.where(strict, _mm(kb, kg, BNT) * decay, 0.0)
    t_inv = _unit_lower_inverse(m, r, c, BNN)
    egc = jnp.exp(gcol)
    glast = gcol[:, CHUNK - 1:CHUNK, :]
    uw = _mm(t_inv, jnp.concatenate([vg * beta, kb * egc], axis=2), BNN)
    u_ref[...] = uw[:, :, 0:DH]
    w_ref[...] = uw[:, :, DH:2 * DH]
    a_ref[...] = _mm(qg, kg, BNT) * decay
    qd_ref[...] = qg * egc
    kd_ref[...] = kg * jnp.exp(glast - gcol)
    egl_ref[...] = jnp.broadcast_to(jnp.exp(glast), (nchunk * HEADS, 1, DH))

    def chunk_body(ci, carry):
        sl = pl.ds(pl.multiple_of(ci * HEADS, HEADS), HEADS)
        s = state_ref[...]
        v_new = u_ref[sl] - _mm(w_ref[sl], s, BNN)
        o = _mm(qd_ref[sl], s, BNN) + _mm(a_ref[sl], v_new, BNN)
        state_ref[...] = s * egl_ref[sl] + _mm(kd_ref[sl], v_new, BTN)
        oraw_ref[pl.ds(pl.multiple_of(ci * CHUNK, CHUNK), CHUNK), :] = jnp.concatenate(
            [o[h] for h in range(HEADS)], axis=1)
        return carry

    lax.fori_loop(0, nchunk, chunk_body, 0)
    o_all = oraw_ref[...]
    ms = _seg_sum(o_all * o_all, seg) * (1.0 / DH)
    o_ref[...] = o_all * lax.rsqrt(ms + NORM_EPS) * ng_ref[...] * _silu(z_ref[...])


GDN_TB = 512


def _gdn(slab, abr, conv_w, pcol, prow, norm_g, b, s):
    tb = min(GDN_TB, s)
    nb = s // tb
    nchunk = tb // CHUNK
    ng = nchunk * HEADS
    grp = lambda w: pltpu.VMEM((ng, CHUNK, w), F32)
    col = lambda off, w: pl.BlockSpec((tb, w), lambda i, j, o=off // w: (i * nb + j, o))
    full = lambda shape: pl.BlockSpec(shape, lambda i, j: (0,) * len(shape))
    return pl.pallas_call(
        functools.partial(_gdn_kernel, tb=tb),
        grid=(b, nb),
        in_specs=[col(OFF_GDN_Q, HW), col(OFF_GDN_K, HW), col(OFF_GDN_V, HW), col(OFF_GDN_Z, HW),
                  col(OFF_GDN_AB, 128),
                  pl.BlockSpec((1, nchunk, 8, CHUNK), lambda i, j: (i, j, 0, 0)),
                  full((GDN_CONV, 3 * HW)), full((8, 128)), full((8, 128)), full((1, HW))],
        out_specs=pl.BlockSpec((tb, HW), lambda i, j: (i * nb + j, 0)),
        out_shape=jax.ShapeDtypeStruct((b * s, HW), F32),
        scratch_shapes=[pltpu.VMEM((tb + 8, 3 * HW), F32), pltpu.VMEM((HEADS, DH, DH), F32),
                        grp(DH), grp(DH), grp(CHUNK), grp(DH), grp(DH),
                        pltpu.VMEM((ng, 1, DH), F32), pltpu.VMEM((tb, HW), F32)],
        compiler_params=_cparams(2),
        name="gdn",
    )(slab, slab, slab, slab, slab, abr, conv_w, pcol, prow, norm_g)


def _rwkv_kernel(p_ref, mu_ref, vec_ref, ww2_ref, wa2_ref, wg2_ref, o_ref,
                 xbuf_ref, state_ref, ta_ref, tpv_ref, rt_ref, qkv_ref, qb_ref, bc_ref, ktv_ref,
                 el_ref, oraw_ref, *, tb):
    nchunk = tb // CHUNK

    @pl.when(pl.program_id(1) == 0)
    def _():
        xbuf_ref[0:8, :] = jnp.zeros((8, 4 * HW), F32)
        state_ref[...] = jnp.zeros_like(state_ref)

    p = p_ref[...]
    xbuf_ref[8:8 + tb, :] = p
    prev = xbuf_ref[7:7 + tb, :]
    xbuf_ref[0:8, :] = xbuf_ref[tb:tb + 8, :]
    pm = p + (prev - p) * mu_ref[...]
    vec = vec_ref[...]
    w0, a0, k_k, k_a = vec[0:1, :], vec[1:2, :], vec[2:3, :], vec[3:4, :]
    rr = pm[:, 0:HW]
    k = pm[:, HW:2 * HW]
    v = pm[:, 2 * HW:3 * HW]
    xw = pm[:, 768:832]
    xa = pm[:, 832:896]
    xg = pm[:, 896:1024]
    w_log = -_softplus(-(w0 + _mm(jnp.tanh(xw), ww2_ref[...]))) - 0.5
    lw = -jnp.exp(w_log)
    a = _sigmoid(a0 + _mm(xa, wa2_ref[...]))
    g = _mm(_sigmoid(xg), wg2_ref[...])
    seg = _seg_ones(HW, DH)
    kk = k * k_k
    kk = kk * lax.rsqrt(_seg_sum(kk * kk, seg) + 1e-6)
    k2 = k * (1.0 + (a - 1.0) * k_a)
    kka = kk * a

    r_i, c_i = _tri_masks(CHUNK)
    incl = c_i <= r_i
    strict = c_i < r_i
    rb = _iota2((tb, tb), 0)
    cb = _iota2((tb, tb), 1)
    l_blk = jnp.where(((rb >> 6) == (cb >> 6)) & (cb <= rb), 1.0, 0.0).astype(F32)
    lwc_all = _mm_exact_lhs(l_blk, lw)

    rg, kg, vg = _to_groups(rr, nchunk), _to_groups(k2, nchunk), _to_groups(v, nchunk)
    kkg, kkag = _to_groups(kk, nchunk), _to_groups(kka, nchunk)
    lwg, lwc = _to_groups(lw, nchunk), _to_groups(lwc_all, nchunk)
    lwl = lwc[:, CHUNK - 1:CHUNK, :]
    e_neg = jnp.exp(-lwc)
    at = kkg * jnp.exp(lwc - lwg)
    rt = rg * jnp.exp(lwc)
    y2 = jnp.concatenate([at, rt], axis=1)
    x2 = jnp.concatenate([kkag * e_neg, kg * e_neg], axis=1)
    gram = _mm(y2, x2, BNT)
    n_m = jnp.where(strict, gram[:, 0:CHUNK, 0:CHUNK], 0.0)
    p_m = jnp.where(strict, gram[:, 0:CHUNK, CHUNK:], 0.0)
    qb_m = jnp.where(incl, gram[:, CHUNK:, 0:CHUNK], 0.0)
    qk_m = jnp.where(incl, gram[:, CHUNK:, CHUNK:], 0.0)
    t_inv = _unit_lower_inverse(n_m, r_i, c_i, BNN)
    ta_ref[...] = _mm(t_inv, at, BNN)
    tpv_ref[...] = _mm(t_inv, _mm(p_m, vg, BNN), BNN)
    rt_ref[...] = rt
    qkv_ref[...] = _mm(qk_m, vg, BNN)
    qb_ref[...] = qb_m
    e_rel = jnp.exp(lwl - lwc)
    bc_ref[...] = kkag * e_rel
    ktv_ref[...] = _mm(vg, kg * e_rel, BTN)
    el_ref[...] = jnp.exp(lwl)

    def chunk_body(ci, carry):
        sl = pl.ds(pl.multiple_of(ci * HEADS, HEADS), HEADS)
        zt = state_ref[...]
        u = _mm(ta_ref[sl], zt, BNT) + tpv_ref[sl]
        o = _mm(rt_ref[sl], zt, BNT) + qkv_ref[sl] - _mm(qb_ref[sl], u, BNN)
        state_ref[...] = zt * el_ref[sl] + ktv_ref[sl] - _mm(u, bc_ref[sl], BTN)
        oraw_ref[pl.ds(pl.multiple_of(ci * CHUNK, CHUNK), CHUNK), :] = jnp.concatenate(
            [o[h] for h in range(HEADS)], axis=1)
        return carry

    lax.fori_loop(0, nchunk, chunk_body, 0)
    ln_w, ln_b, r_k = vec[4:5, :], vec[5:6, :], vec[6:7, :]
    o_all = oraw_ref[...]
    mean = _seg_sum(o_all, seg) * (1.0 / DH)
    cen = o_all - mean
    var = _seg_sum(cen * cen, seg) * (1.0 / DH)
    o_n = cen * lax.rsqrt(var + RWKV_GN_EPS) * ln_w + ln_b
    bonus = _seg_sum(rr * k2 * r_k, seg) * v
    o_ref[...] = (o_n + bonus) * g


RWKV_TB = 512


def _rwkv(slab, mu, vec, ww2, wa2, wg2, b, s):
    tb = min(RWKV_TB, s)
    nb = s // tb
    ng = (tb // CHUNK) * HEADS
    grp = lambda: pltpu.VMEM((ng, CHUNK, DH), F32)
    full = lambda shape: pl.BlockSpec(shape, lambda i, j: (0,) * len(shape))
    return pl.pallas_call(
        functools.partial(_rwkv_kernel, tb=tb),
        grid=(b, nb),
        in_specs=[pl.BlockSpec((tb, 4 * HW), lambda i, j: (i * nb + j, OFF_RWKV // (4 * HW))),
                  full((1, 4 * HW)), full((8, HW)), full((64, HW)), full((64, HW)), full((128, HW))],
        out_specs=pl.BlockSpec((tb, HW), lambda i, j: (i * nb + j, 0)),
        out_shape=jax.ShapeDtypeStruct((b * s, HW), F32),
        scratch_shapes=[pltpu.VMEM((tb + 8, 4 * HW), F32), pltpu.VMEM((HEADS, DH, DH), F32),
                        grp(), grp(), grp(), grp(), grp(), grp(), grp(),
                        pltpu.VMEM((ng, 1, DH), F32), pltpu.VMEM((tb, HW), F32)],
        compiler_params=_cparams(2),
        name="rwkv7",
    )(slab, mu, vec, ww2, wa2, wg2)


def _rope(x, cos_t, sin_lo, sin_hi, half):
    w = x.shape[-1]
    return x * cos_t + pltpu.roll(x, w - half, 1) * sin_lo + pltpu.roll(x, half, 1) * sin_hi


def _tile_lanes(t, n):
    return t if n == 1 else jnp.concatenate([t] * n, axis=1)


def _mla_prep_kernel(cq_ref, ckv_ref, kr_ref, tab_ref, qg_ref, kvg_ref, wq_ref, wk_ref, wvt_ref,
                     q_ref, k_ref, vt_ref):
    cq = cq_ref[...]
    hq = cq * lax.rsqrt(jnp.mean(cq * cq, axis=-1, keepdims=True) + NORM_EPS) * qg_ref[...]
    ckv = ckv_ref[...]
    hkv = ckv * lax.rsqrt(jnp.mean(ckv * ckv, axis=-1, keepdims=True) + NORM_EPS) * kvg_ref[...]
    tab = tab_ref[...]
    cos_t, sin_lo, sin_hi = tab[:, 0:128], tab[:, 128:256], tab[:, 256:384]
    q = _mm(hq, wq_ref[...])
    q = _rope(q, _tile_lanes(cos_t, HEADS), _tile_lanes(sin_lo, HEADS), _tile_lanes(sin_hi, HEADS),
              MLA_ROPE // 2)
    q_ref[...] = (q * ((MLA_NOPE + MLA_ROPE) ** -0.5)).astype(BF16)
    kr = _rope(kr_ref[...], cos_t, sin_lo, sin_hi, MLA_ROPE // 2)
    k = _mm(hkv, wk_ref[...]) + _tile_lanes(kr, HEADS)
    k_ref[...] = k.astype(BF16)
    vt_ref[...] = lax.dot_general(wvt_ref[...], hkv.astype(BF16), NT,
                                  preferred_element_type=F32).astype(BF16)


def _mla_prep(slab, tab, qg, kvg, wq, wk, wvt):
    t = slab.shape[0]
    tm = min(512, t)
    full = lambda shape: pl.BlockSpec(shape, lambda i: (0,) * len(shape))
    return pl.pallas_call(
        _mla_prep_kernel,
        grid=(t // tm,),
        in_specs=[pl.BlockSpec((tm, 256), lambda i: (i, OFF_MLA_CQ // 256)),
                  pl.BlockSpec((tm, 128), lambda i: (i, OFF_MLA_CKV // 128)),
                  pl.BlockSpec((tm, 128), lambda i: (i, OFF_MLA_KR // 128)),
                  pl.BlockSpec((tm, 384), lambda i: (i, 0)),
                  full((1, 256)), full((1, 128)), full((256, 512)), full((128, 512)), full((256, 128))],
        out_specs=[pl.BlockSpec((tm, 512), lambda i: (i, 0)), pl.BlockSpec((tm, 512), lambda i: (i, 0)),
                   pl.BlockSpec((256, tm), lambda i: (0, i))],
        out_shape=[jax.ShapeDtypeStruct((t, 512), BF16), jax.ShapeDtypeStruct((t, 512), BF16),
                   jax.ShapeDtypeStruct((256, t), BF16)],
        compiler_params=_cparams(1),
        name="mla_prep",
    )(slab, slab, slab, tab, qg, kvg, wq, wk, wvt)


def _flash_kernel(q_ref, k_ref, vt_ref, o_ref, m_ref, l_ref, acc_ref, *, tq, tk):
    qi = pl.program_id(1)
    ki = pl.program_id(2)

    @pl.when(ki == 0)
    def _():
        m_ref[...] = jnp.full_like(m_ref, NEG_INF)
        l_ref[...] = jnp.zeros_like(l_ref)
        acc_ref[...] = jnp.zeros_like(acc_ref)

    def step(on_diagonal):
        if on_diagonal:
            causal = _iota2((tk, tq), 0) <= _iota2((tk, tq), 1)
        vt = vt_ref[...]
        scores = [lax.dot_general(k_ref[:, h * MLA_QK:(h + 1) * MLA_QK], q_ref[:, h * MLA_QK:(h + 1) * MLA_QK],
                                  NT, preferred_element_type=F32) for h in range(HEADS)]
        for h in range(HEADS):
            s = scores[h]
            if on_diagonal:
                s = jnp.where(causal, s, NEG_INF)
            m_old = m_ref[h]
            m_new = jnp.maximum(m_old, jnp.max(s, axis=0, keepdims=True))
            alpha = jnp.exp(m_old - m_new)
            p = jnp.exp(s - m_new)
            l_ref[h] = alpha * l_ref[h] + jnp.sum(p, axis=0, keepdims=True)
            pv = jnp.dot(vt[h * DH:(h + 1) * DH, :], p.astype(BF16), preferred_element_type=F32)
            acc_ref[h * DH:(h + 1) * DH, :] = alpha * acc_ref[h * DH:(h + 1) * DH, :] + pv
            m_ref[h] = m_new

    pl.when(ki < qi)(lambda: step(False))
    pl.when(ki == qi)(lambda: step(True))

    @pl.when(ki == pl.num_programs(2) - 1)
    def _():
        o_t = jnp.concatenate([acc_ref[h * DH:(h + 1) * DH, :] / l_ref[h] for h in range(HEADS)], axis=0)
        o_ref[...] = o_t.T


def _flash(q, k, vt, b, s):
    tq = min(512, s)
    tk = tq
    nq = s // tq
    nk = s // tk
    return pl.pallas_call(
        functools.partial(_flash_kernel, tq=tq, tk=tk),
        grid=(b, nq, nk),
        in_specs=[pl.BlockSpec((tq, 512), lambda i, j, kk: (i * nq + j, 0)),
                  pl.BlockSpec((tk, 512), lambda i, j, kk: (i * nk + jnp.minimum(kk, j), 0)),
                  pl.BlockSpec((256, tk), lambda i, j, kk: (0, i * nk + jnp.minimum(kk, j)))],
        out_specs=pl.BlockSpec((tq, 256), lambda i, j, kk: (i * nq + j, 0)),
        out_shape=jax.ShapeDtypeStruct((b * s, 256), F32),
        scratch_shapes=[pltpu.VMEM((HEADS, 1, tq), F32), pltpu.VMEM((HEADS, 1, tq), F32),
                        pltpu.VMEM((HW, tq), F32)],
        compiler_params=_cparams(3),
        name="mla_flash",
    )(q, k, vt)


DSA_TQ = 256
DSA_TK = 512


def _dsa_key_prep_kernel(ikw_ref, k_ref, v_ref, tab_ref, ik_ref, ko_ref, vt_ref):
    tab = tab_ref[...]
    cos_t, sin_lo, sin_hi = tab[:, 0:128], tab[:, 128:256], tab[:, 256:384]
    half = DH // 8
    ik = _rope(ikw_ref[...], cos_t, sin_lo, sin_hi, half)
    ik_ref[...] = ik[:, 0:DH].astype(BF16)
    kr = _rope(k_ref[...], _tile_lanes(cos_t, 2), _tile_lanes(sin_lo, 2), _tile_lanes(sin_hi, 2), half)
    for h in range(HEADS):
        ko_ref[h] = kr[:, h * DH:(h + 1) * DH].astype(BF16)
    vt_ref[0] = v_ref[...].T.astype(BF16)


def _dsa_key_prep(slab, tab):
    t = slab.shape[0]
    tm = min(DSA_TK, t)
    return pl.pallas_call(
        _dsa_key_prep_kernel,
        grid=(t // tm,),
        in_specs=[pl.BlockSpec((tm, 128), lambda i: (i, OFF_DSA_IKW // 128)),
                  pl.BlockSpec((tm, 256), lambda i: (i, OFF_DSA_K // 256)),
                  pl.BlockSpec((tm, 256), lambda i: (i, OFF_DSA_V // 256)),
                  pl.BlockSpec((tm, 384), lambda i: (i, 0))],
        out_specs=[pl.BlockSpec((tm, DH), lambda i: (i, 0)),
                   pl.BlockSpec((HEADS, tm, DH), lambda i: (0, i, 0)),
                   pl.BlockSpec((1, HW, tm), lambda i: (i, 0, 0))],
        out_shape=[jax.ShapeDtypeStruct((t, DH), BF16), jax.ShapeDtypeStruct((HEADS, t, DH), BF16),
                   jax.ShapeDtypeStruct((t // tm, HW, tm), BF16)],
        compiler_params=_cparams(1),
        name="dsa_key_prep",
    )(slab, slab, slab, tab)


def _rope_rows(x, cos8, sin8, nheads):
    parts = []
    for h in range(nheads):
        x1 = x[h * DH:h * DH + 8, :]
        x2 = x[h * DH + 8:h * DH + 16, :]
        parts.append(x1 * cos8 - x2 * sin8)
        parts.append(x2 * cos8 + x1 * sin8)
        parts.append(x[h * DH + 16:(h + 1) * DH, :])
    return jnp.concatenate(parts, axis=0)


def _dsa_kernel(iq_ref, ikw_ref, q_ref, cst_ref, ik_ref, k_ref, vt_ref, kidx_ref, o_ref,
                hi_ref, lo_ref, lom_ref, bias_ref, acc_ref, s_ref, m_ref, l_ref,
                *, tq, tk, n_sel, idx_bits):
    qi = pl.program_id(1)
    nkt = (qi * tq + tq - 1) // tk + 1
    cs = cst_ref[0]
    cos8, sin8 = cs[0:8, :], cs[8:16, :]
    iqt = _rope_rows(iq_ref[...].T, cos8, sin8, IDX_HEADS).astype(BF16)
    iw = ikw_ref[...].T[DH:DH + 8, :] * (IDX_HEADS ** -0.5 * DH ** -0.5)
    q_pos = qi * tq + _iota2((tk, tq), 1)

    def score_body(kt, carry):
        row0 = pl.multiple_of(kt * tk, tk)
        ik = ik_ref[pl.ds(row0, tk), :]
        sc = jnp.zeros((tk, tq), F32)
        for h in range(IDX_HEADS):
            lg = lax.dot_general(ik, iqt[h * DH:(h + 1) * DH, :], NN, preferred_element_type=F32)
            sc = sc + jnp.maximum(lg, 0.0) * iw[h:h + 1, :]
        bits = lax.bitcast_convert_type(sc, I32)
        key = bits ^ ((bits >> 31) & 0x7FFFFFFF)
        key = jnp.where(sc == 0.0, 0, key)
        k_pos = row0 + _iota2((tk, tq), 0)
        key = jnp.where(k_pos <= q_pos, key, INT_MIN)
        hi_ref[pl.ds(row0, tk), :] = (key >> 16).astype(I16)
        lo_ref[pl.ds(row0, tk), :] = ((key & 0xFFFF) - 32768).astype(I16)
        return carry

    lax.fori_loop(0, nkt, score_body, 0)

    one16 = jnp.ones((tk, tq), I16)
    zero16 = jnp.zeros((tk, tq), I16)

    def count(pred_fn):
        def body(kt, acc):
            sl = pl.ds(pl.multiple_of(kt * tk, tk), tk)
            ones = jnp.where(pred_fn(hi_ref[sl, :], lo_ref[sl, :], lom_ref[sl, :], kidx_ref[sl, :]),
                             one16, zero16).reshape(tk // 64, 64, tq)
            for g in range(tk // 64):
                acc = acc + ones[g]
            return acc
        part = lax.fori_loop(0, nkt, body, jnp.zeros((64, tq), I16))
        return jnp.sum(part.astype(I32).astype(F32), axis=0, keepdims=True)

    def bisect16(pred_ge):
        def body(it, thr):
            bit = jnp.maximum(15 - it, 0)
            cand = jnp.where(it == 0, jnp.zeros_like(thr), thr | jnp.left_shift(jnp.int32(1), bit))
            c16 = cand.astype(I16)
            cnt = count(lambda hi, lo, lom, idx: pred_ge(hi, lo, lom, c16))
            return jnp.where(cnt >= float(n_sel), cand, thr)
        return lax.fori_loop(0, 16, body, jnp.full((1, tq), -32768, I32))

    t_hi = bisect16(lambda hi, lo, lom, c: hi >= c)
    t_hi16 = t_hi.astype(I16)

    def mask_lo_body(kt, carry):
        sl = pl.ds(pl.multiple_of(kt * tk, tk), tk)
        hi = hi_ref[sl, :]
        lom_ref[sl, :] = jnp.where(hi > t_hi16, jnp.int16(32767),
                                   jnp.where(hi == t_hi16, lo_ref[sl, :], jnp.int16(-32768)))
        return carry

    lax.fori_loop(0, nkt, mask_lo_body, 0)
    t_lo16 = bisect16(lambda hi, lo, lom, c: lom >= c).astype(I16)

    def is_gt(hi, lo):
        return (hi > t_hi16) | ((hi == t_hi16) & (lo > t_lo16))

    def is_eq(hi, lo):
        return (hi == t_hi16) & (lo == t_lo16)

    cnt_gt = count(lambda hi, lo, lom, idx: is_gt(hi, lo))
    cnt_eq = count(lambda hi, lo, lom, idx: is_eq(hi, lo))
    need = float(n_sel) - cnt_gt

    def idx_body(it, j):
        cand = j + jnp.left_shift(jnp.int32(1), idx_bits - 1 - it)
        c16 = cand.astype(I16)
        cnt = count(lambda hi, lo, lom, idx: is_eq(hi, lo) & (idx < c16))
        return jnp.where(cnt < need, cand, j)

    excess = jnp.max(jnp.where(cnt_eq > need, 1.0, 0.0)) > 0.0
    j_cut16 = lax.cond(excess,
                       lambda: lax.fori_loop(0, idx_bits, idx_body, jnp.zeros((1, tq), I32)),
                       lambda: jnp.full((1, tq), 32767, I32)).astype(I16)

    def bias_body(kt, carry):
        sl = pl.ds(pl.multiple_of(kt * tk, tk), tk)
        hi, lo = hi_ref[sl, :], lo_ref[sl, :]
        sel = (is_gt(hi, lo) | (is_eq(hi, lo) & (kidx_ref[sl, :] <= j_cut16))) & (hi != jnp.int16(-32768))
        bias_ref[sl, :] = jnp.where(sel, jnp.zeros((tk, tq), BF16),
                                    jnp.full((tk, tq), NEG_INF, BF16)).astype(F32)
        return carry

    lax.fori_loop(0, nkt, bias_body, 0)

    qt = (_rope_rows(q_ref[...].T, cos8, sin8, HEADS) * (DH ** -0.5)).astype(BF16)
    acc_ref[...] = jnp.zeros_like(acc_ref)

    def att_body(kt, carry):
        row0 = pl.multiple_of(kt * tk, tk)
        for h in range(HEADS):
            s_ref[h] = lax.dot_general(k_ref[h, pl.ds(row0, tk), :], qt[h * DH:(h + 1) * DH, :], NN,
                                       preferred_element_type=F32)

        @pl.when(kt >= 0)
        def _():
            bias = bias_ref[pl.ds(row0, tk), :]
            for h in range(HEADS):
                hs = slice(h * DH, (h + 1) * DH)
                s = s_ref[h] + bias
                m_old = m_ref[h]
                m_new = jnp.maximum(m_old, jnp.max(s, axis=0, keepdims=True))
                alpha = jnp.exp(m_old - m_new)
                p = jnp.exp(s - m_new)
                pv = lax.dot_general(vt_ref[kt, hs, :], p.astype(BF16), NN,
                                     preferred_element_type=F32)
                acc_ref[hs, :] = alpha * acc_ref[hs, :] + pv
                m_ref[h] = m_new
                l_ref[h] = alpha * l_ref[h] + jnp.sum(p, axis=0, keepdims=True)
        return carry

    m_ref[...] = jnp.full_like(m_ref, NEG_INF)
    l_ref[...] = jnp.zeros_like(l_ref)
    lax.fori_loop(0, nkt, att_body, 0)
    o_t = jnp.concatenate([acc_ref[h * DH:(h + 1) * DH, :] / l_ref[h] for h in range(HEADS)], axis=0)
    o_ref[...] = o_t.T


def _dsa(slab, cst, ik, kr, vt, b, s, n_sel):
    tq = min(DSA_TQ, s)
    tk = min(DSA_TK, s)
    nq = s // tq
    nkt = s // tk
    return pl.pallas_call(
        functools.partial(_dsa_kernel, tq=tq, tk=tk, n_sel=n_sel, idx_bits=int(math.log2(s))),
        grid=(b, nq),
        in_specs=[pl.BlockSpec((tq, IDX_HEADS * DH), lambda i, j: (i * nq + j, OFF_DSA_IQ // (IDX_HEADS * DH))),
                  pl.BlockSpec((tq, 128), lambda i, j: (i * nq + j, OFF_DSA_IKW // 128)),
                  pl.BlockSpec((tq, HW), lambda i, j: (i * nq + j, OFF_DSA_Q // HW)),
                  pl.BlockSpec((1, 16, tq), lambda i, j: (i, 0, j)),
                  pl.BlockSpec((s, DH), lambda i, j: (i, 0)),
                  pl.BlockSpec((HEADS, s, DH), lambda i, j: (0, i, 0)),
                  pl.BlockSpec((nkt, HW, tk), lambda i, j: (i, 0, 0)),
                  pl.BlockSpec((s, tq), lambda i, j: (0, 0))],
        out_specs=pl.BlockSpec((tq, HW), lambda i, j: (i * nq + j, 0)),
        out_shape=jax.ShapeDtypeStruct((b * s, HW), F32),
        scratch_shapes=[pltpu.VMEM((s, tq), I16), pltpu.VMEM((s, tq), I16), pltpu.VMEM((s, tq), I16),
                        pltpu.VMEM((s, tq), F32), pltpu.VMEM((HW, tq), F32),
                        pltpu.VMEM((HEADS, tk, tq), F32), pltpu.VMEM((HEADS, 1, tq), F32),
                        pltpu.VMEM((HEADS, 1, tq), F32)],
        compiler_params=_cparams(2),
        name="dsa",
    )(slab, slab, slab, cst, ik, kr, vt,
      jnp.broadcast_to(jnp.arange(s, dtype=I16)[:, None], (s, tq)))


def _merge_kernel(x_ref, oa_ref, ob_ref, oc_ref, od_ref, ga_ref, gb_ref, gc_ref, gd_ref,
                  wa_ref, wb_ref, wc_ref, wd_ref, wo_ref, o_ref):
    def branch(o_r, g_r, w_r):
        y = jnp.dot(o_r[...].astype(BF16), w_r[...], preferred_element_type=F32)
        return _sigmoid(g_r[...].astype(F32)) * y

    merged = (branch(oa_ref, ga_ref, wa_ref) + branch(ob_ref, gb_ref, wb_ref)
              + branch(oc_ref, gc_ref, wc_ref) + branch(od_ref, gd_ref, wd_ref))
    o_ref[...] = x_ref[...] + jnp.dot(merged.astype(BF16), wo_ref[...], preferred_element_type=F32)


def _merge(x2d, oa, ob, oc, od, gates, wa, wb, wc, wd, wo):
    t, d = x2d.shape
    tm = min(512, t)
    row = lambda w: pl.BlockSpec((tm, w), lambda i: (i, 0))
    gate = lambda n: pl.BlockSpec((tm, d), lambda i, o=n: (i, o))
    full = lambda shape: pl.BlockSpec(shape, lambda i: (0,) * len(shape))
    return pl.pallas_call(
        _merge_kernel,
        grid=(t // tm,),
        in_specs=[row(d), row(HW), row(HW), row(HW), row(HW), gate(0), gate(1), gate(2), gate(3),
                  full((HW, d)), full((HW, d)), full((HW, d)), full((HW, d)), full((d, d))],
        out_specs=row(d),
        out_shape=jax.ShapeDtypeStruct((t, d), F32),
        compiler_params=_cparams(1),
        name="merge",
    )(x2d, oa, ob, oc, od, gates, gates, gates, gates, wa, wb, wc, wd, wo)


def _mlp_kernel(x_ref, g_ref, wu_ref, wd_ref, fg_ref, o_ref, h_ref, acc_ref, *, final_norm):
    fi = pl.program_id(1)

    @pl.when(fi == 0)
    def _():
        x = x_ref[...]
        ms = jnp.mean(x * x, axis=-1, keepdims=True)
        h_ref[...] = (x * lax.rsqrt(ms + NORM_EPS) * g_ref[...]).astype(BF16)
        acc_ref[...] = jnp.zeros_like(acc_ref)

    u = jnp.dot(h_ref[...], wu_ref[...], preferred_element_type=F32)
    u = jnp.square(jnp.maximum(u, 0.0))
    acc_ref[...] += jnp.dot(u.astype(BF16), wd_ref[...], preferred_element_type=F32)

    @pl.when(fi == pl.num_programs(1) - 1)
    def _():
        y = x_ref[...] + acc_ref[...]
        if final_norm:
            ms = jnp.mean(y * y, axis=-1, keepdims=True)
            y = y * lax.rsqrt(ms + NORM_EPS) * fg_ref[...]
        o_ref[...] = y


def _mlp(x2d, g, wu, wd, fg, final_norm):
    t, d = x2d.shape
    f = wu.shape[1]
    tm = min(1024, t)
    tf = 512
    return pl.pallas_call(
        functools.partial(_mlp_kernel, final_norm=final_norm),
        grid=(t // tm, f // tf),
        in_specs=[pl.BlockSpec((tm, d), lambda i, j: (i, 0)),
                  pl.BlockSpec((1, d), lambda i, j: (0, 0)),
                  pl.BlockSpec((d, tf), lambda i, j: (0, j)),
                  pl.BlockSpec((tf, d), lambda i, j: (j, 0)),
                  pl.BlockSpec((1, d), lambda i, j: (0, 0))],
        out_specs=pl.BlockSpec((tm, d), lambda i, j: (i, 0)),
        out_shape=jax.ShapeDtypeStruct((t, d), F32),
        scratch_shapes=[pltpu.VMEM((tm, d), BF16), pltpu.VMEM((tm, d), F32)],
        compiler_params=_cparams(2),
        name="mlp",
    )(x2d, g, wu, wd, fg)


def _slab_weight(w_in, dtype=BF16):
    d = w_in.shape[0]
    z = lambda n: jnp.zeros((d, n), w_in.dtype)
    gdn, mla, rwkv, dsa = 0, 1032, 1448, 2472
    gate = 3824
    cols = [
        w_in[:, rwkv:rwkv + 1024],
        w_in[:, gdn:gdn + 768],
        w_in[:, gdn + 776:gdn + 1032],
        w_in[:, mla:mla + 256],
        w_in[:, mla + 256:mla + 384],
        z(64), w_in[:, mla + 384:mla + 416], z(32),
        w_in[:, dsa:dsa + 512],
        w_in[:, dsa + 584:dsa + 1352],
        w_in[:, gdn + 768:gdn + 776], z(120),
        w_in[:, dsa + 512:dsa + 584], z(56),
        w_in[:, gate:gate + 4096],
    ]
    return jnp.concatenate(cols, axis=1).astype(dtype)


def _mla_weights(w_uq, w_ukv):
    wq = w_uq.reshape(MLA_Q_LORA, HEADS, MLA_NOPE + MLA_ROPE)
    wq = jnp.pad(wq, ((0, 0), (0, 0), (0, MLA_QK - MLA_NOPE - MLA_ROPE))).reshape(MLA_Q_LORA, HEADS * MLA_QK)
    wkv = w_ukv.reshape(MLA_KV_LORA, HEADS, MLA_NOPE + DH)
    wk = jnp.pad(wkv[:, :, :MLA_NOPE], ((0, 0), (0, 0), (0, MLA_QK - MLA_NOPE))).reshape(MLA_KV_LORA, HEADS * MLA_QK)
    wvt = wkv[:, :, MLA_NOPE:].reshape(MLA_KV_LORA, HW).T
    return wq.astype(BF16), wk.astype(BF16), wvt.astype(BF16)


def _rope_tables(positions, rot_dims, offset, period):
    half = rot_dims // 2
    inv_freq = ROPE_THETA ** (-jnp.arange(0, rot_dims, 2, dtype=F32) / rot_dims)
    ang = positions.astype(F32)[..., None] * inv_freq
    cos, sin = jnp.cos(ang), jnp.sin(ang)
    b, s, _ = cos.shape
    lead = jnp.zeros((b, s, offset), F32)
    tail = jnp.zeros((b, s, period - offset - rot_dims), F32)
    zh = jnp.zeros((b, s, half), F32)
    cos_t = jnp.concatenate([lead + 1.0, cos, cos, tail + 1.0], axis=-1)
    sin_lo = jnp.concatenate([lead, -sin, zh, tail], axis=-1)
    sin_hi = jnp.concatenate([lead, zh, sin, tail], axis=-1)
    tab = jnp.concatenate([cos_t, sin_lo, sin_hi], axis=-1)
    return tab.reshape(b * s, 3 * period), cos, sin


def _layer(x2d, b, s, prm, tabs, final_g, is_last):
    (mix_g, w_slab, conv_w, gdn_pcol, gdn_prow, gdn_ng, mla_qg, mla_kvg, wq, wk, wv,
     rwkv_mu, rwkv_vec, ww2, wa2, wg2, wo_gdn, wo_mla, wo_rwkv, wo_dsa, w_o, mlp_g, w_up, w_down) = prm
    mla_tab, dsa_tab, dsa_cst = tabs
    t = b * s
    slab = _in_proj(x2d, mix_g, w_slab[:, :OFF_GATE], F32)
    gates = _in_proj(x2d, mix_g, w_slab[:, OFF_GATE:], BF16)

    ab = slab[:, OFF_GDN_AB:OFF_GDN_AB + 8].reshape(b, s // CHUNK, CHUNK, 8)
    abr = jnp.swapaxes(ab, 2, 3)
    o_gdn = _gdn(slab, abr, conv_w, gdn_pcol, gdn_prow, gdn_ng, b, s)
    q, k, v = _mla_prep(slab, mla_tab, mla_qg, mla_kvg, wq, wk, wv)
    o_mla = _flash(q, k, v, b, s)
    o_rwkv = _rwkv(slab, rwkv_mu, rwkv_vec, ww2, wa2, wg2, b, s)
    ik, kr, vt = _dsa_key_prep(slab, dsa_tab)
    o_dsa = _dsa(slab, dsa_cst, ik, kr, vt, b, s, min(IDX_TOPK_MAX, s // 4))

    x2d = _merge(x2d, o_gdn, o_mla, o_rwkv, o_dsa, gates, wo_gdn, wo_mla, wo_rwkv, wo_dsa, w_o)
    return _mlp(x2d, mlp_g, w_up, w_down, final_g, is_last)


def kernel(x, positions, mix_norm_g, w_in, gdn_conv_w, gdn_a_log, gdn_dt_bias, gdn_norm_g, mla_q_norm_g, mla_kv_norm_g, mla_w_uq, mla_w_ukv, rwkv_mu, rwkv_w0, rwkv_w_w2, rwkv_a0, rwkv_w_a2, rwkv_w_g2, rwkv_k_k, rwkv_k_a, rwkv_r_k, rwkv_ln_w, rwkv_ln_b, w_out_gdn, w_out_mla, w_out_rwkv, w_out_dsa, w_o, mlp_norm_g, w_up, w_down, final_norm_g):
    b, s, d = x.shape
    depth = w_in.shape[0]
    mla_tab, _, _ = _rope_tables(positions, MLA_ROPE, MLA_NOPE, MLA_QK)
    dsa_tab, cos8, sin8 = _rope_tables(positions, DH // 4, 0, 128 // 2)
    dsa_tab = jnp.concatenate([jnp.concatenate([dsa_tab[:, i * 64:(i + 1) * 64]] * 2, axis=1) for i in range(3)], axis=1)
    dsa_cst = jnp.swapaxes(jnp.concatenate([cos8, sin8], axis=-1), 1, 2)
    tabs = (mla_tab, dsa_tab, dsa_cst)
    row = lambda a: a.reshape(1, -1).astype(F32)
    pad8 = lambda rows: jnp.concatenate(rows + [jnp.zeros((8 - len(rows), rows[0].shape[1]), F32)], axis=0)
    x2d = x.reshape(b * s, d)
    final_g = row(final_norm_g)
    for l in range(depth):
        lane4 = lambda a: jnp.pad(a.astype(F32), (0, 124)).reshape(1, 128)
        gdn_pcol = pad8([lane4(gdn_a_log[l]), lane4(gdn_dt_bias[l])])
        gdn_prow = jnp.pad(jnp.stack([gdn_a_log[l], gdn_dt_bias[l]], axis=1).astype(F32), ((0, 4), (0, 126)))
        wq, wk, wv = _mla_weights(mla_w_uq[l], mla_w_ukv[l])
        rwkv_vec = pad8([row(rwkv_w0[l]), row(rwkv_a0[l]), row(rwkv_k_k[l]), row(rwkv_k_a[l]),
                         row(rwkv_ln_w[l]), row(rwkv_ln_b[l]), row(rwkv_r_k[l])])
        prm = (row(mix_norm_g[l]), _slab_weight(w_in[l]), gdn_conv_w[l].astype(F32), gdn_pcol, gdn_prow,
               row(jnp.tile(gdn_norm_g[l], HEADS)), row(mla_q_norm_g[l]), row(mla_kv_norm_g[l]), wq, wk, wv,
               row(rwkv_mu[l]), rwkv_vec, rwkv_w_w2[l].astype(F32), rwkv_w_a2[l].astype(F32),
               rwkv_w_g2[l].astype(F32),
               w_out_gdn[l].astype(BF16), w_out_mla[l].astype(BF16), w_out_rwkv[l].astype(BF16),
               w_out_dsa[l].astype(BF16), w_o[l].astype(BF16), row(mlp_norm_g[l]),
               w_up[l].astype(BF16), w_down[l].astype(BF16))
        x2d = _layer(x2d, b, s, prm, tabs, final_g, l == depth - 1)
    return x2d.reshape(b, s, d)
```

```python
import functools
import math

import numpy as np
import jax
import jax.numpy as jnp
from jax import lax
from jax.experimental import pallas as pl
from jax.experimental.pallas import tpu as pltpu

F32 = jnp.float32
BF16 = jnp.bfloat16
I32 = jnp.int32
I16 = jnp.int16

D_MODEL = 1024
DEPTH = 4
ROPE_THETA = 500000.0
NORM_EPS = 1e-6
NEG_INF = -1e30
HEADS = 4
DH = 64
HW = HEADS * DH
GDN_CONV = 4
CHUNK = 64
MLA_Q_LORA = 256
MLA_KV_LORA = 128
MLA_NOPE = 64
MLA_ROPE = 32
MLA_QK = 128
RWKV_GN_EPS = 64e-5
IDX_HEADS = 8
IDX_TOPK_MAX = 256
D_FF = 4 * D_MODEL
INT_MIN = -2 ** 31
LOG2E = 1.4426950408889634

VMEM_LIMIT_BYTES = 56 * 1024 * 1024

OFF_RWKV = 0
OFF_GDN_Q = 1024
OFF_GDN_K = 1280
OFF_GDN_V = 1536
OFF_GDN_Z = 1792
OFF_MLA_CQ = 2048
OFF_MLA_CKV = 2304
OFF_MLA_KR = 2432
OFF_DSA_IQ = 2560
OFF_DSA_Q = 3072
OFF_DSA_K = 3328
OFF_DSA_V = 3584
OFF_GDN_AB = 3840
OFF_DSA_IKW = 3968
OFF_GATE = 4096
N_SLAB = 8192

NN = (((1,), (0,)), ((), ()))
NT = (((1,), (1,)), ((), ()))
TN = (((0,), (0,)), ((), ()))
BNN = (((2,), (1,)), ((0,), (0,)))
BNT = (((2,), (2,)), ((0,), (0,)))
BTN = (((1,), (1,)), ((0,), (0,)))


def _cparams(n_axes):
    return pltpu.CompilerParams(dimension_semantics=("arbitrary",) * n_axes,
                                vmem_limit_bytes=VMEM_LIMIT_BYTES)


def _mm(a, b, dims=NN):
    return lax.dot_general(a.astype(BF16), b.astype(BF16), dims, preferred_element_type=F32)


def _mm_exact_lhs(m, x, dims=NN):
    mb = m.astype(BF16)
    x0 = x.astype(BF16)
    r1 = x - x0.astype(F32)
    x1 = r1.astype(BF16)
    x2 = (r1 - x1.astype(F32)).astype(BF16)
    d = lambda y: lax.dot_general(mb, y, dims, preferred_element_type=F32)
    return d(x0) + (d(x1) + d(x2))


def _mm_exact_rhs(x, m, dims=NN):
    mb = m.astype(BF16)
    x0 = x.astype(BF16)
    r1 = x - x0.astype(F32)
    x1 = r1.astype(BF16)
    x2 = (r1 - x1.astype(F32)).astype(BF16)
    d = lambda y: lax.dot_general(y, mb, dims, preferred_element_type=F32)
    return d(x0) + (d(x1) + d(x2))


def _iota2(shape, dim):
    return lax.broadcasted_iota(I32, shape, dim)


def _seg_ones(width, seg):
    sh = int(math.log2(seg))
    r = _iota2((width, width), 0) >> sh
    c = _iota2((width, width), 1) >> sh
    return jnp.where(r == c, 1.0, 0.0).astype(F32)


def _seg_sum(x, seg_ones):
    return _mm_exact_rhs(x, seg_ones)


def _softplus(x):
    return jnp.maximum(x, 0.0) + jnp.log1p(jnp.exp(-jnp.abs(x)))


def _sigmoid(x):
    return 1.0 / (1.0 + jnp.exp(-x))


def _silu(x):
    return x * _sigmoid(x)


def _tri_masks(n):
    r = _iota2((n, n), 0)
    c = _iota2((n, n), 1)
    return r, c


def _unit_lower_inverse(m, r, c, dims=NN):
    eye = jnp.where(r == c, 1.0, 0.0).astype(F32)
    same16 = (r >> 4) == (c >> 4)
    same32 = (r >> 5) == (c >> 5)
    md = jnp.where(same16, m, 0.0)
    t = eye - md
    p = md
    for _ in range(3):
        p = _mm(p, p, dims)
        t = t + _mm(t, p, dims)
    c32 = jnp.where(same32 & jnp.logical_not(same16), m, 0.0)
    t = t - _mm(_mm(t, c32, dims), t, dims)
    c64 = jnp.where(same32, 0.0, m)
    t = t - _mm(_mm(t, c64, dims), t, dims)
    return t


IN_PROJ_TN = 512


def _in_proj_kernel(x_ref, g_ref, w_ref, o_ref):
    x = x_ref[...]
    ms = jnp.mean(x * x, axis=-1, keepdims=True)
    h = (x * lax.rsqrt(ms + NORM_EPS) * g_ref[...]).astype(BF16)
    for n0 in range(0, o_ref.shape[1], IN_PROJ_TN):
        o_ref[:, n0:n0 + IN_PROJ_TN] = jnp.dot(
            h, w_ref[:, n0:n0 + IN_PROJ_TN], preferred_element_type=F32).astype(o_ref.dtype)


def _in_proj(x2d, g, w, out_dtype):
    t, d = x2d.shape
    n = w.shape[1]
    tm = min(512, t)
    return pl.pallas_call(
        _in_proj_kernel,
        grid=(t // tm,),
        in_specs=[pl.BlockSpec((tm, d), lambda i: (i, 0)),
                  pl.BlockSpec((1, d), lambda i: (0, 0)),
                  pl.BlockSpec((d, n), lambda i: (0, 0))],
        out_specs=pl.BlockSpec((tm, n), lambda i: (i, 0)),
        out_shape=jax.ShapeDtypeStruct((t, n), out_dtype),
        compiler_params=_cparams(1),
        name="in_proj",
    )(x2d, g, w)


def _to_groups(x, nchunk):
    parts = [x[:, h * DH:(h + 1) * DH].reshape(nchunk, 1, CHUNK, DH) for h in range(HEADS)]
    return jnp.concatenate(parts, axis=1).reshape(nchunk * HEADS, CHUNK, DH)


def _col_groups(x, lane0, nchunk):
    parts = [x[:, lane0 + h:lane0 + h + 1].reshape(nchunk, 1, CHUNK, 1) for h in range(HEADS)]
    return jnp.concatenate(parts, axis=1).reshape(nchunk * HEADS, CHUNK, 1)


def _gdn_kernel(q_ref, k_ref, v_ref, z_ref, ab_ref, abr_ref, cw_ref, pcol_ref, prow_ref, ng_ref,
                o_ref, xbuf_ref, state_ref, u_ref, w_ref, a_ref, qd_ref, kd_ref, egl_ref, oraw_ref,
                *, tb):
    nchunk = tb // CHUNK

    @pl.when(pl.program_id(1) == 0)
    def _():
        xbuf_ref[0:8, :] = jnp.zeros((8, 3 * HW), F32)
        state_ref[...] = jnp.zeros_like(state_ref)

    xbuf_ref[8:8 + tb, 0:HW] = q_ref[...]
    xbuf_ref[8:8 + tb, HW:2 * HW] = k_ref[...]
    xbuf_ref[8:8 + tb, 2 * HW:3 * HW] = v_ref[...]
    cw = cw_ref[...]
    y = xbuf_ref[8:8 + tb, :] * cw[3:4, :]
    y = y + xbuf_ref[7:7 + tb, :] * cw[2:3, :]
    y = y + xbuf_ref[6:6 + tb, :] * cw[1:2, :]
    y = y + xbuf_ref[5:5 + tb, :] * cw[0:1, :]
    xbuf_ref[0:8, :] = xbuf_ref[tb:tb + 8, :]
    y = _silu(y)
    seg = _seg_ones(HW, DH)
    q = y[:, 0:HW]
    k = y[:, HW:2 * HW]
    q = q * lax.rsqrt(_seg_sum(q * q, seg) + 1e-6) * (DH ** -0.5)
    k = k * lax.rsqrt(_seg_sum(k * k, seg) + 1e-6)
    v = y[:, 2 * HW:3 * HW]

    ab = ab_ref[...]
    pcol = pcol_ref[...]
    g_col = -jnp.exp(pcol[0:1, :]) * _softplus(ab + pcol[1:2, :])
    beta_col = _sigmoid(ab)

    r, c = _tri_masks(CHUNK)
    incl = c <= r
    strict = c < r
    l_incl = jnp.where(incl, 1.0, 0.0).astype(F32)

    rb = _iota2((tb, tb), 0)
    cb = _iota2((tb, tb), 1)
    l_blk = jnp.where(((rb >> 6) == (cb >> 6)) & (cb <= rb), 1.0, 0.0).astype(F32)
    gc_col = _mm_exact_lhs(l_blk, g_col)
    prow = prow_ref[...]
    abr = abr_ref[0].reshape(nchunk * 8, CHUNK)
    prow_t = jnp.concatenate([prow] * nchunk, axis=0)
    g_row = -jnp.exp(prow_t[:, 0:1]) * _softplus(abr + prow_t[:, 1:2])
    gc_row = _mm_exact_rhs(g_row, l_incl, NT).reshape(nchunk, 8, CHUNK)

    qg, kg, vg = _to_groups(q, nchunk), _to_groups(k, nchunk), _to_groups(v, nchunk)
    beta = _col_groups(beta_col, 4, nchunk)
    gcol = _col_groups(gc_col, 0, nchunk)
    grow = jnp.concatenate([gc_row[:, h:h + 1, :].reshape(nchunk, 1, 1, CHUNK) for h in range(HEADS)],
                           axis=1).reshape(nchunk * HEADS, 1, CHUNK)
    decay = jnp.where(incl, jnp.exp(jnp.where(incl, gcol - grow, 0.0)), 0.0)
    kb = kg * beta
    m = jnp.where(strict, _mm(kb, kg, BNT) * decay, 0.0)
    t_inv = _unit_lower_inverse(m, r, c, BNN)
    egc = jnp.exp(gcol)
    glast = gcol[:, CHUNK - 1:CHUNK, :]
    uw = _mm(t_inv, jnp.concatenate([vg * beta, kb * egc], axis=2), BNN)
    u_ref[...] = uw[:, :, 0:DH]
    w_ref[...] = uw[:, :, DH:2 * DH]
    a_ref[...] = _mm(qg, kg, BNT) * decay
    qd_ref[...] = qg * egc
    kd_ref[...] = kg * jnp.exp(glast - gcol)
    egl_ref[...] = jnp.broadcast_to(jnp.exp(glast), (nchunk * HEADS, 1, DH))

    def chunk_body(ci, carry):
        sl = pl.ds(pl.multiple_of(ci * HEADS, HEADS), HEADS)
        s = state_ref[...]
        v_new = u_ref[sl] - _mm(w_ref[sl], s, BNN)
        o = _mm(qd_ref[sl], s, BNN) + _mm(a_ref[sl], v_new, BNN)
        state_ref[...] = s * egl_ref[sl] + _mm(kd_ref[sl], v_new, BTN)
        oraw_ref[pl.ds(pl.multiple_of(ci * CHUNK, CHUNK), CHUNK), :] = jnp.concatenate(
            [o[h] for h in range(HEADS)], axis=1)
        return carry

    lax.fori_loop(0, nchunk, chunk_body, 0)
    o_all = oraw_ref[...]
    ms = _seg_sum(o_all * o_all, seg) * (1.0 / DH)
    o_ref[...] = o_all * lax.rsqrt(ms + NORM_EPS) * ng_ref[...] * _silu(z_ref[...])


GDN_TB = 512


def _gdn(slab, abr, conv_w, pcol, prow, norm_g, b, s):
    tb = min(GDN_TB, s)
    nb = s // tb
    nchunk = tb // CHUNK
    ng = nchunk * HEADS
    grp = lambda w: pltpu.VMEM((ng, CHUNK, w), F32)
    col = lambda off, w: pl.BlockSpec((tb, w), lambda i, j, o=off // w: (i * nb + j, o))
    full = lambda shape: pl.BlockSpec(shape, lambda i, j: (0,) * len(shape))
    return pl.pallas_call(
        functools.partial(_gdn_kernel, tb=tb),
        grid=(b, nb),
        in_specs=[col(OFF_GDN_Q, HW), col(OFF_GDN_K, HW), col(OFF_GDN_V, HW), col(OFF_GDN_Z, HW),
                  col(OFF_GDN_AB, 128),
                  pl.BlockSpec((1, nchunk, 8, CHUNK), lambda i, j: (i, j, 0, 0)),
                  full((GDN_CONV, 3 * HW)), full((8, 128)), full((8, 128)), full((1, HW))],
        out_specs=pl.BlockSpec((tb, HW), lambda i, j: (i * nb + j, 0)),
        out_shape=jax.ShapeDtypeStruct((b * s, HW), F32),
        scratch_shapes=[pltpu.VMEM((tb + 8, 3 * HW), F32), pltpu.VMEM((HEADS, DH, DH), F32),
                        grp(DH), grp(DH), grp(CHUNK), grp(DH), grp(DH),
                        pltpu.VMEM((ng, 1, DH), F32), pltpu.VMEM((tb, HW), F32)],
        compiler_params=_cparams(2),
        name="gdn",
    )(slab, slab, slab, slab, slab, abr, conv_w, pcol, prow, norm_g)


def _rwkv_kernel(p_ref, mu_ref, vec_ref, ww2_ref, wa2_ref, wg2_ref, o_ref,
                 xbuf_ref, state_ref, ta_ref, tpv_ref, rt_ref, qkv_ref, qb_ref, bc_ref, ktv_ref,
                 el_ref, oraw_ref, *, tb):
    nchunk = tb // CHUNK

    @pl.when(pl.program_id(1) == 0)
    def _():
        xbuf_ref[0:8, :] = jnp.zeros((8, 4 * HW), F32)
        state_ref[...] = jnp.zeros_like(state_ref)

    p = p_ref[...]
    xbuf_ref[8:8 + tb, :] = p
    prev = xbuf_ref[7:7 + tb, :]
    xbuf_ref[0:8, :] = xbuf_ref[tb:tb + 8, :]
    pm = p + (prev - p) * mu_ref[...]
    vec = vec_ref[...]
    w0, a0, k_k, k_a = vec[0:1, :], vec[1:2, :], vec[2:3, :], vec[3:4, :]
    rr = pm[:, 0:HW]
    k = pm[:, HW:2 * HW]
    v = pm[:, 2 * HW:3 * HW]
    xw = pm[:, 768:832]
    xa = pm[:, 832:896]
    xg = pm[:, 896:1024]
    w_log = -_softplus(-(w0 + _mm(jnp.tanh(xw), ww2_ref[...]))) - 0.5
    lw = -jnp.exp(w_log)
    a = _sigmoid(a0 + _mm(xa, wa2_ref[...]))
    g = _mm(_sigmoid(xg), wg2_ref[...])
    seg = _seg_ones(HW, DH)
    kk = k * k_k
    kk = kk * lax.rsqrt(_seg_sum(kk * kk, seg) + 1e-6)
    k2 = k * (1.0 + (a - 1.0) * k_a)
    kka = kk * a

    r_i, c_i = _tri_masks(CHUNK)
    incl = c_i <= r_i
    strict = c_i < r_i
    rb = _iota2((tb, tb), 0)
    cb = _iota2((tb, tb), 1)
    l_blk = jnp.where(((rb >> 6) == (cb >> 6)) & (cb <= rb), 1.0, 0.0).astype(F32)
    lwc_all = _mm_exact_lhs(l_blk, lw)

    rg, kg, vg = _to_groups(rr, nchunk), _to_groups(k2, nchunk), _to_groups(v, nchunk)
    kkg, kkag = _to_groups(kk, nchunk), _to_groups(kka, nchunk)
    lwg, lwc = _to_groups(lw, nchunk), _to_groups(lwc_all, nchunk)
    lwl = lwc[:, CHUNK - 1:CHUNK, :]
    e_neg = jnp.exp(-lwc)
    at = kkg * jnp.exp(lwc - lwg)
    rt = rg * jnp.exp(lwc)
    y2 = jnp.concatenate([at, rt], axis=1)
    x2 = jnp.concatenate([kkag * e_neg, kg * e_neg], axis=1)
    gram = _mm(y2, x2, BNT)
    n_m = jnp.where(strict, gram[:, 0:CHUNK, 0:CHUNK], 0.0)
    p_m = jnp.where(strict, gram[:, 0:CHUNK, CHUNK:], 0.0)
    qb_m = jnp.where(incl, gram[:, CHUNK:, 0:CHUNK], 0.0)
    qk_m = jnp.where(incl, gram[:, CHUNK:, CHUNK:], 0.0)
    t_inv = _unit_lower_inverse(n_m, r_i, c_i, BNN)
    ta_ref[...] = _mm(t_inv, at, BNN)
    tpv_ref[...] = _mm(t_inv, _mm(p_m, vg, BNN), BNN)
    rt_ref[...] = rt
    qkv_ref[...] = _mm(qk_m, vg, BNN)
    qb_ref[...] = qb_m
    e_rel = jnp.exp(lwl - lwc)
    bc_ref[...] = kkag * e_rel
    ktv_ref[...] = _mm(vg, kg * e_rel, BTN)
    el_ref[...] = jnp.exp(lwl)

    def chunk_body(ci, carry):
        sl = pl.ds(pl.multiple_of(ci * HEADS, HEADS), HEADS)
        zt = state_ref[...]
        u = _mm(ta_ref[sl], zt, BNT) + tpv_ref[sl]
        o = _mm(rt_ref[sl], zt, BNT) + qkv_ref[sl] - _mm(qb_ref[sl], u, BNN)
        state_ref[...] = zt * el_ref[sl] + ktv_ref[sl] - _mm(u, bc_ref[sl], BTN)
        oraw_ref[pl.ds(pl.multiple_of(ci * CHUNK, CHUNK), CHUNK), :] = jnp.concatenate(
            [o[h] for h in range(HEADS)], axis=1)
        return carry

    lax.fori_loop(0, nchunk, chunk_body, 0)
    ln_w, ln_b, r_k = vec[4:5, :], vec[5:6, :], vec[6:7, :]
    o_all = oraw_ref[...]
    mean = _seg_sum(o_all, seg) * (1.0 / DH)
    cen = o_all - mean
    var = _seg_sum(cen * cen, seg) * (1.0 / DH)
    o_n = cen * lax.rsqrt(var + RWKV_GN_EPS) * ln_w + ln_b
    bonus = _seg_sum(rr * k2 * r_k, seg) * v
    o_ref[...] = (o_n + bonus) * g


RWKV_TB = 512


def _rwkv(slab, mu, vec, ww2, wa2, wg2, b, s):
    tb = min(RWKV_TB, s)
    nb = s // tb
    ng = (tb // CHUNK) * HEADS
    grp = lambda: pltpu.VMEM((ng, CHUNK, DH), F32)
    full = lambda shape: pl.BlockSpec(shape, lambda i, j: (0,) * len(shape))
    return pl.pallas_call(
        functools.partial(_rwkv_kernel, tb=tb),
        grid=(b, nb),
        in_specs=[pl.BlockSpec((tb, 4 * HW), lambda i, j: (i * nb + j, OFF_RWKV // (4 * HW))),
                  full((1, 4 * HW)), full((8, HW)), full((64, HW)), full((64, HW)), full((128, HW))],
        out_specs=pl.BlockSpec((tb, HW), lambda i, j: (i * nb + j, 0)),
        out_shape=jax.ShapeDtypeStruct((b * s, HW), F32),
        scratch_shapes=[pltpu.VMEM((tb + 8, 4 * HW), F32), pltpu.VMEM((HEADS, DH, DH), F32),
                        grp(), grp(), grp(), grp(), grp(), grp(), grp(),
                        pltpu.VMEM((ng, 1, DH), F32), pltpu.VMEM((tb, HW), F32)],
        compiler_params=_cparams(2),
        name="rwkv7",
    )(slab, mu, vec, ww2, wa2, wg2)


def _rope(x, cos_t, sin_lo, sin_hi, half):
    w = x.shape[-1]
    return x * cos_t + pltpu.roll(x, w - half, 1) * sin_lo + pltpu.roll(x, half, 1) * sin_hi


def _tile_lanes(t, n):
    return t if n == 1 else jnp.concatenate([t] * n, axis=1)


def _mla_prep_kernel(cq_ref, ckv_ref, kr_ref, tab_ref, qg_ref, kvg_ref, wq_ref, wk_ref, wvt_ref,
                     q_ref, k_ref, vt_ref):
    cq = cq_ref[...]
    hq = cq * lax.rsqrt(jnp.mean(cq * cq, axis=-1, keepdims=True) + NORM_EPS) * qg_ref[...]
    ckv = ckv_ref[...]
    hkv = ckv * lax.rsqrt(jnp.mean(ckv * ckv, axis=-1, keepdims=True) + NORM_EPS) * kvg_ref[...]
    tab = tab_ref[...]
    cos_t, sin_lo, sin_hi = tab[:, 0:128], tab[:, 128:256], tab[:, 256:384]
    q = _mm(hq, wq_ref[...])
    q = _rope(q, _tile_lanes(cos_t, HEADS), _tile_lanes(sin_lo, HEADS), _tile_lanes(sin_hi, HEADS),
              MLA_ROPE // 2)
    q_ref[...] = (q * ((MLA_NOPE + MLA_ROPE) ** -0.5 * LOG2E)).astype(BF16)
    kr = _rope(kr_ref[...], cos_t, sin_lo, sin_hi, MLA_ROPE // 2)
    k = _mm(hkv, wk_ref[...]) + _tile_lanes(kr, HEADS)
    k_ref[...] = k.astype(BF16)
    vt_ref[...] = lax.dot_general(wvt_ref[...], hkv.astype(BF16), NT,
                                  preferred_element_type=F32).astype(BF16)


def _mla_prep(slab, tab, qg, kvg, wq, wk, wvt):
    t = slab.shape[0]
    tm = min(512, t)
    full = lambda shape: pl.BlockSpec(shape, lambda i: (0,) * len(shape))
    return pl.pallas_call(
        _mla_prep_kernel,
        grid=(t // tm,),
        in_specs=[pl.BlockSpec((tm, 256), lambda i: (i, OFF_MLA_CQ // 256)),
                  pl.BlockSpec((tm, 128), lambda i: (i, OFF_MLA_CKV // 128)),
                  pl.BlockSpec((tm, 128), lambda i: (i, OFF_MLA_KR // 128)),
                  pl.BlockSpec((tm, 384), lambda i: (i, 0)),
                  full((1, 256)), full((1, 128)), full((256, 512)), full((128, 512)), full((256, 128))],
        out_specs=[pl.BlockSpec((tm, 512), lambda i: (i, 0)), pl.BlockSpec((tm, 512), lambda i: (i, 0)),
                   pl.BlockSpec((256, tm), lambda i: (0, i))],
        out_shape=[jax.ShapeDtypeStruct((t, 512), BF16), jax.ShapeDtypeStruct((t, 512), BF16),
                   jax.ShapeDtypeStruct((256, t), BF16)],
        compiler_params=_cparams(1),
        name="mla_prep",
    )(slab, slab, slab, tab, qg, kvg, wq, wk, wvt)


def _flash_kernel(q_ref, k_ref, vt_ref, o_ref, m_ref, l_ref, acc_ref, *, tq, tk):
    qi = pl.program_id(1)
    ki = pl.program_id(2)

    @pl.when(ki == 0)
    def _():
        m_ref[...] = jnp.full_like(m_ref, NEG_INF)
        l_ref[...] = jnp.zeros_like(l_ref)
        acc_ref[...] = jnp.zeros_like(acc_ref)

    def step(on_diagonal):
        if on_diagonal:
            causal = _iota2((tk, tq), 0) <= _iota2((tk, tq), 1)
        vt = vt_ref[...]
        ones = jnp.ones((16, tk), BF16)
        scores = [lax.dot_general(k_ref[:, h * MLA_QK:(h + 1) * MLA_QK], q_ref[:, h * MLA_QK:(h + 1) * MLA_QK],
                                  NT, preferred_element_type=F32) for h in range(HEADS)]
        for h in range(HEADS):
            s = scores[h]
            if on_diagonal:
                s = jnp.where(causal, s, NEG_INF)
            m_old = m_ref[h]
            m_new = jnp.maximum(m_old, jnp.max(s, axis=0, keepdims=True))
            alpha = jnp.exp2(m_old - m_new)
            p = jnp.exp2(s - m_new).astype(BF16)
            pv = jnp.dot(jnp.concatenate([vt[h * DH:(h + 1) * DH, :], ones], axis=0), p,
                         preferred_element_type=F32)
            l_ref[h] = alpha * l_ref[h] + pv[DH:DH + 1, :]
            acc_ref[h * DH:(h + 1) * DH, :] = alpha * acc_ref[h * DH:(h + 1) * DH, :] + pv[0:DH, :]
            m_ref[h] = m_new

    pl.when(ki < qi)(lambda: step(False))
    pl.when(ki == qi)(lambda: step(True))

    @pl.when(ki == pl.num_programs(2) - 1)
    def _():
        o_t = jnp.concatenate([acc_ref[h * DH:(h + 1) * DH, :] / l_ref[h] for h in range(HEADS)], axis=0)
        o_ref[...] = o_t.T


def _flash(q, k, vt, b, s):
    tq = min(512, s)
    tk = tq
    nq = s // tq
    nk = s // tk
    return pl.pallas_call(
        functools.partial(_flash_kernel, tq=tq, tk=tk),
        grid=(b, nq, nk),
        in_specs=[pl.BlockSpec((tq, 512), lambda i, j, kk: (i * nq + j, 0)),
                  pl.BlockSpec((tk, 512), lambda i, j, kk: (i * nk + jnp.minimum(kk, j), 0)),
                  pl.BlockSpec((256, tk), lambda i, j, kk: (0, i * nk + jnp.minimum(kk, j)))],
        out_specs=pl.BlockSpec((tq, 256), lambda i, j, kk: (i * nq + j, 0)),
        out_shape=jax.ShapeDtypeStruct((b * s, 256), F32),
        scratch_shapes=[pltpu.VMEM((HEADS, 1, tq), F32), pltpu.VMEM((HEADS, 1, tq), F32),
                        pltpu.VMEM((HW, tq), F32)],
        compiler_params=_cparams(3),
        name="mla_flash",
    )(q, k, vt)


DSA_TQ = 256
DSA_TK = 512


def _dsa_key_prep_kernel(ikw_ref, k_ref, v_ref, tab_ref, ik_ref, ko_ref, vt_ref):
    tab = tab_ref[...]
    cos_t, sin_lo, sin_hi = tab[:, 0:128], tab[:, 128:256], tab[:, 256:384]
    half = DH // 8
    ik = _rope(ikw_ref[...], cos_t, sin_lo, sin_hi, half)
    ik_ref[...] = ik[:, 0:DH].astype(BF16)
    kr = _rope(k_ref[...], _tile_lanes(cos_t, 2), _tile_lanes(sin_lo, 2), _tile_lanes(sin_hi, 2), half)
    for h in range(HEADS):
        ko_ref[h] = kr[:, h * DH:(h + 1) * DH].astype(BF16)
    vt_ref[0] = v_ref[...].T.astype(BF16)


def _dsa_key_prep(slab, tab):
    t = slab.shape[0]
    tm = min(DSA_TK, t)
    return pl.pallas_call(
        _dsa_key_prep_kernel,
        grid=(t // tm,),
        in_specs=[pl.BlockSpec((tm, 128), lambda i: (i, OFF_DSA_IKW // 128)),
                  pl.BlockSpec((tm, 256), lambda i: (i, OFF_DSA_K // 256)),
                  pl.BlockSpec((tm, 256), lambda i: (i, OFF_DSA_V // 256)),
                  pl.BlockSpec((tm, 384), lambda i: (i, 0))],
        out_specs=[pl.BlockSpec((tm, DH), lambda i: (i, 0)),
                   pl.BlockSpec((HEADS, tm, DH), lambda i: (0, i, 0)),
                   pl.BlockSpec((1, HW, tm), lambda i: (i, 0, 0))],
        out_shape=[jax.ShapeDtypeStruct((t, DH), BF16), jax.ShapeDtypeStruct((HEADS, t, DH), BF16),
                   jax.ShapeDtypeStruct((t // tm, HW, tm), BF16)],
        compiler_params=_cparams(1),
        name="dsa_key_prep",
    )(slab, slab, slab, tab)


def _rope_rows(x, cos8, sin8, nheads):
    parts = []
    for h in range(nheads):
        x1 = x[h * DH:h * DH + 8, :]
        x2 = x[h * DH + 8:h * DH + 16, :]
        parts.append(x1 * cos8 - x2 * sin8)
        parts.append(x2 * cos8 + x1 * sin8)
        parts.append(x[h * DH + 16:(h + 1) * DH, :])
    return jnp.concatenate(parts, axis=0)


def _dsa_kernel(iq_ref, ikw_ref, q_ref, cst_ref, ik_ref, k_ref, vt_ref, kidx_ref, o_ref,
                hi_ref, lo_ref, lom_ref, bias_ref, acc_ref, s_ref, m_ref, l_ref,
                *, tq, tk, n_sel, idx_bits):
    qi = pl.program_id(1)
    nkt = (qi * tq + tq - 1) // tk + 1
    cs = cst_ref[0]
    cos8, sin8 = cs[0:8, :], cs[8:16, :]
    iqt = _rope_rows(iq_ref[...].T, cos8, sin8, IDX_HEADS).astype(BF16)
    iw = ikw_ref[...].T[DH:DH + 8, :] * (IDX_HEADS ** -0.5 * DH ** -0.5)
    q_pos = qi * tq + _iota2((tk, tq), 1)

    def score_body(kt, carry):
        row0 = pl.multiple_of(kt * tk, tk)
        ik = ik_ref[pl.ds(row0, tk), :]
        sc = jnp.zeros((tk, tq), F32)
        for h in range(IDX_HEADS):
            lg = lax.dot_general(ik, iqt[h * DH:(h + 1) * DH, :], NN, preferred_element_type=F32)
            sc = sc + jnp.maximum(lg, 0.0) * iw[h:h + 1, :]
        bits = lax.bitcast_convert_type(sc, I32)
        key = bits ^ ((bits >> 31) & 0x7FFFFFFF)
        key = jnp.where(sc == 0.0, 0, key)
        k_pos = row0 + _iota2((tk, tq), 0)
        key = jnp.where(k_pos <= q_pos, key, INT_MIN)
        hi_ref[pl.ds(row0, tk), :] = (key >> 16).astype(I16)
        lo_ref[pl.ds(row0, tk), :] = ((key & 0xFFFF) - 32768).astype(I16)
        return carry

    lax.fori_loop(0, nkt, score_body, 0)

    one16 = jnp.ones((tk, tq), I16)
    zero16 = jnp.zeros((tk, tq), I16)

    def count(*pred_fns):
        def body(kt, accs):
            sl = pl.ds(pl.multiple_of(kt * tk, tk), tk)
            args = (hi_ref[sl, :], lo_ref[sl, :], lom_ref[sl, :], kidx_ref[sl, :])
            out = []
            for acc, fn in zip(accs, pred_fns):
                ones = jnp.where(fn(*args), one16, zero16).reshape(tk // 64, 64, tq)
                for g in range(tk // 64):
                    acc = acc + ones[g]
                out.append(acc)
            return tuple(out)
        parts = lax.fori_loop(0, nkt, body, tuple(jnp.zeros((64, tq), I16) for _ in pred_fns))
        sums = tuple(jnp.sum(p.astype(I32).astype(F32), axis=0, keepdims=True) for p in parts)
        return sums[0] if len(sums) == 1 else sums

    def bisect16(pred_ge):
        def body(it, thr):
            bit = jnp.maximum(15 - it, 0)
            cand = jnp.where(it == 0, jnp.zeros_like(thr), thr | jnp.left_shift(jnp.int32(1), bit))
            c16 = cand.astype(I16)
            cnt = count(lambda hi, lo, lom, idx: pred_ge(hi, lo, lom, c16))
            return jnp.where(cnt >= float(n_sel), cand, thr)
        return lax.fori_loop(0, 16, body, jnp.full((1, tq), -32768, I32))

    t_hi = bisect16(lambda hi, lo, lom, c: hi >= c)
    t_hi16 = t_hi.astype(I16)

    def mask_lo_body(kt, carry):
        sl = pl.ds(pl.multiple_of(kt * tk, tk), tk)
        hi = hi_ref[sl, :]
        lom_ref[sl, :] = jnp.where(hi > t_hi16, jnp.int16(32767),
                                   jnp.where(hi == t_hi16, lo_ref[sl, :], jnp.int16(-32768)))
        return carry

    lax.fori_loop(0, nkt, mask_lo_body, 0)
    t_lo16 = bisect16(lambda hi, lo, lom, c: lom >= c).astype(I16)

    def is_gt(hi, lo):
        return (hi > t_hi16) | ((hi == t_hi16) & (lo > t_lo16))

    def is_eq(hi, lo):
        return (hi == t_hi16) & (lo == t_lo16)

    cnt_gt, cnt_eq = count(lambda hi, lo, lom, idx: is_gt(hi, lo), lambda hi, lo, lom, idx: is_eq(hi, lo))
    need = float(n_sel) - cnt_gt

    def idx_body(it, j):
        cand = j + jnp.left_shift(jnp.int32(1), idx_bits - 1 - it)
        c16 = cand.astype(I16)
        cnt = count(lambda hi, lo, lom, idx: is_eq(hi, lo) & (idx < c16))
        return jnp.where(cnt < need, cand, j)

    excess = jnp.max(jnp.where(cnt_eq > need, 1.0, 0.0)) > 0.0
    j_cut16 = lax.cond(excess,
                       lambda: lax.fori_loop(0, idx_bits, idx_body, jnp.zeros((1, tq), I32)),
                       lambda: jnp.full((1, tq), 32767, I32)).astype(I16)

    def bias_body(kt, carry):
        sl = pl.ds(pl.multiple_of(kt * tk, tk), tk)
        hi, lo = hi_ref[sl, :], lo_ref[sl, :]
        sel = (is_gt(hi, lo) | (is_eq(hi, lo) & (kidx_ref[sl, :] <= j_cut16))) & (hi != jnp.int16(-32768))
        bias_ref[sl, :] = jnp.where(sel, jnp.zeros((tk, tq), BF16),
                                    jnp.full((tk, tq), NEG_INF, BF16)).astype(F32)
        return carry

    lax.fori_loop(0, nkt, bias_body, 0)

    qt = (_rope_rows(q_ref[...].T, cos8, sin8, HEADS) * (DH ** -0.5 * LOG2E)).astype(BF16)
    acc_ref[...] = jnp.zeros_like(acc_ref)

    def att_body(kt, carry):
        row0 = pl.multiple_of(kt * tk, tk)
        for h in range(HEADS):
            s_ref[h] = lax.dot_general(k_ref[h, pl.ds(row0, tk), :], qt[h * DH:(h + 1) * DH, :], NN,
                                       preferred_element_type=F32)

        @pl.when(kt >= 0)
        def _():
            bias = bias_ref[pl.ds(row0, tk), :]
            ones = jnp.ones((16, tk), BF16)
            for h in range(HEADS):
                hs = slice(h * DH, (h + 1) * DH)
                s = s_ref[h] + bias
                m_old = m_ref[h]
                m_new = jnp.maximum(m_old, jnp.max(s, axis=0, keepdims=True))
                alpha = jnp.exp2(m_old - m_new)
                p = jnp.exp2(s - m_new).astype(BF16)
                pv = lax.dot_general(jnp.concatenate([vt_ref[kt, hs, :], ones], axis=0), p, NN,
                                     preferred_element_type=F32)
                acc_ref[hs, :] = alpha * acc_ref[hs, :] + pv[0:DH, :]
                m_ref[h] = m_new
                l_ref[h] = alpha * l_ref[h] + pv[DH:DH + 1, :]
        return carry

    m_ref[...] = jnp.full_like(m_ref, NEG_INF)
    l_ref[...] = jnp.zeros_like(l_ref)
    lax.fori_loop(0, nkt, att_body, 0)
    o_t = jnp.concatenate([acc_ref[h * DH:(h + 1) * DH, :] / l_ref[h] for h in range(HEADS)], axis=0)
    o_ref[...] = o_t.T


def _dsa(slab, cst, ik, kr, vt, b, s, n_sel):
    tq = min(DSA_TQ, s)
    tk = min(DSA_TK, s)
    nq = s // tq
    nkt = s // tk
    return pl.pallas_call(
        functools.partial(_dsa_kernel, tq=tq, tk=tk, n_sel=n_sel, idx_bits=int(math.log2(s))),
        grid=(b, nq),
        in_specs=[pl.BlockSpec((tq, IDX_HEADS * DH), lambda i, j: (i * nq + j, OFF_DSA_IQ // (IDX_HEADS * DH))),
                  pl.BlockSpec((tq, 128), lambda i, j: (i * nq + j, OFF_DSA_IKW // 128)),
                  pl.BlockSpec((tq, HW), lambda i, j: (i * nq + j, OFF_DSA_Q // HW)),
                  pl.BlockSpec((1, 16, tq), lambda i, j: (i, 0, j)),
                  pl.BlockSpec((s, DH), lambda i, j: (i, 0)),
                  pl.BlockSpec((HEADS, s, DH), lambda i, j: (0, i, 0)),
                  pl.BlockSpec((nkt, HW, tk), lambda i, j: (i, 0, 0)),
                  pl.BlockSpec((s, tq), lambda i, j: (0, 0))],
        out_specs=pl.BlockSpec((tq, HW), lambda i, j: (i * nq + j, 0)),
        out_shape=jax.ShapeDtypeStruct((b * s, HW), F32),
        scratch_shapes=[pltpu.VMEM((s, tq), I16), pltpu.VMEM((s, tq), I16), pltpu.VMEM((s, tq), I16),
                        pltpu.VMEM((s, tq), F32), pltpu.VMEM((HW, tq), F32),
                        pltpu.VMEM((HEADS, tk, tq), F32), pltpu.VMEM((HEADS, 1, tq), F32),
                        pltpu.VMEM((HEADS, 1, tq), F32)],
        compiler_params=_cparams(2),
        name="dsa",
    )(slab, slab, slab, cst, ik, kr, vt,
      jnp.broadcast_to(jnp.arange(s, dtype=I16)[:, None], (s, tq)))


def _merge_kernel(x_ref, oa_ref, ob_ref, oc_ref, od_ref, ga_ref, gb_ref, gc_ref, gd_ref,
                  wa_ref, wb_ref, wc_ref, wd_ref, wo_ref, o_ref):
    def branch(o_r, g_r, w_r):
        y = jnp.dot(o_r[...].astype(BF16), w_r[...], preferred_element_type=F32)
        return _sigmoid(g_r[...].astype(F32)) * y

    merged = (branch(oa_ref, ga_ref, wa_ref) + branch(ob_ref, gb_ref, wb_ref)
              + branch(oc_ref, gc_ref, wc_ref) + branch(od_ref, gd_ref, wd_ref))
    o_ref[...] = x_ref[...] + jnp.dot(merged.astype(BF16), wo_ref[...], preferred_element_type=F32)


def _merge(x2d, oa, ob, oc, od, gates, wa, wb, wc, wd, wo):
    t, d = x2d.shape
    tm = min(512, t)
    row = lambda w: pl.BlockSpec((tm, w), lambda i: (i, 0))
    gate = lambda n: pl.BlockSpec((tm, d), lambda i, o=n: (i, o))
    full = lambda shape: pl.BlockSpec(shape, lambda i: (0,) * len(shape))
    return pl.pallas_call(
        _merge_kernel,
        grid=(t // tm,),
        in_specs=[row(d), row(HW), row(HW), row(HW), row(HW), gate(0), gate(1), gate(2), gate(3),
                  full((HW, d)), full((HW, d)), full((HW, d)), full((HW, d)), full((d, d))],
        out_specs=row(d),
        out_shape=jax.ShapeDtypeStruct((t, d), F32),
        compiler_params=_cparams(1),
        name="merge",
    )(x2d, oa, ob, oc, od, gates, gates, gates, gates, wa, wb, wc, wd, wo)


def _mlp_kernel(x_ref, g_ref, wu_ref, wd_ref, fg_ref, o_ref, h_ref, acc_ref, *, final_norm):
    fi = pl.program_id(1)

    @pl.when(fi == 0)
    def _():
        x = x_ref[...]
        ms = jnp.mean(x * x, axis=-1, keepdims=True)
        h_ref[...] = (x * lax.rsqrt(ms + NORM_EPS) * g_ref[...]).astype(BF16)
        acc_ref[...] = jnp.zeros_like(acc_ref)

    u = jnp.dot(h_ref[...], wu_ref[...], preferred_element_type=F32)
    u = jnp.square(jnp.maximum(u, 0.0))
    acc_ref[...] += jnp.dot(u.astype(BF16), wd_ref[...], preferred_element_type=F32)

    @pl.when(fi == pl.num_programs(1) - 1)
    def _():
        y = x_ref[...] + acc_ref[...]
        if final_norm:
            ms = jnp.mean(y * y, axis=-1, keepdims=True)
            y = y * lax.rsqrt(ms + NORM_EPS) * fg_ref[...]
        o_ref[...] = y


def _mlp(x2d, g, wu, wd, fg, final_norm):
    t, d = x2d.shape
    f = wu.shape[1]
    tm = min(1024, t)
    tf = 512
    return pl.pallas_call(
        functools.partial(_mlp_kernel, final_norm=final_norm),
        grid=(t // tm, f // tf),
        in_specs=[pl.BlockSpec((tm, d), lambda i, j: (i, 0)),
                  pl.BlockSpec((1, d), lambda i, j: (0, 0)),
                  pl.BlockSpec((d, tf), lambda i, j: (0, j)),
                  pl.BlockSpec((tf, d), lambda i, j: (j, 0)),
                  pl.BlockSpec((1, d), lambda i, j: (0, 0))],
        out_specs=pl.BlockSpec((tm, d), lambda i, j: (i, 0)),
        out_shape=jax.ShapeDtypeStruct((t, d), F32),
        scratch_shapes=[pltpu.VMEM((tm, d), BF16), pltpu.VMEM((tm, d), F32)],
        compiler_params=_cparams(2),
        name="mlp",
    )(x2d, g, wu, wd, fg)


def _slab_weight(w_in, dtype=BF16):
    d = w_in.shape[0]
    z = lambda n: jnp.zeros((d, n), w_in.dtype)
    gdn, mla, rwkv, dsa = 0, 1032, 1448, 2472
    gate = 3824
    cols = [
        w_in[:, rwkv:rwkv + 1024],
        w_in[:, gdn:gdn + 768],
        w_in[:, gdn + 776:gdn + 1032],
        w_in[:, mla:mla + 256],
        w_in[:, mla + 256:mla + 384],
        z(64), w_in[:, mla + 384:mla + 416], z(32),
        w_in[:, dsa:dsa + 512],
        w_in[:, dsa + 584:dsa + 1352],
        w_in[:, gdn + 768:gdn + 776], z(120),
        w_in[:, dsa + 512:dsa + 584], z(56),
        w_in[:, gate:gate + 4096],
    ]
    return jnp.concatenate(cols, axis=1).astype(dtype)


def _mla_weights(w_uq, w_ukv):
    wq = w_uq.reshape(MLA_Q_LORA, HEADS, MLA_NOPE + MLA_ROPE)
    wq = jnp.pad(wq, ((0, 0), (0, 0), (0, MLA_QK - MLA_NOPE - MLA_ROPE))).reshape(MLA_Q_LORA, HEADS * MLA_QK)
    wkv = w_ukv.reshape(MLA_KV_LORA, HEADS, MLA_NOPE + DH)
    wk = jnp.pad(wkv[:, :, :MLA_NOPE], ((0, 0), (0, 0), (0, MLA_QK - MLA_NOPE))).reshape(MLA_KV_LORA, HEADS * MLA_QK)
    wvt = wkv[:, :, MLA_NOPE:].reshape(MLA_KV_LORA, HW).T
    return wq.astype(BF16), wk.astype(BF16), wvt.astype(BF16)


def _rope_tables(positions, rot_dims, offset, period):
    half = rot_dims // 2
    inv_freq = ROPE_THETA ** (-jnp.arange(0, rot_dims, 2, dtype=F32) / rot_dims)
    ang = positions.astype(F32)[..., None] * inv_freq
    cos, sin = jnp.cos(ang), jnp.sin(ang)
    b, s, _ = cos.shape
    lead = jnp.zeros((b, s, offset), F32)
    tail = jnp.zeros((b, s, period - offset - rot_dims), F32)
    zh = jnp.zeros((b, s, half), F32)
    cos_t = jnp.concatenate([lead + 1.0, cos, cos, tail + 1.0], axis=-1)
    sin_lo = jnp.concatenate([lead, -sin, zh, tail], axis=-1)
    sin_hi = jnp.concatenate([lead, zh, sin, tail], axis=-1)
    tab = jnp.concatenate([cos_t, sin_lo, sin_hi], axis=-1)
    return tab.reshape(b * s, 3 * period), cos, sin


def _layer(x2d, b, s, prm, tabs, final_g, is_last):
    (mix_g, w_slab, conv_w, gdn_pcol, gdn_prow, gdn_ng, mla_qg, mla_kvg, wq, wk, wv,
     rwkv_mu, rwkv_vec, ww2, wa2, wg2, wo_gdn, wo_mla, wo_rwkv, wo_dsa, w_o, mlp_g, w_up, w_down) = prm
    mla_tab, dsa_tab, dsa_cst = tabs
    t = b * s
    slab = _in_proj(x2d, mix_g, w_slab[:, :OFF_GATE], F32)
    gates = _in_proj(x2d, mix_g, w_slab[:, OFF_GATE:], BF16)

    ab = slab[:, OFF_GDN_AB:OFF_GDN_AB + 8].reshape(b, s // CHUNK, CHUNK, 8)
    abr = jnp.swapaxes(ab, 2, 3)
    o_gdn = _gdn(slab, abr, conv_w, gdn_pcol, gdn_prow, gdn_ng, b, s)
    q, k, v = _mla_prep(slab, mla_tab, mla_qg, mla_kvg, wq, wk, wv)
    o_mla = _flash(q, k, v, b, s)
    o_rwkv = _rwkv(slab, rwkv_mu, rwkv_vec, ww2, wa2, wg2, b, s)
    ik, kr, vt = _dsa_key_prep(slab, dsa_tab)
    o_dsa = _dsa(slab, dsa_cst, ik, kr, vt, b, s, min(IDX_TOPK_MAX, s // 4))

    x2d = _merge(x2d, o_gdn, o_mla, o_rwkv, o_dsa, gates, wo_gdn, wo_mla, wo_rwkv, wo_dsa, w_o)
    return _mlp(x2d, mlp_g, w_up, w_down, final_g, is_last)


def kernel(x, positions, mix_norm_g, w_in, gdn_conv_w, gdn_a_log, gdn_dt_bias, gdn_norm_g, mla_q_norm_g, mla_kv_norm_g, mla_w_uq, mla_w_ukv, rwkv_mu, rwkv_w0, rwkv_w_w2, rwkv_a0, rwkv_w_a2, rwkv_w_g2, rwkv_k_k, rwkv_k_a, rwkv_r_k, rwkv_ln_w, rwkv_ln_b, w_out_gdn, w_out_mla, w_out_rwkv, w_out_dsa, w_o, mlp_norm_g, w_up, w_down, final_norm_g):
    b, s, d = x.shape
    depth = w_in.shape[0]
    mla_tab, _, _ = _rope_tables(positions, MLA_ROPE, MLA_NOPE, MLA_QK)
    dsa_tab, cos8, sin8 = _rope_tables(positions, DH // 4, 0, 128 // 2)
    dsa_tab = jnp.concatenate([jnp.concatenate([dsa_tab[:, i * 64:(i + 1) * 64]] * 2, axis=1) for i in range(3)], axis=1)
    dsa_cst = jnp.swapaxes(jnp.concatenate([cos8, sin8], axis=-1), 1, 2)
    tabs = (mla_tab, dsa_tab, dsa_cst)
    row = lambda a: a.reshape(1, -1).astype(F32)
    pad8 = lambda rows: jnp.concatenate(rows + [jnp.zeros((8 - len(rows), rows[0].shape[1]), F32)], axis=0)
    x2d = x.reshape(b * s, d)
    final_g = row(final_norm_g)
    for l in range(depth):
        lane4 = lambda a: jnp.pad(a.astype(F32), (0, 124)).reshape(1, 128)
        gdn_pcol = pad8([lane4(gdn_a_log[l]), lane4(gdn_dt_bias[l])])
        gdn_prow = jnp.pad(jnp.stack([gdn_a_log[l], gdn_dt_bias[l]], axis=1).astype(F32), ((0, 4), (0, 126)))
        wq, wk, wv = _mla_weights(mla_w_uq[l], mla_w_ukv[l])
        rwkv_vec = pad8([row(rwkv_w0[l]), row(rwkv_a0[l]), row(rwkv_k_k[l]), row(rwkv_k_a[l]),
                         row(rwkv_ln_w[l]), row(rwkv_ln_b[l]), row(rwkv_r_k[l])])
        prm = (row(mix_norm_g[l]), _slab_weight(w_in[l]), gdn_conv_w[l].astype(F32), gdn_pcol, gdn_prow,
               row(jnp.tile(gdn_norm_g[l], HEADS)), row(mla_q_norm_g[l]), row(mla_kv_norm_g[l]), wq, wk, wv,
               row(rwkv_mu[l]), rwkv_vec, rwkv_w_w2[l].astype(F32), rwkv_w_a2[l].astype(F32),
               rwkv_w_g2[l].astype(F32),
               w_out_gdn[l].astype(BF16), w_out_mla[l].astype(BF16), w_out_rwkv[l].astype(BF16),
               w_out_dsa[l].astype(BF16), w_o[l].astype(BF16), row(mlp_norm_g[l]),
               w_up[l].astype(BF16), w_down[l].astype(BF16))
        x2d = _layer(x2d, b, s, prm, tabs, final_g, l == depth - 1)
    return x2d.reshape(b, s, d)
```

```python
import functools
import math

import numpy as np
import jax
import jax.numpy as jnp
from jax import lax
from jax.experimental import pallas as pl
from jax.experimental.pallas import tpu as pltpu

F32 = jnp.float32
BF16 = jnp.bfloat16
I32 = jnp.int32
I16 = jnp.int16

D_MODEL = 1024
DEPTH = 4
ROPE_THETA = 500000.0
NORM_EPS = 1e-6
NEG_INF = -1e30
HEADS = 4
DH = 64
HW = HEADS * DH
GDN_CONV = 4
CHUNK = 64
MLA_Q_LORA = 256
MLA_KV_LORA = 128
MLA_NOPE = 64
MLA_ROPE = 32
MLA_QK = 128
RWKV_GN_EPS = 64e-5
IDX_HEADS = 8
IDX_TOPK_MAX = 256
D_FF = 4 * D_MODEL
INT_MIN = -2 ** 31
LOG2E = 1.4426950408889634

VMEM_LIMIT_BYTES = 56 * 1024 * 1024

OFF_RWKV = 0
OFF_GDN_Q = 1024
OFF_GDN_K = 1280
OFF_GDN_V = 1536
OFF_GDN_Z = 1792
OFF_MLA_CQ = 2048
OFF_MLA_CKV = 2304
OFF_MLA_KR = 2432
OFF_DSA_IQ = 2560
OFF_DSA_Q = 3072
OFF_DSA_K = 3328
OFF_DSA_V = 3584
OFF_GDN_AB = 3840
OFF_DSA_IKW = 3968
OFF_GATE = 4096
N_SLAB = 8192

NN = (((1,), (0,)), ((), ()))
NT = (((1,), (1,)), ((), ()))
TN = (((0,), (0,)), ((), ()))
BNN = (((2,), (1,)), ((0,), (0,)))
BNT = (((2,), (2,)), ((0,), (0,)))
BTN = (((1,), (1,)), ((0,), (0,)))


def _cparams(n_axes):
    return pltpu.CompilerParams(dimension_semantics=("arbitrary",) * n_axes,
                                vmem_limit_bytes=VMEM_LIMIT_BYTES)


def _mm(a, b, dims=NN):
    return lax.dot_general(a.astype(BF16), b.astype(BF16), dims, preferred_element_type=F32)


def _mm_exact_lhs(m, x, dims=NN):
    mb = m.astype(BF16)
    x0 = x.astype(BF16)
    r1 = x - x0.astype(F32)
    x1 = r1.astype(BF16)
    x2 = (r1 - x1.astype(F32)).astype(BF16)
    d = lambda y: lax.dot_general(mb, y, dims, preferred_element_type=F32)
    return d(x0) + (d(x1) + d(x2))


def _mm_exact_rhs(x, m, dims=NN):
    mb = m.astype(BF16)
    x0 = x.astype(BF16)
    r1 = x - x0.astype(F32)
    x1 = r1.astype(BF16)
    x2 = (r1 - x1.astype(F32)).astype(BF16)
    d = lambda y: lax.dot_general(y, mb, dims, preferred_element_type=F32)
    return d(x0) + (d(x1) + d(x2))


def _iota2(shape, dim):
    return lax.broadcasted_iota(I32, shape, dim)


def _seg_ones(width, seg):
    sh = int(math.log2(seg))
    r = _iota2((width, width), 0) >> sh
    c = _iota2((width, width), 1) >> sh
    return jnp.where(r == c, 1.0, 0.0).astype(F32)


def _seg_sum(x, seg_ones):
    return _mm_exact_rhs(x, seg_ones)


def _softplus(x):
    return jnp.maximum(x, 0.0) + jnp.log1p(jnp.exp(-jnp.abs(x)))


def _sigmoid(x):
    return 1.0 / (1.0 + jnp.exp(-x))


def _silu(x):
    return x * _sigmoid(x)


def _tri_masks(n):
    r = _iota2((n, n), 0)
    c = _iota2((n, n), 1)
    return r, c


def _unit_lower_inverse(m, r, c, dims=NN):
    eye = jnp.where(r == c, 1.0, 0.0).astype(F32)
    same16 = (r >> 4) == (c >> 4)
    same32 = (r >> 5) == (c >> 5)
    md = jnp.where(same16, m, 0.0)
    t = eye - md
    p = md
    for _ in range(3):
        p = _mm(p, p, dims)
        t = t + _mm(t, p, dims)
    c32 = jnp.where(same32 & jnp.logical_not(same16), m, 0.0)
    t = t - _mm(_mm(t, c32, dims), t, dims)
    c64 = jnp.where(same32, 0.0, m)
    t = t - _mm(_mm(t, c64, dims), t, dims)
    return t


IN_PROJ_TN = 512


def _in_proj_kernel(x_ref, g_ref, w_ref, o_ref):
    x = x_ref[...]
    ms = jnp.mean(x * x, axis=-1, keepdims=True)
    h = (x * lax.rsqrt(ms + NORM_EPS) * g_ref[...]).astype(BF16)
    for n0 in range(0, o_ref.shape[1], IN_PROJ_TN):
        o_ref[:, n0:n0 + IN_PROJ_TN] = jnp.dot(
            h, w_ref[:, n0:n0 + IN_PROJ_TN], preferred_element_type=F32).astype(o_ref.dtype)


def _in_proj(x2d, g, w, out_dtype):
    t, d = x2d.shape
    n = w.shape[1]
    tm = min(512, t)
    return pl.pallas_call(
        _in_proj_kernel,
        grid=(t // tm,),
        in_specs=[pl.BlockSpec((tm, d), lambda i: (i, 0)),
                  pl.BlockSpec((1, d), lambda i: (0, 0)),
                  pl.BlockSpec((d, n), lambda i: (0, 0))],
        out_specs=pl.BlockSpec((tm, n), lambda i: (i, 0)),
        out_shape=jax.ShapeDtypeStruct((t, n), out_dtype),
        compiler_params=_cparams(1),
        name="in_proj",
    )(x2d, g, w)


def _to_groups(x, nchunk):
    parts = [x[:, h * DH:(h + 1) * DH].reshape(nchunk, 1, CHUNK, DH) for h in range(HEADS)]
    return jnp.concatenate(parts, axis=1).reshape(nchunk * HEADS, CHUNK, DH)


def _col_groups(x, lane0, nchunk):
    parts = [x[:, lane0 + h:lane0 + h + 1].reshape(nchunk, 1, CHUNK, 1) for h in range(HEADS)]
    return jnp.concatenate(parts, axis=1).reshape(nchunk * HEADS, CHUNK, 1)


def _gdn_kernel(q_ref, k_ref, v_ref, z_ref, ab_ref, abr_ref, cw_ref, pcol_ref, prow_ref, ng_ref,
                o_ref, xbuf_ref, state_ref, u_ref, w_ref, a_ref, qd_ref, kd_ref, egl_ref, oraw_ref,
                *, tb):
    nchunk = tb // CHUNK

    @pl.when(pl.program_id(1) == 0)
    def _():
        xbuf_ref[0:8, :] = jnp.zeros((8, 3 * HW), F32)
        state_ref[...] = jnp.zeros_like(state_ref)

    xbuf_ref[8:8 + tb, 0:HW] = q_ref[...]
    xbuf_ref[8:8 + tb, HW:2 * HW] = k_ref[...]
    xbuf_ref[8:8 + tb, 2 * HW:3 * HW] = v_ref[...]
    cw = cw_ref[...]
    y = xbuf_ref[8:8 + tb, :] * cw[3:4, :]
    y = y + xbuf_ref[7:7 + tb, :] * cw[2:3, :]
    y = y + xbuf_ref[6:6 + tb, :] * cw[1:2, :]
    y = y + xbuf_ref[5:5 + tb, :] * cw[0:1, :]
    xbuf_ref[0:8, :] = xbuf_ref[tb:tb + 8, :]
    y = _silu(y)
    seg = _seg_ones(HW, DH)
    q = y[:, 0:HW]
    k = y[:, HW:2 * HW]
    q = q * lax.rsqrt(_seg_sum(q * q, seg) + 1e-6) * (DH ** -0.5)
    k = k * lax.rsqrt(_seg_sum(k * k, seg) + 1e-6)
    v = y[:, 2 * HW:3 * HW]

    ab = ab_ref[...]
    pcol = pcol_ref[...]
    g_col = -jnp.exp(pcol[0:1, :]) * _softplus(ab + pcol[1:2, :])
    beta_col = _sigmoid(ab)

    r, c = _tri_masks(CHUNK)
    incl = c <= r
    strict = c < r
    l_incl = jnp.where(incl, 1.0, 0.0).astype(F32)

    rb = _iota2((tb, tb), 0)
    cb = _iota2((tb, tb), 1)
    l_blk = jnp.where(((rb >> 6) == (cb >> 6)) & (cb <= rb), 1.0, 0.0).astype(F32)
    gc_col = _mm_exact_lhs(l_blk, g_col)
    prow = prow_ref[...]
    abr = abr_ref[0].reshape(nchunk * 8, CHUNK)
    prow_t = jnp.concatenate([prow] * nchunk, axis=0)
    g_row = -jnp.exp(prow_t[:, 0:1]) * _softplus(abr + prow_t[:, 1:2])
    gc_row = _mm_exact_rhs(g_row, l_incl, NT).reshape(nchunk, 8, CHUNK)

    qg, kg, vg = _to_groups(q, nchunk), _to_groups(k, nchunk), _to_groups(v, nchunk)
    beta = _col_groups(beta_col, 4, nchunk)
    gcol = _col_groups(gc_col, 0, nchunk)
    grow = jnp.concatenate([gc_row[:, h:h + 1, :].reshape(nchunk, 1, 1, CHUNK) for h in range(HEADS)],
                           axis=1).reshape(nchunk * HEADS, 1, CHUNK)
    decay = jnp.where(incl, jnp.exp(jnp.where(incl, gcol - grow, 0.0)), 0.0)
    kb = kg * beta
    m = jnp.where(strict, _mm(kb, kg, BNT) * decay, 0.0)
    t_inv = _unit_lower_inverse(m, r, c, BNN)
    egc = jnp.exp(gcol)
    glast = gcol[:, CHUNK - 1:CHUNK, :]
    uw = _mm(t_inv, jnp.concatenate([vg * beta, kb * egc], axis=2), BNN)
    u_ref[...] = uw[:, :, 0:DH]
    w_ref[...] = uw[:, :, DH:2 * DH]
    a_ref[...] = _mm(qg, kg, BNT) * decay
    qd_ref[...] = qg * egc
    kd_ref[...] = kg * jnp.exp(glast - gcol)
    egl_ref[...] = jnp.broadcast_to(jnp.exp(glast), (nchunk * HEADS, 1, DH))

    def chunk_body(ci, carry):
        sl = pl.ds(pl.multiple_of(ci * HEADS, HEADS), HEADS)
        s = state_ref[...]
        v_new = u_ref[sl] - _mm(w_ref[sl], s, BNN)
        o = _mm(qd_ref[sl], s, BNN) + _mm(a_ref[sl], v_new, BNN)
        state_ref[...] = s * egl_ref[sl] + _mm(kd_ref[sl], v_new, BTN)
        oraw_ref[pl.ds(pl.multiple_of(ci * CHUNK, CHUNK), CHUNK), :] = jnp.concatenate(
            [o[h] for h in range(HEADS)], axis=1)
        return carry

    lax.fori_loop(0, nchunk, chunk_body, 0)
    o_all = oraw_ref[...]
    ms = _seg_sum(o_all * o_all, seg) * (1.0 / DH)
    o_ref[...] = o_all * lax.rsqrt(ms + NORM_EPS) * ng_ref[...] * _silu(z_ref[...])


GDN_TB = 512


def _gdn(slab, abr, conv_w, pcol, prow, norm_g, b, s):
    tb = min(GDN_TB, s)
    nb = s // tb
    nchunk = tb // CHUNK
    ng = nchunk * HEADS
    grp = lambda w: pltpu.VMEM((ng, CHUNK, w), F32)
    col = lambda off, w: pl.BlockSpec((tb, w), lambda i, j, o=off // w: (i * nb + j, o))
    full = lambda shape: pl.BlockSpec(shape, lambda i, j: (0,) * len(shape))
    return pl.pallas_call(
        functools.partial(_gdn_kernel, tb=tb),
        grid=(b, nb),
        in_specs=[col(OFF_GDN_Q, HW), col(OFF_GDN_K, HW), col(OFF_GDN_V, HW), col(OFF_GDN_Z, HW),
                  col(OFF_GDN_AB, 128),
                  pl.BlockSpec((1, nchunk, 8, CHUNK), lambda i, j: (i, j, 0, 0)),
                  full((GDN_CONV, 3 * HW)), full((8, 128)), full((8, 128)), full((1, HW))],
        out_specs=pl.BlockSpec((tb, HW), lambda i, j: (i * nb + j, 0)),
        out_shape=jax.ShapeDtypeStruct((b * s, HW), F32),
        scratch_shapes=[pltpu.VMEM((tb + 8, 3 * HW), F32), pltpu.VMEM((HEADS, DH, DH), F32),
                        grp(DH), grp(DH), grp(CHUNK), grp(DH), grp(DH),
                        pltpu.VMEM((ng, 1, DH), F32), pltpu.VMEM((tb, HW), F32)],
        compiler_params=_cparams(2),
        name="gdn",
    )(slab, slab, slab, slab, slab, abr, conv_w, pcol, prow, norm_g)


def _rwkv_kernel(p_ref, mu_ref, vec_ref, ww2_ref, wa2_ref, wg2_ref, o_ref,
                 xbuf_ref, state_ref, ta_ref, tpv_ref, rt_ref, qkv_ref, qb_ref, bc_ref, ktv_ref,
                 el_ref, oraw_ref, *, tb):
    nchunk = tb // CHUNK

    @pl.when(pl.program_id(1) == 0)
    def _():
        xbuf_ref[0:8, :] = jnp.zeros((8, 4 * HW), F32)
        state_ref[...] = jnp.zeros_like(state_ref)

    p = p_ref[...]
    xbuf_ref[8:8 + tb, :] = p
    prev = xbuf_ref[7:7 + tb, :]
    xbuf_ref[0:8, :] = xbuf_ref[tb:tb + 8, :]
    pm = p + (prev - p) * mu_ref[...]
    vec = vec_ref[...]
    w0, a0, k_k, k_a = vec[0:1, :], vec[1:2, :], vec[2:3, :], vec[3:4, :]
    rr = pm[:, 0:HW]
    k = pm[:, HW:2 * HW]
    v = pm[:, 2 * HW:3 * HW]
    xw = pm[:, 768:832]
    xa = pm[:, 832:896]
    xg = pm[:, 896:1024]
    w_log = -_softplus(-(w0 + _mm(jnp.tanh(xw), ww2_ref[...]))) - 0.5
    lw = -jnp.exp(w_log)
    a = _sigmoid(a0 + _mm(xa, wa2_ref[...]))
    g = _mm(_sigmoid(xg), wg2_ref[...])
    seg = _seg_ones(HW, DH)
    kk = k * k_k
    kk = kk * lax.rsqrt(_seg_sum(kk * kk, seg) + 1e-6)
    k2 = k * (1.0 + (a - 1.0) * k_a)
    kka = kk * a

    r_i, c_i = _tri_masks(CHUNK)
    incl = c_i <= r_i
    strict = c_i < r_i
    rb = _iota2((tb, tb), 0)
    cb = _iota2((tb, tb), 1)
    l_blk = jnp.where(((rb >> 6) == (cb >> 6)) & (cb <= rb), 1.0, 0.0).astype(F32)
    lwc_all = _mm_exact_lhs(l_blk, lw)

    rg, kg, vg = _to_groups(rr, nchunk), _to_groups(k2, nchunk), _to_groups(v, nchunk)
    kkg, kkag = _to_groups(kk, nchunk), _to_groups(kka, nchunk)
    lwg, lwc = _to_groups(lw, nchunk), _to_groups(lwc_all, nchunk)
    lwl = lwc[:, CHUNK - 1:CHUNK, :]
    e_neg = jnp.exp(-lwc)
    at = kkg * jnp.exp(lwc - lwg)
    rt = rg * jnp.exp(lwc)
    y2 = jnp.concatenate([at, rt], axis=1)
    x2 = jnp.concatenate([kkag * e_neg, kg * e_neg], axis=1)
    gram = _mm(y2, x2, BNT)
    n_m = jnp.where(strict, gram[:, 0:CHUNK, 0:CHUNK], 0.0)
    p_m = jnp.where(strict, gram[:, 0:CHUNK, CHUNK:], 0.0)
    qb_m = jnp.where(incl, gram[:, CHUNK:, 0:CHUNK], 0.0)
    qk_m = jnp.where(incl, gram[:, CHUNK:, CHUNK:], 0.0)
    t_inv = _unit_lower_inverse(n_m, r_i, c_i, BNN)
    ta_ref[...] = _mm(t_inv, at, BNN)
    tpv_ref[...] = _mm(t_inv, _mm(p_m, vg, BNN), BNN)
    rt_ref[...] = rt
    qkv_ref[...] = _mm(qk_m, vg, BNN)
    qb_ref[...] = qb_m
    e_rel = jnp.exp(lwl - lwc)
    bc_ref[...] = kkag * e_rel
    ktv_ref[...] = _mm(vg, kg * e_rel, BTN)
    el_ref[...] = jnp.exp(lwl)

    def chunk_body(ci, carry):
        sl = pl.ds(pl.multiple_of(ci * HEADS, HEADS), HEADS)
        zt = state_ref[...]
        u = _mm(ta_ref[sl], zt, BNT) + tpv_ref[sl]
        o = _mm(rt_ref[sl], zt, BNT) + qkv_ref[sl] - _mm(qb_ref[sl], u, BNN)
        state_ref[...] = zt * el_ref[sl] + ktv_ref[sl] - _mm(u, bc_ref[sl], BTN)
        oraw_ref[pl.ds(pl.multiple_of(ci * CHUNK, CHUNK), CHUNK), :] = jnp.concatenate(
            [o[h] for h in range(HEADS)], axis=1)
        return carry

    lax.fori_loop(0, nchunk, chunk_body, 0)
    ln_w, ln_b, r_k = vec[4:5, :], vec[5:6, :], vec[6:7, :]
    o_all = oraw_ref[...]
    mean = _seg_sum(o_all, seg) * (1.0 / DH)
    cen = o_all - mean
    var = _seg_sum(cen * cen, seg) * (1.0 / DH)
    o_n = cen * lax.rsqrt(var + RWKV_GN_EPS) * ln_w + ln_b
    bonus = _seg_sum(rr * k2 * r_k, seg) * v
    o_ref[...] = (o_n + bonus) * g


RWKV_TB = 512


def _rwkv(slab, mu, vec, ww2, wa2, wg2, b, s):
    tb = min(RWKV_TB, s)
    nb = s // tb
    ng = (tb // CHUNK) * HEADS
    grp = lambda: pltpu.VMEM((ng, CHUNK, DH), F32)
    full = lambda shape: pl.BlockSpec(shape, lambda i, j: (0,) * len(shape))
    return pl.pallas_call(
        functools.partial(_rwkv_kernel, tb=tb),
        grid=(b, nb),
        in_specs=[pl.BlockSpec((tb, 4 * HW), lambda i, j: (i * nb + j, OFF_RWKV // (4 * HW))),
                  full((1, 4 * HW)), full((8, HW)), full((64, HW)), full((64, HW)), full((128, HW))],
        out_specs=pl.BlockSpec((tb, HW), lambda i, j: (i * nb + j, 0)),
        out_shape=jax.ShapeDtypeStruct((b * s, HW), F32),
        scratch_shapes=[pltpu.VMEM((tb + 8, 4 * HW), F32), pltpu.VMEM((HEADS, DH, DH), F32),
                        grp(), grp(), grp(), grp(), grp(), grp(), grp(),
                        pltpu.VMEM((ng, 1, DH), F32), pltpu.VMEM((tb, HW), F32)],
        compiler_params=_cparams(2),
        name="rwkv7",
    )(slab, mu, vec, ww2, wa2, wg2)


def _rope(x, cos_t, sin_lo, sin_hi, half):
    w = x.shape[-1]
    return x * cos_t + pltpu.roll(x, w - half, 1) * sin_lo + pltpu.roll(x, half, 1) * sin_hi


def _tile_lanes(t, n):
    return t if n == 1 else jnp.concatenate([t] * n, axis=1)


def _mla_prep_kernel(cq_ref, ckv_ref, kr_ref, cs_ref, e_ref, base_ref, qg_ref, kvg_ref, wq_ref, wk_ref,
                     wvt_ref, q_ref, k_ref, vt_ref):
    cq = cq_ref[...]
    hq = cq * lax.rsqrt(jnp.mean(cq * cq, axis=-1, keepdims=True) + NORM_EPS) * qg_ref[...]
    ckv = ckv_ref[...]
    hkv = ckv * lax.rsqrt(jnp.mean(ckv * ckv, axis=-1, keepdims=True) + NORM_EPS) * kvg_ref[...]
    cos_t, sin_lo, sin_hi = _rope_table(cs_ref, e_ref, base_ref)
    q = _mm(hq, wq_ref[...])
    q = _rope(q, _tile_lanes(cos_t, HEADS), _tile_lanes(sin_lo, HEADS), _tile_lanes(sin_hi, HEADS),
              MLA_ROPE // 2)
    q_ref[...] = (q * ((MLA_NOPE + MLA_ROPE) ** -0.5 * LOG2E)).astype(BF16)
    kr = _rope(kr_ref[...], cos_t, sin_lo, sin_hi, MLA_ROPE // 2)
    k = _mm(hkv, wk_ref[...]) + _tile_lanes(kr, HEADS)
    k_ref[...] = k.astype(BF16)
    vt_ref[...] = lax.dot_general(wvt_ref[...], hkv.astype(BF16), NT,
                                  preferred_element_type=F32).astype(BF16)


def _mla_prep(slab, cs, qg, kvg, wq, wk, wvt):
    t = slab.shape[0]
    tm = min(512, t)
    full = lambda shape: pl.BlockSpec(shape, lambda i: (0,) * len(shape))
    e, base = _rope_expansion(MLA_ROPE, MLA_NOPE, MLA_QK)
    return pl.pallas_call(
        _mla_prep_kernel,
        grid=(t // tm,),
        in_specs=[pl.BlockSpec((tm, 256), lambda i: (i, OFF_MLA_CQ // 256)),
                  pl.BlockSpec((tm, 128), lambda i: (i, OFF_MLA_CKV // 128)),
                  pl.BlockSpec((tm, 128), lambda i: (i, OFF_MLA_KR // 128)),
                  pl.BlockSpec((tm, MLA_ROPE), lambda i: (i, 0)), full(e.shape), full(base.shape),
                  full((1, 256)), full((1, 128)), full((256, 512)), full((128, 512)), full((256, 128))],
        out_specs=[pl.BlockSpec((tm, 512), lambda i: (i, 0)), pl.BlockSpec((tm, 512), lambda i: (i, 0)),
                   pl.BlockSpec((256, tm), lambda i: (0, i))],
        out_shape=[jax.ShapeDtypeStruct((t, 512), BF16), jax.ShapeDtypeStruct((t, 512), BF16),
                   jax.ShapeDtypeStruct((256, t), BF16)],
        compiler_params=_cparams(1),
        name="mla_prep",
    )(slab, slab, slab, cs, e, base, qg, kvg, wq, wk, wvt)


def _flash_kernel(q_ref, k_ref, vt_ref, o_ref, m_ref, l_ref, acc_ref, *, tq, tk):
    qi = pl.program_id(1)
    ki = pl.program_id(2)

    @pl.when(ki == 0)
    def _():
        m_ref[...] = jnp.full_like(m_ref, NEG_INF)
        l_ref[...] = jnp.zeros_like(l_ref)
        acc_ref[...] = jnp.zeros_like(acc_ref)

    def step(on_diagonal):
        if on_diagonal:
            causal = _iota2((tk, tq), 0) <= _iota2((tk, tq), 1)
        vt = vt_ref[...]
        ones = jnp.ones((16, tk), BF16)
        scores = [lax.dot_general(k_ref[:, h * MLA_QK:(h + 1) * MLA_QK], q_ref[:, h * MLA_QK:(h + 1) * MLA_QK],
                                  NT, preferred_element_type=F32) for h in range(HEADS)]
        for h in range(HEADS):
            s = scores[h]
            if on_diagonal:
                s = jnp.where(causal, s, NEG_INF)
            m_old = m_ref[h]
            m_new = jnp.maximum(m_old, jnp.max(s, axis=0, keepdims=True))
            alpha = jnp.exp2(m_old - m_new)
            p = jnp.exp2(s - m_new).astype(BF16)
            pv = jnp.dot(jnp.concatenate([vt[h * DH:(h + 1) * DH, :], ones], axis=0), p,
                         preferred_element_type=F32)
            l_ref[h] = alpha * l_ref[h] + pv[DH:DH + 1, :]
            acc_ref[h * DH:(h + 1) * DH, :] = alpha * acc_ref[h * DH:(h + 1) * DH, :] + pv[0:DH, :]
            m_ref[h] = m_new

    pl.when(ki < qi)(lambda: step(False))
    pl.when(ki == qi)(lambda: step(True))

    @pl.when(ki == pl.num_programs(2) - 1)
    def _():
        o_t = jnp.concatenate([acc_ref[h * DH:(h + 1) * DH, :] / l_ref[h] for h in range(HEADS)], axis=0)
        o_ref[...] = o_t.T


def _flash(q, k, vt, b, s):
    tq = min(512, s)
    tk = tq
    nq = s // tq
    nk = s // tk
    return pl.pallas_call(
        functools.partial(_flash_kernel, tq=tq, tk=tk),
        grid=(b, nq, nk),
        in_specs=[pl.BlockSpec((tq, 512), lambda i, j, kk: (i * nq + j, 0)),
                  pl.BlockSpec((tk, 512), lambda i, j, kk: (i * nk + jnp.minimum(kk, j), 0)),
                  pl.BlockSpec((256, tk), lambda i, j, kk: (0, i * nk + jnp.minimum(kk, j)))],
        out_specs=pl.BlockSpec((tq, 256), lambda i, j, kk: (i * nq + j, 0)),
        out_shape=jax.ShapeDtypeStruct((b * s, 256), F32),
        scratch_shapes=[pltpu.VMEM((HEADS, 1, tq), F32), pltpu.VMEM((HEADS, 1, tq), F32),
                        pltpu.VMEM((HW, tq), F32)],
        compiler_params=_cparams(3),
        name="mla_flash",
    )(q, k, vt)


DSA_TQ = 256
DSA_TK = 512


def _dsa_prep_kernel(iq_ref, ikw_ref, q_ref, k_ref, v_ref, cs_ref, e_ref, base_ref,
                     ik_ref, ko_ref, vt_ref, iqt_ref, qt_ref, iwt_ref):
    cos_t, sin_lo, sin_hi = _rope_table(cs_ref, e_ref, base_ref)
    half = DH // 8
    rope = lambda x, n: _rope(x, _tile_lanes(cos_t, n), _tile_lanes(sin_lo, n), _tile_lanes(sin_hi, n), half)
    ikw = ikw_ref[...]
    ik_ref[...] = rope(ikw, 1)[:, 0:DH].astype(BF16)
    kr = rope(k_ref[...], 2)
    for h in range(HEADS):
        ko_ref[h] = kr[:, h * DH:(h + 1) * DH].astype(BF16)
    vt_ref[0] = v_ref[...].T.astype(BF16)
    iqt_ref[...] = rope(iq_ref[...], 4).T.astype(BF16)
    qt_ref[...] = (rope(q_ref[...], 2) * (DH ** -0.5 * LOG2E)).T.astype(BF16)
    iwt_ref[...] = ikw.T[DH:DH + 8, :] * (IDX_HEADS ** -0.5 * DH ** -0.5)


def _dsa_prep(slab, cs):
    t = slab.shape[0]
    tm = min(DSA_TK, t)
    full = lambda shape: pl.BlockSpec(shape, lambda i: (0,) * len(shape))
    e, base = _rope_expansion(DH // 4, 0, DH)
    return pl.pallas_call(
        _dsa_prep_kernel,
        grid=(t // tm,),
        in_specs=[pl.BlockSpec((tm, IDX_HEADS * DH), lambda i: (i, OFF_DSA_IQ // (IDX_HEADS * DH))),
                  pl.BlockSpec((tm, 128), lambda i: (i, OFF_DSA_IKW // 128)),
                  pl.BlockSpec((tm, 256), lambda i: (i, OFF_DSA_Q // 256)),
                  pl.BlockSpec((tm, 256), lambda i: (i, OFF_DSA_K // 256)),
                  pl.BlockSpec((tm, 256), lambda i: (i, OFF_DSA_V // 256)),
                  pl.BlockSpec((tm, DH // 4), lambda i: (i, 0)), full(e.shape), full(base.shape)],
        out_specs=[pl.BlockSpec((tm, DH), lambda i: (i, 0)),
                   pl.BlockSpec((HEADS, tm, DH), lambda i: (0, i, 0)),
                   pl.BlockSpec((1, HW, tm), lambda i: (i, 0, 0)),
                   pl.BlockSpec((IDX_HEADS * DH, tm), lambda i: (0, i)),
                   pl.BlockSpec((HW, tm), lambda i: (0, i)),
                   pl.BlockSpec((8, tm), lambda i: (0, i))],
        out_shape=[jax.ShapeDtypeStruct((t, DH), BF16), jax.ShapeDtypeStruct((HEADS, t, DH), BF16),
                   jax.ShapeDtypeStruct((t // tm, HW, tm), BF16),
                   jax.ShapeDtypeStruct((IDX_HEADS * DH, t), BF16), jax.ShapeDtypeStruct((HW, t), BF16),
                   jax.ShapeDtypeStruct((8, t), F32)],
        compiler_params=_cparams(1),
        name="dsa_prep",
    )(slab, slab, slab, slab, slab, cs, e, base)


def _dsa_kernel(iqt_ref, iwt_ref, qt_ref, ik_ref, k_ref, vt_ref, kidx_ref, o_ref,
                hi_ref, lo_ref, lom_ref, bias_ref, acc_ref, s_ref, m_ref, l_ref,
                *, tq, tk, n_sel, idx_bits):
    qi = pl.program_id(1)
    nkt = (qi * tq + tq - 1) // tk + 1
    iqt = iqt_ref[...]
    iw = iwt_ref[...]
    q_pos = qi * tq + _iota2((tk, tq), 1)

    def score_body(kt, carry):
        row0 = pl.multiple_of(kt * tk, tk)
        ik = ik_ref[pl.ds(row0, tk), :]
        sc = jnp.zeros((tk, tq), F32)
        for h in range(IDX_HEADS):
            lg = lax.dot_general(ik, iqt[h * DH:(h + 1) * DH, :], NN, preferred_element_type=F32)
            sc = sc + jnp.maximum(lg, 0.0) * iw[h:h + 1, :]
        bits = lax.bitcast_convert_type(sc, I32)
        key = bits ^ ((bits >> 31) & 0x7FFFFFFF)
        key = jnp.where(sc == 0.0, 0, key)
        k_pos = row0 + _iota2((tk, tq), 0)
        key = jnp.where(k_pos <= q_pos, key, INT_MIN)
        hi_ref[pl.ds(row0, tk), :] = (key >> 16).astype(I16)
        lo_ref[pl.ds(row0, tk), :] = ((key & 0xFFFF) - 32768).astype(I16)
        return carry

    lax.fori_loop(0, nkt, score_body, 0)

    one16 = jnp.ones((tk, tq), I16)
    zero16 = jnp.zeros((tk, tq), I16)

    def count(*pred_fns):
        def body(kt, accs):
            sl = pl.ds(pl.multiple_of(kt * tk, tk), tk)
            args = (hi_ref[sl, :], lo_ref[sl, :], lom_ref[sl, :], kidx_ref[sl, :])
            out = []
            for acc, fn in zip(accs, pred_fns):
                ones = jnp.where(fn(*args), one16, zero16).reshape(tk // 64, 64, tq)
                for g in range(tk // 64):
                    acc = acc + ones[g]
                out.append(acc)
            return tuple(out)
        parts = lax.fori_loop(0, nkt, body, tuple(jnp.zeros((64, tq), I16) for _ in pred_fns))
        sums = tuple(jnp.sum(p.astype(I32).astype(F32), axis=0, keepdims=True) for p in parts)
        return sums[0] if len(sums) == 1 else sums

    def bisect16(pred_ge):
        def body(it, thr):
            bit = jnp.maximum(15 - it, 0)
            cand = jnp.where(it == 0, jnp.zeros_like(thr), thr | jnp.left_shift(jnp.int32(1), bit))
            c16 = cand.astype(I16)
            cnt = count(lambda hi, lo, lom, idx: pred_ge(hi, lo, lom, c16))
            return jnp.where(cnt >= float(n_sel), cand, thr)
        return lax.fori_loop(0, 16, body, jnp.full((1, tq), -32768, I32))

    t_hi = bisect16(lambda hi, lo, lom, c: hi >= c)
    t_hi16 = t_hi.astype(I16)

    def mask_lo_body(kt, carry):
        sl = pl.ds(pl.multiple_of(kt * tk, tk), tk)
        hi = hi_ref[sl, :]
        lom_ref[sl, :] = jnp.where(hi > t_hi16, jnp.int16(32767),
                                   jnp.where(hi == t_hi16, lo_ref[sl, :], jnp.int16(-32768)))
        return carry

    lax.fori_loop(0, nkt, mask_lo_body, 0)
    t_lo16 = bisect16(lambda hi, lo, lom, c: lom >= c).astype(I16)

    def is_gt(hi, lo):
        return (hi > t_hi16) | ((hi == t_hi16) & (lo > t_lo16))

    def is_eq(hi, lo):
        return (hi == t_hi16) & (lo == t_lo16)

    cnt_gt, cnt_eq = count(lambda hi, lo, lom, idx: is_gt(hi, lo), lambda hi, lo, lom, idx: is_eq(hi, lo))
    need = float(n_sel) - cnt_gt

    def idx_body(it, j):
        cand = j + jnp.left_shift(jnp.int32(1), idx_bits - 1 - it)
        c16 = cand.astype(I16)
        cnt = count(lambda hi, lo, lom, idx: is_eq(hi, lo) & (idx < c16))
        return jnp.where(cnt < need, cand, j)

    excess = jnp.max(jnp.where(cnt_eq > need, 1.0, 0.0)) > 0.0
    j_cut16 = lax.cond(excess,
                       lambda: lax.fori_loop(0, idx_bits, idx_body, jnp.zeros((1, tq), I32)),
                       lambda: jnp.full((1, tq), 32767, I32)).astype(I16)

    def bias_body(kt, carry):
        sl = pl.ds(pl.multiple_of(kt * tk, tk), tk)
        hi, lo = hi_ref[sl, :], lo_ref[sl, :]
        sel = (is_gt(hi, lo) | (is_eq(hi, lo) & (kidx_ref[sl, :] <= j_cut16))) & (hi != jnp.int16(-32768))
        bias_ref[sl, :] = jnp.where(sel, jnp.zeros((tk, tq), BF16),
                                    jnp.full((tk, tq), NEG_INF, BF16)).astype(F32)
        return carry

    lax.fori_loop(0, nkt, bias_body, 0)

    qt = qt_ref[...]
    acc_ref[...] = jnp.zeros_like(acc_ref)

    def att_body(kt, carry):
        row0 = pl.multiple_of(kt * tk, tk)
        for h in range(HEADS):
            s_ref[h] = lax.dot_general(k_ref[h, pl.ds(row0, tk), :], qt[h * DH:(h + 1) * DH, :], NN,
                                       preferred_element_type=F32)

        @pl.when(kt >= 0)
        def _():
            bias = bias_ref[pl.ds(row0, tk), :]
            ones = jnp.ones((16, tk), BF16)
            for h in range(HEADS):
                hs = slice(h * DH, (h + 1) * DH)
                s = s_ref[h] + bias
                m_old = m_ref[h]
                m_new = jnp.maximum(m_old, jnp.max(s, axis=0, keepdims=True))
                alpha = jnp.exp2(m_old - m_new)
                p = jnp.exp2(s - m_new).astype(BF16)
                pv = lax.dot_general(jnp.concatenate([vt_ref[kt, hs, :], ones], axis=0), p, NN,
                                     preferred_element_type=F32)
                acc_ref[hs, :] = alpha * acc_ref[hs, :] + pv[0:DH, :]
                m_ref[h] = m_new
                l_ref[h] = alpha * l_ref[h] + pv[DH:DH + 1, :]
        return carry

    m_ref[...] = jnp.full_like(m_ref, NEG_INF)
    l_ref[...] = jnp.zeros_like(l_ref)
    lax.fori_loop(0, nkt, att_body, 0)
    o_t = jnp.concatenate([acc_ref[h * DH:(h + 1) * DH, :] / l_ref[h] for h in range(HEADS)], axis=0)
    o_ref[...] = o_t.T


def _dsa(iqt, iwt, qt, ik, kr, vt, b, s, n_sel):
    tq = min(DSA_TQ, s)
    tk = min(DSA_TK, s)
    nq = s // tq
    nkt = s // tk
    return pl.pallas_call(
        functools.partial(_dsa_kernel, tq=tq, tk=tk, n_sel=n_sel, idx_bits=int(math.log2(s))),
        grid=(b, nq),
        in_specs=[pl.BlockSpec((IDX_HEADS * DH, tq), lambda i, j: (0, i * nq + j)),
                  pl.BlockSpec((8, tq), lambda i, j: (0, i * nq + j)),
                  pl.BlockSpec((HW, tq), lambda i, j: (0, i * nq + j)),
                  pl.BlockSpec((s, DH), lambda i, j: (i, 0)),
                  pl.BlockSpec((HEADS, s, DH), lambda i, j: (0, i, 0)),
                  pl.BlockSpec((nkt, HW, tk), lambda i, j: (i, 0, 0)),
                  pl.BlockSpec((s, tq), lambda i, j: (0, 0))],
        out_specs=pl.BlockSpec((tq, HW), lambda i, j: (i * nq + j, 0)),
        out_shape=jax.ShapeDtypeStruct((b * s, HW), F32),
        scratch_shapes=[pltpu.VMEM((s, tq), I16), pltpu.VMEM((s, tq), I16), pltpu.VMEM((s, tq), I16),
                        pltpu.VMEM((s, tq), F32), pltpu.VMEM((HW, tq), F32),
                        pltpu.VMEM((HEADS, tk, tq), F32), pltpu.VMEM((HEADS, 1, tq), F32),
                        pltpu.VMEM((HEADS, 1, tq), F32)],
        compiler_params=_cparams(2),
        name="dsa",
    )(iqt, iwt, qt, ik, kr, vt,
      jnp.broadcast_to(jnp.arange(s, dtype=I16)[:, None], (s, tq)))


def _merge_kernel(x_ref, oa_ref, ob_ref, oc_ref, od_ref, ga_ref, gb_ref, gc_ref, gd_ref,
                  wa_ref, wb_ref, wc_ref, wd_ref, wo_ref, o_ref):
    def branch(o_r, g_r, w_r):
        y = jnp.dot(o_r[...].astype(BF16), w_r[...], preferred_element_type=F32)
        return _sigmoid(g_r[...].astype(F32)) * y

    merged = (branch(oa_ref, ga_ref, wa_ref) + branch(ob_ref, gb_ref, wb_ref)
              + branch(oc_ref, gc_ref, wc_ref) + branch(od_ref, gd_ref, wd_ref))
    o_ref[...] = x_ref[...] + jnp.dot(merged.astype(BF16), wo_ref[...], preferred_element_type=F32)


def _merge(x2d, oa, ob, oc, od, gates, wa, wb, wc, wd, wo):
    t, d = x2d.shape
    tm = min(512, t)
    row = lambda w: pl.BlockSpec((tm, w), lambda i: (i, 0))
    gate = lambda n: pl.BlockSpec((tm, d), lambda i, o=n: (i, o))
    full = lambda shape: pl.BlockSpec(shape, lambda i: (0,) * len(shape))
    return pl.pallas_call(
        _merge_kernel,
        grid=(t // tm,),
        in_specs=[row(d), row(HW), row(HW), row(HW), row(HW), gate(0), gate(1), gate(2), gate(3),
                  full((HW, d)), full((HW, d)), full((HW, d)), full((HW, d)), full((d, d))],
        out_specs=row(d),
        out_shape=jax.ShapeDtypeStruct((t, d), F32),
        compiler_params=_cparams(1),
        name="merge",
    )(x2d, oa, ob, oc, od, gates, gates, gates, gates, wa, wb, wc, wd, wo)


def _mlp_kernel(x_ref, g_ref, wu_ref, wd_ref, fg_ref, o_ref, h_ref, acc_ref, *, final_norm):
    fi = pl.program_id(1)

    @pl.when(fi == 0)
    def _():
        x = x_ref[...]
        ms = jnp.mean(x * x, axis=-1, keepdims=True)
        h_ref[...] = (x * lax.rsqrt(ms + NORM_EPS) * g_ref[...]).astype(BF16)
        acc_ref[...] = jnp.zeros_like(acc_ref)

    u = jnp.dot(h_ref[...], wu_ref[...], preferred_element_type=F32)
    u = jnp.square(jnp.maximum(u, 0.0))
    acc_ref[...] += jnp.dot(u.astype(BF16), wd_ref[...], preferred_element_type=F32)

    @pl.when(fi == pl.num_programs(1) - 1)
    def _():
        y = x_ref[...] + acc_ref[...]
        if final_norm:
            ms = jnp.mean(y * y, axis=-1, keepdims=True)
            y = y * lax.rsqrt(ms + NORM_EPS) * fg_ref[...]
        o_ref[...] = y


def _mlp(x2d, g, wu, wd, fg, final_norm):
    t, d = x2d.shape
    f = wu.shape[1]
    tm = min(1024, t)
    tf = 512
    return pl.pallas_call(
        functools.partial(_mlp_kernel, final_norm=final_norm),
        grid=(t // tm, f // tf),
        in_specs=[pl.BlockSpec((tm, d), lambda i, j: (i, 0)),
                  pl.BlockSpec((1, d), lambda i, j: (0, 0)),
                  pl.BlockSpec((d, tf), lambda i, j: (0, j)),
                  pl.BlockSpec((tf, d), lambda i, j: (j, 0)),
                  pl.BlockSpec((1, d), lambda i, j: (0, 0))],
        out_specs=pl.BlockSpec((tm, d), lambda i, j: (i, 0)),
        out_shape=jax.ShapeDtypeStruct((t, d), F32),
        scratch_shapes=[pltpu.VMEM((tm, d), BF16), pltpu.VMEM((tm, d), F32)],
        compiler_params=_cparams(2),
        name="mlp",
    )(x2d, g, wu, wd, fg)


def _slab_weight(w_in, dtype=BF16):
    d = w_in.shape[0]
    z = lambda n: jnp.zeros((d, n), w_in.dtype)
    gdn, mla, rwkv, dsa = 0, 1032, 1448, 2472
    gate = 3824
    cols = [
        w_in[:, rwkv:rwkv + 1024],
        w_in[:, gdn:gdn + 768],
        w_in[:, gdn + 776:gdn + 1032],
        w_in[:, mla:mla + 256],
        w_in[:, mla + 256:mla + 384],
        z(64), w_in[:, mla + 384:mla + 416], z(32),
        w_in[:, dsa:dsa + 512],
        w_in[:, dsa + 584:dsa + 1352],
        w_in[:, gdn + 768:gdn + 776], z(120),
        w_in[:, dsa + 512:dsa + 584], z(56),
        w_in[:, gate:gate + 4096],
    ]
    return jnp.concatenate(cols, axis=1).astype(dtype)


def _mla_weights(w_uq, w_ukv):
    wq = w_uq.reshape(MLA_Q_LORA, HEADS, MLA_NOPE + MLA_ROPE)
    wq = jnp.pad(wq, ((0, 0), (0, 0), (0, MLA_QK - MLA_NOPE - MLA_ROPE))).reshape(MLA_Q_LORA, HEADS * MLA_QK)
    wkv = w_ukv.reshape(MLA_KV_LORA, HEADS, MLA_NOPE + DH)
    wk = jnp.pad(wkv[:, :, :MLA_NOPE], ((0, 0), (0, 0), (0, MLA_QK - MLA_NOPE))).reshape(MLA_KV_LORA, HEADS * MLA_QK)
    wvt = wkv[:, :, MLA_NOPE:].reshape(MLA_KV_LORA, HW).T
    return wq.astype(BF16), wk.astype(BF16), wvt.astype(BF16)


def _rope_cos_sin(positions, rot_dims):
    inv_freq = ROPE_THETA ** (-jnp.arange(0, rot_dims, 2, dtype=F32) / rot_dims)
    ang = positions.astype(F32)[..., None] * inv_freq
    return jnp.concatenate([jnp.cos(ang), jnp.sin(ang)], axis=-1)


def _rope_expansion(rot_dims, offset, period):
    half = rot_dims // 2
    e = np.zeros((rot_dims, 3 * 128), np.float32)
    base = np.zeros((1, 3 * 128), np.float32)
    base[0, 0:128] = 1.0
    for rep in range(128 // period):
        lo = rep * period + offset
        for j in range(half):
            e[j, lo + j] = 1.0
            e[j, lo + half + j] = 1.0
            base[0, lo + j] = 0.0
            base[0, lo + half + j] = 0.0
            e[half + j, 128 + lo + j] = -1.0
            e[half + j, 256 + lo + half + j] = 1.0
    return jnp.asarray(e), jnp.asarray(base)


def _rope_table(cs_ref, e_ref, base_ref):
    tab = _mm_exact_rhs(cs_ref[...], e_ref[...]) + base_ref[...]
    return tab[:, 0:128], tab[:, 128:256], tab[:, 256:384]


def _layer(x2d, b, s, prm, tabs, final_g, is_last):
    (mix_g, w_slab, conv_w, gdn_pcol, gdn_prow, gdn_ng, mla_qg, mla_kvg, wq, wk, wv,
     rwkv_mu, rwkv_vec, ww2, wa2, wg2, wo_gdn, wo_mla, wo_rwkv, wo_dsa, w_o, mlp_g, w_up, w_down) = prm
    mla_cs, dsa_cs = tabs
    t = b * s
    slab = _in_proj(x2d, mix_g, w_slab[:, :OFF_GATE], F32)
    gates = _in_proj(x2d, mix_g, w_slab[:, OFF_GATE:], BF16)

    ab = slab[:, OFF_GDN_AB:OFF_GDN_AB + 8].reshape(b, s // CHUNK, CHUNK, 8)
    abr = jnp.swapaxes(ab, 2, 3)
    o_gdn = _gdn(slab, abr, conv_w, gdn_pcol, gdn_prow, gdn_ng, b, s)
    q, k, v = _mla_prep(slab, mla_cs, mla_qg, mla_kvg, wq, wk, wv)
    o_mla = _flash(q, k, v, b, s)
    o_rwkv = _rwkv(slab, rwkv_mu, rwkv_vec, ww2, wa2, wg2, b, s)
    ik, kr, vt, iqt, qt, iwt = _dsa_prep(slab, dsa_cs)
    o_dsa = _dsa(iqt, iwt, qt, ik, kr, vt, b, s, min(IDX_TOPK_MAX, s // 4))

    x2d = _merge(x2d, o_gdn, o_mla, o_rwkv, o_dsa, gates, wo_gdn, wo_mla, wo_rwkv, wo_dsa, w_o)
    return _mlp(x2d, mlp_g, w_up, w_down, final_g, is_last)


def kernel(x, positions, mix_norm_g, w_in, gdn_conv_w, gdn_a_log, gdn_dt_bias, gdn_norm_g, mla_q_norm_g, mla_kv_norm_g, mla_w_uq, mla_w_ukv, rwkv_mu, rwkv_w0, rwkv_w_w2, rwkv_a0, rwkv_w_a2, rwkv_w_g2, rwkv_k_k, rwkv_k_a, rwkv_r_k, rwkv_ln_w, rwkv_ln_b, w_out_gdn, w_out_mla, w_out_rwkv, w_out_dsa, w_o, mlp_norm_g, w_up, w_down, final_norm_g):
    b, s, d = x.shape
    depth = w_in.shape[0]
    tabs = (_rope_cos_sin(positions, MLA_ROPE).reshape(b * s, MLA_ROPE),
            _rope_cos_sin(positions, DH // 4).reshape(b * s, DH // 4))
    row = lambda a: a.reshape(1, -1).astype(F32)
    pad8 = lambda rows: jnp.concatenate(rows + [jnp.zeros((8 - len(rows), rows[0].shape[1]), F32)], axis=0)
    x2d = x.reshape(b * s, d)
    final_g = row(final_norm_g)
    for l in range(depth):
        lane4 = lambda a: jnp.pad(a.astype(F32), (0, 124)).reshape(1, 128)
        gdn_pcol = pad8([lane4(gdn_a_log[l]), lane4(gdn_dt_bias[l])])
        gdn_prow = jnp.pad(jnp.stack([gdn_a_log[l], gdn_dt_bias[l]], axis=1).astype(F32), ((0, 4), (0, 126)))
        wq, wk, wv = _mla_weights(mla_w_uq[l], mla_w_ukv[l])
        rwkv_vec = pad8([row(rwkv_w0[l]), row(rwkv_a0[l]), row(rwkv_k_k[l]), row(rwkv_k_a[l]),
                         row(rwkv_ln_w[l]), row(rwkv_ln_b[l]), row(rwkv_r_k[l])])
        prm = (row(mix_norm_g[l]), _slab_weight(w_in[l]), gdn_conv_w[l].astype(F32), gdn_pcol, gdn_prow,
               row(jnp.tile(gdn_norm_g[l], HEADS)), row(mla_q_norm_g[l]), row(mla_kv_norm_g[l]), wq, wk, wv,
               row(rwkv_mu[l]), rwkv_vec, rwkv_w_w2[l].astype(F32), rwkv_w_a2[l].astype(F32),
               rwkv_w_g2[l].astype(F32),
               w_out_gdn[l].astype(BF16), w_out_mla[l].astype(BF16), w_out_rwkv[l].astype(BF16),
               w_out_dsa[l].astype(BF16), w_o[l].astype(BF16), row(mlp_norm_g[l]),
               w_up[l].astype(BF16), w_down[l].astype(BF16))
        x2d = _layer(x2d, b, s, prm, tabs, final_g, l == depth - 1)
    return x2d.reshape(b, s, d)
```

```python
import functools
import math

import numpy as np
import jax
import jax.numpy as jnp
from jax import lax
from jax.experimental import pallas as pl
from jax.experimental.pallas import tpu as pltpu

F32 = jnp.float32
BF16 = jnp.bfloat16
I32 = jnp.int32
I16 = jnp.int16

D_MODEL = 1024
DEPTH = 4
ROPE_THETA = 500000.0
NORM_EPS = 1e-6
NEG_INF = -1e30
HEADS = 4
DH = 64
HW = HEADS * DH
GDN_CONV = 4
CHUNK = 64
MLA_Q_LORA = 256
MLA_KV_LORA = 128
MLA_NOPE = 64
MLA_ROPE = 32
MLA_QK = 128
RWKV_GN_EPS = 64e-5
IDX_HEADS = 8
IDX_TOPK_MAX = 256
D_FF = 4 * D_MODEL
INT_MIN = -2 ** 31
LOG2E = 1.4426950408889634

VMEM_LIMIT_BYTES = 56 * 1024 * 1024

OFF_RWKV = 0
OFF_GDN_Q = 1024
OFF_GDN_K = 1280
OFF_GDN_V = 1536
OFF_GDN_Z = 1792
OFF_MLA_CQ = 2048
OFF_MLA_CKV = 2304
OFF_MLA_KR = 2432
OFF_DSA_IQ = 2560
OFF_DSA_Q = 3072
OFF_DSA_K = 3328
OFF_DSA_V = 3584
OFF_GDN_AB = 3840
OFF_DSA_IKW = 3968
OFF_GATE = 4096
N_SLAB = 8192

NN = (((1,), (0,)), ((), ()))
NT = (((1,), (1,)), ((), ()))
TN = (((0,), (0,)), ((), ()))
BNN = (((2,), (1,)), ((0,), (0,)))
BNT = (((2,), (2,)), ((0,), (0,)))
BTN = (((1,), (1,)), ((0,), (0,)))


def _cparams(n_axes):
    return pltpu.CompilerParams(dimension_semantics=("arbitrary",) * n_axes,
                                vmem_limit_bytes=VMEM_LIMIT_BYTES)


def _mm(a, b, dims=NN):
    return lax.dot_general(a.astype(BF16), b.astype(BF16), dims, preferred_element_type=F32)


def _mm_exact_lhs(m, x, dims=NN):
    mb = m.astype(BF16)
    x0 = x.astype(BF16)
    r1 = x - x0.astype(F32)
    x1 = r1.astype(BF16)
    x2 = (r1 - x1.astype(F32)).astype(BF16)
    d = lambda y: lax.dot_general(mb, y, dims, preferred_element_type=F32)
    return d(x0) + (d(x1) + d(x2))


def _mm_exact_rhs(x, m, dims=NN):
    mb = m.astype(BF16)
    x0 = x.astype(BF16)
    r1 = x - x0.astype(F32)
    x1 = r1.astype(BF16)
    x2 = (r1 - x1.astype(F32)).astype(BF16)
    d = lambda y: lax.dot_general(y, mb, dims, preferred_element_type=F32)
    return d(x0) + (d(x1) + d(x2))


def _iota2(shape, dim):
    return lax.broadcasted_iota(I32, shape, dim)


def _seg_ones(width, seg):
    sh = int(math.log2(seg))
    r = _iota2((width, width), 0) >> sh
    c = _iota2((width, width), 1) >> sh
    return jnp.where(r == c, 1.0, 0.0).astype(F32)


def _seg_sum(x, seg_ones):
    return _mm_exact_rhs(x, seg_ones)


def _softplus(x):
    return jnp.maximum(x, 0.0) + jnp.log1p(jnp.exp(-jnp.abs(x)))


def _sigmoid(x):
    return 1.0 / (1.0 + jnp.exp(-x))


def _silu(x):
    return x * _sigmoid(x)


def _tri_masks(n):
    r = _iota2((n, n), 0)
    c = _iota2((n, n), 1)
    return r, c


def _unit_lower_inverse(m, r, c, dims=NN):
    eye = jnp.where(r == c, 1.0, 0.0).astype(F32)
    same16 = (r >> 4) == (c >> 4)
    same32 = (r >> 5) == (c >> 5)
    md = jnp.where(same16, m, 0.0)
    t = eye - md
    p = md
    for _ in range(3):
        p = _mm(p, p, dims)
        t = t + _mm(t, p, dims)
    c32 = jnp.where(same32 & jnp.logical_not(same16), m, 0.0)
    t = t - _mm(_mm(t, c32, dims), t, dims)
    c64 = jnp.where(same32, 0.0, m)
    t = t - _mm(_mm(t, c64, dims), t, dims)
    return t


IN_PROJ_TN = 512


def _in_proj_kernel(x_ref, g_ref, w_ref, o_ref):
    x = x_ref[...]
    ms = jnp.mean(x * x, axis=-1, keepdims=True)
    h = (x * lax.rsqrt(ms + NORM_EPS) * g_ref[...]).astype(BF16)
    for n0 in range(0, o_ref.shape[1], IN_PROJ_TN):
        o_ref[:, n0:n0 + IN_PROJ_TN] = jnp.dot(
            h, w_ref[:, n0:n0 + IN_PROJ_TN], preferred_element_type=F32).astype(o_ref.dtype)


def _in_proj(x2d, g, w, out_dtype):
    t, d = x2d.shape
    n = w.shape[1]
    tm = min(512, t)
    return pl.pallas_call(
        _in_proj_kernel,
        grid=(t // tm,),
        in_specs=[pl.BlockSpec((tm, d), lambda i: (i, 0)),
                  pl.BlockSpec((1, d), lambda i: (0, 0)),
                  pl.BlockSpec((d, n), lambda i: (0, 0))],
        out_specs=pl.BlockSpec((tm, n), lambda i: (i, 0)),
        out_shape=jax.ShapeDtypeStruct((t, n), out_dtype),
        compiler_params=_cparams(1),
        name="in_proj",
    )(x2d, g, w)


def _to_groups(x, nchunk):
    parts = [x[:, h * DH:(h + 1) * DH].reshape(nchunk, 1, CHUNK, DH) for h in range(HEADS)]
    return jnp.concatenate(parts, axis=1).reshape(nchunk * HEADS, CHUNK, DH)


def _col_groups(x, lane0, nchunk):
    parts = [x[:, lane0 + h:lane0 + h + 1].reshape(nchunk, 1, CHUNK, 1) for h in range(HEADS)]
    return jnp.concatenate(parts, axis=1).reshape(nchunk * HEADS, CHUNK, 1)


def _gdn_kernel(q_ref, k_ref, v_ref, z_ref, ab_ref, abr_ref, cw_ref, pcol_ref, prow_ref, ng_ref,
                o_ref, xbuf_ref, state_ref, u_ref, w_ref, a_ref, qd_ref, kd_ref, egl_ref, oraw_ref,
                *, tb, nbatch):
    nchunk = tb // CHUNK
    bi = pl.program_id(1)
    xbuf = xbuf_ref.at[bi]

    @pl.when(pl.program_id(0) == 0)
    def _():
        xbuf[0:8, :] = jnp.zeros((8, 3 * HW), F32)
        state_ref[bi] = jnp.zeros((HEADS, DH, DH), F32)

    xbuf[8:8 + tb, 0:HW] = q_ref[...]
    xbuf[8:8 + tb, HW:2 * HW] = k_ref[...]
    xbuf[8:8 + tb, 2 * HW:3 * HW] = v_ref[...]
    cw = cw_ref[...]
    y = xbuf[8:8 + tb, :] * cw[3:4, :]
    y = y + xbuf[7:7 + tb, :] * cw[2:3, :]
    y = y + xbuf[6:6 + tb, :] * cw[1:2, :]
    y = y + xbuf[5:5 + tb, :] * cw[0:1, :]
    xbuf[0:8, :] = xbuf[tb:tb + 8, :]
    y = _silu(y)
    seg = _seg_ones(HW, DH)
    q = y[:, 0:HW]
    k = y[:, HW:2 * HW]
    q = q * lax.rsqrt(_seg_sum(q * q, seg) + 1e-6) * (DH ** -0.5)
    k = k * lax.rsqrt(_seg_sum(k * k, seg) + 1e-6)
    v = y[:, 2 * HW:3 * HW]

    ab = ab_ref[...]
    pcol = pcol_ref[...]
    g_col = -jnp.exp(pcol[0:1, :]) * _softplus(ab + pcol[1:2, :])
    beta_col = _sigmoid(ab)

    r, c = _tri_masks(CHUNK)
    incl = c <= r
    strict = c < r
    l_incl = jnp.where(incl, 1.0, 0.0).astype(F32)

    rb = _iota2((tb, tb), 0)
    cb = _iota2((tb, tb), 1)
    l_blk = jnp.where(((rb >> 6) == (cb >> 6)) & (cb <= rb), 1.0, 0.0).astype(F32)
    gc_col = _mm_exact_lhs(l_blk, g_col)
    prow = prow_ref[...]
    abr = abr_ref[0].reshape(nchunk * 8, CHUNK)
    prow_t = jnp.concatenate([prow] * nchunk, axis=0)
    g_row = -jnp.exp(prow_t[:, 0:1]) * _softplus(abr + prow_t[:, 1:2])
    gc_row = _mm_exact_rhs(g_row, l_incl, NT).reshape(nchunk, 8, CHUNK)

    qg, kg, vg = _to_groups(q, nchunk), _to_groups(k, nchunk), _to_groups(v, nchunk)
    beta = _col_groups(beta_col, 4, nchunk)
    gcol = _col_groups(gc_col, 0, nchunk)
    grow = jnp.concatenate([gc_row[:, h:h + 1, :].reshape(nchunk, 1, 1, CHUNK) for h in range(HEADS)],
                           axis=1).reshape(nchunk * HEADS, 1, CHUNK)
    decay = jnp.where(incl, jnp.exp(jnp.where(incl, gcol - grow, 0.0)), 0.0)
    kb = kg * beta
    m = jnp.where(strict, _mm(kb, kg, BNT) * decay, 0.0)
    t_inv = _unit_lower_inverse(m, r, c, BNN)
    egc = jnp.exp(gcol)
    glast = gcol[:, CHUNK - 1:CHUNK, :]
    uw = _mm(t_inv, jnp.concatenate([vg * beta, kb * egc], axis=2), BNN)
    u_ref[bi] = uw[:, :, 0:DH]
    w_ref[bi] = uw[:, :, DH:2 * DH]
    a_ref[bi] = _mm(qg, kg, BNT) * decay
    qd_ref[bi] = qg * egc
    kd_ref[bi] = kg * jnp.exp(glast - gcol)
    egl_ref[bi] = jnp.broadcast_to(jnp.exp(glast), (nchunk * HEADS, 1, DH))

    @pl.when(bi == nbatch - 1)
    def _():
        nc = nbatch * HEADS

        def chunk_body(ci, carry):
            sl = pl.ds(pl.multiple_of(ci * HEADS, HEADS), HEADS)
            take = lambda ref: ref[:, sl].reshape((nc,) + ref.shape[2:])
            s = state_ref[...].reshape(nc, DH, DH)
            v_new = take(u_ref) - _mm(take(w_ref), s, BNN)
            o = _mm(take(qd_ref), s, BNN) + _mm(take(a_ref), v_new, BNN)
            s = s * take(egl_ref) + _mm(take(kd_ref), v_new, BTN)
            state_ref[...] = s.reshape(nbatch, HEADS, DH, DH)
            rows = pl.ds(pl.multiple_of(ci * CHUNK, CHUNK), CHUNK)
            for bb in range(nbatch):
                oraw_ref[bb, rows, :] = jnp.concatenate([o[bb * HEADS + h] for h in range(HEADS)], axis=1)
            return carry

        lax.fori_loop(0, nchunk, chunk_body, 0)
        o_all = oraw_ref[...].reshape(nbatch * tb, HW)
        ms = _seg_sum(o_all * o_all, seg) * (1.0 / DH)
        o_n = o_all * lax.rsqrt(ms + NORM_EPS) * ng_ref[...] * _silu(z_ref[...].reshape(nbatch * tb, HW))
        o_ref[...] = o_n.reshape(nbatch, tb, HW)


GDN_TB = 512


def _gdn(slab, abr, conv_w, pcol, prow, norm_g, b, s):
    tb = min(GDN_TB, s)
    nb = s // tb
    nchunk = tb // CHUNK
    ng = nchunk * HEADS
    grp = lambda w: pltpu.VMEM((b, ng, CHUNK, w), F32)
    col = lambda off, w: pl.BlockSpec((tb, w), lambda j, i, o=off // w: (i * nb + j, o))
    full = lambda shape: pl.BlockSpec(shape, lambda j, i: (0,) * len(shape))
    slab3 = slab.reshape(b, s, slab.shape[1])
    out = pl.pallas_call(
        functools.partial(_gdn_kernel, tb=tb, nbatch=b),
        grid=(nb, b),
        in_specs=[col(OFF_GDN_Q, HW), col(OFF_GDN_K, HW), col(OFF_GDN_V, HW),
                  pl.BlockSpec((b, tb, HW), lambda j, i: (0, j, OFF_GDN_Z // HW)),
                  col(OFF_GDN_AB, 128),
                  pl.BlockSpec((1, nchunk, 8, CHUNK), lambda j, i: (i, j, 0, 0)),
                  full((GDN_CONV, 3 * HW)), full((8, 128)), full((8, 128)), full((1, HW))],
        out_specs=pl.BlockSpec((b, tb, HW), lambda j, i: (0, j, 0)),
        out_shape=jax.ShapeDtypeStruct((b, s, HW), F32),
        scratch_shapes=[pltpu.VMEM((b, tb + 8, 3 * HW), F32), pltpu.VMEM((b, HEADS, DH, DH), F32),
                        grp(DH), grp(DH), grp(CHUNK), grp(DH), grp(DH),
                        pltpu.VMEM((b, ng, 1, DH), F32), pltpu.VMEM((b, tb, HW), F32)],
        compiler_params=_cparams(2),
        name="gdn",
    )(slab, slab, slab, slab3, slab, abr, conv_w, pcol, prow, norm_g)
    return out.reshape(b * s, HW)


def _rwkv_kernel(p_ref, mu_ref, vec_ref, ww2_ref, wa2_ref, wg2_ref, o_ref,
                 xbuf_ref, state_ref, ta_ref, tpv_ref, rt_ref, qkv_ref, qb_ref, bc_ref, ktv_ref,
                 el_ref, oraw_ref, *, tb):
    nchunk = tb // CHUNK

    @pl.when(pl.program_id(1) == 0)
    def _():
        xbuf_ref[0:8, :] = jnp.zeros((8, 4 * HW), F32)
        state_ref[...] = jnp.zeros_like(state_ref)

    p = p_ref[...]
    xbuf_ref[8:8 + tb, :] = p
    prev = xbuf_ref[7:7 + tb, :]
    xbuf_ref[0:8, :] = xbuf_ref[tb:tb + 8, :]
    pm = p + (prev - p) * mu_ref[...]
    vec = vec_ref[...]
    w0, a0, k_k, k_a = vec[0:1, :], vec[1:2, :], vec[2:3, :], vec[3:4, :]
    rr = pm[:, 0:HW]
    k = pm[:, HW:2 * HW]
    v = pm[:, 2 * HW:3 * HW]
    xw = pm[:, 768:832]
    xa = pm[:, 832:896]
    xg = pm[:, 896:1024]
    w_log = -_softplus(-(w0 + _mm(jnp.tanh(xw), ww2_ref[...]))) - 0.5
    lw = -jnp.exp(w_log)
    a = _sigmoid(a0 + _mm(xa, wa2_ref[...]))
    g = _mm(_sigmoid(xg), wg2_ref[...])
    seg = _seg_ones(HW, DH)
    kk = k * k_k
    kk = kk * lax.rsqrt(_seg_sum(kk * kk, seg) + 1e-6)
    k2 = k * (1.0 + (a - 1.0) * k_a)
    kka = kk * a

    r_i, c_i = _tri_masks(CHUNK)
    incl = c_i <= r_i
    strict = c_i < r_i
    rb = _iota2((tb, tb), 0)
    cb = _iota2((tb, tb), 1)
    l_blk = jnp.where(((rb >> 6) == (cb >> 6)) & (cb <= rb), 1.0, 0.0).astype(F32)
    lwc_all = _mm_exact_lhs(l_blk, lw)

    rg, kg, vg = _to_groups(rr, nchunk), _to_groups(k2, nchunk), _to_groups(v, nchunk)
    kkg, kkag = _to_groups(kk, nchunk), _to_groups(kka, nchunk)
    lwg, lwc = _to_groups(lw, nchunk), _to_groups(lwc_all, nchunk)
    lwl = lwc[:, CHUNK - 1:CHUNK, :]
    e_neg = jnp.exp(-lwc)
    at = kkg * jnp.exp(lwc - lwg)
    rt = rg * jnp.exp(lwc)
    y2 = jnp.concatenate([at, rt], axis=1)
    x2 = jnp.concatenate([kkag * e_neg, kg * e_neg], axis=1)
    gram = _mm(y2, x2, BNT)
    n_m = jnp.where(strict, gram[:, 0:CHUNK, 0:CHUNK], 0.0)
    p_m = jnp.where(strict, gram[:, 0:CHUNK, CHUNK:], 0.0)
    qb_m = jnp.where(incl, gram[:, CHUNK:, 0:CHUNK], 0.0)
    qk_m = jnp.where(incl, gram[:, CHUNK:, CHUNK:], 0.0)
    t_inv = _unit_lower_inverse(n_m, r_i, c_i, BNN)
    ta_ref[...] = _mm(t_inv, at, BNN)
    tpv_ref[...] = _mm(t_inv, _mm(p_m, vg, BNN), BNN)
    rt_ref[...] = rt
    qkv_ref[...] = _mm(qk_m, vg, BNN)
    qb_ref[...] = qb_m
    e_rel = jnp.exp(lwl - lwc)
    bc_ref[...] = kkag * e_rel
    ktv_ref[...] = _mm(vg, kg * e_rel, BTN)
    el_ref[...] = jnp.exp(lwl)

    def chunk_body(ci, carry):
        sl = pl.ds(pl.multiple_of(ci * HEADS, HEADS), HEADS)
        zt = state_ref[...]
        u = _mm(ta_ref[sl], zt, BNT) + tpv_ref[sl]
        o = _mm(rt_ref[sl], zt, BNT) + qkv_ref[sl] - _mm(qb_ref[sl], u, BNN)
        state_ref[...] = zt * el_ref[sl] + ktv_ref[sl] - _mm(u, bc_ref[sl], BTN)
        oraw_ref[pl.ds(pl.multiple_of(ci * CHUNK, CHUNK), CHUNK), :] = jnp.concatenate(
            [o[h] for h in range(HEADS)], axis=1)
        return carry

    lax.fori_loop(0, nchunk, chunk_body, 0)
    ln_w, ln_b, r_k = vec[4:5, :], vec[5:6, :], vec[6:7, :]
    o_all = oraw_ref[...]
    mean = _seg_sum(o_all, seg) * (1.0 / DH)
    cen = o_all - mean
    var = _seg_sum(cen * cen, seg) * (1.0 / DH)
    o_n = cen * lax.rsqrt(var + RWKV_GN_EPS) * ln_w + ln_b
    bonus = _seg_sum(rr * k2 * r_k, seg) * v
    o_ref[...] = (o_n + bonus) * g


RWKV_TB = 512


def _rwkv(slab, mu, vec, ww2, wa2, wg2, b, s):
    tb = min(RWKV_TB, s)
    nb = s // tb
    ng = (tb // CHUNK) * HEADS
    grp = lambda: pltpu.VMEM((ng, CHUNK, DH), F32)
    full = lambda shape: pl.BlockSpec(shape, lambda i, j: (0,) * len(shape))
    return pl.pallas_call(
        functools.partial(_rwkv_kernel, tb=tb),
        grid=(b, nb),
        in_specs=[pl.BlockSpec((tb, 4 * HW), lambda i, j: (i * nb + j, OFF_RWKV // (4 * HW))),
                  full((1, 4 * HW)), full((8, HW)), full((64, HW)), full((64, HW)), full((128, HW))],
        out_specs=pl.BlockSpec((tb, HW), lambda i, j: (i * nb + j, 0)),
        out_shape=jax.ShapeDtypeStruct((b * s, HW), F32),
        scratch_shapes=[pltpu.VMEM((tb + 8, 4 * HW), F32), pltpu.VMEM((HEADS, DH, DH), F32),
                        grp(), grp(), grp(), grp(), grp(), grp(), grp(),
                        pltpu.VMEM((ng, 1, DH), F32), pltpu.VMEM((tb, HW), F32)],
        compiler_params=_cparams(2),
        name="rwkv7",
    )(slab, mu, vec, ww2, wa2, wg2)


def _rope(x, cos_t, sin_lo, sin_hi, half):
    w = x.shape[-1]
    return x * cos_t + pltpu.roll(x, w - half, 1) * sin_lo + pltpu.roll(x, half, 1) * sin_hi


def _tile_lanes(t, n):
    return t if n == 1 else jnp.concatenate([t] * n, axis=1)


def _mla_prep_kernel(cq_ref, ckv_ref, kr_ref, cs_ref, e_ref, base_ref, qg_ref, kvg_ref, wq_ref, wk_ref,
                     wvt_ref, q_ref, k_ref, vt_ref):
    cq = cq_ref[...]
    hq = cq * lax.rsqrt(jnp.mean(cq * cq, axis=-1, keepdims=True) + NORM_EPS) * qg_ref[...]
    ckv = ckv_ref[...]
    hkv = ckv * lax.rsqrt(jnp.mean(ckv * ckv, axis=-1, keepdims=True) + NORM_EPS) * kvg_ref[...]
    cos_t, sin_lo, sin_hi = _rope_table(cs_ref, e_ref, base_ref)
    q = _mm(hq, wq_ref[...])
    q = _rope(q, _tile_lanes(cos_t, HEADS), _tile_lanes(sin_lo, HEADS), _tile_lanes(sin_hi, HEADS),
              MLA_ROPE // 2)
    q_ref[...] = (q * ((MLA_NOPE + MLA_ROPE) ** -0.5 * LOG2E)).astype(BF16)
    kr = _rope(kr_ref[...], cos_t, sin_lo, sin_hi, MLA_ROPE // 2)
    k = _mm(hkv, wk_ref[...]) + _tile_lanes(kr, HEADS)
    k_ref[...] = k.astype(BF16)
    vt_ref[...] = lax.dot_general(wvt_ref[...], hkv.astype(BF16), NT,
                                  preferred_element_type=F32).astype(BF16)


def _mla_prep(slab, cs, qg, kvg, wq, wk, wvt):
    t = slab.shape[0]
    tm = min(512, t)
    full = lambda shape: pl.BlockSpec(shape, lambda i: (0,) * len(shape))
    e, base = _rope_expansion(MLA_ROPE, MLA_NOPE, MLA_QK)
    return pl.pallas_call(
        _mla_prep_kernel,
        grid=(t // tm,),
        in_specs=[pl.BlockSpec((tm, 256), lambda i: (i, OFF_MLA_CQ // 256)),
                  pl.BlockSpec((tm, 128), lambda i: (i, OFF_MLA_CKV // 128)),
                  pl.BlockSpec((tm, 128), lambda i: (i, OFF_MLA_KR // 128)),
                  pl.BlockSpec((tm, MLA_ROPE), lambda i: (i, 0)), full(e.shape), full(base.shape),
                  full((1, 256)), full((1, 128)), full((256, 512)), full((128, 512)), full((256, 128))],
        out_specs=[pl.BlockSpec((tm, 512), lambda i: (i, 0)), pl.BlockSpec((tm, 512), lambda i: (i, 0)),
                   pl.BlockSpec((256, tm), lambda i: (0, i))],
        out_shape=[jax.ShapeDtypeStruct((t, 512), BF16), jax.ShapeDtypeStruct((t, 512), BF16),
                   jax.ShapeDtypeStruct((256, t), BF16)],
        compiler_params=_cparams(1),
        name="mla_prep",
    )(slab, slab, slab, cs, e, base, qg, kvg, wq, wk, wvt)


def _flash_kernel(q_ref, k_ref, vt_ref, o_ref, m_ref, l_ref, acc_ref, *, tq, tk):
    qi = pl.program_id(1)
    ki = pl.program_id(2)

    @pl.when(ki == 0)
    def _():
        m_ref[...] = jnp.full_like(m_ref, NEG_INF)
        l_ref[...] = jnp.zeros_like(l_ref)
        acc_ref[...] = jnp.zeros_like(acc_ref)

    def step(on_diagonal):
        if on_diagonal:
            causal = _iota2((tk, tq), 0) <= _iota2((tk, tq), 1)
        vt = vt_ref[...]
        ones = jnp.ones((16, tk), BF16)
        scores = [lax.dot_general(k_ref[:, h * MLA_QK:(h + 1) * MLA_QK], q_ref[:, h * MLA_QK:(h + 1) * MLA_QK],
                                  NT, preferred_element_type=F32) for h in range(HEADS)]
        for h in range(HEADS):
            s = scores[h]
            if on_diagonal:
                s = jnp.where(causal, s, NEG_INF)
            m_old = m_ref[h]
            m_new = jnp.maximum(m_old, jnp.max(s, axis=0, keepdims=True))
            alpha = jnp.exp2(m_old - m_new)
            p = jnp.exp2(s - m_new).astype(BF16)
            pv = jnp.dot(jnp.concatenate([vt[h * DH:(h + 1) * DH, :], ones], axis=0), p,
                         preferred_element_type=F32)
            l_ref[h] = alpha * l_ref[h] + pv[DH:DH + 1, :]
            acc_ref[h * DH:(h + 1) * DH, :] = alpha * acc_ref[h * DH:(h + 1) * DH, :] + pv[0:DH, :]
            m_ref[h] = m_new

    pl.when(ki < qi)(lambda: step(False))
    pl.when(ki == qi)(lambda: step(True))

    @pl.when(ki == pl.num_programs(2) - 1)
    def _():
        o_t = jnp.concatenate([acc_ref[h * DH:(h + 1) * DH, :] / l_ref[h] for h in range(HEADS)], axis=0)
        o_ref[...] = o_t.T


def _flash(q, k, vt, b, s):
    tq = min(512, s)
    tk = tq
    nq = s // tq
    nk = s // tk
    return pl.pallas_call(
        functools.partial(_flash_kernel, tq=tq, tk=tk),
        grid=(b, nq, nk),
        in_specs=[pl.BlockSpec((tq, 512), lambda i, j, kk: (i * nq + j, 0)),
                  pl.BlockSpec((tk, 512), lambda i, j, kk: (i * nk + jnp.minimum(kk, j), 0)),
                  pl.BlockSpec((256, tk), lambda i, j, kk: (0, i * nk + jnp.minimum(kk, j)))],
        out_specs=pl.BlockSpec((tq, 256), lambda i, j, kk: (i * nq + j, 0)),
        out_shape=jax.ShapeDtypeStruct((b * s, 256), F32),
        scratch_shapes=[pltpu.VMEM((HEADS, 1, tq), F32), pltpu.VMEM((HEADS, 1, tq), F32),
                        pltpu.VMEM((HW, tq), F32)],
        compiler_params=_cparams(3),
        name="mla_flash",
    )(q, k, vt)


DSA_TQ = 256
DSA_TK = 512


def _dsa_prep_kernel(iq_ref, ikw_ref, q_ref, k_ref, v_ref, cs_ref, e_ref, base_ref,
                     ik_ref, ko_ref, vt_ref, iqt_ref, qt_ref, iwt_ref):
    cos_t, sin_lo, sin_hi = _rope_table(cs_ref, e_ref, base_ref)
    half = DH // 8
    rope = lambda x, n: _rope(x, _tile_lanes(cos_t, n), _tile_lanes(sin_lo, n), _tile_lanes(sin_hi, n), half)
    ikw = ikw_ref[...]
    ik_ref[...] = rope(ikw, 1)[:, 0:DH].astype(BF16)
    kr = rope(k_ref[...], 2)
    for h in range(HEADS):
        ko_ref[h] = kr[:, h * DH:(h + 1) * DH].astype(BF16)
    vt_ref[0] = v_ref[...].T.astype(BF16)
    iqt_ref[...] = rope(iq_ref[...], 4).T.astype(BF16)
    qt_ref[...] = (rope(q_ref[...], 2) * (DH ** -0.5 * LOG2E)).T.astype(BF16)
    iwt_ref[...] = ikw.T[DH:DH + 8, :] * (IDX_HEADS ** -0.5 * DH ** -0.5)


def _dsa_prep(slab, cs):
    t = slab.shape[0]
    tm = min(DSA_TK, t)
    full = lambda shape: pl.BlockSpec(shape, lambda i: (0,) * len(shape))
    e, base = _rope_expansion(DH // 4, 0, DH)
    return pl.pallas_call(
        _dsa_prep_kernel,
        grid=(t // tm,),
        in_specs=[pl.BlockSpec((tm, IDX_HEADS * DH), lambda i: (i, OFF_DSA_IQ // (IDX_HEADS * DH))),
                  pl.BlockSpec((tm, 128), lambda i: (i, OFF_DSA_IKW // 128)),
                  pl.BlockSpec((tm, 256), lambda i: (i, OFF_DSA_Q // 256)),
                  pl.BlockSpec((tm, 256), lambda i: (i, OFF_DSA_K // 256)),
                  pl.BlockSpec((tm, 256), lambda i: (i, OFF_DSA_V // 256)),
                  pl.BlockSpec((tm, DH // 4), lambda i: (i, 0)), full(e.shape), full(base.shape)],
        out_specs=[pl.BlockSpec((tm, DH), lambda i: (i, 0)),
                   pl.BlockSpec((HEADS, tm, DH), lambda i: (0, i, 0)),
                   pl.BlockSpec((1, HW, tm), lambda i: (i, 0, 0)),
                   pl.BlockSpec((IDX_HEADS * DH, tm), lambda i: (0, i)),
                   pl.BlockSpec((HW, tm), lambda i: (0, i)),
                   pl.BlockSpec((8, tm), lambda i: (0, i))],
        out_shape=[jax.ShapeDtypeStruct((t, DH), BF16), jax.ShapeDtypeStruct((HEADS, t, DH), BF16),
                   jax.ShapeDtypeStruct((t // tm, HW, tm), BF16),
                   jax.ShapeDtypeStruct((IDX_HEADS * DH, t), BF16), jax.ShapeDtypeStruct((HW, t), BF16),
                   jax.ShapeDtypeStruct((8, t), F32)],
        compiler_params=_cparams(1),
        name="dsa_prep",
    )(slab, slab, slab, slab, slab, cs, e, base)


def _dsa_kernel(iqt_ref, iwt_ref, qt_ref, ik_ref, k_ref, vt_ref, kidx_ref, o_ref,
                hi_ref, lo_ref, lom_ref, bias_ref, acc_ref, s_ref, m_ref, l_ref,
                *, tq, tk, n_sel, idx_bits):
    qi = pl.program_id(1)
    nkt = (qi * tq + tq - 1) // tk + 1
    iqt = iqt_ref[...]
    iw = iwt_ref[...]
    q_pos = qi * tq + _iota2((tk, tq), 1)

    def score_body(kt, carry):
        row0 = pl.multiple_of(kt * tk, tk)
        ik = ik_ref[pl.ds(row0, tk), :]
        sc = jnp.zeros((tk, tq), F32)
        for h in range(IDX_HEADS):
            lg = lax.dot_general(ik, iqt[h * DH:(h + 1) * DH, :], NN, preferred_element_type=F32)
            sc = sc + jnp.maximum(lg, 0.0) * iw[h:h + 1, :]
        bits = lax.bitcast_convert_type(sc, I32)
        key = bits ^ ((bits >> 31) & 0x7FFFFFFF)
        key = jnp.where(sc == 0.0, 0, key)
        k_pos = row0 + _iota2((tk, tq), 0)
        key = jnp.where(k_pos <= q_pos, key, INT_MIN)
        hi_ref[pl.ds(row0, tk), :] = (key >> 16).astype(I16)
        lo_ref[pl.ds(row0, tk), :] = ((key & 0xFFFF) - 32768).astype(I16)
        return carry

    lax.fori_loop(0, nkt, score_body, 0)

    one16 = jnp.ones((tk, tq), I16)
    zero16 = jnp.zeros((tk, tq), I16)

    def count(*pred_fns):
        def body(kt, accs):
            sl = pl.ds(pl.multiple_of(kt * tk, tk), tk)
            args = (hi_ref[sl, :], lo_ref[sl, :], lom_ref[sl, :], kidx_ref[sl, :])
            out = []
            for acc, fn in zip(accs, pred_fns):
                ones = jnp.where(fn(*args), one16, zero16).reshape(tk // 64, 64, tq)
                for g in range(tk // 64):
                    acc = acc + ones[g]
                out.append(acc)
            return tuple(out)
        parts = lax.fori_loop(0, nkt, body, tuple(jnp.zeros((64, tq), I16) for _ in pred_fns))
        sums = tuple(jnp.sum(p.astype(I32).astype(F32), axis=0, keepdims=True) for p in parts)
        return sums[0] if len(sums) == 1 else sums

    def bisect16(pred_ge):
        def body(it, thr):
            bit = jnp.maximum(15 - it, 0)
            cand = jnp.where(it == 0, jnp.zeros_like(thr), thr | jnp.left_shift(jnp.int32(1), bit))
            c16 = cand.astype(I16)
            cnt = count(lambda hi, lo, lom, idx: pred_ge(hi, lo, lom, c16))
            return jnp.where(cnt >= float(n_sel), cand, thr)
        return lax.fori_loop(0, 16, body, jnp.full((1, tq), -32768, I32))

    t_hi = bisect16(lambda hi, lo, lom, c: hi >= c)
    t_hi16 = t_hi.astype(I16)

    def mask_lo_body(kt, carry):
        sl = pl.ds(pl.multiple_of(kt * tk, tk), tk)
        hi = hi_ref[sl, :]
        lom_ref[sl, :] = jnp.where(hi > t_hi16, jnp.int16(32767),
                                   jnp.where(hi == t_hi16, lo_ref[sl, :], jnp.int16(-32768)))
        return carry

    lax.fori_loop(0, nkt, mask_lo_body, 0)
    t_lo16 = bisect16(lambda hi, lo, lom, c: lom >= c).astype(I16)

    def is_gt(hi, lo):
        return (hi > t_hi16) | ((hi == t_hi16) & (lo > t_lo16))

    def is_eq(hi, lo):
        return (hi == t_hi16) & (lo == t_lo16)

    cnt_gt, cnt_eq = count(lambda hi, lo, lom, idx: is_gt(hi, lo), lambda hi, lo, lom, idx: is_eq(hi, lo))
    need = float(n_sel) - cnt_gt

    def idx_body(it, j):
        cand = j + jnp.left_shift(jnp.int32(1), idx_bits - 1 - it)
        c16 = cand.astype(I16)
        cnt = count(lambda hi, lo, lom, idx: is_eq(hi, lo) & (idx < c16))
        return jnp.where(cnt < need, cand, j)

    excess = jnp.max(jnp.where(cnt_eq > need, 1.0, 0.0)) > 0.0
    j_cut16 = lax.cond(excess,
                       lambda: lax.fori_loop(0, idx_bits, idx_body, jnp.zeros((1, tq), I32)),
                       lambda: jnp.full((1, tq), 32767, I32)).astype(I16)

    def bias_body(kt, carry):
        sl = pl.ds(pl.multiple_of(kt * tk, tk), tk)
        hi, lo = hi_ref[sl, :], lo_ref[sl, :]
        sel = (is_gt(hi, lo) | (is_eq(hi, lo) & (kidx_ref[sl, :] <= j_cut16))) & (hi != jnp.int16(-32768))
        bias_ref[sl, :] = jnp.where(sel, jnp.zeros((tk, tq), BF16),
                                    jnp.full((tk, tq), NEG_INF, BF16)).astype(F32)
        return carry

    lax.fori_loop(0, nkt, bias_body, 0)

    qt = qt_ref[...]
    acc_ref[...] = jnp.zeros_like(acc_ref)

    def att_body(kt, carry):
        row0 = pl.multiple_of(kt * tk, tk)
        for h in range(HEADS):
            s_ref[h] = lax.dot_general(k_ref[h, pl.ds(row0, tk), :], qt[h * DH:(h + 1) * DH, :], NN,
                                       preferred_element_type=F32)

        @pl.when(kt >= 0)
        def _():
            bias = bias_ref[pl.ds(row0, tk), :]
            ones = jnp.ones((16, tk), BF16)
            for h in range(HEADS):
                hs = slice(h * DH, (h + 1) * DH)
                s = s_ref[h] + bias
                m_old = m_ref[h]
                m_new = jnp.maximum(m_old, jnp.max(s, axis=0, keepdims=True))
                alpha = jnp.exp2(m_old - m_new)
                p = jnp.exp2(s - m_new).astype(BF16)
                pv = lax.dot_general(jnp.concatenate([vt_ref[kt, hs, :], ones], axis=0), p, NN,
                                     preferred_element_type=F32)
                acc_ref[hs, :] = alpha * acc_ref[hs, :] + pv[0:DH, :]
                m_ref[h] = m_new
                l_ref[h] = alpha * l_ref[h] + pv[DH:DH + 1, :]
        return carry

    m_ref[...] = jnp.full_like(m_ref, NEG_INF)
    l_ref[...] = jnp.zeros_like(l_ref)
    lax.fori_loop(0, nkt, att_body, 0)
    o_t = jnp.concatenate([acc_ref[h * DH:(h + 1) * DH, :] / l_ref[h] for h in range(HEADS)], axis=0)
    o_ref[...] = o_t.T


def _dsa(iqt, iwt, qt, ik, kr, vt, b, s, n_sel):
    tq = min(DSA_TQ, s)
    tk = min(DSA_TK, s)
    nq = s // tq
    nkt = s // tk
    return pl.pallas_call(
        functools.partial(_dsa_kernel, tq=tq, tk=tk, n_sel=n_sel, idx_bits=int(math.log2(s))),
        grid=(b, nq),
        in_specs=[pl.BlockSpec((IDX_HEADS * DH, tq), lambda i, j: (0, i * nq + j)),
                  pl.BlockSpec((8, tq), lambda i, j: (0, i * nq + j)),
                  pl.BlockSpec((HW, tq), lambda i, j: (0, i * nq + j)),
                  pl.BlockSpec((s, DH), lambda i, j: (i, 0)),
                  pl.BlockSpec((HEADS, s, DH), lambda i, j: (0, i, 0)),
                  pl.BlockSpec((nkt, HW, tk), lambda i, j: (i, 0, 0)),
                  pl.BlockSpec((s, tq), lambda i, j: (0, 0))],
        out_specs=pl.BlockSpec((tq, HW), lambda i, j: (i * nq + j, 0)),
        out_shape=jax.ShapeDtypeStruct((b * s, HW), F32),
        scratch_shapes=[pltpu.VMEM((s, tq), I16), pltpu.VMEM((s, tq), I16), pltpu.VMEM((s, tq), I16),
                        pltpu.VMEM((s, tq), F32), pltpu.VMEM((HW, tq), F32),
                        pltpu.VMEM((HEADS, tk, tq), F32), pltpu.VMEM((HEADS, 1, tq), F32),
                        pltpu.VMEM((HEADS, 1, tq), F32)],
        compiler_params=_cparams(2),
        name="dsa",
    )(iqt, iwt, qt, ik, kr, vt,
      jnp.broadcast_to(jnp.arange(s, dtype=I16)[:, None], (s, tq)))


def _merge_kernel(x_ref, oa_ref, ob_ref, oc_ref, od_ref, ga_ref, gb_ref, gc_ref, gd_ref,
                  wa_ref, wb_ref, wc_ref, wd_ref, wo_ref, o_ref):
    def branch(o_r, g_r, w_r):
        y = jnp.dot(o_r[...].astype(BF16), w_r[...], preferred_element_type=F32)
        return _sigmoid(g_r[...].astype(F32)) * y

    merged = (branch(oa_ref, ga_ref, wa_ref) + branch(ob_ref, gb_ref, wb_ref)
              + branch(oc_ref, gc_ref, wc_ref) + branch(od_ref, gd_ref, wd_ref))
    o_ref[...] = x_ref[...] + jnp.dot(merged.astype(BF16), wo_ref[...], preferred_element_type=F32)


def _merge(x2d, oa, ob, oc, od, gates, wa, wb, wc, wd, wo):
    t, d = x2d.shape
    tm = min(512, t)
    row = lambda w: pl.BlockSpec((tm, w), lambda i: (i, 0))
    gate = lambda n: pl.BlockSpec((tm, d), lambda i, o=n: (i, o))
    full = lambda shape: pl.BlockSpec(shape, lambda i: (0,) * len(shape))
    return pl.pallas_call(
        _merge_kernel,
        grid=(t // tm,),
        in_specs=[row(d), row(HW), row(HW), row(HW), row(HW), gate(0), gate(1), gate(2), gate(3),
                  full((HW, d)), full((HW, d)), full((HW, d)), full((HW, d)), full((d, d))],
        out_specs=row(d),
        out_shape=jax.ShapeDtypeStruct((t, d), F32),
        compiler_params=_cparams(1),
        name="merge",
    )(x2d, oa, ob, oc, od, gates, gates, gates, gates, wa, wb, wc, wd, wo)


def _mlp_kernel(x_ref, g_ref, wu_ref, wd_ref, fg_ref, o_ref, h_ref, acc_ref, *, final_norm):
    fi = pl.program_id(1)

    @pl.when(fi == 0)
    def _():
        x = x_ref[...]
        ms = jnp.mean(x * x, axis=-1, keepdims=True)
        h_ref[...] = (x * lax.rsqrt(ms + NORM_EPS) * g_ref[...]).astype(BF16)
        acc_ref[...] = jnp.zeros_like(acc_ref)

    u = jnp.dot(h_ref[...], wu_ref[...], preferred_element_type=F32)
    u = jnp.square(jnp.maximum(u, 0.0))
    acc_ref[...] += jnp.dot(u.astype(BF16), wd_ref[...], preferred_element_type=F32)

    @pl.when(fi == pl.num_programs(1) - 1)
    def _():
        y = x_ref[...] + acc_ref[...]
        if final_norm:
            ms = jnp.mean(y * y, axis=-1, keepdims=True)
            y = y * lax.rsqrt(ms + NORM_EPS) * fg_ref[...]
        o_ref[...] = y


def _mlp(x2d, g, wu, wd, fg, final_norm):
    t, d = x2d.shape
    f = wu.shape[1]
    tm = min(1024, t)
    tf = 512
    return pl.pallas_call(
        functools.partial(_mlp_kernel, final_norm=final_norm),
        grid=(t // tm, f // tf),
        in_specs=[pl.BlockSpec((tm, d), lambda i, j: (i, 0)),
                  pl.BlockSpec((1, d), lambda i, j: (0, 0)),
                  pl.BlockSpec((d, tf), lambda i, j: (0, j)),
                  pl.BlockSpec((tf, d), lambda i, j: (j, 0)),
                  pl.BlockSpec((1, d), lambda i, j: (0, 0))],
        out_specs=pl.BlockSpec((tm, d), lambda i, j: (i, 0)),
        out_shape=jax.ShapeDtypeStruct((t, d), F32),
        scratch_shapes=[pltpu.VMEM((tm, d), BF16), pltpu.VMEM((tm, d), F32)],
        compiler_params=_cparams(2),
        name="mlp",
    )(x2d, g, wu, wd, fg)


def _slab_weight(w_in, dtype=BF16):
    d = w_in.shape[0]
    z = lambda n: jnp.zeros((d, n), w_in.dtype)
    gdn, mla, rwkv, dsa = 0, 1032, 1448, 2472
    gate = 3824
    cols = [
        w_in[:, rwkv:rwkv + 1024],
        w_in[:, gdn:gdn + 768],
        w_in[:, gdn + 776:gdn + 1032],
        w_in[:, mla:mla + 256],
        w_in[:, mla + 256:mla + 384],
        z(64), w_in[:, mla + 384:mla + 416], z(32),
        w_in[:, dsa:dsa + 512],
        w_in[:, dsa + 584:dsa + 1352],
        w_in[:, gdn + 768:gdn + 776], z(120),
        w_in[:, dsa + 512:dsa + 584], z(56),
        w_in[:, gate:gate + 4096],
    ]
    return jnp.concatenate(cols, axis=1).astype(dtype)


def _mla_weights(w_uq, w_ukv):
    wq = w_uq.reshape(MLA_Q_LORA, HEADS, MLA_NOPE + MLA_ROPE)
    wq = jnp.pad(wq, ((0, 0), (0, 0), (0, MLA_QK - MLA_NOPE - MLA_ROPE))).reshape(MLA_Q_LORA, HEADS * MLA_QK)
    wkv = w_ukv.reshape(MLA_KV_LORA, HEADS, MLA_NOPE + DH)
    wk = jnp.pad(wkv[:, :, :MLA_NOPE], ((0, 0), (0, 0), (0, MLA_QK - MLA_NOPE))).reshape(MLA_KV_LORA, HEADS * MLA_QK)
    wvt = wkv[:, :, MLA_NOPE:].reshape(MLA_KV_LORA, HW).T
    return wq.astype(BF16), wk.astype(BF16), wvt.astype(BF16)


def _rope_cos_sin(positions, rot_dims):
    inv_freq = ROPE_THETA ** (-jnp.arange(0, rot_dims, 2, dtype=F32) / rot_dims)
    ang = positions.astype(F32)[..., None] * inv_freq
    return jnp.concatenate([jnp.cos(ang), jnp.sin(ang)], axis=-1)


def _rope_expansion(rot_dims, offset, period):
    half = rot_dims // 2
    e = np.zeros((rot_dims, 3 * 128), np.float32)
    base = np.zeros((1, 3 * 128), np.float32)
    base[0, 0:128] = 1.0
    for rep in range(128 // period):
        lo = rep * period + offset
        for j in range(half):
            e[j, lo + j] = 1.0
            e[j, lo + half + j] = 1.0
            base[0, lo + j] = 0.0
            base[0, lo + half + j] = 0.0
            e[half + j, 128 + lo + j] = -1.0
            e[half + j, 256 + lo + half + j] = 1.0
    return jnp.asarray(e), jnp.asarray(base)


def _rope_table(cs_ref, e_ref, base_ref):
    tab = _mm_exact_rhs(cs_ref[...], e_ref[...]) + base_ref[...]
    return tab[:, 0:128], tab[:, 128:256], tab[:, 256:384]


def _layer(x2d, b, s, prm, tabs, final_g, is_last):
    (mix_g, w_slab, conv_w, gdn_pcol, gdn_prow, gdn_ng, mla_qg, mla_kvg, wq, wk, wv,
     rwkv_mu, rwkv_vec, ww2, wa2, wg2, wo_gdn, wo_mla, wo_rwkv, wo_dsa, w_o, mlp_g, w_up, w_down) = prm
    mla_cs, dsa_cs = tabs
    t = b * s
    slab = _in_proj(x2d, mix_g, w_slab[:, :OFF_GATE], F32)
    gates = _in_proj(x2d, mix_g, w_slab[:, OFF_GATE:], BF16)

    ab = slab[:, OFF_GDN_AB:OFF_GDN_AB + 8].reshape(b, s // CHUNK, CHUNK, 8)
    abr = jnp.swapaxes(ab, 2, 3)
    o_gdn = _gdn(slab, abr, conv_w, gdn_pcol, gdn_prow, gdn_ng, b, s)
    q, k, v = _mla_prep(slab, mla_cs, mla_qg, mla_kvg, wq, wk, wv)
    o_mla = _flash(q, k, v, b, s)
    o_rwkv = _rwkv(slab, rwkv_mu, rwkv_vec, ww2, wa2, wg2, b, s)
    ik, kr, vt, iqt, qt, iwt = _dsa_prep(slab, dsa_cs)
    o_dsa = _dsa(iqt, iwt, qt, ik, kr, vt, b, s, min(IDX_TOPK_MAX, s // 4))

    x2d = _merge(x2d, o_gdn, o_mla, o_rwkv, o_dsa, gates, wo_gdn, wo_mla, wo_rwkv, wo_dsa, w_o)
    return _mlp(x2d, mlp_g, w_up, w_down, final_g, is_last)


def kernel(x, positions, mix_norm_g, w_in, gdn_conv_w, gdn_a_log, gdn_dt_bias, gdn_norm_g, mla_q_norm_g, mla_kv_norm_g, mla_w_uq, mla_w_ukv, rwkv_mu, rwkv_w0, rwkv_w_w2, rwkv_a0, rwkv_w_a2, rwkv_w_g2, rwkv_k_k, rwkv_k_a, rwkv_r_k, rwkv_ln_w, rwkv_ln_b, w_out_gdn, w_out_mla, w_out_rwkv, w_out_dsa, w_o, mlp_norm_g, w_up, w_down, final_norm_g):
    b, s, d = x.shape
    depth = w_in.shape[0]
    tabs = (_rope_cos_sin(positions, MLA_ROPE).reshape(b * s, MLA_ROPE),
            _rope_cos_sin(positions, DH // 4).reshape(b * s, DH // 4))
    row = lambda a: a.reshape(1, -1).astype(F32)
    pad8 = lambda rows: jnp.concatenate(rows + [jnp.zeros((8 - len(rows), rows[0].shape[1]), F32)], axis=0)
    x2d = x.reshape(b * s, d)
    final_g = row(final_norm_g)
    for l in range(depth):
        lane4 = lambda a: jnp.pad(a.astype(F32), (0, 124)).reshape(1, 128)
        gdn_pcol = pad8([lane4(gdn_a_log[l]), lane4(gdn_dt_bias[l])])
        gdn_prow = jnp.pad(jnp.stack([gdn_a_log[l], gdn_dt_bias[l]], axis=1).astype(F32), ((0, 4), (0, 126)))
        wq, wk, wv = _mla_weights(mla_w_uq[l], mla_w_ukv[l])
        rwkv_vec = pad8([row(rwkv_w0[l]), row(rwkv_a0[l]), row(rwkv_k_k[l]), row(rwkv_k_a[l]),
                         row(rwkv_ln_w[l]), row(rwkv_ln_b[l]), row(rwkv_r_k[l])])
        prm = (row(mix_norm_g[l]), _slab_weight(w_in[l]), gdn_conv_w[l].astype(F32), gdn_pcol, gdn_prow,
               row(jnp.tile(gdn_norm_g[l], HEADS)), row(mla_q_norm_g[l]), row(mla_kv_norm_g[l]), wq, wk, wv,
               row(rwkv_mu[l]), rwkv_vec, rwkv_w_w2[l].astype(F32), rwkv_w_a2[l].astype(F32),
               rwkv_w_g2[l].astype(F32),
               w_out_gdn[l].astype(BF16), w_out_mla[l].astype(BF16), w_out_rwkv[l].astype(BF16),
               w_out_dsa[l].astype(BF16), w_o[l].astype(BF16), row(mlp_norm_g[l]),
               w_up[l].astype(BF16), w_down[l].astype(BF16))
        x2d = _layer(x2d, b, s, prm, tabs, final_g, l == depth - 1)
    return x2d.reshape(b, s, d)
```

```python
import functools
import math

import numpy as np
import jax
import jax.numpy as jnp
from jax import lax
from jax.experimental import pallas as pl
from jax.experimental.pallas import tpu as pltpu

F32 = jnp.float32
BF16 = jnp.bfloat16
I32 = jnp.int32
I16 = jnp.int16

D_MODEL = 1024
DEPTH = 4
ROPE_THETA = 500000.0
NORM_EPS = 1e-6
NEG_INF = -1e30
HEADS = 4
DH = 64
HW = HEADS * DH
GDN_CONV = 4
CHUNK = 64
MLA_Q_LORA = 256
MLA_KV_LORA = 128
MLA_NOPE = 64
MLA_ROPE = 32
MLA_QK = 128
RWKV_GN_EPS = 64e-5
IDX_HEADS = 8
IDX_TOPK_MAX = 256
D_FF = 4 * D_MODEL
INT_MIN = -2 ** 31
LOG2E = 1.4426950408889634

VMEM_LIMIT_BYTES = 56 * 1024 * 1024

OFF_RWKV = 0
OFF_GDN_Q = 1024
OFF_GDN_K = 1280
OFF_GDN_V = 1536
OFF_GDN_Z = 1792
OFF_MLA_CQ = 2048
OFF_MLA_CKV = 2304
OFF_MLA_KR = 2432
OFF_DSA_IQ = 2560
OFF_DSA_Q = 3072
OFF_DSA_K = 3328
OFF_DSA_V = 3584
OFF_GDN_AB = 3840
OFF_DSA_IKW = 3968
OFF_GATE = 4096
N_SLAB = 8192

NN = (((1,), (0,)), ((), ()))
NT = (((1,), (1,)), ((), ()))
TN = (((0,), (0,)), ((), ()))
BNN = (((2,), (1,)), ((0,), (0,)))
BNT = (((2,), (2,)), ((0,), (0,)))
BTN = (((1,), (1,)), ((0,), (0,)))


def _cparams(n_axes):
    return pltpu.CompilerParams(dimension_semantics=("arbitrary",) * n_axes,
                                vmem_limit_bytes=VMEM_LIMIT_BYTES)


def _mm(a, b, dims=NN):
    return lax.dot_general(a.astype(BF16), b.astype(BF16), dims, preferred_element_type=F32)


def _mm_exact_lhs(m, x, dims=NN):
    mb = m.astype(BF16)
    x0 = x.astype(BF16)
    r1 = x - x0.astype(F32)
    x1 = r1.astype(BF16)
    x2 = (r1 - x1.astype(F32)).astype(BF16)
    d = lambda y: lax.dot_general(mb, y, dims, preferred_element_type=F32)
    return d(x0) + (d(x1) + d(x2))


def _mm_exact_rhs(x, m, dims=NN):
    mb = m.astype(BF16)
    x0 = x.astype(BF16)
    r1 = x - x0.astype(F32)
    x1 = r1.astype(BF16)
    x2 = (r1 - x1.astype(F32)).astype(BF16)
    d = lambda y: lax.dot_general(y, mb, dims, preferred_element_type=F32)
    return d(x0) + (d(x1) + d(x2))


def _iota2(shape, dim):
    return lax.broadcasted_iota(I32, shape, dim)


def _seg_ones(width, seg):
    sh = int(math.log2(seg))
    r = _iota2((width, width), 0) >> sh
    c = _iota2((width, width), 1) >> sh
    return jnp.where(r == c, 1.0, 0.0).astype(F32)


def _seg_sum(x, seg_ones):
    return _mm_exact_rhs(x, seg_ones)


def _softplus(x):
    return jnp.maximum(x, 0.0) + jnp.log1p(jnp.exp(-jnp.abs(x)))


def _sigmoid(x):
    return 1.0 / (1.0 + jnp.exp(-x))


def _silu(x):
    return x * _sigmoid(x)


def _tri_masks(n):
    r = _iota2((n, n), 0)
    c = _iota2((n, n), 1)
    return r, c


def _unit_lower_inverse(m, r, c, dims=NN):
    eye = jnp.where(r == c, 1.0, 0.0).astype(F32)
    same16 = (r >> 4) == (c >> 4)
    same32 = (r >> 5) == (c >> 5)
    md = jnp.where(same16, m, 0.0)
    t = eye - md
    p = md
    for _ in range(3):
        p = _mm(p, p, dims)
        t = t + _mm(t, p, dims)
    c32 = jnp.where(same32 & jnp.logical_not(same16), m, 0.0)
    t = t - _mm(_mm(t, c32, dims), t, dims)
    c64 = jnp.where(same32, 0.0, m)
    t = t - _mm(_mm(t, c64, dims), t, dims)
    return t


IN_PROJ_TN = 512


def _in_proj_kernel(x_ref, g_ref, w_ref, o_ref):
    x = x_ref[...]
    ms = jnp.mean(x * x, axis=-1, keepdims=True)
    h = (x * lax.rsqrt(ms + NORM_EPS) * g_ref[...]).astype(BF16)
    for n0 in range(0, o_ref.shape[1], IN_PROJ_TN):
        o_ref[:, n0:n0 + IN_PROJ_TN] = jnp.dot(
            h, w_ref[:, n0:n0 + IN_PROJ_TN], preferred_element_type=F32).astype(o_ref.dtype)


def _in_proj(x2d, g, w, out_dtype):
    t, d = x2d.shape
    n = w.shape[1]
    tm = min(512, t)
    return pl.pallas_call(
        _in_proj_kernel,
        grid=(t // tm,),
        in_specs=[pl.BlockSpec((tm, d), lambda i: (i, 0)),
                  pl.BlockSpec((1, d), lambda i: (0, 0)),
                  pl.BlockSpec((d, n), lambda i: (0, 0))],
        out_specs=pl.BlockSpec((tm, n), lambda i: (i, 0)),
        out_shape=jax.ShapeDtypeStruct((t, n), out_dtype),
        compiler_params=_cparams(1),
        name="in_proj",
    )(x2d, g, w)


def _to_groups(x, nchunk):
    parts = [x[:, h * DH:(h + 1) * DH].reshape(nchunk, 1, CHUNK, DH) for h in range(HEADS)]
    return jnp.concatenate(parts, axis=1).reshape(nchunk * HEADS, CHUNK, DH)


def _col_groups(x, lane0, nchunk):
    parts = [x[:, lane0 + h:lane0 + h + 1].reshape(nchunk, 1, CHUNK, 1) for h in range(HEADS)]
    return jnp.concatenate(parts, axis=1).reshape(nchunk * HEADS, CHUNK, 1)


def _gdn_kernel(q_ref, k_ref, v_ref, z_ref, ab_ref, abr_ref, cw_ref, pcol_ref, prow_ref, ng_ref,
                o_ref, xbuf_ref, state_ref, u_ref, w_ref, a_ref, qd_ref, kd_ref, egl_ref, oraw_ref,
                carry_ref, *, tb, nbatch):
    nchunk = tb // CHUNK
    bi = pl.program_id(1)
    xbuf = xbuf_ref

    @pl.when(pl.program_id(0) == 0)
    def _():
        carry_ref[bi] = jnp.zeros((8, 3 * HW), F32)
        state_ref[bi] = jnp.zeros((HEADS, DH, DH), F32)

    xbuf[0:8, :] = carry_ref[bi]
    xbuf[8:8 + tb, 0:HW] = q_ref[...]
    xbuf[8:8 + tb, HW:2 * HW] = k_ref[...]
    xbuf[8:8 + tb, 2 * HW:3 * HW] = v_ref[...]
    cw = cw_ref[...]
    y = xbuf[8:8 + tb, :] * cw[3:4, :]
    y = y + xbuf[7:7 + tb, :] * cw[2:3, :]
    y = y + xbuf[6:6 + tb, :] * cw[1:2, :]
    y = y + xbuf[5:5 + tb, :] * cw[0:1, :]
    carry_ref[bi] = xbuf[tb:tb + 8, :]
    y = _silu(y)
    seg = _seg_ones(HW, DH)
    q = y[:, 0:HW]
    k = y[:, HW:2 * HW]
    q = q * lax.rsqrt(_seg_sum(q * q, seg) + 1e-6) * (DH ** -0.5)
    k = k * lax.rsqrt(_seg_sum(k * k, seg) + 1e-6)
    v = y[:, 2 * HW:3 * HW]

    ab = ab_ref[...]
    pcol = pcol_ref[...]
    g_col = -jnp.exp(pcol[0:1, :]) * _softplus(ab + pcol[1:2, :])
    beta_col = _sigmoid(ab)

    r, c = _tri_masks(CHUNK)
    incl = c <= r
    strict = c < r
    l_incl = jnp.where(incl, 1.0, 0.0).astype(F32)

    rb = _iota2((tb, tb), 0)
    cb = _iota2((tb, tb), 1)
    l_blk = jnp.where(((rb >> 6) == (cb >> 6)) & (cb <= rb), 1.0, 0.0).astype(F32)
    gc_col = _mm_exact_lhs(l_blk, g_col)
    prow = prow_ref[...]
    abr = abr_ref[0].reshape(nchunk * 8, CHUNK)
    prow_t = jnp.concatenate([prow] * nchunk, axis=0)
    g_row = -jnp.exp(prow_t[:, 0:1]) * _softplus(abr + prow_t[:, 1:2])
    gc_row = _mm_exact_rhs(g_row, l_incl, NT).reshape(nchunk, 8, CHUNK)

    qg, kg, vg = _to_groups(q, nchunk), _to_groups(k, nchunk), _to_groups(v, nchunk)
    beta = _col_groups(beta_col, 4, nchunk)
    gcol = _col_groups(gc_col, 0, nchunk)
    grow = jnp.concatenate([gc_row[:, h:h + 1, :].reshape(nchunk, 1, 1, CHUNK) for h in range(HEADS)],
                           axis=1).reshape(nchunk * HEADS, 1, CHUNK)
    decay = jnp.where(incl, jnp.exp(jnp.where(incl, gcol - grow, 0.0)), 0.0)
    kb = kg * beta
    m = jnp.where(strict, _mm(kb, kg, BNT) * decay, 0.0)
    t_inv = _unit_lower_inverse(m, r, c, BNN)
    egc = jnp.exp(gcol)
    glast = gcol[:, CHUNK - 1:CHUNK, :]
    uw = _mm(t_inv, jnp.concatenate([vg * beta, kb * egc], axis=2), BNN)
    u_ref[bi] = uw[:, :, 0:DH]
    w_ref[bi] = uw[:, :, DH:2 * DH].astype(BF16)
    a_ref[bi] = (_mm(qg, kg, BNT) * decay).astype(BF16)
    qd_ref[bi] = (qg * egc).astype(BF16)
    kd_ref[bi] = (kg * jnp.exp(glast - gcol)).astype(BF16)
    egl_ref[bi] = jnp.broadcast_to(jnp.exp(glast), (nchunk * HEADS, 1, DH))

    @pl.when(bi == nbatch - 1)
    def _():
        nc = nbatch * HEADS

        def chunk_body(ci, carry):
            sl = pl.ds(pl.multiple_of(ci * HEADS, HEADS), HEADS)
            take = lambda ref: ref[:, sl].reshape((nc,) + ref.shape[2:])
            s = state_ref[...].reshape(nc, DH, DH)
            v_new = take(u_ref) - _mm(take(w_ref), s, BNN)
            o = _mm(take(qd_ref), s, BNN) + _mm(take(a_ref), v_new, BNN)
            s = s * take(egl_ref) + _mm(take(kd_ref), v_new, BTN)
            state_ref[...] = s.reshape(nbatch, HEADS, DH, DH)
            rows = pl.ds(pl.multiple_of(ci * CHUNK, CHUNK), CHUNK)
            for bb in range(nbatch):
                oraw_ref[bb, rows, :] = jnp.concatenate([o[bb * HEADS + h] for h in range(HEADS)], axis=1)
            return carry

        lax.fori_loop(0, nchunk, chunk_body, 0)
        o_all = oraw_ref[...].reshape(nbatch * tb, HW)
        ms = _seg_sum(o_all * o_all, seg) * (1.0 / DH)
        o_n = o_all * lax.rsqrt(ms + NORM_EPS) * ng_ref[...] * _silu(z_ref[...].reshape(nbatch * tb, HW))
        o_ref[...] = o_n.reshape(nbatch, tb, HW)


GDN_TB = 512


def _gdn(slab, abr, conv_w, pcol, prow, norm_g, b, s):
    tb = min(GDN_TB, s)
    nb = s // tb
    nchunk = tb // CHUNK
    ng = nchunk * HEADS
    grp = lambda w, dt=BF16: pltpu.VMEM((b, ng, CHUNK, w), dt)
    col = lambda off, w: pl.BlockSpec((tb, w), lambda j, i, o=off // w: (i * nb + j, o))
    full = lambda shape: pl.BlockSpec(shape, lambda j, i: (0,) * len(shape))
    slab3 = slab.reshape(b, s, slab.shape[1])
    out = pl.pallas_call(
        functools.partial(_gdn_kernel, tb=tb, nbatch=b),
        grid=(nb, b),
        in_specs=[col(OFF_GDN_Q, HW), col(OFF_GDN_K, HW), col(OFF_GDN_V, HW),
                  pl.BlockSpec((b, tb, HW), lambda j, i: (0, j, OFF_GDN_Z // HW)),
                  col(OFF_GDN_AB, 128),
                  pl.BlockSpec((1, nchunk, 8, CHUNK), lambda j, i: (i, j, 0, 0)),
                  full((GDN_CONV, 3 * HW)), full((8, 128)), full((8, 128)), full((1, HW))],
        out_specs=pl.BlockSpec((b, tb, HW), lambda j, i: (0, j, 0)),
        out_shape=jax.ShapeDtypeStruct((b, s, HW), F32),
        scratch_shapes=[pltpu.VMEM((tb + 8, 3 * HW), F32), pltpu.VMEM((b, HEADS, DH, DH), F32),
                        grp(DH, F32), grp(DH), grp(CHUNK), grp(DH), grp(DH),
                        pltpu.VMEM((b, ng, 1, DH), F32), pltpu.VMEM((b, tb, HW), F32),
                        pltpu.VMEM((b, 8, 3 * HW), F32)],
        compiler_params=_cparams(2),
        name="gdn",
    )(slab, slab, slab, slab3, slab, abr, conv_w, pcol, prow, norm_g)
    return out.reshape(b * s, HW)


def _rwkv_kernel(p_ref, mu_ref, vec_ref, ww2_ref, wa2_ref, wg2_ref, o_ref,
                 xbuf_ref, state_ref, ta_ref, tpv_ref, rt_ref, qkv_ref, qb_ref, bc_ref, ktv_ref,
                 el_ref, oraw_ref, carry_ref, post_ref, *, tb, nbatch):
    nchunk = tb // CHUNK
    bi = pl.program_id(1)

    @pl.when(pl.program_id(0) == 0)
    def _():
        carry_ref[bi] = jnp.zeros((8, 4 * HW), F32)
        state_ref[bi] = jnp.zeros((HEADS, DH, DH), F32)

    p = p_ref[...]
    xbuf_ref[0:8, :] = carry_ref[bi]
    xbuf_ref[8:8 + tb, :] = p
    prev = xbuf_ref[7:7 + tb, :]
    carry_ref[bi] = xbuf_ref[tb:tb + 8, :]
    pm = p + (prev - p) * mu_ref[...]
    vec = vec_ref[...]
    w0, a0, k_k, k_a = vec[0:1, :], vec[1:2, :], vec[2:3, :], vec[3:4, :]
    rr = pm[:, 0:HW]
    k = pm[:, HW:2 * HW]
    v = pm[:, 2 * HW:3 * HW]
    xw = pm[:, 768:832]
    xa = pm[:, 832:896]
    xg = pm[:, 896:1024]
    w_log = -_softplus(-(w0 + _mm(jnp.tanh(xw), ww2_ref[...]))) - 0.5
    lw = -jnp.exp(w_log)
    a = _sigmoid(a0 + _mm(xa, wa2_ref[...]))
    g = _mm(_sigmoid(xg), wg2_ref[...])
    seg = _seg_ones(HW, DH)
    kk = k * k_k
    kk = kk * lax.rsqrt(_seg_sum(kk * kk, seg) + 1e-6)
    k2 = k * (1.0 + (a - 1.0) * k_a)
    kka = kk * a

    r_i, c_i = _tri_masks(CHUNK)
    incl = c_i <= r_i
    strict = c_i < r_i
    rb = _iota2((tb, tb), 0)
    cb = _iota2((tb, tb), 1)
    l_blk = jnp.where(((rb >> 6) == (cb >> 6)) & (cb <= rb), 1.0, 0.0).astype(F32)
    lwc_all = _mm_exact_lhs(l_blk, lw)

    rg, kg, vg = _to_groups(rr, nchunk), _to_groups(k2, nchunk), _to_groups(v, nchunk)
    kkg, kkag = _to_groups(kk, nchunk), _to_groups(kka, nchunk)
    lwg, lwc = _to_groups(lw, nchunk), _to_groups(lwc_all, nchunk)
    lwl = lwc[:, CHUNK - 1:CHUNK, :]
    e_neg = jnp.exp(-lwc)
    at = kkg * jnp.exp(lwc - lwg)
    rt = rg * jnp.exp(lwc)
    y2 = jnp.concatenate([at, rt], axis=1)
    x2 = jnp.concatenate([kkag * e_neg, kg * e_neg], axis=1)
    gram = _mm(y2, x2, BNT)
    n_m = jnp.where(strict, gram[:, 0:CHUNK, 0:CHUNK], 0.0)
    p_m = jnp.where(strict, gram[:, 0:CHUNK, CHUNK:], 0.0)
    qb_m = jnp.where(incl, gram[:, CHUNK:, 0:CHUNK], 0.0)
    qk_m = jnp.where(incl, gram[:, CHUNK:, CHUNK:], 0.0)
    t_inv = _unit_lower_inverse(n_m, r_i, c_i, BNN)
    ta_ref[bi] = _mm(t_inv, at, BNN).astype(BF16)
    tpv_ref[bi] = _mm(t_inv, _mm(p_m, vg, BNN), BNN)
    rt_ref[bi] = rt.astype(BF16)
    qkv_ref[bi] = _mm(qk_m, vg, BNN)
    qb_ref[bi] = qb_m.astype(BF16)
    e_rel = jnp.exp(lwl - lwc)
    bc_ref[bi] = (kkag * e_rel).astype(BF16)
    ktv_ref[bi] = _mm(vg, kg * e_rel, BTN)
    el_ref[bi] = jnp.exp(lwl)
    ln_w, ln_b, r_k = vec[4:5, :], vec[5:6, :], vec[6:7, :]
    post_ref[bi, :, 0:HW] = _seg_sum(rr * k2 * r_k, seg) * v
    post_ref[bi, :, HW:2 * HW] = g

    @pl.when(bi == nbatch - 1)
    def _():
        nc = nbatch * HEADS

        def chunk_body(ci, carry):
            sl = pl.ds(pl.multiple_of(ci * HEADS, HEADS), HEADS)
            take = lambda ref: ref[:, sl].reshape((nc,) + ref.shape[2:])
            zt = state_ref[...].reshape(nc, DH, DH)
            u = _mm(take(ta_ref), zt, BNT) + take(tpv_ref)
            o = _mm(take(rt_ref), zt, BNT) + take(qkv_ref) - _mm(take(qb_ref), u, BNN)
            zt = zt * take(el_ref) + take(ktv_ref) - _mm(u, take(bc_ref), BTN)
            state_ref[...] = zt.reshape(nbatch, HEADS, DH, DH)
            rows = pl.ds(pl.multiple_of(ci * CHUNK, CHUNK), CHUNK)
            for bb in range(nbatch):
                oraw_ref[bb, rows, :] = jnp.concatenate([o[bb * HEADS + h] for h in range(HEADS)], axis=1)
            return carry

        lax.fori_loop(0, nchunk, chunk_body, 0)
        o_all = oraw_ref[...].reshape(nbatch * tb, HW)
        mean = _seg_sum(o_all, seg) * (1.0 / DH)
        cen = o_all - mean
        var = _seg_sum(cen * cen, seg) * (1.0 / DH)
        o_n = cen * lax.rsqrt(var + RWKV_GN_EPS) * ln_w + ln_b
        post = post_ref[...].reshape(nbatch * tb, 2 * HW)
        o_ref[...] = ((o_n + post[:, 0:HW]) * post[:, HW:2 * HW]).reshape(nbatch, tb, HW)


RWKV_TB = 512


def _rwkv(slab, mu, vec, ww2, wa2, wg2, b, s):
    tb = min(RWKV_TB, s)
    nb = s // tb
    ng = (tb // CHUNK) * HEADS
    grp = lambda dt=F32: pltpu.VMEM((b, ng, CHUNK, DH), dt)
    full = lambda shape: pl.BlockSpec(shape, lambda j, i: (0,) * len(shape))
    out = pl.pallas_call(
        functools.partial(_rwkv_kernel, tb=tb, nbatch=b),
        grid=(nb, b),
        in_specs=[pl.BlockSpec((tb, 4 * HW), lambda j, i: (i * nb + j, OFF_RWKV // (4 * HW))),
                  full((1, 4 * HW)), full((8, HW)), full((64, HW)), full((64, HW)), full((128, HW))],
        out_specs=pl.BlockSpec((b, tb, HW), lambda j, i: (0, j, 0)),
        out_shape=jax.ShapeDtypeStruct((b, s, HW), F32),
        scratch_shapes=[pltpu.VMEM((tb + 8, 4 * HW), F32), pltpu.VMEM((b, HEADS, DH, DH), F32),
                        grp(BF16), grp(), grp(BF16), grp(), grp(BF16), grp(BF16), grp(),
                        pltpu.VMEM((b, ng, 1, DH), F32), pltpu.VMEM((b, tb, HW), F32),
                        pltpu.VMEM((b, 8, 4 * HW), F32), pltpu.VMEM((b, tb, 2 * HW), F32)],
        compiler_params=_cparams(2),
        name="rwkv7",
    )(slab, mu, vec, ww2, wa2, wg2)
    return out.reshape(b * s, HW)


def _rope(x, cos_t, sin_lo, sin_hi, half):
    w = x.shape[-1]
    return x * cos_t + pltpu.roll(x, w - half, 1) * sin_lo + pltpu.roll(x, half, 1) * sin_hi


def _tile_lanes(t, n):
    return t if n == 1 else jnp.concatenate([t] * n, axis=1)


def _mla_prep_kernel(cq_ref, ckv_ref, kr_ref, cs_ref, e_ref, base_ref, qg_ref, kvg_ref, wq_ref, wk_ref,
                     wvt_ref, q_ref, k_ref, vt_ref):
    cq = cq_ref[...]
    hq = cq * lax.rsqrt(jnp.mean(cq * cq, axis=-1, keepdims=True) + NORM_EPS) * qg_ref[...]
    ckv = ckv_ref[...]
    hkv = ckv * lax.rsqrt(jnp.mean(ckv * ckv, axis=-1, keepdims=True) + NORM_EPS) * kvg_ref[...]
    cos_t, sin_lo, sin_hi = _rope_table(cs_ref, e_ref, base_ref)
    q = _mm(hq, wq_ref[...])
    q = _rope(q, _tile_lanes(cos_t, HEADS), _tile_lanes(sin_lo, HEADS), _tile_lanes(sin_hi, HEADS),
              MLA_ROPE // 2)
    q_ref[...] = (q * ((MLA_NOPE + MLA_ROPE) ** -0.5 * LOG2E)).astype(BF16)
    kr = _rope(kr_ref[...], cos_t, sin_lo, sin_hi, MLA_ROPE // 2)
    k = _mm(hkv, wk_ref[...]) + _tile_lanes(kr, HEADS)
    k_ref[...] = k.astype(BF16)
    vt_ref[...] = lax.dot_general(wvt_ref[...], hkv.astype(BF16), NT,
                                  preferred_element_type=F32).astype(BF16)


def _mla_prep(slab, cs, qg, kvg, wq, wk, wvt):
    t = slab.shape[0]
    tm = min(512, t)
    full = lambda shape: pl.BlockSpec(shape, lambda i: (0,) * len(shape))
    e, base = _rope_expansion(MLA_ROPE, MLA_NOPE, MLA_QK)
    return pl.pallas_call(
        _mla_prep_kernel,
        grid=(t // tm,),
        in_specs=[pl.BlockSpec((tm, 256), lambda i: (i, OFF_MLA_CQ // 256)),
                  pl.BlockSpec((tm, 128), lambda i: (i, OFF_MLA_CKV // 128)),
                  pl.BlockSpec((tm, 128), lambda i: (i, OFF_MLA_KR // 128)),
                  pl.BlockSpec((tm, MLA_ROPE), lambda i: (i, 0)), full(e.shape), full(base.shape),
                  full((1, 256)), full((1, 128)), full((256, 512)), full((128, 512)), full((256, 128))],
        out_specs=[pl.BlockSpec((tm, 512), lambda i: (i, 0)), pl.BlockSpec((tm, 512), lambda i: (i, 0)),
                   pl.BlockSpec((256, tm), lambda i: (0, i))],
        out_shape=[jax.ShapeDtypeStruct((t, 512), BF16), jax.ShapeDtypeStruct((t, 512), BF16),
                   jax.ShapeDtypeStruct((256, t), BF16)],
        compiler_params=_cparams(1),
        name="mla_prep",
    )(slab, slab, slab, cs, e, base, qg, kvg, wq, wk, wvt)


def _flash_kernel(q_ref, k_ref, vt_ref, o_ref, m_ref, l_ref, acc_ref, *, tq, tk):
    qi = pl.program_id(1)
    ki = pl.program_id(2)

    @pl.when(ki == 0)
    def _():
        m_ref[...] = jnp.full_like(m_ref, NEG_INF)
        l_ref[...] = jnp.zeros_like(l_ref)
        acc_ref[...] = jnp.zeros_like(acc_ref)

    def step(on_diagonal):
        if on_diagonal:
            causal = _iota2((tk, tq), 0) <= _iota2((tk, tq), 1)
        vt = vt_ref[...]
        ones = jnp.ones((16, tk), BF16)
        scores = [lax.dot_general(k_ref[:, h * MLA_QK:(h + 1) * MLA_QK], q_ref[:, h * MLA_QK:(h + 1) * MLA_QK],
                                  NT, preferred_element_type=F32) for h in range(HEADS)]
        for h in range(HEADS):
            s = scores[h]
            if on_diagonal:
                s = jnp.where(causal, s, NEG_INF)
            m_old = m_ref[h]
            m_new = jnp.maximum(m_old, jnp.max(s, axis=0, keepdims=True))
            alpha = jnp.exp2(m_old - m_new)
            p = jnp.exp2(s - m_new).astype(BF16)
            pv = jnp.dot(jnp.concatenate([vt[h * DH:(h + 1) * DH, :], ones], axis=0), p,
                         preferred_element_type=F32)
            l_ref[h] = alpha * l_ref[h] + pv[DH:DH + 1, :]
            acc_ref[h * DH:(h + 1) * DH, :] = alpha * acc_ref[h * DH:(h + 1) * DH, :] + pv[0:DH, :]
            m_ref[h] = m_new

    pl.when(ki < qi)(lambda: step(False))
    pl.when(ki == qi)(lambda: step(True))

    @pl.when(ki == pl.num_programs(2) - 1)
    def _():
        o_t = jnp.concatenate([acc_ref[h * DH:(h + 1) * DH, :] / l_ref[h] for h in range(HEADS)], axis=0)
        o_ref[...] = o_t.T


def _flash(q, k, vt, b, s):
    tq = min(512, s)
    tk = tq
    nq = s // tq
    nk = s // tk
    return pl.pallas_call(
        functools.partial(_flash_kernel, tq=tq, tk=tk),
        grid=(b, nq, nk),
        in_specs=[pl.BlockSpec((tq, 512), lambda i, j, kk: (i * nq + j, 0)),
                  pl.BlockSpec((tk, 512), lambda i, j, kk: (i * nk + jnp.minimum(kk, j), 0)),
                  pl.BlockSpec((256, tk), lambda i, j, kk: (0, i * nk + jnp.minimum(kk, j)))],
        out_specs=pl.BlockSpec((tq, 256), lambda i, j, kk: (i * nq + j, 0)),
        out_shape=jax.ShapeDtypeStruct((b * s, 256), F32),
        scratch_shapes=[pltpu.VMEM((HEADS, 1, tq), F32), pltpu.VMEM((HEADS, 1, tq), F32),
                        pltpu.VMEM((HW, tq), F32)],
        compiler_params=_cparams(3),
        name="mla_flash",
    )(q, k, vt)


DSA_TQ = 256
DSA_TK = 512


def _dsa_prep_kernel(iq_ref, ikw_ref, q_ref, k_ref, v_ref, cs_ref, e_ref, base_ref,
                     ik_ref, ko_ref, vt_ref, iqt_ref, qt_ref, iwt_ref):
    cos_t, sin_lo, sin_hi = _rope_table(cs_ref, e_ref, base_ref)
    half = DH // 8
    rope = lambda x, n: _rope(x, _tile_lanes(cos_t, n), _tile_lanes(sin_lo, n), _tile_lanes(sin_hi, n), half)
    ikw = ikw_ref[...]
    ik_ref[...] = rope(ikw, 1)[:, 0:DH].astype(BF16)
    kr = rope(k_ref[...], 2)
    for h in range(HEADS):
        ko_ref[h] = kr[:, h * DH:(h + 1) * DH].astype(BF16)
    vt_ref[0] = v_ref[...].T.astype(BF16)
    iqt_ref[...] = rope(iq_ref[...], 4).T.astype(BF16)
    qt_ref[...] = (rope(q_ref[...], 2) * (DH ** -0.5 * LOG2E)).T.astype(BF16)
    iwt_ref[...] = ikw.T[DH:DH + 8, :] * (IDX_HEADS ** -0.5 * DH ** -0.5)


def _dsa_prep(slab, cs):
    t = slab.shape[0]
    tm = min(DSA_TK, t)
    full = lambda shape: pl.BlockSpec(shape, lambda i: (0,) * len(shape))
    e, base = _rope_expansion(DH // 4, 0, DH)
    return pl.pallas_call(
        _dsa_prep_kernel,
        grid=(t // tm,),
        in_specs=[pl.BlockSpec((tm, IDX_HEADS * DH), lambda i: (i, OFF_DSA_IQ // (IDX_HEADS * DH))),
                  pl.BlockSpec((tm, 128), lambda i: (i, OFF_DSA_IKW // 128)),
                  pl.BlockSpec((tm, 256), lambda i: (i, OFF_DSA_Q // 256)),
                  pl.BlockSpec((tm, 256), lambda i: (i, OFF_DSA_K // 256)),
                  pl.BlockSpec((tm, 256), lambda i: (i, OFF_DSA_V // 256)),
                  pl.BlockSpec((tm, DH // 4), lambda i: (i, 0)), full(e.shape), full(base.shape)],
        out_specs=[pl.BlockSpec((tm, DH), lambda i: (i, 0)),
                   pl.BlockSpec((HEADS, tm, DH), lambda i: (0, i, 0)),
                   pl.BlockSpec((1, HW, tm), lambda i: (i, 0, 0)),
                   pl.BlockSpec((IDX_HEADS * DH, tm), lambda i: (0, i)),
                   pl.BlockSpec((HW, tm), lambda i: (0, i)),
                   pl.BlockSpec((8, tm), lambda i: (0, i))],
        out_shape=[jax.ShapeDtypeStruct((t, DH), BF16), jax.ShapeDtypeStruct((HEADS, t, DH), BF16),
                   jax.ShapeDtypeStruct((t // tm, HW, tm), BF16),
                   jax.ShapeDtypeStruct((IDX_HEADS * DH, t), BF16), jax.ShapeDtypeStruct((HW, t), BF16),
                   jax.ShapeDtypeStruct((8, t), F32)],
        compiler_params=_cparams(1),
        name="dsa_prep",
    )(slab, slab, slab, slab, slab, cs, e, base)


def _dsa_kernel(iqt_ref, iwt_ref, qt_ref, ik_ref, k_ref, vt_ref, kidx_ref, o_ref,
                hi_ref, lo_ref, lom_ref, bias_ref, acc_ref, s_ref, m_ref, l_ref,
                *, tq, tk, n_sel, idx_bits):
    qi = pl.program_id(1)
    nkt = (qi * tq + tq - 1) // tk + 1
    iqt = iqt_ref[...]
    iw = iwt_ref[...]
    q_pos = qi * tq + _iota2((tk, tq), 1)

    def score_body(kt, carry):
        row0 = pl.multiple_of(kt * tk, tk)
        ik = ik_ref[pl.ds(row0, tk), :]
        sc = jnp.zeros((tk, tq), F32)
        for h in range(IDX_HEADS):
            lg = lax.dot_general(ik, iqt[h * DH:(h + 1) * DH, :], NN, preferred_element_type=F32)
            sc = sc + jnp.maximum(lg, 0.0) * iw[h:h + 1, :]
        bits = lax.bitcast_convert_type(sc, I32)
        key = bits ^ ((bits >> 31) & 0x7FFFFFFF)
        key = jnp.where(sc == 0.0, 0, key)
        k_pos = row0 + _iota2((tk, tq), 0)
        key = jnp.where(k_pos <= q_pos, key, INT_MIN)
        hi_ref[pl.ds(row0, tk), :] = (key >> 16).astype(I16)
        lo_ref[pl.ds(row0, tk), :] = ((key & 0xFFFF) - 32768).astype(I16)
        return carry

    lax.fori_loop(0, nkt, score_body, 0)

    one16 = jnp.ones((tk, tq), I16)
    zero16 = jnp.zeros((tk, tq), I16)

    def count(*pred_fns):
        def body(kt, accs):
            sl = pl.ds(pl.multiple_of(kt * tk, tk), tk)
            args = (hi_ref[sl, :], lo_ref[sl, :], lom_ref[sl, :], kidx_ref[sl, :])
            out = []
            for acc, fn in zip(accs, pred_fns):
                ones = jnp.where(fn(*args), one16, zero16).reshape(tk // 64, 64, tq)
                for g in range(tk // 64):
                    acc = acc + ones[g]
                out.append(acc)
            return tuple(out)
        parts = lax.fori_loop(0, nkt, body, tuple(jnp.zeros((64, tq), I16) for _ in pred_fns))
        sums = tuple(jnp.sum(p.astype(I32).astype(F32), axis=0, keepdims=True) for p in parts)
        return sums[0] if len(sums) == 1 else sums

    def bisect16(pred_ge):
        def body(it, thr):
            bit = jnp.maximum(15 - it, 0)
            cand = jnp.where(it == 0, jnp.zeros_like(thr), thr | jnp.left_shift(jnp.int32(1), bit))
            c16 = cand.astype(I16)
            cnt = count(lambda hi, lo, lom, idx: pred_ge(hi, lo, lom, c16))
            return jnp.where(cnt >= float(n_sel), cand, thr)
        return lax.fori_loop(0, 16, body, jnp.full((1, tq), -32768, I32))

    t_hi = bisect16(lambda hi, lo, lom, c: hi >= c)
    t_hi16 = t_hi.astype(I16)

    def mask_lo_body(kt, carry):
        sl = pl.ds(pl.multiple_of(kt * tk, tk), tk)
        hi = hi_ref[sl, :]
        lom_ref[sl, :] = jnp.where(hi > t_hi16, jnp.int16(32767),
                                   jnp.where(hi == t_hi16, lo_ref[sl, :], jnp.int16(-32768)))
        return carry

    lax.fori_loop(0, nkt, mask_lo_body, 0)
    t_lo16 = bisect16(lambda hi, lo, lom, c: lom >= c).astype(I16)

    def is_gt(hi, lo):
        return (hi > t_hi16) | ((hi == t_hi16) & (lo > t_lo16))

    def is_eq(hi, lo):
        return (hi == t_hi16) & (lo == t_lo16)

    cnt_gt, cnt_eq = count(lambda hi, lo, lom, idx: is_gt(hi, lo), lambda hi, lo, lom, idx: is_eq(hi, lo))
    need = float(n_sel) - cnt_gt

    def idx_body(it, j):
        cand = j + jnp.left_shift(jnp.int32(1), idx_bits - 1 - it)
        c16 = cand.astype(I16)
        cnt = count(lambda hi, lo, lom, idx: is_eq(hi, lo) & (idx < c16))
        return jnp.where(cnt < need, cand, j)

    excess = jnp.max(jnp.where(cnt_eq > need, 1.0, 0.0)) > 0.0
    j_cut16 = lax.cond(excess,
                       lambda: lax.fori_loop(0, idx_bits, idx_body, jnp.zeros((1, tq), I32)),
                       lambda: jnp.full((1, tq), 32767, I32)).astype(I16)

    def bias_body(kt, carry):
        sl = pl.ds(pl.multiple_of(kt * tk, tk), tk)
        hi, lo = hi_ref[sl, :], lo_ref[sl, :]
        sel = (is_gt(hi, lo) | (is_eq(hi, lo) & (kidx_ref[sl, :] <= j_cut16))) & (hi != jnp.int16(-32768))
        bias_ref[sl, :] = jnp.where(sel, jnp.zeros((tk, tq), BF16),
                                    jnp.full((tk, tq), NEG_INF, BF16)).astype(F32)
        return carry

    lax.fori_loop(0, nkt, bias_body, 0)

    qt = qt_ref[...]
    acc_ref[...] = jnp.zeros_like(acc_ref)

    def att_body(kt, carry):
        row0 = pl.multiple_of(kt * tk, tk)
        for h in range(HEADS):
            s_ref[h] = lax.dot_general(k_ref[h, pl.ds(row0, tk), :], qt[h * DH:(h + 1) * DH, :], NN,
                                       preferred_element_type=F32)

        @pl.when(kt >= 0)
        def _():
            bias = bias_ref[pl.ds(row0, tk), :]
            ones = jnp.ones((16, tk), BF16)
            for h in range(HEADS):
                hs = slice(h * DH, (h + 1) * DH)
                s = s_ref[h] + bias
                m_old = m_ref[h]
                m_new = jnp.maximum(m_old, jnp.max(s, axis=0, keepdims=True))
                alpha = jnp.exp2(m_old - m_new)
                p = jnp.exp2(s - m_new).astype(BF16)
                pv = lax.dot_general(jnp.concatenate([vt_ref[kt, hs, :], ones], axis=0), p, NN,
                                     preferred_element_type=F32)
                acc_ref[hs, :] = alpha * acc_ref[hs, :] + pv[0:DH, :]
                m_ref[h] = m_new
                l_ref[h] = alpha * l_ref[h] + pv[DH:DH + 1, :]
        return carry

    m_ref[...] = jnp.full_like(m_ref, NEG_INF)
    l_ref[...] = jnp.zeros_like(l_ref)
    lax.fori_loop(0, nkt, att_body, 0)
    o_t = jnp.concatenate([acc_ref[h * DH:(h + 1) * DH, :] / l_ref[h] for h in range(HEADS)], axis=0)
    o_ref[...] = o_t.T


def _dsa(iqt, iwt, qt, ik, kr, vt, b, s, n_sel):
    tq = min(DSA_TQ, s)
    tk = min(DSA_TK, s)
    nq = s // tq
    nkt = s // tk
    return pl.pallas_call(
        functools.partial(_dsa_kernel, tq=tq, tk=tk, n_sel=n_sel, idx_bits=int(math.log2(s))),
        grid=(b, nq),
        in_specs=[pl.BlockSpec((IDX_HEADS * DH, tq), lambda i, j: (0, i * nq + j)),
                  pl.BlockSpec((8, tq), lambda i, j: (0, i * nq + j)),
                  pl.BlockSpec((HW, tq), lambda i, j: (0, i * nq + j)),
                  pl.BlockSpec((s, DH), lambda i, j: (i, 0)),
                  pl.BlockSpec((HEADS, s, DH), lambda i, j: (0, i, 0)),
                  pl.BlockSpec((nkt, HW, tk), lambda i, j: (i, 0, 0)),
                  pl.BlockSpec((s, tq), lambda i, j: (0, 0))],
        out_specs=pl.BlockSpec((tq, HW), lambda i, j: (i * nq + j, 0)),
        out_shape=jax.ShapeDtypeStruct((b * s, HW), F32),
        scratch_shapes=[pltpu.VMEM((s, tq), I16), pltpu.VMEM((s, tq), I16), pltpu.VMEM((s, tq), I16),
                        pltpu.VMEM((s, tq), F32), pltpu.VMEM((HW, tq), F32),
                        pltpu.VMEM((HEADS, tk, tq), F32), pltpu.VMEM((HEADS, 1, tq), F32),
                        pltpu.VMEM((HEADS, 1, tq), F32)],
        compiler_params=_cparams(2),
        name="dsa",
    )(iqt, iwt, qt, ik, kr, vt,
      jnp.broadcast_to(jnp.arange(s, dtype=I16)[:, None], (s, tq)))


def _merge_kernel(x_ref, oa_ref, ob_ref, oc_ref, od_ref, ga_ref, gb_ref, gc_ref, gd_ref,
                  wa_ref, wb_ref, wc_ref, wd_ref, wo_ref, o_ref):
    def branch(o_r, g_r, w_r):
        y = jnp.dot(o_r[...].astype(BF16), w_r[...], preferred_element_type=F32)
        return _sigmoid(g_r[...].astype(F32)) * y

    merged = (branch(oa_ref, ga_ref, wa_ref) + branch(ob_ref, gb_ref, wb_ref)
              + branch(oc_ref, gc_ref, wc_ref) + branch(od_ref, gd_ref, wd_ref))
    o_ref[...] = x_ref[...] + jnp.dot(merged.astype(BF16), wo_ref[...], preferred_element_type=F32)


def _merge(x2d, oa, ob, oc, od, gates, wa, wb, wc, wd, wo):
    t, d = x2d.shape
    tm = min(512, t)
    row = lambda w: pl.BlockSpec((tm, w), lambda i: (i, 0))
    gate = lambda n: pl.BlockSpec((tm, d), lambda i, o=n: (i, o))
    full = lambda shape: pl.BlockSpec(shape, lambda i: (0,) * len(shape))
    return pl.pallas_call(
        _merge_kernel,
        grid=(t // tm,),
        in_specs=[row(d), row(HW), row(HW), row(HW), row(HW), gate(0), gate(1), gate(2), gate(3),
                  full((HW, d)), full((HW, d)), full((HW, d)), full((HW, d)), full((d, d))],
        out_specs=row(d),
        out_shape=jax.ShapeDtypeStruct((t, d), F32),
        compiler_params=_cparams(1),
        name="merge",
    )(x2d, oa, ob, oc, od, gates, gates, gates, gates, wa, wb, wc, wd, wo)


def _mlp_kernel(x_ref, g_ref, wu_ref, wd_ref, fg_ref, o_ref, h_ref, acc_ref, *, final_norm):
    fi = pl.program_id(1)

    @pl.when(fi == 0)
    def _():
        x = x_ref[...]
        ms = jnp.mean(x * x, axis=-1, keepdims=True)
        h_ref[...] = (x * lax.rsqrt(ms + NORM_EPS) * g_ref[...]).astype(BF16)
        acc_ref[...] = jnp.zeros_like(acc_ref)

    u = jnp.dot(h_ref[...], wu_ref[...], preferred_element_type=F32)
    u = jnp.square(jnp.maximum(u, 0.0))
    acc_ref[...] += jnp.dot(u.astype(BF16), wd_ref[...], preferred_element_type=F32)

    @pl.when(fi == pl.num_programs(1) - 1)
    def _():
        y = x_ref[...] + acc_ref[...]
        if final_norm:
            ms = jnp.mean(y * y, axis=-1, keepdims=True)
            y = y * lax.rsqrt(ms + NORM_EPS) * fg_ref[...]
        o_ref[...] = y


def _mlp(x2d, g, wu, wd, fg, final_norm):
    t, d = x2d.shape
    f = wu.shape[1]
    tm = min(1024, t)
    tf = 512
    return pl.pallas_call(
        functools.partial(_mlp_kernel, final_norm=final_norm),
        grid=(t // tm, f // tf),
        in_specs=[pl.BlockSpec((tm, d), lambda i, j: (i, 0)),
                  pl.BlockSpec((1, d), lambda i, j: (0, 0)),
                  pl.BlockSpec((d, tf), lambda i, j: (0, j)),
                  pl.BlockSpec((tf, d), lambda i, j: (j, 0)),
                  pl.BlockSpec((1, d), lambda i, j: (0, 0))],
        out_specs=pl.BlockSpec((tm, d), lambda i, j: (i, 0)),
        out_shape=jax.ShapeDtypeStruct((t, d), F32),
        scratch_shapes=[pltpu.VMEM((tm, d), BF16), pltpu.VMEM((tm, d), F32)],
        compiler_params=_cparams(2),
        name="mlp",
    )(x2d, g, wu, wd, fg)


def _slab_weight(w_in, dtype=BF16):
    d = w_in.shape[0]
    z = lambda n: jnp.zeros((d, n), w_in.dtype)
    gdn, mla, rwkv, dsa = 0, 1032, 1448, 2472
    gate = 3824
    cols = [
        w_in[:, rwkv:rwkv + 1024],
        w_in[:, gdn:gdn + 768],
        w_in[:, gdn + 776:gdn + 1032],
        w_in[:, mla:mla + 256],
        w_in[:, mla + 256:mla + 384],
        z(64), w_in[:, mla + 384:mla + 416], z(32),
        w_in[:, dsa:dsa + 512],
        w_in[:, dsa + 584:dsa + 1352],
        w_in[:, gdn + 768:gdn + 776], z(120),
        w_in[:, dsa + 512:dsa + 584], z(56),
        w_in[:, gate:gate + 4096],
    ]
    return jnp.concatenate(cols, axis=1).astype(dtype)


def _mla_weights(w_uq, w_ukv):
    wq = w_uq.reshape(MLA_Q_LORA, HEADS, MLA_NOPE + MLA_ROPE)
    wq = jnp.pad(wq, ((0, 0), (0, 0), (0, MLA_QK - MLA_NOPE - MLA_ROPE))).reshape(MLA_Q_LORA, HEADS * MLA_QK)
    wkv = w_ukv.reshape(MLA_KV_LORA, HEADS, MLA_NOPE + DH)
    wk = jnp.pad(wkv[:, :, :MLA_NOPE], ((0, 0), (0, 0), (0, MLA_QK - MLA_NOPE))).reshape(MLA_KV_LORA, HEADS * MLA_QK)
    wvt = wkv[:, :, MLA_NOPE:].reshape(MLA_KV_LORA, HW).T
    return wq.astype(BF16), wk.astype(BF16), wvt.astype(BF16)


def _rope_cos_sin(positions, rot_dims):
    inv_freq = ROPE_THETA ** (-jnp.arange(0, rot_dims, 2, dtype=F32) / rot_dims)
    ang = positions.astype(F32)[..., None] * inv_freq
    return jnp.concatenate([jnp.cos(ang), jnp.sin(ang)], axis=-1)


def _rope_expansion(rot_dims, offset, period):
    half = rot_dims // 2
    e = np.zeros((rot_dims, 3 * 128), np.float32)
    base = np.zeros((1, 3 * 128), np.float32)
    base[0, 0:128] = 1.0
    for rep in range(128 // period):
        lo = rep * period + offset
        for j in range(half):
            e[j, lo + j] = 1.0
            e[j, lo + half + j] = 1.0
            base[0, lo + j] = 0.0
            base[0, lo + half + j] = 0.0
            e[half + j, 128 + lo + j] = -1.0
            e[half + j, 256 + lo + half + j] = 1.0
    return jnp.asarray(e), jnp.asarray(base)


def _rope_table(cs_ref, e_ref, base_ref):
    tab = _mm_exact_rhs(cs_ref[...], e_ref[...]) + base_ref[...]
    return tab[:, 0:128], tab[:, 128:256], tab[:, 256:384]


def _layer(x2d, b, s, prm, tabs, final_g, is_last):
    (mix_g, w_slab, conv_w, gdn_pcol, gdn_prow, gdn_ng, mla_qg, mla_kvg, wq, wk, wv,
     rwkv_mu, rwkv_vec, ww2, wa2, wg2, wo_gdn, wo_mla, wo_rwkv, wo_dsa, w_o, mlp_g, w_up, w_down) = prm
    mla_cs, dsa_cs = tabs
    t = b * s
    slab = _in_proj(x2d, mix_g, w_slab[:, :OFF_GATE], F32)
    gates = _in_proj(x2d, mix_g, w_slab[:, OFF_GATE:], BF16)

    ab = slab[:, OFF_GDN_AB:OFF_GDN_AB + 8].reshape(b, s // CHUNK, CHUNK, 8)
    abr = jnp.swapaxes(ab, 2, 3)
    o_gdn = _gdn(slab, abr, conv_w, gdn_pcol, gdn_prow, gdn_ng, b, s)
    q, k, v = _mla_prep(slab, mla_cs, mla_qg, mla_kvg, wq, wk, wv)
    o_mla = _flash(q, k, v, b, s)
    o_rwkv = _rwkv(slab, rwkv_mu, rwkv_vec, ww2, wa2, wg2, b, s)
    ik, kr, vt, iqt, qt, iwt = _dsa_prep(slab, dsa_cs)
    o_dsa = _dsa(iqt, iwt, qt, ik, kr, vt, b, s, min(IDX_TOPK_MAX, s // 4))

    x2d = _merge(x2d, o_gdn, o_mla, o_rwkv, o_dsa, gates, wo_gdn, wo_mla, wo_rwkv, wo_dsa, w_o)
    return _mlp(x2d, mlp_g, w_up, w_down, final_g, is_last)


def kernel(x, positions, mix_norm_g, w_in, gdn_conv_w, gdn_a_log, gdn_dt_bias, gdn_norm_g, mla_q_norm_g, mla_kv_norm_g, mla_w_uq, mla_w_ukv, rwkv_mu, rwkv_w0, rwkv_w_w2, rwkv_a0, rwkv_w_a2, rwkv_w_g2, rwkv_k_k, rwkv_k_a, rwkv_r_k, rwkv_ln_w, rwkv_ln_b, w_out_gdn, w_out_mla, w_out_rwkv, w_out_dsa, w_o, mlp_norm_g, w_up, w_down, final_norm_g):
    b, s, d = x.shape
    depth = w_in.shape[0]
    tabs = (_rope_cos_sin(positions, MLA_ROPE).reshape(b * s, MLA_ROPE),
            _rope_cos_sin(positions, DH // 4).reshape(b * s, DH // 4))
    row = lambda a: a.reshape(1, -1).astype(F32)
    pad8 = lambda rows: jnp.concatenate(rows + [jnp.zeros((8 - len(rows), rows[0].shape[1]), F32)], axis=0)
    x2d = x.reshape(b * s, d)
    final_g = row(final_norm_g)
    for l in range(depth):
        lane4 = lambda a: jnp.pad(a.astype(F32), (0, 124)).reshape(1, 128)
        gdn_pcol = pad8([lane4(gdn_a_log[l]), lane4(gdn_dt_bias[l])])
        gdn_prow = jnp.pad(jnp.stack([gdn_a_log[l], gdn_dt_bias[l]], axis=1).astype(F32), ((0, 4), (0, 126)))
        wq, wk, wv = _mla_weights(mla_w_uq[l], mla_w_ukv[l])
        rwkv_vec = pad8([row(rwkv_w0[l]), row(rwkv_a0[l]), row(rwkv_k_k[l]), row(rwkv_k_a[l]),
                         row(rwkv_ln_w[l]), row(rwkv_ln_b[l]), row(rwkv_r_k[l])])
        prm = (row(mix_norm_g[l]), _slab_weight(w_in[l]), gdn_conv_w[l].astype(F32), gdn_pcol, gdn_prow,
               row(jnp.tile(gdn_norm_g[l], HEADS)), row(mla_q_norm_g[l]), row(mla_kv_norm_g[l]), wq, wk, wv,
               row(rwkv_mu[l]), rwkv_vec, rwkv_w_w2[l].astype(F32), rwkv_w_a2[l].astype(F32),
               rwkv_w_g2[l].astype(F32),
               w_out_gdn[l].astype(BF16), w_out_mla[l].astype(BF16), w_out_rwkv[l].astype(BF16),
               w_out_dsa[l].astype(BF16), w_o[l].astype(BF16), row(mlp_norm_g[l]),
               w_up[l].astype(BF16), w_down[l].astype(BF16))
        x2d = _layer(x2d, b, s, prm, tabs, final_g, l == depth - 1)
    return x2d.reshape(b, s, d)
```

```python
import functools
import math

import numpy as np
import jax
import jax.numpy as jnp
from jax import lax
from jax.experimental import pallas as pl
from jax.experimental.pallas import tpu as pltpu

F32 = jnp.float32
BF16 = jnp.bfloat16
I32 = jnp.int32
I16 = jnp.int16

D_MODEL = 1024
DEPTH = 4
ROPE_THETA = 500000.0
NORM_EPS = 1e-6
NEG_INF = -1e30
HEADS = 4
DH = 64
HW = HEADS * DH
GDN_CONV = 4
CHUNK = 64
MLA_Q_LORA = 256
MLA_KV_LORA = 128
MLA_NOPE = 64
MLA_ROPE = 32
MLA_QK = 128
RWKV_GN_EPS = 64e-5
IDX_HEADS = 8
IDX_TOPK_MAX = 256
D_FF = 4 * D_MODEL
INT_MIN = -2 ** 31
LOG2E = 1.4426950408889634

VMEM_LIMIT_BYTES = 56 * 1024 * 1024

OFF_RWKV = 0
OFF_GDN_Q = 1024
OFF_GDN_K = 1280
OFF_GDN_V = 1536
OFF_GDN_Z = 1792
OFF_MLA_CQ = 2048
OFF_MLA_CKV = 2304
OFF_MLA_KR = 2432
OFF_DSA_IQ = 2560
OFF_DSA_Q = 3072
OFF_DSA_K = 3328
OFF_DSA_V = 3584
OFF_GDN_AB = 3840
OFF_DSA_IKW = 3968
OFF_GATE = 4096
N_SLAB = 8192

NN = (((1,), (0,)), ((), ()))
NT = (((1,), (1,)), ((), ()))
TN = (((0,), (0,)), ((), ()))
BNN = (((2,), (1,)), ((0,), (0,)))
BNT = (((2,), (2,)), ((0,), (0,)))
BTN = (((1,), (1,)), ((0,), (0,)))


def _cparams(n_axes):
    return pltpu.CompilerParams(dimension_semantics=("arbitrary",) * n_axes,
                                vmem_limit_bytes=VMEM_LIMIT_BYTES)


def _mm(a, b, dims=NN):
    return lax.dot_general(a.astype(BF16), b.astype(BF16), dims, preferred_element_type=F32)


def _mm_exact_lhs(m, x, dims=NN):
    mb = m.astype(BF16)
    x0 = x.astype(BF16)
    r1 = x - x0.astype(F32)
    x1 = r1.astype(BF16)
    x2 = (r1 - x1.astype(F32)).astype(BF16)
    d = lambda y: lax.dot_general(mb, y, dims, preferred_element_type=F32)
    return d(x0) + (d(x1) + d(x2))


def _mm_exact_rhs(x, m, dims=NN):
    mb = m.astype(BF16)
    x0 = x.astype(BF16)
    r1 = x - x0.astype(F32)
    x1 = r1.astype(BF16)
    x2 = (r1 - x1.astype(F32)).astype(BF16)
    d = lambda y: lax.dot_general(y, mb, dims, preferred_element_type=F32)
    return d(x0) + (d(x1) + d(x2))


def _iota2(shape, dim):
    return lax.broadcasted_iota(I32, shape, dim)


def _seg_ones(width, seg):
    sh = int(math.log2(seg))
    r = _iota2((width, width), 0) >> sh
    c = _iota2((width, width), 1) >> sh
    return jnp.where(r == c, 1.0, 0.0).astype(F32)


def _seg_sum(x, seg_ones):
    return _mm_exact_rhs(x, seg_ones)


def _softplus(x):
    return jnp.maximum(x, 0.0) + jnp.log1p(jnp.exp(-jnp.abs(x)))


def _sigmoid(x):
    return 1.0 / (1.0 + jnp.exp(-x))


def _silu(x):
    return x * _sigmoid(x)


def _tri_masks(n):
    r = _iota2((n, n), 0)
    c = _iota2((n, n), 1)
    return r, c


def _unit_lower_inverse(m, r, c, dims=NN):
    eye = jnp.where(r == c, 1.0, 0.0).astype(F32)
    same16 = (r >> 4) == (c >> 4)
    same32 = (r >> 5) == (c >> 5)
    md = jnp.where(same16, m, 0.0)
    t = eye - md
    p = md
    for _ in range(3):
        p = _mm(p, p, dims)
        t = t + _mm(t, p, dims)
    c32 = jnp.where(same32 & jnp.logical_not(same16), m, 0.0)
    t = t - _mm(_mm(t, c32, dims), t, dims)
    c64 = jnp.where(same32, 0.0, m)
    t = t - _mm(_mm(t, c64, dims), t, dims)
    return t


IN_PROJ_TN = 512


def _in_proj_kernel(x_ref, g_ref, w_ref, o_ref):
    x = x_ref[...]
    ms = jnp.mean(x * x, axis=-1, keepdims=True)
    h = (x * lax.rsqrt(ms + NORM_EPS) * g_ref[...]).astype(BF16)
    for n0 in range(0, o_ref.shape[1], IN_PROJ_TN):
        o_ref[:, n0:n0 + IN_PROJ_TN] = jnp.dot(
            h, w_ref[:, n0:n0 + IN_PROJ_TN], preferred_element_type=F32).astype(o_ref.dtype)


def _in_proj(x2d, g, w, out_dtype):
    t, d = x2d.shape
    n = w.shape[1]
    tm = min(512, t)
    return pl.pallas_call(
        _in_proj_kernel,
        grid=(t // tm,),
        in_specs=[pl.BlockSpec((tm, d), lambda i: (i, 0)),
                  pl.BlockSpec((1, d), lambda i: (0, 0)),
                  pl.BlockSpec((d, n), lambda i: (0, 0))],
        out_specs=pl.BlockSpec((tm, n), lambda i: (i, 0)),
        out_shape=jax.ShapeDtypeStruct((t, n), out_dtype),
        compiler_params=_cparams(1),
        name="in_proj",
    )(x2d, g, w)


def _to_groups(x, nchunk):
    parts = [x[:, h * DH:(h + 1) * DH].reshape(nchunk, 1, CHUNK, DH) for h in range(HEADS)]
    return jnp.concatenate(parts, axis=1).reshape(nchunk * HEADS, CHUNK, DH)


def _col_groups(x, lane0, nchunk):
    parts = [x[:, lane0 + h:lane0 + h + 1].reshape(nchunk, 1, CHUNK, 1) for h in range(HEADS)]
    return jnp.concatenate(parts, axis=1).reshape(nchunk * HEADS, CHUNK, 1)


def _gdn_kernel(q_ref, k_ref, v_ref, z_ref, ab_ref, abr_ref, cw_ref, pcol_ref, prow_ref, ng_ref,
                o_ref, xbuf_ref, state_ref, u_ref, w_ref, a_ref, qd_ref, kd_ref, egl_ref, oraw_ref,
                carry_ref, *, tb, nbatch):
    nchunk = tb // CHUNK
    bi = pl.program_id(1)
    xbuf = xbuf_ref

    @pl.when(pl.program_id(0) == 0)
    def _():
        carry_ref[bi] = jnp.zeros((8, 3 * HW), F32)
        state_ref[bi] = jnp.zeros((HEADS, DH, DH), F32)

    xbuf[0:8, :] = carry_ref[bi]
    xbuf[8:8 + tb, 0:HW] = q_ref[...]
    xbuf[8:8 + tb, HW:2 * HW] = k_ref[...]
    xbuf[8:8 + tb, 2 * HW:3 * HW] = v_ref[...]
    cw = cw_ref[...]
    y = xbuf[8:8 + tb, :] * cw[3:4, :]
    y = y + xbuf[7:7 + tb, :] * cw[2:3, :]
    y = y + xbuf[6:6 + tb, :] * cw[1:2, :]
    y = y + xbuf[5:5 + tb, :] * cw[0:1, :]
    carry_ref[bi] = xbuf[tb:tb + 8, :]
    y = _silu(y)
    seg = _seg_ones(HW, DH)
    q = y[:, 0:HW]
    k = y[:, HW:2 * HW]
    q = q * lax.rsqrt(_seg_sum(q * q, seg) + 1e-6) * (DH ** -0.5)
    k = k * lax.rsqrt(_seg_sum(k * k, seg) + 1e-6)
    v = y[:, 2 * HW:3 * HW]

    ab = ab_ref[...]
    pcol = pcol_ref[...]
    g_col = -jnp.exp(pcol[0:1, :]) * _softplus(ab + pcol[1:2, :])
    beta_col = _sigmoid(ab)

    r, c = _tri_masks(CHUNK)
    incl = c <= r
    strict = c < r
    l_incl = jnp.where(incl, 1.0, 0.0).astype(F32)

    rb = _iota2((tb, tb), 0)
    cb = _iota2((tb, tb), 1)
    l_blk = jnp.where(((rb >> 6) == (cb >> 6)) & (cb <= rb), 1.0, 0.0).astype(F32)
    gc_col = _mm_exact_lhs(l_blk, g_col)
    prow = prow_ref[...]
    abr = abr_ref[0].reshape(nchunk * 8, CHUNK)
    prow_t = jnp.concatenate([prow] * nchunk, axis=0)
    g_row = -jnp.exp(prow_t[:, 0:1]) * _softplus(abr + prow_t[:, 1:2])
    gc_row = _mm_exact_rhs(g_row, l_incl, NT).reshape(nchunk, 8, CHUNK)

    qg, kg, vg = _to_groups(q, nchunk), _to_groups(k, nchunk), _to_groups(v, nchunk)
    beta = _col_groups(beta_col, 4, nchunk)
    gcol = _col_groups(gc_col, 0, nchunk)
    grow = jnp.concatenate([gc_row[:, h:h + 1, :].reshape(nchunk, 1, 1, CHUNK) for h in range(HEADS)],
                           axis=1).reshape(nchunk * HEADS, 1, CHUNK)
    decay = jnp.where(incl, jnp.exp(jnp.where(incl, gcol - grow, 0.0)), 0.0)
    kb = kg * beta
    m = jnp.where(strict, _mm(kb, kg, BNT) * decay, 0.0)
    t_inv = _unit_lower_inverse(m, r, c, BNN)
    egc = jnp.exp(gcol)
    glast = gcol[:, CHUNK - 1:CHUNK, :]
    uw = _mm(t_inv, jnp.concatenate([vg * beta, kb * egc], axis=2), BNN)
    u_ref[bi] = uw[:, :, 0:DH]
    w_ref[bi] = uw[:, :, DH:2 * DH].astype(BF16)
    a_ref[bi] = (_mm(qg, kg, BNT) * decay).astype(BF16)
    qd_ref[bi] = (qg * egc).astype(BF16)
    kd_ref[bi] = (kg * jnp.exp(glast - gcol)).astype(BF16)
    egl_ref[bi] = jnp.broadcast_to(jnp.exp(glast), (nchunk * HEADS, 1, DH))

    @pl.when(bi == nbatch - 1)
    def _():
        nc = nbatch * HEADS

        def chunk_body(ci, carry):
            sl = pl.ds(pl.multiple_of(ci * HEADS, HEADS), HEADS)
            take = lambda ref: ref[:, sl].reshape((nc,) + ref.shape[2:])
            s = state_ref[...].reshape(nc, DH, DH)
            v_new = take(u_ref) - _mm(take(w_ref), s, BNN)
            o = _mm(take(qd_ref), s, BNN) + _mm(take(a_ref), v_new, BNN)
            s = s * take(egl_ref) + _mm(take(kd_ref), v_new, BTN)
            state_ref[...] = s.reshape(nbatch, HEADS, DH, DH)
            rows = pl.ds(pl.multiple_of(ci * CHUNK, CHUNK), CHUNK)
            for bb in range(nbatch):
                oraw_ref[bb, rows, :] = jnp.concatenate([o[bb * HEADS + h] for h in range(HEADS)], axis=1)
            return carry

        lax.fori_loop(0, nchunk, chunk_body, 0)
        o_all = oraw_ref[...].reshape(nbatch * tb, HW)
        ms = _seg_sum(o_all * o_all, seg) * (1.0 / DH)
        o_n = o_all * lax.rsqrt(ms + NORM_EPS) * ng_ref[...] * _silu(z_ref[...].reshape(nbatch * tb, HW))
        o_ref[...] = o_n.reshape(nbatch, tb, HW)


GDN_TB = 512


def _gdn(slab, abr, conv_w, pcol, prow, norm_g, b, s):
    tb = min(GDN_TB, s)
    nb = s // tb
    nchunk = tb // CHUNK
    ng = nchunk * HEADS
    grp = lambda w, dt=BF16: pltpu.VMEM((b, ng, CHUNK, w), dt)
    col = lambda off, w: pl.BlockSpec((tb, w), lambda j, i, o=off // w: (i * nb + j, o))
    full = lambda shape: pl.BlockSpec(shape, lambda j, i: (0,) * len(shape))
    slab3 = slab.reshape(b, s, slab.shape[1])
    out = pl.pallas_call(
        functools.partial(_gdn_kernel, tb=tb, nbatch=b),
        grid=(nb, b),
        in_specs=[col(OFF_GDN_Q, HW), col(OFF_GDN_K, HW), col(OFF_GDN_V, HW),
                  pl.BlockSpec((b, tb, HW), lambda j, i: (0, j, OFF_GDN_Z // HW)),
                  col(OFF_GDN_AB, 128),
                  pl.BlockSpec((1, nchunk, 8, CHUNK), lambda j, i: (i, j, 0, 0)),
                  full((GDN_CONV, 3 * HW)), full((8, 128)), full((8, 128)), full((1, HW))],
        out_specs=pl.BlockSpec((b, tb, HW), lambda j, i: (0, j, 0)),
        out_shape=jax.ShapeDtypeStruct((b, s, HW), F32),
        scratch_shapes=[pltpu.VMEM((tb + 8, 3 * HW), F32), pltpu.VMEM((b, HEADS, DH, DH), F32),
                        grp(DH, F32), grp(DH), grp(CHUNK), grp(DH), grp(DH),
                        pltpu.VMEM((b, ng, 1, DH), F32), pltpu.VMEM((b, tb, HW), F32),
                        pltpu.VMEM((b, 8, 3 * HW), F32)],
        compiler_params=_cparams(2),
        name="gdn",
    )(slab, slab, slab, slab3, slab, abr, conv_w, pcol, prow, norm_g)
    return out.reshape(b * s, HW)


def _rwkv_kernel(p_ref, mu_ref, vec_ref, ww2_ref, wa2_ref, wg2_ref, o_ref,
                 xbuf_ref, state_ref, ta_ref, tpv_ref, rt_ref, qkv_ref, qb_ref, bc_ref, ktv_ref,
                 el_ref, oraw_ref, carry_ref, post_ref, *, tb, nbatch):
    nchunk = tb // CHUNK
    bi = pl.program_id(1)

    @pl.when(pl.program_id(0) == 0)
    def _():
        carry_ref[bi] = jnp.zeros((8, 4 * HW), F32)
        state_ref[bi] = jnp.zeros((HEADS, DH, DH), F32)

    p = p_ref[...]
    xbuf_ref[0:8, :] = carry_ref[bi]
    xbuf_ref[8:8 + tb, :] = p
    prev = xbuf_ref[7:7 + tb, :]
    carry_ref[bi] = xbuf_ref[tb:tb + 8, :]
    pm = p + (prev - p) * mu_ref[...]
    vec = vec_ref[...]
    w0, a0, k_k, k_a = vec[0:1, :], vec[1:2, :], vec[2:3, :], vec[3:4, :]
    rr = pm[:, 0:HW]
    k = pm[:, HW:2 * HW]
    v = pm[:, 2 * HW:3 * HW]
    xw = pm[:, 768:832]
    xa = pm[:, 832:896]
    xg = pm[:, 896:1024]
    w_log = -_softplus(-(w0 + _mm(jnp.tanh(xw), ww2_ref[...]))) - 0.5
    lw = -jnp.exp(w_log)
    a = _sigmoid(a0 + _mm(xa, wa2_ref[...]))
    g = _mm(_sigmoid(xg), wg2_ref[...])
    seg = _seg_ones(HW, DH)
    kk = k * k_k
    kk = kk * lax.rsqrt(_seg_sum(kk * kk, seg) + 1e-6)
    k2 = k * (1.0 + (a - 1.0) * k_a)
    kka = kk * a

    r_i, c_i = _tri_masks(CHUNK)
    incl = c_i <= r_i
    strict = c_i < r_i
    rb = _iota2((tb, tb), 0)
    cb = _iota2((tb, tb), 1)
    l_blk = jnp.where(((rb >> 6) == (cb >> 6)) & (cb <= rb), 1.0, 0.0).astype(F32)
    lwc_all = _mm_exact_lhs(l_blk, lw)

    rg, kg, vg = _to_groups(rr, nchunk), _to_groups(k2, nchunk), _to_groups(v, nchunk)
    kkg, kkag = _to_groups(kk, nchunk), _to_groups(kka, nchunk)
    lwg, lwc = _to_groups(lw, nchunk), _to_groups(lwc_all, nchunk)
    lwl = lwc[:, CHUNK - 1:CHUNK, :]
    e_neg = jnp.exp(-lwc)
    at = kkg * jnp.exp(lwc - lwg)
    rt = rg * jnp.exp(lwc)
    y2 = jnp.concatenate([at, rt], axis=1)
    x2 = jnp.concatenate([kkag * e_neg, kg * e_neg], axis=1)
    gram = _mm(y2, x2, BNT)
    n_m = jnp.where(strict, gram[:, 0:CHUNK, 0:CHUNK], 0.0)
    p_m = jnp.where(strict, gram[:, 0:CHUNK, CHUNK:], 0.0)
    qb_m = jnp.where(incl, gram[:, CHUNK:, 0:CHUNK], 0.0)
    qk_m = jnp.where(incl, gram[:, CHUNK:, CHUNK:], 0.0)
    t_inv = _unit_lower_inverse(n_m, r_i, c_i, BNN)
    ta_ref[bi] = _mm(t_inv, at, BNN).astype(BF16)
    tpv_ref[bi] = _mm(t_inv, _mm(p_m, vg, BNN), BNN)
    rt_ref[bi] = rt.astype(BF16)
    qkv_ref[bi] = _mm(qk_m, vg, BNN)
    qb_ref[bi] = qb_m.astype(BF16)
    e_rel = jnp.exp(lwl - lwc)
    bc_ref[bi] = (kkag * e_rel).astype(BF16)
    ktv_ref[bi] = _mm(vg, kg * e_rel, BTN)
    el_ref[bi] = jnp.exp(lwl)
    ln_w, ln_b, r_k = vec[4:5, :], vec[5:6, :], vec[6:7, :]
    post_ref[bi, :, 0:HW] = _seg_sum(rr * k2 * r_k, seg) * v
    post_ref[bi, :, HW:2 * HW] = g

    @pl.when(bi == nbatch - 1)
    def _():
        nc = nbatch * HEADS

        def chunk_body(ci, carry):
            sl = pl.ds(pl.multiple_of(ci * HEADS, HEADS), HEADS)
            take = lambda ref: ref[:, sl].reshape((nc,) + ref.shape[2:])
            zt = state_ref[...].reshape(nc, DH, DH)
            u = _mm(take(ta_ref), zt, BNT) + take(tpv_ref)
            o = _mm(take(rt_ref), zt, BNT) + take(qkv_ref) - _mm(take(qb_ref), u, BNN)
            zt = zt * take(el_ref) + take(ktv_ref) - _mm(u, take(bc_ref), BTN)
            state_ref[...] = zt.reshape(nbatch, HEADS, DH, DH)
            rows = pl.ds(pl.multiple_of(ci * CHUNK, CHUNK), CHUNK)
            for bb in range(nbatch):
                oraw_ref[bb, rows, :] = jnp.concatenate([o[bb * HEADS + h] for h in range(HEADS)], axis=1)
            return carry

        lax.fori_loop(0, nchunk, chunk_body, 0)
        o_all = oraw_ref[...].reshape(nbatch * tb, HW)
        mean = _seg_sum(o_all, seg) * (1.0 / DH)
        cen = o_all - mean
        var = _seg_sum(cen * cen, seg) * (1.0 / DH)
        o_n = cen * lax.rsqrt(var + RWKV_GN_EPS) * ln_w + ln_b
        post = post_ref[...].reshape(nbatch * tb, 2 * HW)
        o_ref[...] = ((o_n + post[:, 0:HW]) * post[:, HW:2 * HW]).reshape(nbatch, tb, HW)


RWKV_TB = 512


def _rwkv(slab, mu, vec, ww2, wa2, wg2, b, s):
    tb = min(RWKV_TB, s)
    nb = s // tb
    ng = (tb // CHUNK) * HEADS
    grp = lambda dt=F32: pltpu.VMEM((b, ng, CHUNK, DH), dt)
    full = lambda shape: pl.BlockSpec(shape, lambda j, i: (0,) * len(shape))
    out = pl.pallas_call(
        functools.partial(_rwkv_kernel, tb=tb, nbatch=b),
        grid=(nb, b),
        in_specs=[pl.BlockSpec((tb, 4 * HW), lambda j, i: (i * nb + j, OFF_RWKV // (4 * HW))),
                  full((1, 4 * HW)), full((8, HW)), full((64, HW)), full((64, HW)), full((128, HW))],
        out_specs=pl.BlockSpec((b, tb, HW), lambda j, i: (0, j, 0)),
        out_shape=jax.ShapeDtypeStruct((b, s, HW), F32),
        scratch_shapes=[pltpu.VMEM((tb + 8, 4 * HW), F32), pltpu.VMEM((b, HEADS, DH, DH), F32),
                        grp(BF16), grp(), grp(BF16), grp(), grp(BF16), grp(BF16), grp(),
                        pltpu.VMEM((b, ng, 1, DH), F32), pltpu.VMEM((b, tb, HW), F32),
                        pltpu.VMEM((b, 8, 4 * HW), F32), pltpu.VMEM((b, tb, 2 * HW), F32)],
        compiler_params=_cparams(2),
        name="rwkv7",
    )(slab, mu, vec, ww2, wa2, wg2)
    return out.reshape(b * s, HW)


def _rope(x, cos_t, sin_lo, sin_hi, half):
    w = x.shape[-1]
    return x * cos_t + pltpu.roll(x, w - half, 1) * sin_lo + pltpu.roll(x, half, 1) * sin_hi


def _tile_lanes(t, n):
    return t if n == 1 else jnp.concatenate([t] * n, axis=1)


def _mla_prep_kernel(cq_ref, ckv_ref, kr_ref, cs_ref, e_ref, base_ref, qg_ref, kvg_ref, wq_ref, wk_ref,
                     wvt_ref, q_ref, k_ref, vt_ref):
    cq = cq_ref[...]
    hq = cq * lax.rsqrt(jnp.mean(cq * cq, axis=-1, keepdims=True) + NORM_EPS) * qg_ref[...]
    ckv = ckv_ref[...]
    hkv = ckv * lax.rsqrt(jnp.mean(ckv * ckv, axis=-1, keepdims=True) + NORM_EPS) * kvg_ref[...]
    cos_t, sin_lo, sin_hi = _rope_table(cs_ref, e_ref, base_ref)
    q = _mm(hq, wq_ref[...])
    q = _rope(q, _tile_lanes(cos_t, HEADS), _tile_lanes(sin_lo, HEADS), _tile_lanes(sin_hi, HEADS),
              MLA_ROPE // 2)
    q_ref[...] = (q * ((MLA_NOPE + MLA_ROPE) ** -0.5 * LOG2E)).astype(BF16)
    kr = _rope(kr_ref[...], cos_t, sin_lo, sin_hi, MLA_ROPE // 2)
    k = _mm(hkv, wk_ref[...]) + _tile_lanes(kr, HEADS)
    k_ref[...] = k.astype(BF16)
    vt_ref[...] = lax.dot_general(wvt_ref[...], hkv.astype(BF16), NT,
                                  preferred_element_type=F32).astype(BF16)


def _mla_prep(slab, cs, qg, kvg, wq, wk, wvt):
    t = slab.shape[0]
    tm = min(512, t)
    full = lambda shape: pl.BlockSpec(shape, lambda i: (0,) * len(shape))
    e, base = _rope_expansion(MLA_ROPE, MLA_NOPE, MLA_QK)
    return pl.pallas_call(
        _mla_prep_kernel,
        grid=(t // tm,),
        in_specs=[pl.BlockSpec((tm, 256), lambda i: (i, OFF_MLA_CQ // 256)),
                  pl.BlockSpec((tm, 128), lambda i: (i, OFF_MLA_CKV // 128)),
                  pl.BlockSpec((tm, 128), lambda i: (i, OFF_MLA_KR // 128)),
                  pl.BlockSpec((tm, MLA_ROPE), lambda i: (i, 0)), full(e.shape), full(base.shape),
                  full((1, 256)), full((1, 128)), full((256, 512)), full((128, 512)), full((256, 128))],
        out_specs=[pl.BlockSpec((tm, 512), lambda i: (i, 0)), pl.BlockSpec((tm, 512), lambda i: (i, 0)),
                   pl.BlockSpec((256, tm), lambda i: (0, i))],
        out_shape=[jax.ShapeDtypeStruct((t, 512), BF16), jax.ShapeDtypeStruct((t, 512), BF16),
                   jax.ShapeDtypeStruct((256, t), BF16)],
        compiler_params=_cparams(1),
        name="mla_prep",
    )(slab, slab, slab, cs, e, base, qg, kvg, wq, wk, wvt)


def _flash_kernel(q_ref, k_ref, vt_ref, o_ref, m_ref, l_ref, acc_ref, *, tq, tk):
    qi = pl.program_id(1)
    ki = pl.program_id(2)

    @pl.when(ki == 0)
    def _():
        m_ref[...] = jnp.full_like(m_ref, NEG_INF)
        l_ref[...] = jnp.zeros_like(l_ref)
        acc_ref[...] = jnp.zeros_like(acc_ref)

    def step(on_diagonal):
        if on_diagonal:
            causal = _iota2((tk, tq), 0) <= _iota2((tk, tq), 1)
        vt = vt_ref[...]
        ones = jnp.ones((16, tk), BF16)
        scores = [lax.dot_general(k_ref[:, h * MLA_QK:(h + 1) * MLA_QK], q_ref[:, h * MLA_QK:(h + 1) * MLA_QK],
                                  NT, preferred_element_type=F32) for h in range(HEADS)]
        for h in range(HEADS):
            s = scores[h]
            if on_diagonal:
                s = jnp.where(causal, s, NEG_INF)
            m_old = m_ref[h]
            m_new = jnp.maximum(m_old, jnp.max(s, axis=0, keepdims=True))
            alpha = jnp.exp2(m_old - m_new)
            p = jnp.exp2(s - m_new).astype(BF16)
            pv = jnp.dot(jnp.concatenate([vt[h * DH:(h + 1) * DH, :], ones], axis=0), p,
                         preferred_element_type=F32)
            l_ref[h] = alpha * l_ref[h] + pv[DH:DH + 1, :]
            acc_ref[h * DH:(h + 1) * DH, :] = alpha * acc_ref[h * DH:(h + 1) * DH, :] + pv[0:DH, :]
            m_ref[h] = m_new

    pl.when(ki < qi)(lambda: step(False))
    pl.when(ki == qi)(lambda: step(True))

    @pl.when(ki == pl.num_programs(2) - 1)
    def _():
        o_t = jnp.concatenate([acc_ref[h * DH:(h + 1) * DH, :] / l_ref[h] for h in range(HEADS)], axis=0)
        o_ref[...] = o_t.T


def _flash(q, k, vt, b, s):
    tq = min(512, s)
    tk = tq
    nq = s // tq
    nk = s // tk
    return pl.pallas_call(
        functools.partial(_flash_kernel, tq=tq, tk=tk),
        grid=(b, nq, nk),
        in_specs=[pl.BlockSpec((tq, 512), lambda i, j, kk: (i * nq + j, 0)),
                  pl.BlockSpec((tk, 512), lambda i, j, kk: (i * nk + jnp.minimum(kk, j), 0)),
                  pl.BlockSpec((256, tk), lambda i, j, kk: (0, i * nk + jnp.minimum(kk, j)))],
        out_specs=pl.BlockSpec((tq, 256), lambda i, j, kk: (i * nq + j, 0)),
        out_shape=jax.ShapeDtypeStruct((b * s, 256), F32),
        scratch_shapes=[pltpu.VMEM((HEADS, 1, tq), F32), pltpu.VMEM((HEADS, 1, tq), F32),
                        pltpu.VMEM((HW, tq), F32)],
        compiler_params=_cparams(3),
        name="mla_flash",
    )(q, k, vt)


DSA_TQ = 512
DSA_TK = 512


def _dsa_prep_kernel(iq_ref, ikw_ref, q_ref, k_ref, v_ref, cs_ref, e_ref, base_ref,
                     ik_ref, ko_ref, vt_ref, iqt_ref, qt_ref, iwt_ref):
    cos_t, sin_lo, sin_hi = _rope_table(cs_ref, e_ref, base_ref)
    half = DH // 8
    rope = lambda x, n: _rope(x, _tile_lanes(cos_t, n), _tile_lanes(sin_lo, n), _tile_lanes(sin_hi, n), half)
    ikw = ikw_ref[...]
    ik_ref[...] = rope(ikw, 1)[:, 0:DH].astype(BF16)
    kr = rope(k_ref[...], 2)
    for h in range(HEADS):
        ko_ref[h] = kr[:, h * DH:(h + 1) * DH].astype(BF16)
    vt_ref[0] = v_ref[...].T.astype(BF16)
    iqt_ref[...] = rope(iq_ref[...], 4).T.astype(BF16)
    qt_ref[...] = (rope(q_ref[...], 2) * (DH ** -0.5 * LOG2E)).T.astype(BF16)
    iwt_ref[...] = ikw.T[DH:DH + 8, :] * (IDX_HEADS ** -0.5 * DH ** -0.5)


def _dsa_prep(slab, cs):
    t = slab.shape[0]
    tm = min(DSA_TK, t)
    full = lambda shape: pl.BlockSpec(shape, lambda i: (0,) * len(shape))
    e, base = _rope_expansion(DH // 4, 0, DH)
    return pl.pallas_call(
        _dsa_prep_kernel,
        grid=(t // tm,),
        in_specs=[pl.BlockSpec((tm, IDX_HEADS * DH), lambda i: (i, OFF_DSA_IQ // (IDX_HEADS * DH))),
                  pl.BlockSpec((tm, 128), lambda i: (i, OFF_DSA_IKW // 128)),
                  pl.BlockSpec((tm, 256), lambda i: (i, OFF_DSA_Q // 256)),
                  pl.BlockSpec((tm, 256), lambda i: (i, OFF_DSA_K // 256)),
                  pl.BlockSpec((tm, 256), lambda i: (i, OFF_DSA_V // 256)),
                  pl.BlockSpec((tm, DH // 4), lambda i: (i, 0)), full(e.shape), full(base.shape)],
        out_specs=[pl.BlockSpec((tm, DH), lambda i: (i, 0)),
                   pl.BlockSpec((HEADS, tm, DH), lambda i: (0, i, 0)),
                   pl.BlockSpec((1, HW, tm), lambda i: (i, 0, 0)),
                   pl.BlockSpec((IDX_HEADS * DH, tm), lambda i: (0, i)),
                   pl.BlockSpec((HW, tm), lambda i: (0, i)),
                   pl.BlockSpec((8, tm), lambda i: (0, i))],
        out_shape=[jax.ShapeDtypeStruct((t, DH), BF16), jax.ShapeDtypeStruct((HEADS, t, DH), BF16),
                   jax.ShapeDtypeStruct((t // tm, HW, tm), BF16),
                   jax.ShapeDtypeStruct((IDX_HEADS * DH, t), BF16), jax.ShapeDtypeStruct((HW, t), BF16),
                   jax.ShapeDtypeStruct((8, t), F32)],
        compiler_params=_cparams(1),
        name="dsa_prep",
    )(slab, slab, slab, slab, slab, cs, e, base)


def _dsa_kernel(iqt_ref, iwt_ref, qt_ref, ik_ref, k_ref, vt_ref, kidx_ref, o_ref,
                hi_ref, lo_ref, lom_ref, bias_ref, acc_ref, s_ref, m_ref, l_ref,
                *, tq, tk, n_sel, idx_bits):
    qi = pl.program_id(1)
    nkt = (qi * tq + tq - 1) // tk + 1
    iqt = iqt_ref[...]
    iw = iwt_ref[...]
    q_pos = qi * tq + _iota2((tk, tq), 1)

    def score_body(kt, carry):
        row0 = pl.multiple_of(kt * tk, tk)
        ik = ik_ref[pl.ds(row0, tk), :]
        sc = jnp.zeros((tk, tq), F32)
        for h in range(IDX_HEADS):
            lg = lax.dot_general(ik, iqt[h * DH:(h + 1) * DH, :], NN, preferred_element_type=F32)
            sc = sc + jnp.maximum(lg, 0.0) * iw[h:h + 1, :]
        bits = lax.bitcast_convert_type(sc, I32)
        key = bits ^ ((bits >> 31) & 0x7FFFFFFF)
        key = jnp.where(sc == 0.0, 0, key)
        k_pos = row0 + _iota2((tk, tq), 0)
        key = jnp.where(k_pos <= q_pos, key, INT_MIN)
        hi_ref[pl.ds(row0, tk), :] = (key >> 16).astype(I16)
        lo_ref[pl.ds(row0, tk), :] = ((key & 0xFFFF) - 32768).astype(I16)
        return carry

    lax.fori_loop(0, nkt, score_body, 0)

    one16 = jnp.ones((tk, tq), I16)
    zero16 = jnp.zeros((tk, tq), I16)

    def count(*pred_fns):
        def body(kt, accs):
            sl = pl.ds(pl.multiple_of(kt * tk, tk), tk)
            args = (hi_ref[sl, :], lo_ref[sl, :], lom_ref[sl, :], kidx_ref[sl, :])
            out = []
            for acc, fn in zip(accs, pred_fns):
                ones = jnp.where(fn(*args), one16, zero16).reshape(tk // 64, 64, tq)
                for g in range(tk // 64):
                    acc = acc + ones[g]
                out.append(acc)
            return tuple(out)
        parts = lax.fori_loop(0, nkt, body, tuple(jnp.zeros((64, tq), I16) for _ in pred_fns))
        sums = tuple(jnp.sum(p.astype(I32).astype(F32), axis=0, keepdims=True) for p in parts)
        return sums[0] if len(sums) == 1 else sums

    def bisect16(pred_ge):
        def body(it, thr):
            bit = jnp.maximum(15 - it, 0)
            cand = jnp.where(it == 0, jnp.zeros_like(thr), thr | jnp.left_shift(jnp.int32(1), bit))
            c16 = cand.astype(I16)
            cnt = count(lambda hi, lo, lom, idx: pred_ge(hi, lo, lom, c16))
            return jnp.where(cnt >= float(n_sel), cand, thr)
        return lax.fori_loop(0, 16, body, jnp.full((1, tq), -32768, I32))

    t_hi = bisect16(lambda hi, lo, lom, c: hi >= c)
    t_hi16 = t_hi.astype(I16)

    def mask_lo_body(kt, carry):
        sl = pl.ds(pl.multiple_of(kt * tk, tk), tk)
        hi = hi_ref[sl, :]
        lom_ref[sl, :] = jnp.where(hi > t_hi16, jnp.int16(32767),
                                   jnp.where(hi == t_hi16, lo_ref[sl, :], jnp.int16(-32768)))
        return carry

    lax.fori_loop(0, nkt, mask_lo_body, 0)
    t_lo16 = bisect16(lambda hi, lo, lom, c: lom >= c).astype(I16)

    def is_gt(hi, lo):
        return (hi > t_hi16) | ((hi == t_hi16) & (lo > t_lo16))

    def is_eq(hi, lo):
        return (hi == t_hi16) & (lo == t_lo16)

    cnt_gt, cnt_eq = count(lambda hi, lo, lom, idx: is_gt(hi, lo), lambda hi, lo, lom, idx: is_eq(hi, lo))
    need = float(n_sel) - cnt_gt

    def idx_body(it, j):
        cand = j + jnp.left_shift(jnp.int32(1), idx_bits - 1 - it)
        c16 = cand.astype(I16)
        cnt = count(lambda hi, lo, lom, idx: is_eq(hi, lo) & (idx < c16))
        return jnp.where(cnt < need, cand, j)

    excess = jnp.max(jnp.where(cnt_eq > need, 1.0, 0.0)) > 0.0
    j_cut16 = lax.cond(excess,
                       lambda: lax.fori_loop(0, idx_bits, idx_body, jnp.zeros((1, tq), I32)),
                       lambda: jnp.full((1, tq), 32767, I32)).astype(I16)

    def bias_body(kt, carry):
        sl = pl.ds(pl.multiple_of(kt * tk, tk), tk)
        hi, lo = hi_ref[sl, :], lo_ref[sl, :]
        sel = (is_gt(hi, lo) | (is_eq(hi, lo) & (kidx_ref[sl, :] <= j_cut16))) & (hi != jnp.int16(-32768))
        bias_ref[sl, :] = jnp.where(sel, jnp.zeros((tk, tq), BF16),
                                    jnp.full((tk, tq), NEG_INF, BF16)).astype(F32)
        return carry

    lax.fori_loop(0, nkt, bias_body, 0)

    qt = qt_ref[...]
    acc_ref[...] = jnp.zeros_like(acc_ref)

    def att_body(kt, carry):
        row0 = pl.multiple_of(kt * tk, tk)
        for h in range(HEADS):
            s_ref[h] = lax.dot_general(k_ref[h, pl.ds(row0, tk), :], qt[h * DH:(h + 1) * DH, :], NN,
                                       preferred_element_type=F32)

        @pl.when(kt >= 0)
        def _():
            bias = bias_ref[pl.ds(row0, tk), :]
            ones = jnp.ones((16, tk), BF16)
            for h in range(HEADS):
                hs = slice(h * DH, (h + 1) * DH)
                s = s_ref[h] + bias
                m_old = m_ref[h]
                m_new = jnp.maximum(m_old, jnp.max(s, axis=0, keepdims=True))
                alpha = jnp.exp2(m_old - m_new)
                p = jnp.exp2(s - m_new).astype(BF16)
                pv = lax.dot_general(jnp.concatenate([vt_ref[kt, hs, :], ones], axis=0), p, NN,
                                     preferred_element_type=F32)
                acc_ref[hs, :] = alpha * acc_ref[hs, :] + pv[0:DH, :]
                m_ref[h] = m_new
                l_ref[h] = alpha * l_ref[h] + pv[DH:DH + 1, :]
        return carry

    m_ref[...] = jnp.full_like(m_ref, NEG_INF)
    l_ref[...] = jnp.zeros_like(l_ref)
    lax.fori_loop(0, nkt, att_body, 0)
    o_t = jnp.concatenate([acc_ref[h * DH:(h + 1) * DH, :] / l_ref[h] for h in range(HEADS)], axis=0)
    o_ref[...] = o_t.T


def _dsa(iqt, iwt, qt, ik, kr, vt, b, s, n_sel):
    tq = min(DSA_TQ, s)
    tk = min(DSA_TK, s)
    nq = s // tq
    nkt = s // tk
    return pl.pallas_call(
        functools.partial(_dsa_kernel, tq=tq, tk=tk, n_sel=n_sel, idx_bits=int(math.log2(s))),
        grid=(b, nq),
        in_specs=[pl.BlockSpec((IDX_HEADS * DH, tq), lambda i, j: (0, i * nq + j)),
                  pl.BlockSpec((8, tq), lambda i, j: (0, i * nq + j)),
                  pl.BlockSpec((HW, tq), lambda i, j: (0, i * nq + j)),
                  pl.BlockSpec((s, DH), lambda i, j: (i, 0)),
                  pl.BlockSpec((HEADS, s, DH), lambda i, j: (0, i, 0)),
                  pl.BlockSpec((nkt, HW, tk), lambda i, j: (i, 0, 0)),
                  pl.BlockSpec((s, tq), lambda i, j: (0, 0))],
        out_specs=pl.BlockSpec((tq, HW), lambda i, j: (i * nq + j, 0)),
        out_shape=jax.ShapeDtypeStruct((b * s, HW), F32),
        scratch_shapes=[pltpu.VMEM((s, tq), I16), pltpu.VMEM((s, tq), I16), pltpu.VMEM((s, tq), I16),
                        pltpu.VMEM((s, tq), F32), pltpu.VMEM((HW, tq), F32),
                        pltpu.VMEM((HEADS, tk, tq), F32), pltpu.VMEM((HEADS, 1, tq), F32),
                        pltpu.VMEM((HEADS, 1, tq), F32)],
        compiler_params=_cparams(2),
        name="dsa",
    )(iqt, iwt, qt, ik, kr, vt,
      jnp.broadcast_to(jnp.arange(s, dtype=I16)[:, None], (s, tq)))


def _merge_kernel(x_ref, oa_ref, ob_ref, oc_ref, od_ref, ga_ref, gb_ref, gc_ref, gd_ref,
                  wa_ref, wb_ref, wc_ref, wd_ref, wo_ref, o_ref):
    def branch(o_r, g_r, w_r):
        y = jnp.dot(o_r[...].astype(BF16), w_r[...], preferred_element_type=F32)
        return _sigmoid(g_r[...].astype(F32)) * y

    merged = (branch(oa_ref, ga_ref, wa_ref) + branch(ob_ref, gb_ref, wb_ref)
              + branch(oc_ref, gc_ref, wc_ref) + branch(od_ref, gd_ref, wd_ref))
    o_ref[...] = x_ref[...] + jnp.dot(merged.astype(BF16), wo_ref[...], preferred_element_type=F32)


def _merge(x2d, oa, ob, oc, od, gates, wa, wb, wc, wd, wo):
    t, d = x2d.shape
    tm = min(512, t)
    row = lambda w: pl.BlockSpec((tm, w), lambda i: (i, 0))
    gate = lambda n: pl.BlockSpec((tm, d), lambda i, o=n: (i, o))
    full = lambda shape: pl.BlockSpec(shape, lambda i: (0,) * len(shape))
    return pl.pallas_call(
        _merge_kernel,
        grid=(t // tm,),
        in_specs=[row(d), row(HW), row(HW), row(HW), row(HW), gate(0), gate(1), gate(2), gate(3),
                  full((HW, d)), full((HW, d)), full((HW, d)), full((HW, d)), full((d, d))],
        out_specs=row(d),
        out_shape=jax.ShapeDtypeStruct((t, d), F32),
        compiler_params=_cparams(1),
        name="merge",
    )(x2d, oa, ob, oc, od, gates, gates, gates, gates, wa, wb, wc, wd, wo)


def _mlp_kernel(x_ref, g_ref, wu_ref, wd_ref, fg_ref, o_ref, h_ref, acc_ref, *, final_norm):
    fi = pl.program_id(1)

    @pl.when(fi == 0)
    def _():
        x = x_ref[...]
        ms = jnp.mean(x * x, axis=-1, keepdims=True)
        h_ref[...] = (x * lax.rsqrt(ms + NORM_EPS) * g_ref[...]).astype(BF16)
        acc_ref[...] = jnp.zeros_like(acc_ref)

    u = jnp.dot(h_ref[...], wu_ref[...], preferred_element_type=F32)
    u = jnp.square(jnp.maximum(u, 0.0))
    acc_ref[...] += jnp.dot(u.astype(BF16), wd_ref[...], preferred_element_type=F32)

    @pl.when(fi == pl.num_programs(1) - 1)
    def _():
        y = x_ref[...] + acc_ref[...]
        if final_norm:
            ms = jnp.mean(y * y, axis=-1, keepdims=True)
            y = y * lax.rsqrt(ms + NORM_EPS) * fg_ref[...]
        o_ref[...] = y


def _mlp(x2d, g, wu, wd, fg, final_norm):
    t, d = x2d.shape
    f = wu.shape[1]
    tm = min(1024, t)
    tf = 512
    return pl.pallas_call(
        functools.partial(_mlp_kernel, final_norm=final_norm),
        grid=(t // tm, f // tf),
        in_specs=[pl.BlockSpec((tm, d), lambda i, j: (i, 0)),
                  pl.BlockSpec((1, d), lambda i, j: (0, 0)),
                  pl.BlockSpec((d, tf), lambda i, j: (0, j)),
                  pl.BlockSpec((tf, d), lambda i, j: (j, 0)),
                  pl.BlockSpec((1, d), lambda i, j: (0, 0))],
        out_specs=pl.BlockSpec((tm, d), lambda i, j: (i, 0)),
        out_shape=jax.ShapeDtypeStruct((t, d), F32),
        scratch_shapes=[pltpu.VMEM((tm, d), BF16), pltpu.VMEM((tm, d), F32)],
        compiler_params=_cparams(2),
        name="mlp",
    )(x2d, g, wu, wd, fg)


def _slab_weight(w_in, dtype=BF16):
    d = w_in.shape[0]
    z = lambda n: jnp.zeros((d, n), w_in.dtype)
    gdn, mla, rwkv, dsa = 0, 1032, 1448, 2472
    gate = 3824
    cols = [
        w_in[:, rwkv:rwkv + 1024],
        w_in[:, gdn:gdn + 768],
        w_in[:, gdn + 776:gdn + 1032],
        w_in[:, mla:mla + 256],
        w_in[:, mla + 256:mla + 384],
        z(64), w_in[:, mla + 384:mla + 416], z(32),
        w_in[:, dsa:dsa + 512],
        w_in[:, dsa + 584:dsa + 1352],
        w_in[:, gdn + 768:gdn + 776], z(120),
        w_in[:, dsa + 512:dsa + 584], z(56),
        w_in[:, gate:gate + 4096],
    ]
    return jnp.concatenate(cols, axis=1).astype(dtype)


def _mla_weights(w_uq, w_ukv):
    wq = w_uq.reshape(MLA_Q_LORA, HEADS, MLA_NOPE + MLA_ROPE)
    wq = jnp.pad(wq, ((0, 0), (0, 0), (0, MLA_QK - MLA_NOPE - MLA_ROPE))).reshape(MLA_Q_LORA, HEADS * MLA_QK)
    wkv = w_ukv.reshape(MLA_KV_LORA, HEADS, MLA_NOPE + DH)
    wk = jnp.pad(wkv[:, :, :MLA_NOPE], ((0, 0), (0, 0), (0, MLA_QK - MLA_NOPE))).reshape(MLA_KV_LORA, HEADS * MLA_QK)
    wvt = wkv[:, :, MLA_NOPE:].reshape(MLA_KV_LORA, HW).T
    return wq.astype(BF16), wk.astype(BF16), wvt.astype(BF16)


def _rope_cos_sin(positions, rot_dims):
    inv_freq = ROPE_THETA ** (-jnp.arange(0, rot_dims, 2, dtype=F32) / rot_dims)
    ang = positions.astype(F32)[..., None] * inv_freq
    return jnp.concatenate([jnp.cos(ang), jnp.sin(ang)], axis=-1)


def _rope_expansion(rot_dims, offset, period):
    half = rot_dims // 2
    e = np.zeros((rot_dims, 3 * 128), np.float32)
    base = np.zeros((1, 3 * 128), np.float32)
    base[0, 0:128] = 1.0
    for rep in range(128 // period):
        lo = rep * period + offset
        for j in range(half):
            e[j, lo + j] = 1.0
            e[j, lo + half + j] = 1.0
            base[0, lo + j] = 0.0
            base[0, lo + half + j] = 0.0
            e[half + j, 128 + lo + j] = -1.0
            e[half + j, 256 + lo + half + j] = 1.0
    return jnp.asarray(e), jnp.asarray(base)


def _rope_table(cs_ref, e_ref, base_ref):
    tab = _mm_exact_rhs(cs_ref[...], e_ref[...]) + base_ref[...]
    return tab[:, 0:128], tab[:, 128:256], tab[:, 256:384]


def _layer(x2d, b, s, prm, tabs, final_g, is_last):
    (mix_g, w_slab, conv_w, gdn_pcol, gdn_prow, gdn_ng, mla_qg, mla_kvg, wq, wk, wv,
     rwkv_mu, rwkv_vec, ww2, wa2, wg2, wo_gdn, wo_mla, wo_rwkv, wo_dsa, w_o, mlp_g, w_up, w_down) = prm
    mla_cs, dsa_cs = tabs
    t = b * s
    slab = _in_proj(x2d, mix_g, w_slab[:, :OFF_GATE], F32)
    gates = _in_proj(x2d, mix_g, w_slab[:, OFF_GATE:], BF16)

    ab = slab[:, OFF_GDN_AB:OFF_GDN_AB + 8].reshape(b, s // CHUNK, CHUNK, 8)
    abr = jnp.swapaxes(ab, 2, 3)
    o_gdn = _gdn(slab, abr, conv_w, gdn_pcol, gdn_prow, gdn_ng, b, s)
    q, k, v = _mla_prep(slab, mla_cs, mla_qg, mla_kvg, wq, wk, wv)
    o_mla = _flash(q, k, v, b, s)
    o_rwkv = _rwkv(slab, rwkv_mu, rwkv_vec, ww2, wa2, wg2, b, s)
    ik, kr, vt, iqt, qt, iwt = _dsa_prep(slab, dsa_cs)
    o_dsa = _dsa(iqt, iwt, qt, ik, kr, vt, b, s, min(IDX_TOPK_MAX, s // 4))

    x2d = _merge(x2d, o_gdn, o_mla, o_rwkv, o_dsa, gates, wo_gdn, wo_mla, wo_rwkv, wo_dsa, w_o)
    return _mlp(x2d, mlp_g, w_up, w_down, final_g, is_last)


def kernel(x, positions, mix_norm_g, w_in, gdn_conv_w, gdn_a_log, gdn_dt_bias, gdn_norm_g, mla_q_norm_g, mla_kv_norm_g, mla_w_uq, mla_w_ukv, rwkv_mu, rwkv_w0, rwkv_w_w2, rwkv_a0, rwkv_w_a2, rwkv_w_g2, rwkv_k_k, rwkv_k_a, rwkv_r_k, rwkv_ln_w, rwkv_ln_b, w_out_gdn, w_out_mla, w_out_rwkv, w_out_dsa, w_o, mlp_norm_g, w_up, w_down, final_norm_g):
    b, s, d = x.shape
    depth = w_in.shape[0]
    tabs = (_rope_cos_sin(positions, MLA_ROPE).reshape(b * s, MLA_ROPE),
            _rope_cos_sin(positions, DH // 4).reshape(b * s, DH // 4))
    row = lambda a: a.reshape(1, -1).astype(F32)
    pad8 = lambda rows: jnp.concatenate(rows + [jnp.zeros((8 - len(rows), rows[0].shape[1]), F32)], axis=0)
    x2d = x.reshape(b * s, d)
    final_g = row(final_norm_g)
    for l in range(depth):
        lane4 = lambda a: jnp.pad(a.astype(F32), (0, 124)).reshape(1, 128)
        gdn_pcol = pad8([lane4(gdn_a_log[l]), lane4(gdn_dt_bias[l])])
        gdn_prow = jnp.pad(jnp.stack([gdn_a_log[l], gdn_dt_bias[l]], axis=1).astype(F32), ((0, 4), (0, 126)))
        wq, wk, wv = _mla_weights(mla_w_uq[l], mla_w_ukv[l])
        rwkv_vec = pad8([row(rwkv_w0[l]), row(rwkv_a0[l]), row(rwkv_k_k[l]), row(rwkv_k_a[l]),
                         row(rwkv_ln_w[l]), row(rwkv_ln_b[l]), row(rwkv_r_k[l])])
        prm = (row(mix_norm_g[l]), _slab_weight(w_in[l]), gdn_conv_w[l].astype(F32), gdn_pcol, gdn_prow,
               row(jnp.tile(gdn_norm_g[l], HEADS)), row(mla_q_norm_g[l]), row(mla_kv_norm_g[l]), wq, wk, wv,
               row(rwkv_mu[l]), rwkv_vec, rwkv_w_w2[l].astype(F32), rwkv_w_a2[l].astype(F32),
               rwkv_w_g2[l].astype(F32),
               w_out_gdn[l].astype(BF16), w_out_mla[l].astype(BF16), w_out_rwkv[l].astype(BF16),
               w_out_dsa[l].astype(BF16), w_o[l].astype(BF16), row(mlp_norm_g[l]),
               w_up[l].astype(BF16), w_down[l].astype(BF16))
        x2d = _layer(x2d, b, s, prm, tabs, final_g, l == depth - 1)
    return x2d.reshape(b, s, d)
```

```python
import functools
import math

import numpy as np
import jax
import jax.numpy as jnp
from jax import lax
from jax.experimental import pallas as pl
from jax.experimental.pallas import tpu as pltpu

F32 = jnp.float32
BF16 = jnp.bfloat16
I32 = jnp.int32
I16 = jnp.int16

D_MODEL = 1024
DEPTH = 4
ROPE_THETA = 500000.0
NORM_EPS = 1e-6
NEG_INF = -1e30
HEADS = 4
DH = 64
HW = HEADS * DH
GDN_CONV = 4
CHUNK = 64
MLA_Q_LORA = 256
MLA_KV_LORA = 128
MLA_NOPE = 64
MLA_ROPE = 32
MLA_QK = 128
RWKV_GN_EPS = 64e-5
IDX_HEADS = 8
IDX_TOPK_MAX = 256
D_FF = 4 * D_MODEL
INT_MIN = -2 ** 31
LOG2E = 1.4426950408889634

VMEM_LIMIT_BYTES = 56 * 1024 * 1024

OFF_RWKV = 0
OFF_GDN_Q = 1024
OFF_GDN_K = 1280
OFF_GDN_V = 1536
OFF_GDN_Z = 1792
OFF_MLA_CQ = 2048
OFF_MLA_CKV = 2304
OFF_MLA_KR = 2432
OFF_DSA_IQ = 2560
OFF_DSA_Q = 3072
OFF_DSA_K = 3328
OFF_DSA_V = 3584
OFF_GDN_AB = 3840
OFF_DSA_IKW = 3968
OFF_GATE = 4096
N_SLAB = 8192

NN = (((1,), (0,)), ((), ()))
NT = (((1,), (1,)), ((), ()))
TN = (((0,), (0,)), ((), ()))
BNN = (((2,), (1,)), ((0,), (0,)))
BNT = (((2,), (2,)), ((0,), (0,)))
BTN = (((1,), (1,)), ((0,), (0,)))


def _cparams(n_axes):
    return pltpu.CompilerParams(dimension_semantics=("arbitrary",) * n_axes,
                                vmem_limit_bytes=VMEM_LIMIT_BYTES)


def _mm(a, b, dims=NN):
    return lax.dot_general(a.astype(BF16), b.astype(BF16), dims, preferred_element_type=F32)


def _mm_exact_lhs(m, x, dims=NN):
    mb = m.astype(BF16)
    x0 = x.astype(BF16)
    r1 = x - x0.astype(F32)
    x1 = r1.astype(BF16)
    x2 = (r1 - x1.astype(F32)).astype(BF16)
    d = lambda y: lax.dot_general(mb, y, dims, preferred_element_type=F32)
    return d(x0) + (d(x1) + d(x2))


def _mm_exact_rhs(x, m, dims=NN):
    mb = m.astype(BF16)
    x0 = x.astype(BF16)
    r1 = x - x0.astype(F32)
    x1 = r1.astype(BF16)
    x2 = (r1 - x1.astype(F32)).astype(BF16)
    d = lambda y: lax.dot_general(y, mb, dims, preferred_element_type=F32)
    return d(x0) + (d(x1) + d(x2))


def _iota2(shape, dim):
    return lax.broadcasted_iota(I32, shape, dim)


def _seg_ones(width, seg):
    sh = int(math.log2(seg))
    r = _iota2((width, width), 0) >> sh
    c = _iota2((width, width), 1) >> sh
    return jnp.where(r == c, 1.0, 0.0).astype(F32)


def _seg_sum(x, seg_ones):
    return _mm_exact_rhs(x, seg_ones)


def _softplus(x):
    return jnp.maximum(x, 0.0) + jnp.log1p(jnp.exp(-jnp.abs(x)))


def _sigmoid(x):
    return 1.0 / (1.0 + jnp.exp(-x))


def _silu(x):
    return x * _sigmoid(x)


def _tri_masks(n):
    r = _iota2((n, n), 0)
    c = _iota2((n, n), 1)
    return r, c


def _unit_lower_inverse(m, r, c, dims=NN):
    eye = jnp.where(r == c, 1.0, 0.0).astype(F32)
    same16 = (r >> 4) == (c >> 4)
    same32 = (r >> 5) == (c >> 5)
    md = jnp.where(same16, m, 0.0)
    t = eye - md
    p = md
    for _ in range(3):
        p = _mm(p, p, dims)
        t = t + _mm(t, p, dims)
    c32 = jnp.where(same32 & jnp.logical_not(same16), m, 0.0)
    t = t - _mm(_mm(t, c32, dims), t, dims)
    c64 = jnp.where(same32, 0.0, m)
    t = t - _mm(_mm(t, c64, dims), t, dims)
    return t


IN_PROJ_TN = 512


def _in_proj_kernel(x_ref, g_ref, w_ref, o_ref):
    x = x_ref[...]
    ms = jnp.mean(x * x, axis=-1, keepdims=True)
    h = (x * lax.rsqrt(ms + NORM_EPS) * g_ref[...]).astype(BF16)
    for n0 in range(0, o_ref.shape[1], IN_PROJ_TN):
        o_ref[:, n0:n0 + IN_PROJ_TN] = jnp.dot(
            h, w_ref[:, n0:n0 + IN_PROJ_TN], preferred_element_type=F32).astype(o_ref.dtype)


def _in_proj(x2d, g, w, out_dtype):
    t, d = x2d.shape
    n = w.shape[1]
    tm = min(512, t)
    return pl.pallas_call(
        _in_proj_kernel,
        grid=(t // tm,),
        in_specs=[pl.BlockSpec((tm, d), lambda i: (i, 0)),
                  pl.BlockSpec((1, d), lambda i: (0, 0)),
                  pl.BlockSpec((d, n), lambda i: (0, 0))],
        out_specs=pl.BlockSpec((tm, n), lambda i: (i, 0)),
        out_shape=jax.ShapeDtypeStruct((t, n), out_dtype),
        compiler_params=_cparams(1),
        name="in_proj",
    )(x2d, g, w)


def _to_groups(x, nchunk):
    parts = [x[:, h * DH:(h + 1) * DH].reshape(nchunk, 1, CHUNK, DH) for h in range(HEADS)]
    return jnp.concatenate(parts, axis=1).reshape(nchunk * HEADS, CHUNK, DH)


def _col_groups(x, lane0, nchunk):
    parts = [x[:, lane0 + h:lane0 + h + 1].reshape(nchunk, 1, CHUNK, 1) for h in range(HEADS)]
    return jnp.concatenate(parts, axis=1).reshape(nchunk * HEADS, CHUNK, 1)


def _gdn_kernel(q_ref, k_ref, v_ref, z_ref, ab_ref, abr_ref, cw_ref, pcol_ref, prow_ref, ng_ref,
                o_ref, xbuf_ref, state_ref, u_ref, w_ref, a_ref, qd_ref, kd_ref, egl_ref, oraw_ref,
                carry_ref, *, tb, nbatch):
    nchunk = tb // CHUNK
    bi = pl.program_id(1)
    xbuf = xbuf_ref

    @pl.when(pl.program_id(0) == 0)
    def _():
        carry_ref[bi] = jnp.zeros((8, 3 * HW), F32)
        state_ref[bi] = jnp.zeros((HEADS, DH, DH), F32)

    xbuf[0:8, :] = carry_ref[bi]
    xbuf[8:8 + tb, 0:HW] = q_ref[...]
    xbuf[8:8 + tb, HW:2 * HW] = k_ref[...]
    xbuf[8:8 + tb, 2 * HW:3 * HW] = v_ref[...]
    cw = cw_ref[...]
    y = xbuf[8:8 + tb, :] * cw[3:4, :]
    y = y + xbuf[7:7 + tb, :] * cw[2:3, :]
    y = y + xbuf[6:6 + tb, :] * cw[1:2, :]
    y = y + xbuf[5:5 + tb, :] * cw[0:1, :]
    carry_ref[bi] = xbuf[tb:tb + 8, :]
    y = _silu(y)
    seg = _seg_ones(HW, DH)
    q = y[:, 0:HW]
    k = y[:, HW:2 * HW]
    q = q * lax.rsqrt(_seg_sum(q * q, seg) + 1e-6) * (DH ** -0.5)
    k = k * lax.rsqrt(_seg_sum(k * k, seg) + 1e-6)
    v = y[:, 2 * HW:3 * HW]

    ab = ab_ref[...]
    pcol = pcol_ref[...]
    g_col = -jnp.exp(pcol[0:1, :]) * _softplus(ab + pcol[1:2, :])
    beta_col = _sigmoid(ab)

    r, c = _tri_masks(CHUNK)
    incl = c <= r
    strict = c < r
    l_incl = jnp.where(incl, 1.0, 0.0).astype(F32)

    rb = _iota2((tb, tb), 0)
    cb = _iota2((tb, tb), 1)
    l_blk = jnp.where(((rb >> 6) == (cb >> 6)) & (cb <= rb), 1.0, 0.0).astype(F32)
    gc_col = _mm_exact_lhs(l_blk, g_col)
    prow = prow_ref[...]
    abr = abr_ref[0].reshape(nchunk * 8, CHUNK)
    prow_t = jnp.concatenate([prow] * nchunk, axis=0)
    g_row = -jnp.exp(prow_t[:, 0:1]) * _softplus(abr + prow_t[:, 1:2])
    gc_row = _mm_exact_rhs(g_row, l_incl, NT).reshape(nchunk, 8, CHUNK)

    qg, kg, vg = _to_groups(q, nchunk), _to_groups(k, nchunk), _to_groups(v, nchunk)
    beta = _col_groups(beta_col, 4, nchunk)
    gcol = _col_groups(gc_col, 0, nchunk)
    grow = jnp.concatenate([gc_row[:, h:h + 1, :].reshape(nchunk, 1, 1, CHUNK) for h in range(HEADS)],
                           axis=1).reshape(nchunk * HEADS, 1, CHUNK)
    decay = jnp.where(incl, jnp.exp(jnp.where(incl, gcol - grow, 0.0)), 0.0)
    kb = kg * beta
    m = jnp.where(strict, _mm(kb, kg, BNT) * decay, 0.0)
    t_inv = _unit_lower_inverse(m, r, c, BNN)
    egc = jnp.exp(gcol)
    glast = gcol[:, CHUNK - 1:CHUNK, :]
    uw = _mm(t_inv, jnp.concatenate([vg * beta, kb * egc], axis=2), BNN)
    u_ref[bi] = uw[:, :, 0:DH]
    w_ref[bi] = uw[:, :, DH:2 * DH].astype(BF16)
    a_ref[bi] = (_mm(qg, kg, BNT) * decay).astype(BF16)
    qd_ref[bi] = (qg * egc).astype(BF16)
    kd_ref[bi] = (kg * jnp.exp(glast - gcol)).astype(BF16)
    egl_ref[bi] = jnp.broadcast_to(jnp.exp(glast), (nchunk * HEADS, 1, DH))

    @pl.when(bi == nbatch - 1)
    def _():
        nc = nbatch * HEADS

        def chunk_body(ci, carry):
            sl = pl.ds(pl.multiple_of(ci * HEADS, HEADS), HEADS)
            take = lambda ref: ref[:, sl].reshape((nc,) + ref.shape[2:])
            s = state_ref[...].reshape(nc, DH, DH)
            v_new = take(u_ref) - _mm(take(w_ref), s, BNN)
            o = _mm(take(qd_ref), s, BNN) + _mm(take(a_ref), v_new, BNN)
            s = s * take(egl_ref) + _mm(take(kd_ref), v_new, BTN)
            state_ref[...] = s.reshape(nbatch, HEADS, DH, DH)
            rows = pl.ds(pl.multiple_of(ci * CHUNK, CHUNK), CHUNK)
            for bb in range(nbatch):
                oraw_ref[bb, rows, :] = jnp.concatenate([o[bb * HEADS + h] for h in range(HEADS)], axis=1)
            return carry

        lax.fori_loop(0, nchunk, chunk_body, 0)
        o_all = oraw_ref[...].reshape(nbatch * tb, HW)
        ms = _seg_sum(o_all * o_all, seg) * (1.0 / DH)
        o_n = o_all * lax.rsqrt(ms + NORM_EPS) * ng_ref[...] * _silu(z_ref[...].reshape(nbatch * tb, HW))
        o_ref[...] = o_n.reshape(nbatch, tb, HW)


GDN_TB = 512


def _gdn(slab, abr, conv_w, pcol, prow, norm_g, b, s):
    tb = min(GDN_TB, s)
    nb = s // tb
    nchunk = tb // CHUNK
    ng = nchunk * HEADS
    grp = lambda w, dt=BF16: pltpu.VMEM((b, ng, CHUNK, w), dt)
    col = lambda off, w: pl.BlockSpec((tb, w), lambda j, i, o=off // w: (i * nb + j, o))
    full = lambda shape: pl.BlockSpec(shape, lambda j, i: (0,) * len(shape))
    slab3 = slab.reshape(b, s, slab.shape[1])
    out = pl.pallas_call(
        functools.partial(_gdn_kernel, tb=tb, nbatch=b),
        grid=(nb, b),
        in_specs=[col(OFF_GDN_Q, HW), col(OFF_GDN_K, HW), col(OFF_GDN_V, HW),
                  pl.BlockSpec((b, tb, HW), lambda j, i: (0, j, OFF_GDN_Z // HW)),
                  col(OFF_GDN_AB, 128),
                  pl.BlockSpec((1, nchunk, 8, CHUNK), lambda j, i: (i, j, 0, 0)),
                  full((GDN_CONV, 3 * HW)), full((8, 128)), full((8, 128)), full((1, HW))],
        out_specs=pl.BlockSpec((b, tb, HW), lambda j, i: (0, j, 0)),
        out_shape=jax.ShapeDtypeStruct((b, s, HW), F32),
        scratch_shapes=[pltpu.VMEM((tb + 8, 3 * HW), F32), pltpu.VMEM((b, HEADS, DH, DH), F32),
                        grp(DH, F32), grp(DH), grp(CHUNK), grp(DH), grp(DH),
                        pltpu.VMEM((b, ng, 1, DH), F32), pltpu.VMEM((b, tb, HW), F32),
                        pltpu.VMEM((b, 8, 3 * HW), F32)],
        compiler_params=_cparams(2),
        name="gdn",
    )(slab, slab, slab, slab3, slab, abr, conv_w, pcol, prow, norm_g)
    return out.reshape(b * s, HW)


def _rwkv_kernel(p_ref, mu_ref, vec_ref, ww2_ref, wa2_ref, wg2_ref, o_ref,
                 xbuf_ref, state_ref, ta_ref, tpv_ref, rt_ref, qkv_ref, qb_ref, bc_ref, ktv_ref,
                 el_ref, oraw_ref, carry_ref, post_ref, *, tb, nbatch):
    nchunk = tb // CHUNK
    bi = pl.program_id(1)

    @pl.when(pl.program_id(0) == 0)
    def _():
        carry_ref[bi] = jnp.zeros((8, 4 * HW), F32)
        state_ref[bi] = jnp.zeros((HEADS, DH, DH), F32)

    p = p_ref[...]
    xbuf_ref[0:8, :] = carry_ref[bi]
    xbuf_ref[8:8 + tb, :] = p
    prev = xbuf_ref[7:7 + tb, :]
    carry_ref[bi] = xbuf_ref[tb:tb + 8, :]
    pm = p + (prev - p) * mu_ref[...]
    vec = vec_ref[...]
    w0, a0, k_k, k_a = vec[0:1, :], vec[1:2, :], vec[2:3, :], vec[3:4, :]
    rr = pm[:, 0:HW]
    k = pm[:, HW:2 * HW]
    v = pm[:, 2 * HW:3 * HW]
    xw = pm[:, 768:832]
    xa = pm[:, 832:896]
    xg = pm[:, 896:1024]
    w_log = -_softplus(-(w0 + _mm(jnp.tanh(xw), ww2_ref[...]))) - 0.5
    lw = -jnp.exp(w_log)
    a = _sigmoid(a0 + _mm(xa, wa2_ref[...]))
    g = _mm(_sigmoid(xg), wg2_ref[...])
    seg = _seg_ones(HW, DH)
    kk = k * k_k
    kk = kk * lax.rsqrt(_seg_sum(kk * kk, seg) + 1e-6)
    k2 = k * (1.0 + (a - 1.0) * k_a)
    kka = kk * a

    r_i, c_i = _tri_masks(CHUNK)
    incl = c_i <= r_i
    strict = c_i < r_i
    rb = _iota2((tb, tb), 0)
    cb = _iota2((tb, tb), 1)
    l_blk = jnp.where(((rb >> 6) == (cb >> 6)) & (cb <= rb), 1.0, 0.0).astype(F32)
    lwc_all = _mm_exact_lhs(l_blk, lw)

    rg, kg, vg = _to_groups(rr, nchunk), _to_groups(k2, nchunk), _to_groups(v, nchunk)
    kkg, kkag = _to_groups(kk, nchunk), _to_groups(kka, nchunk)
    lwg, lwc = _to_groups(lw, nchunk), _to_groups(lwc_all, nchunk)
    lwl = lwc[:, CHUNK - 1:CHUNK, :]
    e_neg = jnp.exp(-lwc)
    at = kkg * jnp.exp(lwc - lwg)
    rt = rg * jnp.exp(lwc)
    y2 = jnp.concatenate([at, rt], axis=1)
    x2 = jnp.concatenate([kkag * e_neg, kg * e_neg], axis=1)
    gram = _mm(y2, x2, BNT)
    n_m = jnp.where(strict, gram[:, 0:CHUNK, 0:CHUNK], 0.0)
    p_m = jnp.where(strict, gram[:, 0:CHUNK, CHUNK:], 0.0)
    qb_m = jnp.where(incl, gram[:, CHUNK:, 0:CHUNK], 0.0)
    qk_m = jnp.where(incl, gram[:, CHUNK:, CHUNK:], 0.0)
    t_inv = _unit_lower_inverse(n_m, r_i, c_i, BNN)
    ta_ref[bi] = _mm(t_inv, at, BNN).astype(BF16)
    tpv_ref[bi] = _mm(t_inv, _mm(p_m, vg, BNN), BNN)
    rt_ref[bi] = rt.astype(BF16)
    qkv_ref[bi] = _mm(qk_m, vg, BNN)
    qb_ref[bi] = qb_m.astype(BF16)
    e_rel = jnp.exp(lwl - lwc)
    bc_ref[bi] = (kkag * e_rel).astype(BF16)
    ktv_ref[bi] = _mm(vg, kg * e_rel, BTN)
    el_ref[bi] = jnp.exp(lwl)
    ln_w, ln_b, r_k = vec[4:5, :], vec[5:6, :], vec[6:7, :]
    post_ref[bi, :, 0:HW] = _seg_sum(rr * k2 * r_k, seg) * v
    post_ref[bi, :, HW:2 * HW] = g

    @pl.when(bi == nbatch - 1)
    def _():
        nc = nbatch * HEADS

        def chunk_body(ci, carry):
            sl = pl.ds(pl.multiple_of(ci * HEADS, HEADS), HEADS)
            take = lambda ref: ref[:, sl].reshape((nc,) + ref.shape[2:])
            zt = state_ref[...].reshape(nc, DH, DH)
            u = _mm(take(ta_ref), zt, BNT) + take(tpv_ref)
            o = _mm(take(rt_ref), zt, BNT) + take(qkv_ref) - _mm(take(qb_ref), u, BNN)
            zt = zt * take(el_ref) + take(ktv_ref) - _mm(u, take(bc_ref), BTN)
            state_ref[...] = zt.reshape(nbatch, HEADS, DH, DH)
            rows = pl.ds(pl.multiple_of(ci * CHUNK, CHUNK), CHUNK)
            for bb in range(nbatch):
                oraw_ref[bb, rows, :] = jnp.concatenate([o[bb * HEADS + h] for h in range(HEADS)], axis=1)
            return carry

        lax.fori_loop(0, nchunk, chunk_body, 0)
        o_all = oraw_ref[...].reshape(nbatch * tb, HW)
        mean = _seg_sum(o_all, seg) * (1.0 / DH)
        cen = o_all - mean
        var = _seg_sum(cen * cen, seg) * (1.0 / DH)
        o_n = cen * lax.rsqrt(var + RWKV_GN_EPS) * ln_w + ln_b
        post = post_ref[...].reshape(nbatch * tb, 2 * HW)
        o_ref[...] = ((o_n + post[:, 0:HW]) * post[:, HW:2 * HW]).reshape(nbatch, tb, HW)


RWKV_TB = 512


def _rwkv(slab, mu, vec, ww2, wa2, wg2, b, s):
    tb = min(RWKV_TB, s)
    nb = s // tb
    ng = (tb // CHUNK) * HEADS
    grp = lambda dt=F32: pltpu.VMEM((b, ng, CHUNK, DH), dt)
    full = lambda shape: pl.BlockSpec(shape, lambda j, i: (0,) * len(shape))
    out = pl.pallas_call(
        functools.partial(_rwkv_kernel, tb=tb, nbatch=b),
        grid=(nb, b),
        in_specs=[pl.BlockSpec((tb, 4 * HW), lambda j, i: (i * nb + j, OFF_RWKV // (4 * HW))),
                  full((1, 4 * HW)), full((8, HW)), full((64, HW)), full((64, HW)), full((128, HW))],
        out_specs=pl.BlockSpec((b, tb, HW), lambda j, i: (0, j, 0)),
        out_shape=jax.ShapeDtypeStruct((b, s, HW), F32),
        scratch_shapes=[pltpu.VMEM((tb + 8, 4 * HW), F32), pltpu.VMEM((b, HEADS, DH, DH), F32),
                        grp(BF16), grp(), grp(BF16), grp(), grp(BF16), grp(BF16), grp(),
                        pltpu.VMEM((b, ng, 1, DH), F32), pltpu.VMEM((b, tb, HW), F32),
                        pltpu.VMEM((b, 8, 4 * HW), F32), pltpu.VMEM((b, tb, 2 * HW), F32)],
        compiler_params=_cparams(2),
        name="rwkv7",
    )(slab, mu, vec, ww2, wa2, wg2)
    return out.reshape(b * s, HW)


def _rope(x, cos_t, sin_lo, sin_hi, half):
    w = x.shape[-1]
    return x * cos_t + pltpu.roll(x, w - half, 1) * sin_lo + pltpu.roll(x, half, 1) * sin_hi


def _tile_lanes(t, n):
    return t if n == 1 else jnp.concatenate([t] * n, axis=1)


def _mla_prep_kernel(cq_ref, ckv_ref, kr_ref, cs_ref, e_ref, base_ref, qg_ref, kvg_ref, wq_ref, wk_ref,
                     wvt_ref, q_ref, k_ref, vt_ref):
    cq = cq_ref[...]
    hq = cq * lax.rsqrt(jnp.mean(cq * cq, axis=-1, keepdims=True) + NORM_EPS) * qg_ref[...]
    ckv = ckv_ref[...]
    hkv = ckv * lax.rsqrt(jnp.mean(ckv * ckv, axis=-1, keepdims=True) + NORM_EPS) * kvg_ref[...]
    cos_t, sin_lo, sin_hi = _rope_table(cs_ref, e_ref, base_ref)
    q = _mm(hq, wq_ref[...])
    q = _rope(q, _tile_lanes(cos_t, HEADS), _tile_lanes(sin_lo, HEADS), _tile_lanes(sin_hi, HEADS),
              MLA_ROPE // 2)
    q_ref[...] = (q * ((MLA_NOPE + MLA_ROPE) ** -0.5 * LOG2E)).astype(BF16)
    kr = _rope(kr_ref[...], cos_t, sin_lo, sin_hi, MLA_ROPE // 2)
    k = _mm(hkv, wk_ref[...]) + _tile_lanes(kr, HEADS)
    k_ref[...] = k.astype(BF16)
    vt_ref[...] = lax.dot_general(wvt_ref[...], hkv.astype(BF16), NT,
                                  preferred_element_type=F32).astype(BF16)


def _mla_prep(slab, cs, qg, kvg, wq, wk, wvt):
    t = slab.shape[0]
    tm = min(512, t)
    full = lambda shape: pl.BlockSpec(shape, lambda i: (0,) * len(shape))
    e, base = _rope_expansion(MLA_ROPE, MLA_NOPE, MLA_QK)
    return pl.pallas_call(
        _mla_prep_kernel,
        grid=(t // tm,),
        in_specs=[pl.BlockSpec((tm, 256), lambda i: (i, OFF_MLA_CQ // 256)),
                  pl.BlockSpec((tm, 128), lambda i: (i, OFF_MLA_CKV // 128)),
                  pl.BlockSpec((tm, 128), lambda i: (i, OFF_MLA_KR // 128)),
                  pl.BlockSpec((tm, MLA_ROPE), lambda i: (i, 0)), full(e.shape), full(base.shape),
                  full((1, 256)), full((1, 128)), full((256, 512)), full((128, 512)), full((256, 128))],
        out_specs=[pl.BlockSpec((tm, 512), lambda i: (i, 0)), pl.BlockSpec((tm, 512), lambda i: (i, 0)),
                   pl.BlockSpec((256, tm), lambda i: (0, i))],
        out_shape=[jax.ShapeDtypeStruct((t, 512), BF16), jax.ShapeDtypeStruct((t, 512), BF16),
                   jax.ShapeDtypeStruct((256, t), BF16)],
        compiler_params=_cparams(1),
        name="mla_prep",
    )(slab, slab, slab, cs, e, base, qg, kvg, wq, wk, wvt)


def _flash_kernel(q_ref, k_ref, vt_ref, o_ref, m_ref, l_ref, acc_ref, *, tq, tk):
    qi = pl.program_id(1)
    ki = pl.program_id(2)

    @pl.when(ki == 0)
    def _():
        m_ref[...] = jnp.full_like(m_ref, NEG_INF)
        l_ref[...] = jnp.zeros_like(l_ref)
        acc_ref[...] = jnp.zeros_like(acc_ref)

    def step(on_diagonal):
        if on_diagonal:
            causal = _iota2((tk, tq), 0) <= _iota2((tk, tq), 1)
        vt = vt_ref[...]
        ones = jnp.ones((16, tk), BF16)
        scores = [lax.dot_general(k_ref[:, h * MLA_QK:(h + 1) * MLA_QK], q_ref[:, h * MLA_QK:(h + 1) * MLA_QK],
                                  NT, preferred_element_type=F32) for h in range(HEADS)]
        for h in range(HEADS):
            s = scores[h]
            if on_diagonal:
                s = jnp.where(causal, s, NEG_INF)
            m_old = m_ref[h]
            m_new = jnp.maximum(m_old, jnp.max(s, axis=0, keepdims=True))
            alpha = jnp.exp2(m_old - m_new)
            p = jnp.exp2(s - m_new).astype(BF16)
            pv = jnp.dot(jnp.concatenate([vt[h * DH:(h + 1) * DH, :], ones], axis=0), p,
                         preferred_element_type=F32)
            l_ref[h] = alpha * l_ref[h] + pv[DH:DH + 1, :]
            acc_ref[h * DH:(h + 1) * DH, :] = alpha * acc_ref[h * DH:(h + 1) * DH, :] + pv[0:DH, :]
            m_ref[h] = m_new

    pl.when(ki < qi)(lambda: step(False))
    pl.when(ki == qi)(lambda: step(True))

    @pl.when(ki == pl.num_programs(2) - 1)
    def _():
        o_t = jnp.concatenate([acc_ref[h * DH:(h + 1) * DH, :] / l_ref[h] for h in range(HEADS)], axis=0)
        o_ref[...] = o_t.T


def _flash(q, k, vt, b, s):
    tq = min(512, s)
    tk = tq
    nq = s // tq
    nk = s // tk
    return pl.pallas_call(
        functools.partial(_flash_kernel, tq=tq, tk=tk),
        grid=(b, nq, nk),
        in_specs=[pl.BlockSpec((tq, 512), lambda i, j, kk: (i * nq + j, 0)),
                  pl.BlockSpec((tk, 512), lambda i, j, kk: (i * nk + jnp.minimum(kk, j), 0)),
                  pl.BlockSpec((256, tk), lambda i, j, kk: (0, i * nk + jnp.minimum(kk, j)))],
        out_specs=pl.BlockSpec((tq, 256), lambda i, j, kk: (i * nq + j, 0)),
        out_shape=jax.ShapeDtypeStruct((b * s, 256), F32),
        scratch_shapes=[pltpu.VMEM((HEADS, 1, tq), F32), pltpu.VMEM((HEADS, 1, tq), F32),
                        pltpu.VMEM((HW, tq), F32)],
        compiler_params=_cparams(3),
        name="mla_flash",
    )(q, k, vt)


DSA_TQ = 256
DSA_TK = 512


def _dsa_prep_kernel(iq_ref, ikw_ref, q_ref, k_ref, v_ref, cs_ref, e_ref, base_ref,
                     ik_ref, ko_ref, vt_ref, iqt_ref, qt_ref, iwt_ref):
    cos_t, sin_lo, sin_hi = _rope_table(cs_ref, e_ref, base_ref)
    half = DH // 8
    rope = lambda x, n: _rope(x, _tile_lanes(cos_t, n), _tile_lanes(sin_lo, n), _tile_lanes(sin_hi, n), half)
    ikw = ikw_ref[...]
    ik_ref[...] = rope(ikw, 1)[:, 0:DH].astype(BF16)
    kr = rope(k_ref[...], 2)
    for h in range(HEADS):
        ko_ref[h] = kr[:, h * DH:(h + 1) * DH].astype(BF16)
    vt_ref[0] = v_ref[...].T.astype(BF16)
    iqt_ref[...] = rope(iq_ref[...], 4).T.astype(BF16)
    qt_ref[...] = (rope(q_ref[...], 2) * (DH ** -0.5 * LOG2E)).T.astype(BF16)
    iwt_ref[...] = ikw.T[DH:DH + 8, :] * (IDX_HEADS ** -0.5 * DH ** -0.5)


def _dsa_prep(slab, cs):
    t = slab.shape[0]
    tm = min(DSA_TK, t)
    full = lambda shape: pl.BlockSpec(shape, lambda i: (0,) * len(shape))
    e, base = _rope_expansion(DH // 4, 0, DH)
    return pl.pallas_call(
        _dsa_prep_kernel,
        grid=(t // tm,),
        in_specs=[pl.BlockSpec((tm, IDX_HEADS * DH), lambda i: (i, OFF_DSA_IQ // (IDX_HEADS * DH))),
                  pl.BlockSpec((tm, 128), lambda i: (i, OFF_DSA_IKW // 128)),
                  pl.BlockSpec((tm, 256), lambda i: (i, OFF_DSA_Q // 256)),
                  pl.BlockSpec((tm, 256), lambda i: (i, OFF_DSA_K // 256)),
                  pl.BlockSpec((tm, 256), lambda i: (i, OFF_DSA_V // 256)),
                  pl.BlockSpec((tm, DH // 4), lambda i: (i, 0)), full(e.shape), full(base.shape)],
        out_specs=[pl.BlockSpec((tm, DH), lambda i: (i, 0)),
                   pl.BlockSpec((HEADS, tm, DH), lambda i: (0, i, 0)),
                   pl.BlockSpec((1, HW, tm), lambda i: (i, 0, 0)),
                   pl.BlockSpec((IDX_HEADS * DH, tm), lambda i: (0, i)),
                   pl.BlockSpec((HW, tm), lambda i: (0, i)),
                   pl.BlockSpec((8, tm), lambda i: (0, i))],
        out_shape=[jax.ShapeDtypeStruct((t, DH), BF16), jax.ShapeDtypeStruct((HEADS, t, DH), BF16),
                   jax.ShapeDtypeStruct((t // tm, HW, tm), BF16),
                   jax.ShapeDtypeStruct((IDX_HEADS * DH, t), BF16), jax.ShapeDtypeStruct((HW, t), BF16),
                   jax.ShapeDtypeStruct((8, t), F32)],
        compiler_params=_cparams(1),
        name="dsa_prep",
    )(slab, slab, slab, slab, slab, cs, e, base)


def _dsa_kernel(iqt_ref, iwt_ref, qt_ref, ik_ref, k_ref, vt_ref, kidx_ref, o_ref,
                hi_ref, lo_ref, lom_ref, bias_ref, acc_ref, s_ref, m_ref, l_ref,
                *, tq, tk, n_sel, idx_bits):
    qi = pl.program_id(1)
    nkt = (qi * tq + tq - 1) // tk + 1
    iqt = iqt_ref[...]
    iw = iwt_ref[...]
    q_pos = qi * tq + _iota2((tk, tq), 1)
    qt = qt_ref[...]

    def score_body(kt, carry):
        row0 = pl.multiple_of(kt * tk, tk)
        ik = ik_ref[pl.ds(row0, tk), :]
        sc = jnp.zeros((tk, tq), F32)
        for h in range(IDX_HEADS):
            lg = lax.dot_general(ik, iqt[h * DH:(h + 1) * DH, :], NN, preferred_element_type=F32)
            sc = sc + jnp.maximum(lg, 0.0) * iw[h:h + 1, :]
        bits = lax.bitcast_convert_type(sc, I32)
        key = bits ^ ((bits >> 31) & 0x7FFFFFFF)
        key = jnp.where(sc == 0.0, 0, key)
        k_pos = row0 + _iota2((tk, tq), 0)
        key = jnp.where(k_pos <= q_pos, key, INT_MIN)
        hi_ref[pl.ds(row0, tk), :] = (key >> 16).astype(I16)
        lo_ref[pl.ds(row0, tk), :] = ((key & 0xFFFF) - 32768).astype(I16)
        for h in range(HEADS):
            s_ref[h, pl.ds(row0, tk), :] = lax.dot_general(k_ref[h, pl.ds(row0, tk), :],
                                                           qt[h * DH:(h + 1) * DH, :], NN,
                                                           preferred_element_type=F32)
        return carry

    lax.fori_loop(0, nkt, score_body, 0)

    one16 = jnp.ones((tk, tq), I16)
    zero16 = jnp.zeros((tk, tq), I16)

    def count(*pred_fns):
        def body(kt, accs):
            sl = pl.ds(pl.multiple_of(kt * tk, tk), tk)
            args = (hi_ref[sl, :], lo_ref[sl, :], lom_ref[sl, :], kidx_ref[sl, :])
            out = []
            for acc, fn in zip(accs, pred_fns):
                ones = jnp.where(fn(*args), one16, zero16).reshape(tk // 64, 64, tq)
                for g in range(tk // 64):
                    acc = acc + ones[g]
                out.append(acc)
            return tuple(out)
        parts = lax.fori_loop(0, nkt, body, tuple(jnp.zeros((64, tq), I16) for _ in pred_fns))
        sums = tuple(jnp.sum(p.astype(I32).astype(F32), axis=0, keepdims=True) for p in parts)
        return sums[0] if len(sums) == 1 else sums

    def bisect16(pred_ge):
        def body(it, thr):
            bit = jnp.maximum(15 - it, 0)
            cand = jnp.where(it == 0, jnp.zeros_like(thr), thr | jnp.left_shift(jnp.int32(1), bit))
            c16 = cand.astype(I16)
            cnt = count(lambda hi, lo, lom, idx: pred_ge(hi, lo, lom, c16))
            return jnp.where(cnt >= float(n_sel), cand, thr)
        return lax.fori_loop(0, 16, body, jnp.full((1, tq), -32768, I32))

    t_hi = bisect16(lambda hi, lo, lom, c: hi >= c)
    t_hi16 = t_hi.astype(I16)

    def mask_lo_body(kt, carry):
        sl = pl.ds(pl.multiple_of(kt * tk, tk), tk)
        hi = hi_ref[sl, :]
        lom_ref[sl, :] = jnp.where(hi > t_hi16, jnp.int16(32767),
                                   jnp.where(hi == t_hi16, lo_ref[sl, :], jnp.int16(-32768)))
        return carry

    lax.fori_loop(0, nkt, mask_lo_body, 0)
    t_lo16 = bisect16(lambda hi, lo, lom, c: lom >= c).astype(I16)

    def is_gt(hi, lo):
        return (hi > t_hi16) | ((hi == t_hi16) & (lo > t_lo16))

    def is_eq(hi, lo):
        return (hi == t_hi16) & (lo == t_lo16)

    cnt_gt, cnt_eq = count(lambda hi, lo, lom, idx: is_gt(hi, lo), lambda hi, lo, lom, idx: is_eq(hi, lo))
    need = float(n_sel) - cnt_gt

    def idx_body(it, j):
        cand = j + jnp.left_shift(jnp.int32(1), idx_bits - 1 - it)
        c16 = cand.astype(I16)
        cnt = count(lambda hi, lo, lom, idx: is_eq(hi, lo) & (idx < c16))
        return jnp.where(cnt < need, cand, j)

    excess = jnp.max(jnp.where(cnt_eq > need, 1.0, 0.0)) > 0.0
    j_cut16 = lax.cond(excess,
                       lambda: lax.fori_loop(0, idx_bits, idx_body, jnp.zeros((1, tq), I32)),
                       lambda: jnp.full((1, tq), 32767, I32)).astype(I16)

    def bias_body(kt, carry):
        sl = pl.ds(pl.multiple_of(kt * tk, tk), tk)
        hi, lo = hi_ref[sl, :], lo_ref[sl, :]
        sel = (is_gt(hi, lo) | (is_eq(hi, lo) & (kidx_ref[sl, :] <= j_cut16))) & (hi != jnp.int16(-32768))
        bias_ref[sl, :] = jnp.where(sel, jnp.zeros((tk, tq), BF16),
                                    jnp.full((tk, tq), NEG_INF, BF16)).astype(F32)
        return carry

    lax.fori_loop(0, nkt, bias_body, 0)

    acc_ref[...] = jnp.zeros_like(acc_ref)

    def att_body(kt, carry):
        sl = pl.ds(pl.multiple_of(kt * tk, tk), tk)
        bias = bias_ref[sl, :]
        ones = jnp.ones((16, tk), BF16)
        for h in range(HEADS):
            hs = slice(h * DH, (h + 1) * DH)
            s = s_ref[h, sl, :] + bias
            m_old = m_ref[h]
            m_new = jnp.maximum(m_old, jnp.max(s, axis=0, keepdims=True))
            alpha = jnp.exp2(m_old - m_new)
            p = jnp.exp2(s - m_new).astype(BF16)
            pv = lax.dot_general(jnp.concatenate([vt_ref[kt, hs, :], ones], axis=0), p, NN,
                                 preferred_element_type=F32)
            acc_ref[hs, :] = alpha * acc_ref[hs, :] + pv[0:DH, :]
            m_ref[h] = m_new
            l_ref[h] = alpha * l_ref[h] + pv[DH:DH + 1, :]
        return carry

    m_ref[...] = jnp.full_like(m_ref, NEG_INF)
    l_ref[...] = jnp.zeros_like(l_ref)
    lax.fori_loop(0, nkt, att_body, 0)
    o_t = jnp.concatenate([acc_ref[h * DH:(h + 1) * DH, :] / l_ref[h] for h in range(HEADS)], axis=0)
    o_ref[...] = o_t.T


def _dsa(iqt, iwt, qt, ik, kr, vt, b, s, n_sel):
    tq = min(DSA_TQ, s)
    tk = min(DSA_TK, s)
    nq = s // tq
    nkt = s // tk
    return pl.pallas_call(
        functools.partial(_dsa_kernel, tq=tq, tk=tk, n_sel=n_sel, idx_bits=int(math.log2(s))),
        grid=(b, nq),
        in_specs=[pl.BlockSpec((IDX_HEADS * DH, tq), lambda i, j: (0, i * nq + j)),
                  pl.BlockSpec((8, tq), lambda i, j: (0, i * nq + j)),
                  pl.BlockSpec((HW, tq), lambda i, j: (0, i * nq + j)),
                  pl.BlockSpec((s, DH), lambda i, j: (i, 0)),
                  pl.BlockSpec((HEADS, s, DH), lambda i, j: (0, i, 0)),
                  pl.BlockSpec((nkt, HW, tk), lambda i, j: (i, 0, 0)),
                  pl.BlockSpec((s, tq), lambda i, j: (0, 0))],
        out_specs=pl.BlockSpec((tq, HW), lambda i, j: (i * nq + j, 0)),
        out_shape=jax.ShapeDtypeStruct((b * s, HW), F32),
        scratch_shapes=[pltpu.VMEM((s, tq), I16), pltpu.VMEM((s, tq), I16), pltpu.VMEM((s, tq), I16),
                        pltpu.VMEM((s, tq), F32), pltpu.VMEM((HW, tq), F32),
                        pltpu.VMEM((HEADS, s, tq), F32), pltpu.VMEM((HEADS, 1, tq), F32),
                        pltpu.VMEM((HEADS, 1, tq), F32)],
        compiler_params=_cparams(2),
        name="dsa",
    )(iqt, iwt, qt, ik, kr, vt,
      jnp.broadcast_to(jnp.arange(s, dtype=I16)[:, None], (s, tq)))


def _merge_kernel(x_ref, oa_ref, ob_ref, oc_ref, od_ref, ga_ref, gb_ref, gc_ref, gd_ref,
                  wa_ref, wb_ref, wc_ref, wd_ref, wo_ref, o_ref):
    def branch(o_r, g_r, w_r):
        y = jnp.dot(o_r[...].astype(BF16), w_r[...], preferred_element_type=F32)
        return _sigmoid(g_r[...].astype(F32)) * y

    merged = (branch(oa_ref, ga_ref, wa_ref) + branch(ob_ref, gb_ref, wb_ref)
              + branch(oc_ref, gc_ref, wc_ref) + branch(od_ref, gd_ref, wd_ref))
    o_ref[...] = x_ref[...] + jnp.dot(merged.astype(BF16), wo_ref[...], preferred_element_type=F32)


def _merge(x2d, oa, ob, oc, od, gates, wa, wb, wc, wd, wo):
    t, d = x2d.shape
    tm = min(512, t)
    row = lambda w: pl.BlockSpec((tm, w), lambda i: (i, 0))
    gate = lambda n: pl.BlockSpec((tm, d), lambda i, o=n: (i, o))
    full = lambda shape: pl.BlockSpec(shape, lambda i: (0,) * len(shape))
    return pl.pallas_call(
        _merge_kernel,
        grid=(t // tm,),
        in_specs=[row(d), row(HW), row(HW), row(HW), row(HW), gate(0), gate(1), gate(2), gate(3),
                  full((HW, d)), full((HW, d)), full((HW, d)), full((HW, d)), full((d, d))],
        out_specs=row(d),
        out_shape=jax.ShapeDtypeStruct((t, d), F32),
        compiler_params=_cparams(1),
        name="merge",
    )(x2d, oa, ob, oc, od, gates, gates, gates, gates, wa, wb, wc, wd, wo)


def _mlp_kernel(x_ref, g_ref, wu_ref, wd_ref, fg_ref, o_ref, h_ref, acc_ref, *, final_norm):
    fi = pl.program_id(1)

    @pl.when(fi == 0)
    def _():
        x = x_ref[...]
        ms = jnp.mean(x * x, axis=-1, keepdims=True)
        h_ref[...] = (x * lax.rsqrt(ms + NORM_EPS) * g_ref[...]).astype(BF16)
        acc_ref[...] = jnp.zeros_like(acc_ref)

    u = jnp.dot(h_ref[...], wu_ref[...], preferred_element_type=F32)
    u = jnp.square(jnp.maximum(u, 0.0))
    acc_ref[...] += jnp.dot(u.astype(BF16), wd_ref[...], preferred_element_type=F32)

    @pl.when(fi == pl.num_programs(1) - 1)
    def _():
        y = x_ref[...] + acc_ref[...]
        if final_norm:
            ms = jnp.mean(y * y, axis=-1, keepdims=True)
            y = y * lax.rsqrt(ms + NORM_EPS) * fg_ref[...]
        o_ref[...] = y


def _mlp(x2d, g, wu, wd, fg, final_norm):
    t, d = x2d.shape
    f = wu.shape[1]
    tm = min(1024, t)
    tf = 512
    return pl.pallas_call(
        functools.partial(_mlp_kernel, final_norm=final_norm),
        grid=(t // tm, f // tf),
        in_specs=[pl.BlockSpec((tm, d), lambda i, j: (i, 0)),
                  pl.BlockSpec((1, d), lambda i, j: (0, 0)),
                  pl.BlockSpec((d, tf), lambda i, j: (0, j)),
                  pl.BlockSpec((tf, d), lambda i, j: (j, 0)),
                  pl.BlockSpec((1, d), lambda i, j: (0, 0))],
        out_specs=pl.BlockSpec((tm, d), lambda i, j: (i, 0)),
        out_shape=jax.ShapeDtypeStruct((t, d), F32),
        scratch_shapes=[pltpu.VMEM((tm, d), BF16), pltpu.VMEM((tm, d), F32)],
        compiler_params=_cparams(2),
        name="mlp",
    )(x2d, g, wu, wd, fg)


def _slab_weight(w_in, dtype=BF16):
    d = w_in.shape[0]
    z = lambda n: jnp.zeros((d, n), w_in.dtype)
    gdn, mla, rwkv, dsa = 0, 1032, 1448, 2472
    gate = 3824
    cols = [
        w_in[:, rwkv:rwkv + 1024],
        w_in[:, gdn:gdn + 768],
        w_in[:, gdn + 776:gdn + 1032],
        w_in[:, mla:mla + 256],
        w_in[:, mla + 256:mla + 384],
        z(64), w_in[:, mla + 384:mla + 416], z(32),
        w_in[:, dsa:dsa + 512],
        w_in[:, dsa + 584:dsa + 1352],
        w_in[:, gdn + 768:gdn + 776], z(120),
        w_in[:, dsa + 512:dsa + 584], z(56),
        w_in[:, gate:gate + 4096],
    ]
    return jnp.concatenate(cols, axis=1).astype(dtype)


def _mla_weights(w_uq, w_ukv):
    wq = w_uq.reshape(MLA_Q_LORA, HEADS, MLA_NOPE + MLA_ROPE)
    wq = jnp.pad(wq, ((0, 0), (0, 0), (0, MLA_QK - MLA_NOPE - MLA_ROPE))).reshape(MLA_Q_LORA, HEADS * MLA_QK)
    wkv = w_ukv.reshape(MLA_KV_LORA, HEADS, MLA_NOPE + DH)
    wk = jnp.pad(wkv[:, :, :MLA_NOPE], ((0, 0), (0, 0), (0, MLA_QK - MLA_NOPE))).reshape(MLA_KV_LORA, HEADS * MLA_QK)
    wvt = wkv[:, :, MLA_NOPE:].reshape(MLA_KV_LORA, HW).T
    return wq.astype(BF16), wk.astype(BF16), wvt.astype(BF16)


def _rope_cos_sin(positions, rot_dims):
    inv_freq = ROPE_THETA ** (-jnp.arange(0, rot_dims, 2, dtype=F32) / rot_dims)
    ang = positions.astype(F32)[..., None] * inv_freq
    return jnp.concatenate([jnp.cos(ang), jnp.sin(ang)], axis=-1)


def _rope_expansion(rot_dims, offset, period):
    half = rot_dims // 2
    e = np.zeros((rot_dims, 3 * 128), np.float32)
    base = np.zeros((1, 3 * 128), np.float32)
    base[0, 0:128] = 1.0
    for rep in range(128 // period):
        lo = rep * period + offset
        for j in range(half):
            e[j, lo + j] = 1.0
            e[j, lo + half + j] = 1.0
            base[0, lo + j] = 0.0
            base[0, lo + half + j] = 0.0
            e[half + j, 128 + lo + j] = -1.0
            e[half + j, 256 + lo + half + j] = 1.0
    return jnp.asarray(e), jnp.asarray(base)


def _rope_table(cs_ref, e_ref, base_ref):
    tab = _mm_exact_rhs(cs_ref[...], e_ref[...]) + base_ref[...]
    return tab[:, 0:128], tab[:, 128:256], tab[:, 256:384]


def _layer(x2d, b, s, prm, tabs, final_g, is_last):
    (mix_g, w_slab, conv_w, gdn_pcol, gdn_prow, gdn_ng, mla_qg, mla_kvg, wq, wk, wv,
     rwkv_mu, rwkv_vec, ww2, wa2, wg2, wo_gdn, wo_mla, wo_rwkv, wo_dsa, w_o, mlp_g, w_up, w_down) = prm
    mla_cs, dsa_cs = tabs
    t = b * s
    slab = _in_proj(x2d, mix_g, w_slab[:, :OFF_GATE], F32)
    gates = _in_proj(x2d, mix_g, w_slab[:, OFF_GATE:], BF16)

    ab = slab[:, OFF_GDN_AB:OFF_GDN_AB + 8].reshape(b, s // CHUNK, CHUNK, 8)
    abr = jnp.swapaxes(ab, 2, 3)
    o_gdn = _gdn(slab, abr, conv_w, gdn_pcol, gdn_prow, gdn_ng, b, s)
    q, k, v = _mla_prep(slab, mla_cs, mla_qg, mla_kvg, wq, wk, wv)
    o_mla = _flash(q, k, v, b, s)
    o_rwkv = _rwkv(slab, rwkv_mu, rwkv_vec, ww2, wa2, wg2, b, s)
    ik, kr, vt, iqt, qt, iwt = _dsa_prep(slab, dsa_cs)
    o_dsa = _dsa(iqt, iwt, qt, ik, kr, vt, b, s, min(IDX_TOPK_MAX, s // 4))

    x2d = _merge(x2d, o_gdn, o_mla, o_rwkv, o_dsa, gates, wo_gdn, wo_mla, wo_rwkv, wo_dsa, w_o)
    return _mlp(x2d, mlp_g, w_up, w_down, final_g, is_last)


def kernel(x, positions, mix_norm_g, w_in, gdn_conv_w, gdn_a_log, gdn_dt_bias, gdn_norm_g, mla_q_norm_g, mla_kv_norm_g, mla_w_uq, mla_w_ukv, rwkv_mu, rwkv_w0, rwkv_w_w2, rwkv_a0, rwkv_w_a2, rwkv_w_g2, rwkv_k_k, rwkv_k_a, rwkv_r_k, rwkv_ln_w, rwkv_ln_b, w_out_gdn, w_out_mla, w_out_rwkv, w_out_dsa, w_o, mlp_norm_g, w_up, w_down, final_norm_g):
    b, s, d = x.shape
    depth = w_in.shape[0]
    tabs = (_rope_cos_sin(positions, MLA_ROPE).reshape(b * s, MLA_ROPE),
            _rope_cos_sin(positions, DH // 4).reshape(b * s, DH // 4))
    row = lambda a: a.reshape(1, -1).astype(F32)
    pad8 = lambda rows: jnp.concatenate(rows + [jnp.zeros((8 - len(rows), rows[0].shape[1]), F32)], axis=0)
    x2d = x.reshape(b * s, d)
    final_g = row(final_norm_g)
    for l in range(depth):
        lane4 = lambda a: jnp.pad(a.astype(F32), (0, 124)).reshape(1, 128)
        gdn_pcol = pad8([lane4(gdn_a_log[l]), lane4(gdn_dt_bias[l])])
        gdn_prow = jnp.pad(jnp.stack([gdn_a_log[l], gdn_dt_bias[l]], axis=1).astype(F32), ((0, 4), (0, 126)))
        wq, wk, wv = _mla_weights(mla_w_uq[l], mla_w_ukv[l])
        rwkv_vec = pad8([row(rwkv_w0[l]), row(rwkv_a0[l]), row(rwkv_k_k[l]), row(rwkv_k_a[l]),
                         row(rwkv_ln_w[l]), row(rwkv_ln_b[l]), row(rwkv_r_k[l])])
        prm = (row(mix_norm_g[l]), _slab_weight(w_in[l]), gdn_conv_w[l].astype(F32), gdn_pcol, gdn_prow,
               row(jnp.tile(gdn_norm_g[l], HEADS)), row(mla_q_norm_g[l]), row(mla_kv_norm_g[l]), wq, wk, wv,
               row(rwkv_mu[l]), rwkv_vec, rwkv_w_w2[l].astype(F32), rwkv_w_a2[l].astype(F32),
               rwkv_w_g2[l].astype(F32),
               w_out_gdn[l].astype(BF16), w_out_mla[l].astype(BF16), w_out_rwkv[l].astype(BF16),
               w_out_dsa[l].astype(BF16), w_o[l].astype(BF16), row(mlp_norm_g[l]),
               w_up[l].astype(BF16), w_down[l].astype(BF16))
        x2d = _layer(x2d, b, s, prm, tabs, final_g, l == depth - 1)
    return x2d.reshape(b, s, d)
```

```python
import functools
import math

import numpy as np
import jax
import jax.numpy as jnp
from jax import lax
from jax.experimental import pallas as pl
from jax.experimental.pallas import tpu as pltpu

F32 = jnp.float32
BF16 = jnp.bfloat16
I32 = jnp.int32
I16 = jnp.int16

D_MODEL = 1024
DEPTH = 4
ROPE_THETA = 500000.0
NORM_EPS = 1e-6
NEG_INF = -1e30
HEADS = 4
DH = 64
HW = HEADS * DH
GDN_CONV = 4
CHUNK = 64
MLA_Q_LORA = 256
MLA_KV_LORA = 128
MLA_NOPE = 64
MLA_ROPE = 32
MLA_QK = 128
RWKV_GN_EPS = 64e-5
IDX_HEADS = 8
IDX_TOPK_MAX = 256
D_FF = 4 * D_MODEL
INT_MIN = -2 ** 31
LOG2E = 1.4426950408889634

VMEM_LIMIT_BYTES = 56 * 1024 * 1024

OFF_RWKV = 0
OFF_GDN_Q = 1024
OFF_GDN_K = 1280
OFF_GDN_V = 1536
OFF_GDN_Z = 1792
OFF_MLA_CQ = 2048
OFF_MLA_CKV = 2304
OFF_MLA_KR = 2432
OFF_DSA_IQ = 2560
OFF_DSA_Q = 3072
OFF_DSA_K = 3328
OFF_DSA_V = 3584
OFF_GDN_AB = 3840
OFF_DSA_IKW = 3968
OFF_GATE = 4096
N_SLAB = 8192

NN = (((1,), (0,)), ((), ()))
NT = (((1,), (1,)), ((), ()))
TN = (((0,), (0,)), ((), ()))
BNN = (((2,), (1,)), ((0,), (0,)))
BNT = (((2,), (2,)), ((0,), (0,)))
BTN = (((1,), (1,)), ((0,), (0,)))


def _cparams(n_axes):
    return pltpu.CompilerParams(dimension_semantics=("arbitrary",) * n_axes,
                                vmem_limit_bytes=VMEM_LIMIT_BYTES)


def _mm(a, b, dims=NN):
    return lax.dot_general(a.astype(BF16), b.astype(BF16), dims, preferred_element_type=F32)


def _mm_exact_lhs(m, x, dims=NN):
    mb = m.astype(BF16)
    x0 = x.astype(BF16)
    r1 = x - x0.astype(F32)
    x1 = r1.astype(BF16)
    x2 = (r1 - x1.astype(F32)).astype(BF16)
    d = lambda y: lax.dot_general(mb, y, dims, preferred_element_type=F32)
    return d(x0) + (d(x1) + d(x2))


def _mm_exact_rhs(x, m, dims=NN):
    mb = m.astype(BF16)
    x0 = x.astype(BF16)
    r1 = x - x0.astype(F32)
    x1 = r1.astype(BF16)
    x2 = (r1 - x1.astype(F32)).astype(BF16)
    d = lambda y: lax.dot_general(y, mb, dims, preferred_element_type=F32)
    return d(x0) + (d(x1) + d(x2))


def _iota2(shape, dim):
    return lax.broadcasted_iota(I32, shape, dim)


def _seg_ones(width, seg):
    sh = int(math.log2(seg))
    r = _iota2((width, width), 0) >> sh
    c = _iota2((width, width), 1) >> sh
    return jnp.where(r == c, 1.0, 0.0).astype(F32)


def _seg_sum(x, seg_ones):
    return _mm_exact_rhs(x, seg_ones)


def _softplus(x):
    return jnp.maximum(x, 0.0) + jnp.log1p(jnp.exp(-jnp.abs(x)))


def _sigmoid(x):
    return 1.0 / (1.0 + jnp.exp(-x))


def _silu(x):
    return x * _sigmoid(x)


def _tri_masks(n):
    r = _iota2((n, n), 0)
    c = _iota2((n, n), 1)
    return r, c


def _unit_lower_inverse(m, r, c, dims=NN):
    eye = jnp.where(r == c, 1.0, 0.0).astype(F32)
    same16 = (r >> 4) == (c >> 4)
    same32 = (r >> 5) == (c >> 5)
    md = jnp.where(same16, m, 0.0)
    t = eye - md
    p = md
    for _ in range(3):
        p = _mm(p, p, dims)
        t = t + _mm(t, p, dims)
    c32 = jnp.where(same32 & jnp.logical_not(same16), m, 0.0)
    t = t - _mm(_mm(t, c32, dims), t, dims)
    c64 = jnp.where(same32, 0.0, m)
    t = t - _mm(_mm(t, c64, dims), t, dims)
    return t


IN_PROJ_TN = 512


def _in_proj_kernel(x_ref, g_ref, w_ref, o_ref):
    x = x_ref[...]
    ms = jnp.mean(x * x, axis=-1, keepdims=True)
    h = (x * lax.rsqrt(ms + NORM_EPS) * g_ref[...]).astype(BF16)
    for n0 in range(0, o_ref.shape[1], IN_PROJ_TN):
        o_ref[:, n0:n0 + IN_PROJ_TN] = jnp.dot(
            h, w_ref[:, n0:n0 + IN_PROJ_TN], preferred_element_type=F32).astype(o_ref.dtype)


def _in_proj(x2d, g, w, out_dtype):
    t, d = x2d.shape
    n = w.shape[1]
    tm = min(512, t)
    return pl.pallas_call(
        _in_proj_kernel,
        grid=(t // tm,),
        in_specs=[pl.BlockSpec((tm, d), lambda i: (i, 0)),
                  pl.BlockSpec((1, d), lambda i: (0, 0)),
                  pl.BlockSpec((d, n), lambda i: (0, 0))],
        out_specs=pl.BlockSpec((tm, n), lambda i: (i, 0)),
        out_shape=jax.ShapeDtypeStruct((t, n), out_dtype),
        compiler_params=_cparams(1),
        name="in_proj",
    )(x2d, g, w)


def _to_groups(x, nchunk):
    parts = [x[:, h * DH:(h + 1) * DH].reshape(nchunk, 1, CHUNK, DH) for h in range(HEADS)]
    return jnp.concatenate(parts, axis=1).reshape(nchunk * HEADS, CHUNK, DH)


def _col_groups(x, lane0, nchunk):
    parts = [x[:, lane0 + h:lane0 + h + 1].reshape(nchunk, 1, CHUNK, 1) for h in range(HEADS)]
    return jnp.concatenate(parts, axis=1).reshape(nchunk * HEADS, CHUNK, 1)


def _gdn_kernel(q_ref, k_ref, v_ref, z_ref, ab_ref, abr_ref, cw_ref, pcol_ref, prow_ref, ng_ref,
                o_ref, xbuf_ref, state_ref, u_ref, w_ref, a_ref, qd_ref, kd_ref, egl_ref, oraw_ref,
                carry_ref, *, tb, nbatch):
    nchunk = tb // CHUNK
    bi = pl.program_id(1)
    xbuf = xbuf_ref

    @pl.when(pl.program_id(0) == 0)
    def _():
        carry_ref[bi] = jnp.zeros((8, 3 * HW), F32)
        state_ref[bi] = jnp.zeros((HEADS, DH, DH), F32)

    xbuf[0:8, :] = carry_ref[bi]
    xbuf[8:8 + tb, 0:HW] = q_ref[...]
    xbuf[8:8 + tb, HW:2 * HW] = k_ref[...]
    xbuf[8:8 + tb, 2 * HW:3 * HW] = v_ref[...]
    cw = cw_ref[...]
    y = xbuf[8:8 + tb, :] * cw[3:4, :]
    y = y + xbuf[7:7 + tb, :] * cw[2:3, :]
    y = y + xbuf[6:6 + tb, :] * cw[1:2, :]
    y = y + xbuf[5:5 + tb, :] * cw[0:1, :]
    carry_ref[bi] = xbuf[tb:tb + 8, :]
    y = _silu(y)
    seg = _seg_ones(HW, DH)
    q = y[:, 0:HW]
    k = y[:, HW:2 * HW]
    q = q * lax.rsqrt(_seg_sum(q * q, seg) + 1e-6) * (DH ** -0.5)
    k = k * lax.rsqrt(_seg_sum(k * k, seg) + 1e-6)
    v = y[:, 2 * HW:3 * HW]

    ab = ab_ref[...]
    pcol = pcol_ref[...]
    g_col = -jnp.exp(pcol[0:1, :]) * _softplus(ab + pcol[1:2, :])
    beta_col = _sigmoid(ab)

    r, c = _tri_masks(CHUNK)
    incl = c <= r
    strict = c < r
    l_incl = jnp.where(incl, 1.0, 0.0).astype(F32)

    rb = _iota2((tb, tb), 0)
    cb = _iota2((tb, tb), 1)
    l_blk = jnp.where(((rb >> 6) == (cb >> 6)) & (cb <= rb), 1.0, 0.0).astype(F32)
    gc_col = _mm_exact_lhs(l_blk, g_col)
    prow = prow_ref[...]
    abr = abr_ref[0].reshape(nchunk * 8, CHUNK)
    prow_t = jnp.concatenate([prow] * nchunk, axis=0)
    g_row = -jnp.exp(prow_t[:, 0:1]) * _softplus(abr + prow_t[:, 1:2])
    gc_row = _mm_exact_rhs(g_row, l_incl, NT).reshape(nchunk, 8, CHUNK)

    qg, kg, vg = _to_groups(q, nchunk), _to_groups(k, nchunk), _to_groups(v, nchunk)
    beta = _col_groups(beta_col, 4, nchunk)
    gcol = _col_groups(gc_col, 0, nchunk)
    grow = jnp.concatenate([gc_row[:, h:h + 1, :].reshape(nchunk, 1, 1, CHUNK) for h in range(HEADS)],
                           axis=1).reshape(nchunk * HEADS, 1, CHUNK)
    decay = jnp.where(incl, jnp.exp(jnp.where(incl, gcol - grow, 0.0)), 0.0)
    kb = kg * beta
    m = jnp.where(strict, _mm(kb, kg, BNT) * decay, 0.0)
    t_inv = _unit_lower_inverse(m, r, c, BNN)
    egc = jnp.exp(gcol)
    glast = gcol[:, CHUNK - 1:CHUNK, :]
    uw = _mm(t_inv, jnp.concatenate([vg * beta, kb * egc], axis=2), BNN)
    u_ref[bi] = uw[:, :, 0:DH]
    w_ref[bi] = uw[:, :, DH:2 * DH].astype(BF16)
    a_ref[bi] = (_mm(qg, kg, BNT) * decay).astype(BF16)
    qd_ref[bi] = (qg * egc).astype(BF16)
    kd_ref[bi] = (kg * jnp.exp(glast - gcol)).astype(BF16)
    egl_ref[bi] = jnp.broadcast_to(jnp.exp(glast), (nchunk * HEADS, 1, DH))

    @pl.when(bi == nbatch - 1)
    def _():
        nc = nbatch * HEADS

        def chunk_body(ci, carry):
            sl = pl.ds(pl.multiple_of(ci * HEADS, HEADS), HEADS)
            take = lambda ref: ref[:, sl].reshape((nc,) + ref.shape[2:])
            s = state_ref[...].reshape(nc, DH, DH)
            v_new = take(u_ref) - _mm(take(w_ref), s, BNN)
            o = _mm(take(qd_ref), s, BNN) + _mm(take(a_ref), v_new, BNN)
            s = s * take(egl_ref) + _mm(take(kd_ref), v_new, BTN)
            state_ref[...] = s.reshape(nbatch, HEADS, DH, DH)
            rows = pl.ds(pl.multiple_of(ci * CHUNK, CHUNK), CHUNK)
            for bb in range(nbatch):
                oraw_ref[bb, rows, :] = jnp.concatenate([o[bb * HEADS + h] for h in range(HEADS)], axis=1)
            return carry

        lax.fori_loop(0, nchunk, chunk_body, 0)
        o_all = oraw_ref[...].reshape(nbatch * tb, HW)
        ms = _seg_sum(o_all * o_all, seg) * (1.0 / DH)
        o_n = o_all * lax.rsqrt(ms + NORM_EPS) * ng_ref[...] * _silu(z_ref[...].reshape(nbatch * tb, HW))
        o_ref[...] = o_n.reshape(nbatch, tb, HW).astype(o_ref.dtype)


GDN_TB = 512


def _gdn(slab, abr, conv_w, pcol, prow, norm_g, b, s):
    tb = min(GDN_TB, s)
    nb = s // tb
    nchunk = tb // CHUNK
    ng = nchunk * HEADS
    grp = lambda w, dt=BF16: pltpu.VMEM((b, ng, CHUNK, w), dt)
    col = lambda off, w: pl.BlockSpec((tb, w), lambda j, i, o=off // w: (i * nb + j, o))
    full = lambda shape: pl.BlockSpec(shape, lambda j, i: (0,) * len(shape))
    slab3 = slab.reshape(b, s, slab.shape[1])
    out = pl.pallas_call(
        functools.partial(_gdn_kernel, tb=tb, nbatch=b),
        grid=(nb, b),
        in_specs=[col(OFF_GDN_Q, HW), col(OFF_GDN_K, HW), col(OFF_GDN_V, HW),
                  pl.BlockSpec((b, tb, HW), lambda j, i: (0, j, OFF_GDN_Z // HW)),
                  col(OFF_GDN_AB, 128),
                  pl.BlockSpec((1, nchunk, 8, CHUNK), lambda j, i: (i, j, 0, 0)),
                  full((GDN_CONV, 3 * HW)), full((8, 128)), full((8, 128)), full((1, HW))],
        out_specs=pl.BlockSpec((b, tb, HW), lambda j, i: (0, j, 0)),
        out_shape=jax.ShapeDtypeStruct((b, s, HW), BF16),
        scratch_shapes=[pltpu.VMEM((tb + 8, 3 * HW), F32), pltpu.VMEM((b, HEADS, DH, DH), F32),
                        grp(DH, F32), grp(DH), grp(CHUNK), grp(DH), grp(DH),
                        pltpu.VMEM((b, ng, 1, DH), F32), pltpu.VMEM((b, tb, HW), F32),
                        pltpu.VMEM((b, 8, 3 * HW), F32)],
        compiler_params=_cparams(2),
        name="gdn",
    )(slab, slab, slab, slab3, slab, abr, conv_w, pcol, prow, norm_g)
    return out.reshape(b * s, HW)


def _rwkv_kernel(p_ref, mu_ref, vec_ref, ww2_ref, wa2_ref, wg2_ref, o_ref,
                 xbuf_ref, state_ref, ta_ref, tpv_ref, rt_ref, qkv_ref, qb_ref, bc_ref, ktv_ref,
                 el_ref, oraw_ref, carry_ref, post_ref, *, tb, nbatch):
    nchunk = tb // CHUNK
    bi = pl.program_id(1)

    @pl.when(pl.program_id(0) == 0)
    def _():
        carry_ref[bi] = jnp.zeros((8, 4 * HW), F32)
        state_ref[bi] = jnp.zeros((HEADS, DH, DH), F32)

    p = p_ref[...]
    xbuf_ref[0:8, :] = carry_ref[bi]
    xbuf_ref[8:8 + tb, :] = p
    prev = xbuf_ref[7:7 + tb, :]
    carry_ref[bi] = xbuf_ref[tb:tb + 8, :]
    pm = p + (prev - p) * mu_ref[...]
    vec = vec_ref[...]
    w0, a0, k_k, k_a = vec[0:1, :], vec[1:2, :], vec[2:3, :], vec[3:4, :]
    rr = pm[:, 0:HW]
    k = pm[:, HW:2 * HW]
    v = pm[:, 2 * HW:3 * HW]
    xw = pm[:, 768:832]
    xa = pm[:, 832:896]
    xg = pm[:, 896:1024]
    w_log = -_softplus(-(w0 + _mm(jnp.tanh(xw), ww2_ref[...]))) - 0.5
    lw = -jnp.exp(w_log)
    a = _sigmoid(a0 + _mm(xa, wa2_ref[...]))
    g = _mm(_sigmoid(xg), wg2_ref[...])
    seg = _seg_ones(HW, DH)
    kk = k * k_k
    kk = kk * lax.rsqrt(_seg_sum(kk * kk, seg) + 1e-6)
    k2 = k * (1.0 + (a - 1.0) * k_a)
    kka = kk * a

    r_i, c_i = _tri_masks(CHUNK)
    incl = c_i <= r_i
    strict = c_i < r_i
    rb = _iota2((tb, tb), 0)
    cb = _iota2((tb, tb), 1)
    l_blk = jnp.where(((rb >> 6) == (cb >> 6)) & (cb <= rb), 1.0, 0.0).astype(F32)
    lwc_all = _mm_exact_lhs(l_blk, lw)

    rg, kg, vg = _to_groups(rr, nchunk), _to_groups(k2, nchunk), _to_groups(v, nchunk)
    kkg, kkag = _to_groups(kk, nchunk), _to_groups(kka, nchunk)
    lwg, lwc = _to_groups(lw, nchunk), _to_groups(lwc_all, nchunk)
    lwl = lwc[:, CHUNK - 1:CHUNK, :]
    e_neg = jnp.exp(-lwc)
    at = kkg * jnp.exp(lwc - lwg)
    rt = rg * jnp.exp(lwc)
    y2 = jnp.concatenate([at, rt], axis=1)
    x2 = jnp.concatenate([kkag * e_neg, kg * e_neg], axis=1)
    gram = _mm(y2, x2, BNT)
    n_m = jnp.where(strict, gram[:, 0:CHUNK, 0:CHUNK], 0.0)
    p_m = jnp.where(strict, gram[:, 0:CHUNK, CHUNK:], 0.0)
    qb_m = jnp.where(incl, gram[:, CHUNK:, 0:CHUNK], 0.0)
    qk_m = jnp.where(incl, gram[:, CHUNK:, CHUNK:], 0.0)
    t_inv = _unit_lower_inverse(n_m, r_i, c_i, BNN)
    ta_ref[bi] = _mm(t_inv, at, BNN).astype(BF16)
    tpv_ref[bi] = _mm(t_inv, _mm(p_m, vg, BNN), BNN)
    rt_ref[bi] = rt.astype(BF16)
    qkv_ref[bi] = _mm(qk_m, vg, BNN)
    qb_ref[bi] = qb_m.astype(BF16)
    e_rel = jnp.exp(lwl - lwc)
    bc_ref[bi] = (kkag * e_rel).astype(BF16)
    ktv_ref[bi] = _mm(vg, kg * e_rel, BTN)
    el_ref[bi] = jnp.exp(lwl)
    ln_w, ln_b, r_k = vec[4:5, :], vec[5:6, :], vec[6:7, :]
    post_ref[bi, :, 0:HW] = _seg_sum(rr * k2 * r_k, seg) * v
    post_ref[bi, :, HW:2 * HW] = g

    @pl.when(bi == nbatch - 1)
    def _():
        nc = nbatch * HEADS

        def chunk_body(ci, carry):
            sl = pl.ds(pl.multiple_of(ci * HEADS, HEADS), HEADS)
            take = lambda ref: ref[:, sl].reshape((nc,) + ref.shape[2:])
            zt = state_ref[...].reshape(nc, DH, DH)
            u = _mm(take(ta_ref), zt, BNT) + take(tpv_ref)
            o = _mm(take(rt_ref), zt, BNT) + take(qkv_ref) - _mm(take(qb_ref), u, BNN)
            zt = zt * take(el_ref) + take(ktv_ref) - _mm(u, take(bc_ref), BTN)
            state_ref[...] = zt.reshape(nbatch, HEADS, DH, DH)
            rows = pl.ds(pl.multiple_of(ci * CHUNK, CHUNK), CHUNK)
            for bb in range(nbatch):
                oraw_ref[bb, rows, :] = jnp.concatenate([o[bb * HEADS + h] for h in range(HEADS)], axis=1)
            return carry

        lax.fori_loop(0, nchunk, chunk_body, 0)
        o_all = oraw_ref[...].reshape(nbatch * tb, HW)
        mean = _seg_sum(o_all, seg) * (1.0 / DH)
        cen = o_all - mean
        var = _seg_sum(cen * cen, seg) * (1.0 / DH)
        o_n = cen * lax.rsqrt(var + RWKV_GN_EPS) * ln_w + ln_b
        post = post_ref[...].reshape(nbatch * tb, 2 * HW)
        o_ref[...] = ((o_n + post[:, 0:HW]) * post[:, HW:2 * HW]).reshape(nbatch, tb, HW).astype(o_ref.dtype)


RWKV_TB = 512


def _rwkv(slab, mu, vec, ww2, wa2, wg2, b, s):
    tb = min(RWKV_TB, s)
    nb = s // tb
    ng = (tb // CHUNK) * HEADS
    grp = lambda dt=F32: pltpu.VMEM((b, ng, CHUNK, DH), dt)
    full = lambda shape: pl.BlockSpec(shape, lambda j, i: (0,) * len(shape))
    out = pl.pallas_call(
        functools.partial(_rwkv_kernel, tb=tb, nbatch=b),
        grid=(nb, b),
        in_specs=[pl.BlockSpec((tb, 4 * HW), lambda j, i: (i * nb + j, OFF_RWKV // (4 * HW))),
                  full((1, 4 * HW)), full((8, HW)), full((64, HW)), full((64, HW)), full((128, HW))],
        out_specs=pl.BlockSpec((b, tb, HW), lambda j, i: (0, j, 0)),
        out_shape=jax.ShapeDtypeStruct((b, s, HW), BF16),
        scratch_shapes=[pltpu.VMEM((tb + 8, 4 * HW), F32), pltpu.VMEM((b, HEADS, DH, DH), F32),
                        grp(BF16), grp(), grp(BF16), grp(), grp(BF16), grp(BF16), grp(),
                        pltpu.VMEM((b, ng, 1, DH), F32), pltpu.VMEM((b, tb, HW), F32),
                        pltpu.VMEM((b, 8, 4 * HW), F32), pltpu.VMEM((b, tb, 2 * HW), F32)],
        compiler_params=_cparams(2),
        name="rwkv7",
    )(slab, mu, vec, ww2, wa2, wg2)
    return out.reshape(b * s, HW)


def _rope(x, cos_t, sin_lo, sin_hi, half):
    w = x.shape[-1]
    return x * cos_t + pltpu.roll(x, w - half, 1) * sin_lo + pltpu.roll(x, half, 1) * sin_hi


def _tile_lanes(t, n):
    return t if n == 1 else jnp.concatenate([t] * n, axis=1)


def _mla_prep_kernel(cq_ref, ckv_ref, kr_ref, cs_ref, e_ref, base_ref, qg_ref, kvg_ref, wq_ref, wk_ref,
                     wvt_ref, q_ref, k_ref, vt_ref):
    cq = cq_ref[...]
    hq = cq * lax.rsqrt(jnp.mean(cq * cq, axis=-1, keepdims=True) + NORM_EPS) * qg_ref[...]
    ckv = ckv_ref[...]
    hkv = ckv * lax.rsqrt(jnp.mean(ckv * ckv, axis=-1, keepdims=True) + NORM_EPS) * kvg_ref[...]
    cos_t, sin_lo, sin_hi = _rope_table(cs_ref, e_ref, base_ref)
    q = _mm(hq, wq_ref[...])
    q = _rope(q, _tile_lanes(cos_t, HEADS), _tile_lanes(sin_lo, HEADS), _tile_lanes(sin_hi, HEADS),
              MLA_ROPE // 2)
    q_ref[...] = (q * ((MLA_NOPE + MLA_ROPE) ** -0.5 * LOG2E)).astype(BF16)
    kr = _rope(kr_ref[...], cos_t, sin_lo, sin_hi, MLA_ROPE // 2)
    k = _mm(hkv, wk_ref[...]) + _tile_lanes(kr, HEADS)
    k_ref[...] = k.astype(BF16)
    vt_ref[...] = lax.dot_general(wvt_ref[...], hkv.astype(BF16), NT,
                                  preferred_element_type=F32).astype(BF16)


def _mla_prep(slab, cs, qg, kvg, wq, wk, wvt):
    t = slab.shape[0]
    tm = min(512, t)
    full = lambda shape: pl.BlockSpec(shape, lambda i: (0,) * len(shape))
    e, base = _rope_expansion(MLA_ROPE, MLA_NOPE, MLA_QK)
    return pl.pallas_call(
        _mla_prep_kernel,
        grid=(t // tm,),
        in_specs=[pl.BlockSpec((tm, 256), lambda i: (i, OFF_MLA_CQ // 256)),
                  pl.BlockSpec((tm, 128), lambda i: (i, OFF_MLA_CKV // 128)),
                  pl.BlockSpec((tm, 128), lambda i: (i, OFF_MLA_KR // 128)),
                  pl.BlockSpec((tm, MLA_ROPE), lambda i: (i, 0)), full(e.shape), full(base.shape),
                  full((1, 256)), full((1, 128)), full((256, 512)), full((128, 512)), full((256, 128))],
        out_specs=[pl.BlockSpec((tm, 512), lambda i: (i, 0)), pl.BlockSpec((tm, 512), lambda i: (i, 0)),
                   pl.BlockSpec((256, tm), lambda i: (0, i))],
        out_shape=[jax.ShapeDtypeStruct((t, 512), BF16), jax.ShapeDtypeStruct((t, 512), BF16),
                   jax.ShapeDtypeStruct((256, t), BF16)],
        compiler_params=_cparams(1),
        name="mla_prep",
    )(slab, slab, slab, cs, e, base, qg, kvg, wq, wk, wvt)


def _flash_kernel(q_ref, k_ref, vt_ref, o_ref, m_ref, l_ref, acc_ref, *, tq, tk):
    qi = pl.program_id(1)
    ki = pl.program_id(2)

    @pl.when(ki == 0)
    def _():
        m_ref[...] = jnp.full_like(m_ref, NEG_INF)
        l_ref[...] = jnp.zeros_like(l_ref)
        acc_ref[...] = jnp.zeros_like(acc_ref)

    def step(on_diagonal):
        if on_diagonal:
            causal = _iota2((tk, tq), 0) <= _iota2((tk, tq), 1)
        vt = vt_ref[...]
        ones = jnp.ones((16, tk), BF16)
        scores = [lax.dot_general(k_ref[:, h * MLA_QK:(h + 1) * MLA_QK], q_ref[:, h * MLA_QK:(h + 1) * MLA_QK],
                                  NT, preferred_element_type=F32) for h in range(HEADS)]
        for h in range(HEADS):
            s = scores[h]
            if on_diagonal:
                s = jnp.where(causal, s, NEG_INF)
            m_old = m_ref[h]
            m_new = jnp.maximum(m_old, jnp.max(s, axis=0, keepdims=True))
            alpha = jnp.exp2(m_old - m_new)
            p = jnp.exp2(s - m_new).astype(BF16)
            pv = jnp.dot(jnp.concatenate([vt[h * DH:(h + 1) * DH, :], ones], axis=0), p,
                         preferred_element_type=F32)
            l_ref[h] = alpha * l_ref[h] + pv[DH:DH + 1, :]
            acc_ref[h * DH:(h + 1) * DH, :] = alpha * acc_ref[h * DH:(h + 1) * DH, :] + pv[0:DH, :]
            m_ref[h] = m_new

    pl.when(ki < qi)(lambda: step(False))
    pl.when(ki == qi)(lambda: step(True))

    @pl.when(ki == pl.num_programs(2) - 1)
    def _():
        o_t = jnp.concatenate([acc_ref[h * DH:(h + 1) * DH, :] / l_ref[h] for h in range(HEADS)], axis=0)
        o_ref[...] = o_t.T.astype(o_ref.dtype)


def _flash(q, k, vt, b, s):
    tq = min(512, s)
    tk = tq
    nq = s // tq
    nk = s // tk
    return pl.pallas_call(
        functools.partial(_flash_kernel, tq=tq, tk=tk),
        grid=(b, nq, nk),
        in_specs=[pl.BlockSpec((tq, 512), lambda i, j, kk: (i * nq + j, 0)),
                  pl.BlockSpec((tk, 512), lambda i, j, kk: (i * nk + jnp.minimum(kk, j), 0)),
                  pl.BlockSpec((256, tk), lambda i, j, kk: (0, i * nk + jnp.minimum(kk, j)))],
        out_specs=pl.BlockSpec((tq, 256), lambda i, j, kk: (i * nq + j, 0)),
        out_shape=jax.ShapeDtypeStruct((b * s, 256), BF16),
        scratch_shapes=[pltpu.VMEM((HEADS, 1, tq), F32), pltpu.VMEM((HEADS, 1, tq), F32),
                        pltpu.VMEM((HW, tq), F32)],
        compiler_params=_cparams(3),
        name="mla_flash",
    )(q, k, vt)


DSA_TQ = 256
DSA_TK = 512


def _dsa_prep_kernel(iq_ref, ikw_ref, q_ref, k_ref, v_ref, cs_ref, e_ref, base_ref,
                     ik_ref, ko_ref, vt_ref, iqt_ref, qt_ref, iwt_ref):
    cos_t, sin_lo, sin_hi = _rope_table(cs_ref, e_ref, base_ref)
    half = DH // 8
    rope = lambda x, n: _rope(x, _tile_lanes(cos_t, n), _tile_lanes(sin_lo, n), _tile_lanes(sin_hi, n), half)
    ikw = ikw_ref[...]
    ik_ref[...] = rope(ikw, 1)[:, 0:DH].astype(BF16)
    kr = rope(k_ref[...], 2)
    for h in range(HEADS):
        ko_ref[h] = kr[:, h * DH:(h + 1) * DH].astype(BF16)
    vt_ref[0] = v_ref[...].T.astype(BF16)
    iqt_ref[...] = rope(iq_ref[...], 4).T.astype(BF16)
    qt_ref[...] = (rope(q_ref[...], 2) * (DH ** -0.5 * LOG2E)).T.astype(BF16)
    iwt_ref[...] = ikw.T[DH:DH + 8, :] * (IDX_HEADS ** -0.5 * DH ** -0.5)


def _dsa_prep(slab, cs):
    t = slab.shape[0]
    tm = min(DSA_TK, t)
    full = lambda shape: pl.BlockSpec(shape, lambda i: (0,) * len(shape))
    e, base = _rope_expansion(DH // 4, 0, DH)
    return pl.pallas_call(
        _dsa_prep_kernel,
        grid=(t // tm,),
        in_specs=[pl.BlockSpec((tm, IDX_HEADS * DH), lambda i: (i, OFF_DSA_IQ // (IDX_HEADS * DH))),
                  pl.BlockSpec((tm, 128), lambda i: (i, OFF_DSA_IKW // 128)),
                  pl.BlockSpec((tm, 256), lambda i: (i, OFF_DSA_Q // 256)),
                  pl.BlockSpec((tm, 256), lambda i: (i, OFF_DSA_K // 256)),
                  pl.BlockSpec((tm, 256), lambda i: (i, OFF_DSA_V // 256)),
                  pl.BlockSpec((tm, DH // 4), lambda i: (i, 0)), full(e.shape), full(base.shape)],
        out_specs=[pl.BlockSpec((tm, DH), lambda i: (i, 0)),
                   pl.BlockSpec((HEADS, tm, DH), lambda i: (0, i, 0)),
                   pl.BlockSpec((1, HW, tm), lambda i: (i, 0, 0)),
                   pl.BlockSpec((IDX_HEADS * DH, tm), lambda i: (0, i)),
                   pl.BlockSpec((HW, tm), lambda i: (0, i)),
                   pl.BlockSpec((8, tm), lambda i: (0, i))],
        out_shape=[jax.ShapeDtypeStruct((t, DH), BF16), jax.ShapeDtypeStruct((HEADS, t, DH), BF16),
                   jax.ShapeDtypeStruct((t // tm, HW, tm), BF16),
                   jax.ShapeDtypeStruct((IDX_HEADS * DH, t), BF16), jax.ShapeDtypeStruct((HW, t), BF16),
                   jax.ShapeDtypeStruct((8, t), F32)],
        compiler_params=_cparams(1),
        name="dsa_prep",
    )(slab, slab, slab, slab, slab, cs, e, base)


def _dsa_kernel(iqt_ref, iwt_ref, qt_ref, ik_ref, k_ref, vt_ref, kidx_ref, o_ref,
                hi_ref, lo_ref, lom_ref, bias_ref, acc_ref, s_ref, m_ref, l_ref,
                *, tq, tk, n_sel, idx_bits):
    qi = pl.program_id(1)
    nkt = (qi * tq + tq - 1) // tk + 1
    iqt = iqt_ref[...]
    iw = iwt_ref[...]
    q_pos = qi * tq + _iota2((tk, tq), 1)
    qt = qt_ref[...]

    def score_body(kt, carry):
        row0 = pl.multiple_of(kt * tk, tk)
        ik = ik_ref[pl.ds(row0, tk), :]
        sc = jnp.zeros((tk, tq), F32)
        for h in range(IDX_HEADS):
            lg = lax.dot_general(ik, iqt[h * DH:(h + 1) * DH, :], NN, preferred_element_type=F32)
            sc = sc + jnp.maximum(lg, 0.0) * iw[h:h + 1, :]
        bits = lax.bitcast_convert_type(sc, I32)
        key = bits ^ ((bits >> 31) & 0x7FFFFFFF)
        key = jnp.where(sc == 0.0, 0, key)
        k_pos = row0 + _iota2((tk, tq), 0)
        key = jnp.where(k_pos <= q_pos, key, INT_MIN)
        hi_ref[pl.ds(row0, tk), :] = (key >> 16).astype(I16)
        lo_ref[pl.ds(row0, tk), :] = ((key & 0xFFFF) - 32768).astype(I16)
        for h in range(HEADS):
            s_ref[h, pl.ds(row0, tk), :] = lax.dot_general(k_ref[h, pl.ds(row0, tk), :],
                                                           qt[h * DH:(h + 1) * DH, :], NN,
                                                           preferred_element_type=F32)
        return carry

    lax.fori_loop(0, nkt, score_body, 0)

    one16 = jnp.ones((tk, tq), I16)
    zero16 = jnp.zeros((tk, tq), I16)

    def count(*pred_fns):
        def body(kt, accs):
            sl = pl.ds(pl.multiple_of(kt * tk, tk), tk)
            args = (hi_ref[sl, :], lo_ref[sl, :], lom_ref[sl, :], kidx_ref[sl, :])
            out = []
            for acc, fn in zip(accs, pred_fns):
                ones = jnp.where(fn(*args), one16, zero16).reshape(tk // 64, 64, tq)
                for g in range(tk // 64):
                    acc = acc + ones[g]
                out.append(acc)
            return tuple(out)
        parts = lax.fori_loop(0, nkt, body, tuple(jnp.zeros((64, tq), I16) for _ in pred_fns))
        sums = tuple(jnp.sum(p.astype(I32).astype(F32), axis=0, keepdims=True) for p in parts)
        return sums[0] if len(sums) == 1 else sums

    def bisect16(pred_ge):
        def body(it, thr):
            bit = jnp.maximum(15 - it, 0)
            cand = jnp.where(it == 0, jnp.zeros_like(thr), thr | jnp.left_shift(jnp.int32(1), bit))
            c16 = cand.astype(I16)
            cnt = count(lambda hi, lo, lom, idx: pred_ge(hi, lo, lom, c16))
            return jnp.where(cnt >= float(n_sel), cand, thr)
        return lax.fori_loop(0, 16, body, jnp.full((1, tq), -32768, I32))

    t_hi = bisect16(lambda hi, lo, lom, c: hi >= c)
    t_hi16 = t_hi.astype(I16)

    def mask_lo_body(kt, carry):
        sl = pl.ds(pl.multiple_of(kt * tk, tk), tk)
        hi = hi_ref[sl, :]
        lom_ref[sl, :] = jnp.where(hi > t_hi16, jnp.int16(32767),
                                   jnp.where(hi == t_hi16, lo_ref[sl, :], jnp.int16(-32768)))
        return carry

    lax.fori_loop(0, nkt, mask_lo_body, 0)
    t_lo16 = bisect16(lambda hi, lo, lom, c: lom >= c).astype(I16)

    def is_gt(hi, lo):
        return (hi > t_hi16) | ((hi == t_hi16) & (lo > t_lo16))

    def is_eq(hi, lo):
        return (hi == t_hi16) & (lo == t_lo16)

    def write_bias(sl, hi, sel_eq, gt):
        sel = (gt | sel_eq) & (hi != jnp.int16(-32768))
        bias_ref[sl, :] = jnp.where(sel, jnp.zeros((tk, tq), BF16),
                                    jnp.full((tk, tq), NEG_INF, BF16)).astype(F32)

    def tally_body(kt, accs):
        sl = pl.ds(pl.multiple_of(kt * tk, tk), tk)
        hi, lo = hi_ref[sl, :], lo_ref[sl, :]
        gt, eq = is_gt(hi, lo), is_eq(hi, lo)
        write_bias(sl, hi, eq, gt)
        out = []
        for acc, m in zip(accs, (gt, eq)):
            ones = jnp.where(m, one16, zero16).reshape(tk // 64, 64, tq)
            for g in range(tk // 64):
                acc = acc + ones[g]
            out.append(acc)
        return tuple(out)

    parts = lax.fori_loop(0, nkt, tally_body, (jnp.zeros((64, tq), I16), jnp.zeros((64, tq), I16)))
    cnt_gt, cnt_eq = (jnp.sum(p.astype(I32).astype(F32), axis=0, keepdims=True) for p in parts)
    need = float(n_sel) - cnt_gt

    @pl.when(jnp.max(jnp.where(cnt_eq > need, 1.0, 0.0)) > 0.0)
    def _():
        def idx_body(it, j):
            cand = j + jnp.left_shift(jnp.int32(1), idx_bits - 1 - it)
            c16 = cand.astype(I16)
            cnt = count(lambda hi, lo, lom, idx: is_eq(hi, lo) & (idx < c16))
            return jnp.where(cnt < need, cand, j)

        j_cut16 = lax.fori_loop(0, idx_bits, idx_body, jnp.zeros((1, tq), I32)).astype(I16)

        def rewrite_body(kt, carry):
            sl = pl.ds(pl.multiple_of(kt * tk, tk), tk)
            hi, lo = hi_ref[sl, :], lo_ref[sl, :]
            write_bias(sl, hi, is_eq(hi, lo) & (kidx_ref[sl, :] <= j_cut16), is_gt(hi, lo))
            return carry

        lax.fori_loop(0, nkt, rewrite_body, 0)

    acc_ref[...] = jnp.zeros_like(acc_ref)

    def att_body(kt, carry):
        sl = pl.ds(pl.multiple_of(kt * tk, tk), tk)
        bias = bias_ref[sl, :]
        ones = jnp.ones((16, tk), BF16)
        for h in range(HEADS):
            hs = slice(h * DH, (h + 1) * DH)
            s = s_ref[h, sl, :] + bias
            m_old = m_ref[h]
            m_new = jnp.maximum(m_old, jnp.max(s, axis=0, keepdims=True))
            alpha = jnp.exp2(m_old - m_new)
            p = jnp.exp2(s - m_new).astype(BF16)
            pv = lax.dot_general(jnp.concatenate([vt_ref[kt, hs, :], ones], axis=0), p, NN,
                                 preferred_element_type=F32)
            acc_ref[hs, :] = alpha * acc_ref[hs, :] + pv[0:DH, :]
            m_ref[h] = m_new
            l_ref[h] = alpha * l_ref[h] + pv[DH:DH + 1, :]
        return carry

    m_ref[...] = jnp.full_like(m_ref, NEG_INF)
    l_ref[...] = jnp.zeros_like(l_ref)
    lax.fori_loop(0, nkt, att_body, 0)
    o_t = jnp.concatenate([acc_ref[h * DH:(h + 1) * DH, :] / l_ref[h] for h in range(HEADS)], axis=0)
    o_ref[...] = o_t.T.astype(o_ref.dtype)


def _dsa(iqt, iwt, qt, ik, kr, vt, b, s, n_sel):
    tq = min(DSA_TQ, s)
    tk = min(DSA_TK, s)
    nq = s // tq
    nkt = s // tk
    return pl.pallas_call(
        functools.partial(_dsa_kernel, tq=tq, tk=tk, n_sel=n_sel, idx_bits=int(math.log2(s))),
        grid=(b, nq),
        in_specs=[pl.BlockSpec((IDX_HEADS * DH, tq), lambda i, j: (0, i * nq + j)),
                  pl.BlockSpec((8, tq), lambda i, j: (0, i * nq + j)),
                  pl.BlockSpec((HW, tq), lambda i, j: (0, i * nq + j)),
                  pl.BlockSpec((s, DH), lambda i, j: (i, 0)),
                  pl.BlockSpec((HEADS, s, DH), lambda i, j: (0, i, 0)),
                  pl.BlockSpec((nkt, HW, tk), lambda i, j: (i, 0, 0)),
                  pl.BlockSpec((s, tq), lambda i, j: (0, 0))],
        out_specs=pl.BlockSpec((tq, HW), lambda i, j: (i * nq + j, 0)),
        out_shape=jax.ShapeDtypeStruct((b * s, HW), BF16),
        scratch_shapes=[pltpu.VMEM((s, tq), I16), pltpu.VMEM((s, tq), I16), pltpu.VMEM((s, tq), I16),
                        pltpu.VMEM((s, tq), F32), pltpu.VMEM((HW, tq), F32),
                        pltpu.VMEM((HEADS, s, tq), F32), pltpu.VMEM((HEADS, 1, tq), F32),
                        pltpu.VMEM((HEADS, 1, tq), F32)],
        compiler_params=_cparams(2),
        name="dsa",
    )(iqt, iwt, qt, ik, kr, vt,
      jnp.broadcast_to(jnp.arange(s, dtype=I16)[:, None], (s, tq)))


def _merge_kernel(x_ref, oa_ref, ob_ref, oc_ref, od_ref, ga_ref, gb_ref, gc_ref, gd_ref,
                  wa_ref, wb_ref, wc_ref, wd_ref, wo_ref, o_ref):
    def branch(o_r, g_r, w_r):
        y = jnp.dot(o_r[...].astype(BF16), w_r[...], preferred_element_type=F32)
        return _sigmoid(g_r[...].astype(F32)) * y

    merged = (branch(oa_ref, ga_ref, wa_ref) + branch(ob_ref, gb_ref, wb_ref)
              + branch(oc_ref, gc_ref, wc_ref) + branch(od_ref, gd_ref, wd_ref))
    o_ref[...] = x_ref[...] + jnp.dot(merged.astype(BF16), wo_ref[...], preferred_element_type=F32)


def _merge(x2d, oa, ob, oc, od, gates, wa, wb, wc, wd, wo):
    t, d = x2d.shape
    tm = min(512, t)
    row = lambda w: pl.BlockSpec((tm, w), lambda i: (i, 0))
    gate = lambda n: pl.BlockSpec((tm, d), lambda i, o=n: (i, o))
    full = lambda shape: pl.BlockSpec(shape, lambda i: (0,) * len(shape))
    return pl.pallas_call(
        _merge_kernel,
        grid=(t // tm,),
        in_specs=[row(d), row(HW), row(HW), row(HW), row(HW), gate(0), gate(1), gate(2), gate(3),
                  full((HW, d)), full((HW, d)), full((HW, d)), full((HW, d)), full((d, d))],
        out_specs=row(d),
        out_shape=jax.ShapeDtypeStruct((t, d), F32),
        compiler_params=_cparams(1),
        name="merge",
    )(x2d, oa, ob, oc, od, gates, gates, gates, gates, wa, wb, wc, wd, wo)


def _mlp_kernel(x_ref, g_ref, wu_ref, wd_ref, fg_ref, o_ref, h_ref, acc_ref, *, final_norm):
    fi = pl.program_id(1)

    @pl.when(fi == 0)
    def _():
        x = x_ref[...]
        ms = jnp.mean(x * x, axis=-1, keepdims=True)
        h_ref[...] = (x * lax.rsqrt(ms + NORM_EPS) * g_ref[...]).astype(BF16)
        acc_ref[...] = jnp.zeros_like(acc_ref)

    u = jnp.dot(h_ref[...], wu_ref[...], preferred_element_type=F32)
    u = jnp.square(jnp.maximum(u, 0.0))
    acc_ref[...] += jnp.dot(u.astype(BF16), wd_ref[...], preferred_element_type=F32)

    @pl.when(fi == pl.num_programs(1) - 1)
    def _():
        y = x_ref[...] + acc_ref[...]
        if final_norm:
            ms = jnp.mean(y * y, axis=-1, keepdims=True)
            y = y * lax.rsqrt(ms + NORM_EPS) * fg_ref[...]
        o_ref[...] = y


def _mlp(x2d, g, wu, wd, fg, final_norm):
    t, d = x2d.shape
    f = wu.shape[1]
    tm = min(1024, t)
    tf = 512
    return pl.pallas_call(
        functools.partial(_mlp_kernel, final_norm=final_norm),
        grid=(t // tm, f // tf),
        in_specs=[pl.BlockSpec((tm, d), lambda i, j: (i, 0)),
                  pl.BlockSpec((1, d), lambda i, j: (0, 0)),
                  pl.BlockSpec((d, tf), lambda i, j: (0, j)),
                  pl.BlockSpec((tf, d), lambda i, j: (j, 0)),
                  pl.BlockSpec((1, d), lambda i, j: (0, 0))],
        out_specs=pl.BlockSpec((tm, d), lambda i, j: (i, 0)),
        out_shape=jax.ShapeDtypeStruct((t, d), F32),
        scratch_shapes=[pltpu.VMEM((tm, d), BF16), pltpu.VMEM((tm, d), F32)],
        compiler_params=_cparams(2),
        name="mlp",
    )(x2d, g, wu, wd, fg)


def _slab_weight(w_in, dtype=BF16):
    d = w_in.shape[0]
    z = lambda n: jnp.zeros((d, n), w_in.dtype)
    gdn, mla, rwkv, dsa = 0, 1032, 1448, 2472
    gate = 3824
    cols = [
        w_in[:, rwkv:rwkv + 1024],
        w_in[:, gdn:gdn + 768],
        w_in[:, gdn + 776:gdn + 1032],
        w_in[:, mla:mla + 256],
        w_in[:, mla + 256:mla + 384],
        z(64), w_in[:, mla + 384:mla + 416], z(32),
        w_in[:, dsa:dsa + 512],
        w_in[:, dsa + 584:dsa + 1352],
        w_in[:, gdn + 768:gdn + 776], z(120),
        w_in[:, dsa + 512:dsa + 584], z(56),
        w_in[:, gate:gate + 4096],
    ]
    return jnp.concatenate(cols, axis=1).astype(dtype)


def _mla_weights(w_uq, w_ukv):
    wq = w_uq.reshape(MLA_Q_LORA, HEADS, MLA_NOPE + MLA_ROPE)
    wq = jnp.pad(wq, ((0, 0), (0, 0), (0, MLA_QK - MLA_NOPE - MLA_ROPE))).reshape(MLA_Q_LORA, HEADS * MLA_QK)
    wkv = w_ukv.reshape(MLA_KV_LORA, HEADS, MLA_NOPE + DH)
    wk = jnp.pad(wkv[:, :, :MLA_NOPE], ((0, 0), (0, 0), (0, MLA_QK - MLA_NOPE))).reshape(MLA_KV_LORA, HEADS * MLA_QK)
    wvt = wkv[:, :, MLA_NOPE:].reshape(MLA_KV_LORA, HW).T
    return wq.astype(BF16), wk.astype(BF16), wvt.astype(BF16)


def _rope_cos_sin(positions, rot_dims):
    inv_freq = ROPE_THETA ** (-jnp.arange(0, rot_dims, 2, dtype=F32) / rot_dims)
    ang = positions.astype(F32)[..., None] * inv_freq
    return jnp.concatenate([jnp.cos(ang), jnp.sin(ang)], axis=-1)


def _rope_expansion(rot_dims, offset, period):
    half = rot_dims // 2
    e = np.zeros((rot_dims, 3 * 128), np.float32)
    base = np.zeros((1, 3 * 128), np.float32)
    base[0, 0:128] = 1.0
    for rep in range(128 // period):
        lo = rep * period + offset
        for j in range(half):
            e[j, lo + j] = 1.0
            e[j, lo + half + j] = 1.0
            base[0, lo + j] = 0.0
            base[0, lo + half + j] = 0.0
            e[half + j, 128 + lo + j] = -1.0
            e[half + j, 256 + lo + half + j] = 1.0
    return jnp.asarray(e), jnp.asarray(base)


def _rope_table(cs_ref, e_ref, base_ref):
    tab = _mm_exact_rhs(cs_ref[...], e_ref[...]) + base_ref[...]
    return tab[:, 0:128], tab[:, 128:256], tab[:, 256:384]


def _layer(x2d, b, s, prm, tabs, final_g, is_last):
    (mix_g, w_slab, conv_w, gdn_pcol, gdn_prow, gdn_ng, mla_qg, mla_kvg, wq, wk, wv,
     rwkv_mu, rwkv_vec, ww2, wa2, wg2, wo_gdn, wo_mla, wo_rwkv, wo_dsa, w_o, mlp_g, w_up, w_down) = prm
    mla_cs, dsa_cs = tabs
    t = b * s
    slab = _in_proj(x2d, mix_g, w_slab[:, :OFF_GATE], F32)
    gates = _in_proj(x2d, mix_g, w_slab[:, OFF_GATE:], BF16)

    ab = slab[:, OFF_GDN_AB:OFF_GDN_AB + 8].reshape(b, s // CHUNK, CHUNK, 8)
    abr = jnp.swapaxes(ab, 2, 3)
    o_gdn = _gdn(slab, abr, conv_w, gdn_pcol, gdn_prow, gdn_ng, b, s)
    q, k, v = _mla_prep(slab, mla_cs, mla_qg, mla_kvg, wq, wk, wv)
    o_mla = _flash(q, k, v, b, s)
    o_rwkv = _rwkv(slab, rwkv_mu, rwkv_vec, ww2, wa2, wg2, b, s)
    ik, kr, vt, iqt, qt, iwt = _dsa_prep(slab, dsa_cs)
    o_dsa = _dsa(iqt, iwt, qt, ik, kr, vt, b, s, min(IDX_TOPK_MAX, s // 4))

    x2d = _merge(x2d, o_gdn, o_mla, o_rwkv, o_dsa, gates, wo_gdn, wo_mla, wo_rwkv, wo_dsa, w_o)
    return _mlp(x2d, mlp_g, w_up, w_down, final_g, is_last)


def kernel(x, positions, mix_norm_g, w_in, gdn_conv_w, gdn_a_log, gdn_dt_bias, gdn_norm_g, mla_q_norm_g, mla_kv_norm_g, mla_w_uq, mla_w_ukv, rwkv_mu, rwkv_w0, rwkv_w_w2, rwkv_a0, rwkv_w_a2, rwkv_w_g2, rwkv_k_k, rwkv_k_a, rwkv_r_k, rwkv_ln_w, rwkv_ln_b, w_out_gdn, w_out_mla, w_out_rwkv, w_out_dsa, w_o, mlp_norm_g, w_up, w_down, final_norm_g):
    b, s, d = x.shape
    depth = w_in.shape[0]
    tabs = (_rope_cos_sin(positions, MLA_ROPE).reshape(b * s, MLA_ROPE),
            _rope_cos_sin(positions, DH // 4).reshape(b * s, DH // 4))
    row = lambda a: a.reshape(1, -1).astype(F32)
    pad8 = lambda rows: jnp.concatenate(rows + [jnp.zeros((8 - len(rows), rows[0].shape[1]), F32)], axis=0)
    x2d = x.reshape(b * s, d)
    final_g = row(final_norm_g)
    for l in range(depth):
        lane4 = lambda a: jnp.pad(a.astype(F32), (0, 124)).reshape(1, 128)
        gdn_pcol = pad8([lane4(gdn_a_log[l]), lane4(gdn_dt_bias[l])])
        gdn_prow = jnp.pad(jnp.stack([gdn_a_log[l], gdn_dt_bias[l]], axis=1).astype(F32), ((0, 4), (0, 126)))
        wq, wk, wv = _mla_weights(mla_w_uq[l], mla_w_ukv[l])
        rwkv_vec = pad8([row(rwkv_w0[l]), row(rwkv_a0[l]), row(rwkv_k_k[l]), row(rwkv_k_a[l]),
                         row(rwkv_ln_w[l]), row(rwkv_ln_b[l]), row(rwkv_r_k[l])])
        prm = (row(mix_norm_g[l]), _slab_weight(w_in[l]), gdn_conv_w[l].astype(F32), gdn_pcol, gdn_prow,
               row(jnp.tile(gdn_norm_g[l], HEADS)), row(mla_q_norm_g[l]), row(mla_kv_norm_g[l]), wq, wk, wv,
               row(rwkv_mu[l]), rwkv_vec, rwkv_w_w2[l].astype(F32), rwkv_w_a2[l].astype(F32),
               rwkv_w_g2[l].astype(F32),
               w_out_gdn[l].astype(BF16), w_out_mla[l].astype(BF16), w_out_rwkv[l].astype(BF16),
               w_out_dsa[l].astype(BF16), w_o[l].astype(BF16), row(mlp_norm_g[l]),
               w_up[l].astype(BF16), w_down[l].astype(BF16))
        x2d = _layer(x2d, b, s, prm, tabs, final_g, l == depth - 1)
    return x2d.reshape(b, s, d)
```

```python
import functools
import math

import numpy as np
import jax
import jax.numpy as jnp
from jax import lax
from jax.experimental import pallas as pl
from jax.experimental.pallas import tpu as pltpu

F32 = jnp.float32
BF16 = jnp.bfloat16
I32 = jnp.int32
I16 = jnp.int16

D_MODEL = 1024
DEPTH = 4
ROPE_THETA = 500000.0
NORM_EPS = 1e-6
NEG_INF = -1e30
HEADS = 4
DH = 64
HW = HEADS * DH
GDN_CONV = 4
CHUNK = 64
MLA_Q_LORA = 256
MLA_KV_LORA = 128
MLA_NOPE = 64
MLA_ROPE = 32
MLA_QK = 128
RWKV_GN_EPS = 64e-5
IDX_HEADS = 8
IDX_TOPK_MAX = 256
D_FF = 4 * D_MODEL
INT_MIN = -2 ** 31
LOG2E = 1.4426950408889634

VMEM_LIMIT_BYTES = 56 * 1024 * 1024

OFF_RWKV = 0
OFF_GDN_Q = 1024
OFF_GDN_K = 1280
OFF_GDN_V = 1536
OFF_GDN_Z = 1792
OFF_MLA_CQ = 2048
OFF_MLA_CKV = 2304
OFF_MLA_KR = 2432
OFF_DSA_IQ = 2560
OFF_DSA_Q = 3072
OFF_DSA_K = 3328
OFF_DSA_V = 3584
OFF_GDN_AB = 3840
OFF_DSA_IKW = 3968
OFF_GATE = 4096
N_SLAB = 8192

NN = (((1,), (0,)), ((), ()))
NT = (((1,), (1,)), ((), ()))
TN = (((0,), (0,)), ((), ()))
BNN = (((2,), (1,)), ((0,), (0,)))
BNT = (((2,), (2,)), ((0,), (0,)))
BTN = (((1,), (1,)), ((0,), (0,)))


def _cparams(n_axes):
    return pltpu.CompilerParams(dimension_semantics=("arbitrary",) * n_axes,
                                vmem_limit_bytes=VMEM_LIMIT_BYTES)


def _mm(a, b, dims=NN):
    return lax.dot_general(a.astype(BF16), b.astype(BF16), dims, preferred_element_type=F32)


def _mm_exact_lhs(m, x, dims=NN):
    mb = m.astype(BF16)
    x0 = x.astype(BF16)
    r1 = x - x0.astype(F32)
    x1 = r1.astype(BF16)
    x2 = (r1 - x1.astype(F32)).astype(BF16)
    d = lambda y: lax.dot_general(mb, y, dims, preferred_element_type=F32)
    return d(x0) + (d(x1) + d(x2))


def _mm_exact_rhs(x, m, dims=NN):
    mb = m.astype(BF16)
    x0 = x.astype(BF16)
    r1 = x - x0.astype(F32)
    x1 = r1.astype(BF16)
    x2 = (r1 - x1.astype(F32)).astype(BF16)
    d = lambda y: lax.dot_general(y, mb, dims, preferred_element_type=F32)
    return d(x0) + (d(x1) + d(x2))


def _iota2(shape, dim):
    return lax.broadcasted_iota(I32, shape, dim)


def _seg_ones(width, seg):
    sh = int(math.log2(seg))
    r = _iota2((width, width), 0) >> sh
    c = _iota2((width, width), 1) >> sh
    return jnp.where(r == c, 1.0, 0.0).astype(F32)


def _seg_sum(x, seg_ones):
    return _mm_exact_rhs(x, seg_ones)


def _softplus(x):
    return jnp.maximum(x, 0.0) + jnp.log1p(jnp.exp(-jnp.abs(x)))


def _sigmoid(x):
    return 0.5 * jnp.tanh(0.5 * x) + 0.5


def _silu(x):
    return x * _sigmoid(x)


def _tri_masks(n):
    r = _iota2((n, n), 0)
    c = _iota2((n, n), 1)
    return r, c


def _unit_lower_inverse(m, r, c, dims=NN):
    eye = jnp.where(r == c, 1.0, 0.0).astype(F32)
    same16 = (r >> 4) == (c >> 4)
    same32 = (r >> 5) == (c >> 5)
    md = jnp.where(same16, m, 0.0)
    t = eye - md
    p = md
    for _ in range(3):
        p = _mm(p, p, dims)
        t = t + _mm(t, p, dims)
    c32 = jnp.where(same32 & jnp.logical_not(same16), m, 0.0)
    t = t - _mm(_mm(t, c32, dims), t, dims)
    c64 = jnp.where(same32, 0.0, m)
    t = t - _mm(_mm(t, c64, dims), t, dims)
    return t


IN_PROJ_TN = 512


def _in_proj_kernel(x_ref, g_ref, w_ref, o_ref):
    x = x_ref[...]
    ms = jnp.mean(x * x, axis=-1, keepdims=True)
    h = (x * lax.rsqrt(ms + NORM_EPS) * g_ref[...]).astype(BF16)
    for n0 in range(0, o_ref.shape[1], IN_PROJ_TN):
        o_ref[:, n0:n0 + IN_PROJ_TN] = jnp.dot(
            h, w_ref[:, n0:n0 + IN_PROJ_TN], preferred_element_type=F32).astype(o_ref.dtype)


def _in_proj(x2d, g, w, out_dtype):
    t, d = x2d.shape
    n = w.shape[1]
    tm = min(512, t)
    return pl.pallas_call(
        _in_proj_kernel,
        grid=(t // tm,),
        in_specs=[pl.BlockSpec((tm, d), lambda i: (i, 0)),
                  pl.BlockSpec((1, d), lambda i: (0, 0)),
                  pl.BlockSpec((d, n), lambda i: (0, 0))],
        out_specs=pl.BlockSpec((tm, n), lambda i: (i, 0)),
        out_shape=jax.ShapeDtypeStruct((t, n), out_dtype),
        compiler_params=_cparams(1),
        name="in_proj",
    )(x2d, g, w)


def _to_groups(x, nchunk):
    parts = [x[:, h * DH:(h + 1) * DH].reshape(nchunk, 1, CHUNK, DH) for h in range(HEADS)]
    return jnp.concatenate(parts, axis=1).reshape(nchunk * HEADS, CHUNK, DH)


def _col_groups(x, lane0, nchunk):
    parts = [x[:, lane0 + h:lane0 + h + 1].reshape(nchunk, 1, CHUNK, 1) for h in range(HEADS)]
    return jnp.concatenate(parts, axis=1).reshape(nchunk * HEADS, CHUNK, 1)


def _gdn_kernel(q_ref, k_ref, v_ref, z_ref, ab_ref, abr_ref, cw_ref, pcol_ref, prow_ref, ng_ref,
                o_ref, xbuf_ref, state_ref, u_ref, w_ref, a_ref, qd_ref, kd_ref, egl_ref, oraw_ref,
                carry_ref, *, tb, nbatch):
    nchunk = tb // CHUNK
    bi = pl.program_id(1)
    xbuf = xbuf_ref

    @pl.when(pl.program_id(0) == 0)
    def _():
        carry_ref[bi] = jnp.zeros((8, 3 * HW), F32)
        state_ref[bi] = jnp.zeros((HEADS, DH, DH), F32)

    xbuf[0:8, :] = carry_ref[bi]
    xbuf[8:8 + tb, 0:HW] = q_ref[...]
    xbuf[8:8 + tb, HW:2 * HW] = k_ref[...]
    xbuf[8:8 + tb, 2 * HW:3 * HW] = v_ref[...]
    cw = cw_ref[...]
    y = xbuf[8:8 + tb, :] * cw[3:4, :]
    y = y + xbuf[7:7 + tb, :] * cw[2:3, :]
    y = y + xbuf[6:6 + tb, :] * cw[1:2, :]
    y = y + xbuf[5:5 + tb, :] * cw[0:1, :]
    carry_ref[bi] = xbuf[tb:tb + 8, :]
    y = _silu(y)
    seg = _seg_ones(HW, DH)
    q = y[:, 0:HW]
    k = y[:, HW:2 * HW]
    q = q * lax.rsqrt(_seg_sum(q * q, seg) + 1e-6) * (DH ** -0.5)
    k = k * lax.rsqrt(_seg_sum(k * k, seg) + 1e-6)
    v = y[:, 2 * HW:3 * HW]

    ab = ab_ref[...]
    pcol = pcol_ref[...]
    g_col = -jnp.exp(pcol[0:1, :]) * _softplus(ab + pcol[1:2, :])
    beta_col = _sigmoid(ab)

    r, c = _tri_masks(CHUNK)
    incl = c <= r
    strict = c < r
    l_incl = jnp.where(incl, 1.0, 0.0).astype(F32)

    rb = _iota2((tb, tb), 0)
    cb = _iota2((tb, tb), 1)
    l_blk = jnp.where(((rb >> 6) == (cb >> 6)) & (cb <= rb), 1.0, 0.0).astype(F32)
    gc_col = _mm_exact_lhs(l_blk, g_col)
    prow = prow_ref[...]
    abr = abr_ref[0].reshape(nchunk * 8, CHUNK)
    prow_t = jnp.concatenate([prow] * nchunk, axis=0)
    g_row = -jnp.exp(prow_t[:, 0:1]) * _softplus(abr + prow_t[:, 1:2])
    gc_row = _mm_exact_rhs(g_row, l_incl, NT).reshape(nchunk, 8, CHUNK)

    qg, kg, vg = _to_groups(q, nchunk), _to_groups(k, nchunk), _to_groups(v, nchunk)
    beta = _col_groups(beta_col, 4, nchunk)
    gcol = _col_groups(gc_col, 0, nchunk)
    grow = jnp.concatenate([gc_row[:, h:h + 1, :].reshape(nchunk, 1, 1, CHUNK) for h in range(HEADS)],
                           axis=1).reshape(nchunk * HEADS, 1, CHUNK)
    decay = jnp.where(incl, jnp.exp(jnp.where(incl, gcol - grow, 0.0)), 0.0)
    kb = kg * beta
    m = jnp.where(strict, _mm(kb, kg, BNT) * decay, 0.0)
    t_inv = _unit_lower_inverse(m, r, c, BNN)
    egc = jnp.exp(gcol)
    glast = gcol[:, CHUNK - 1:CHUNK, :]
    uw = _mm(t_inv, jnp.concatenate([vg * beta, kb * egc], axis=2), BNN)
    u_ref[bi] = uw[:, :, 0:DH]
    w_ref[bi] = uw[:, :, DH:2 * DH].astype(BF16)
    a_ref[bi] = (_mm(qg, kg, BNT) * decay).astype(BF16)
    qd_ref[bi] = (qg * egc).astype(BF16)
    kd_ref[bi] = (kg * jnp.exp(glast - gcol)).astype(BF16)
    egl_ref[bi] = jnp.broadcast_to(jnp.exp(glast), (nchunk * HEADS, 1, DH))

    @pl.when(bi == nbatch - 1)
    def _():
        nc = nbatch * HEADS

        def chunk_body(ci, carry):
            sl = pl.ds(pl.multiple_of(ci * HEADS, HEADS), HEADS)
            take = lambda ref: ref[:, sl].reshape((nc,) + ref.shape[2:])
            s = state_ref[...].reshape(nc, DH, DH)
            v_new = take(u_ref) - _mm(take(w_ref), s, BNN)
            o = _mm(take(qd_ref), s, BNN) + _mm(take(a_ref), v_new, BNN)
            s = s * take(egl_ref) + _mm(take(kd_ref), v_new, BTN)
            state_ref[...] = s.reshape(nbatch, HEADS, DH, DH)
            rows = pl.ds(pl.multiple_of(ci * CHUNK, CHUNK), CHUNK)
            for bb in range(nbatch):
                oraw_ref[bb, rows, :] = jnp.concatenate([o[bb * HEADS + h] for h in range(HEADS)], axis=1)
            return carry

        lax.fori_loop(0, nchunk, chunk_body, 0)
        o_all = oraw_ref[...].reshape(nbatch * tb, HW)
        ms = _seg_sum(o_all * o_all, seg) * (1.0 / DH)
        o_n = o_all * lax.rsqrt(ms + NORM_EPS) * ng_ref[...] * _silu(z_ref[...].reshape(nbatch * tb, HW))
        o_ref[...] = o_n.reshape(nbatch, tb, HW).astype(o_ref.dtype)


GDN_TB = 512


def _gdn(slab, abr, conv_w, pcol, prow, norm_g, b, s):
    tb = min(GDN_TB, s)
    nb = s // tb
    nchunk = tb // CHUNK
    ng = nchunk * HEADS
    grp = lambda w, dt=BF16: pltpu.VMEM((b, ng, CHUNK, w), dt)
    col = lambda off, w: pl.BlockSpec((tb, w), lambda j, i, o=off // w: (i * nb + j, o))
    full = lambda shape: pl.BlockSpec(shape, lambda j, i: (0,) * len(shape))
    slab3 = slab.reshape(b, s, slab.shape[1])
    out = pl.pallas_call(
        functools.partial(_gdn_kernel, tb=tb, nbatch=b),
        grid=(nb, b),
        in_specs=[col(OFF_GDN_Q, HW), col(OFF_GDN_K, HW), col(OFF_GDN_V, HW),
                  pl.BlockSpec((b, tb, HW), lambda j, i: (0, j, OFF_GDN_Z // HW)),
                  col(OFF_GDN_AB, 128),
                  pl.BlockSpec((1, nchunk, 8, CHUNK), lambda j, i: (i, j, 0, 0)),
                  full((GDN_CONV, 3 * HW)), full((8, 128)), full((8, 128)), full((1, HW))],
        out_specs=pl.BlockSpec((b, tb, HW), lambda j, i: (0, j, 0)),
        out_shape=jax.ShapeDtypeStruct((b, s, HW), BF16),
        scratch_shapes=[pltpu.VMEM((tb + 8, 3 * HW), F32), pltpu.VMEM((b, HEADS, DH, DH), F32),
                        grp(DH, F32), grp(DH), grp(CHUNK), grp(DH), grp(DH),
                        pltpu.VMEM((b, ng, 1, DH), F32), pltpu.VMEM((b, tb, HW), F32),
                        pltpu.VMEM((b, 8, 3 * HW), F32)],
        compiler_params=_cparams(2),
        name="gdn",
    )(slab, slab, slab, slab3, slab, abr, conv_w, pcol, prow, norm_g)
    return out.reshape(b * s, HW)


def _rwkv_kernel(p_ref, mu_ref, vec_ref, ww2_ref, wa2_ref, wg2_ref, o_ref,
                 xbuf_ref, state_ref, ta_ref, tpv_ref, rt_ref, qkv_ref, qb_ref, bc_ref, ktv_ref,
                 el_ref, oraw_ref, carry_ref, post_ref, *, tb, nbatch):
    nchunk = tb // CHUNK
    bi = pl.program_id(1)

    @pl.when(pl.program_id(0) == 0)
    def _():
        carry_ref[bi] = jnp.zeros((8, 4 * HW), F32)
        state_ref[bi] = jnp.zeros((HEADS, DH, DH), F32)

    p = p_ref[...]
    xbuf_ref[0:8, :] = carry_ref[bi]
    xbuf_ref[8:8 + tb, :] = p
    prev = xbuf_ref[7:7 + tb, :]
    carry_ref[bi] = xbuf_ref[tb:tb + 8, :]
    pm = p + (prev - p) * mu_ref[...]
    vec = vec_ref[...]
    w0, a0, k_k, k_a = vec[0:1, :], vec[1:2, :], vec[2:3, :], vec[3:4, :]
    rr = pm[:, 0:HW]
    k = pm[:, HW:2 * HW]
    v = pm[:, 2 * HW:3 * HW]
    xw = pm[:, 768:832]
    xa = pm[:, 832:896]
    xg = pm[:, 896:1024]
    w_log = -_softplus(-(w0 + _mm(jnp.tanh(xw), ww2_ref[...]))) - 0.5
    lw = -jnp.exp(w_log)
    a = _sigmoid(a0 + _mm(xa, wa2_ref[...]))
    g = _mm(_sigmoid(xg), wg2_ref[...])
    seg = _seg_ones(HW, DH)
    kk = k * k_k
    kk = kk * lax.rsqrt(_seg_sum(kk * kk, seg) + 1e-6)
    k2 = k * (1.0 + (a - 1.0) * k_a)
    kka = kk * a

    r_i, c_i = _tri_masks(CHUNK)
    incl = c_i <= r_i
    strict = c_i < r_i
    rb = _iota2((tb, tb), 0)
    cb = _iota2((tb, tb), 1)
    l_blk = jnp.where(((rb >> 6) == (cb >> 6)) & (cb <= rb), 1.0, 0.0).astype(F32)
    lwc_all = _mm_exact_lhs(l_blk, lw)

    rg, kg, vg = _to_groups(rr, nchunk), _to_groups(k2, nchunk), _to_groups(v, nchunk)
    kkg, kkag = _to_groups(kk, nchunk), _to_groups(kka, nchunk)
    lwg, lwc = _to_groups(lw, nchunk), _to_groups(lwc_all, nchunk)
    lwl = lwc[:, CHUNK - 1:CHUNK, :]
    e_neg = jnp.exp(-lwc)
    at = kkg * jnp.exp(lwc - lwg)
    rt = rg * jnp.exp(lwc)
    y2 = jnp.concatenate([at, rt], axis=1)
    x2 = jnp.concatenate([kkag * e_neg, kg * e_neg], axis=1)
    gram = _mm(y2, x2, BNT)
    n_m = jnp.where(strict, gram[:, 0:CHUNK, 0:CHUNK], 0.0)
    p_m = jnp.where(strict, gram[:, 0:CHUNK, CHUNK:], 0.0)
    qb_m = jnp.where(incl, gram[:, CHUNK:, 0:CHUNK], 0.0)
    qk_m = jnp.where(incl, gram[:, CHUNK:, CHUNK:], 0.0)
    t_inv = _unit_lower_inverse(n_m, r_i, c_i, BNN)
    ta_ref[bi] = _mm(t_inv, at, BNN).astype(BF16)
    tpv_ref[bi] = _mm(t_inv, _mm(p_m, vg, BNN), BNN)
    rt_ref[bi] = rt.astype(BF16)
    qkv_ref[bi] = _mm(qk_m, vg, BNN)
    qb_ref[bi] = qb_m.astype(BF16)
    e_rel = jnp.exp(lwl - lwc)
    bc_ref[bi] = (kkag * e_rel).astype(BF16)
    ktv_ref[bi] = _mm(vg, kg * e_rel, BTN)
    el_ref[bi] = jnp.exp(lwl)
    ln_w, ln_b, r_k = vec[4:5, :], vec[5:6, :], vec[6:7, :]
    post_ref[bi, :, 0:HW] = _seg_sum(rr * k2 * r_k, seg) * v
    post_ref[bi, :, HW:2 * HW] = g

    @pl.when(bi == nbatch - 1)
    def _():
        nc = nbatch * HEADS

        def chunk_body(ci, carry):
            sl = pl.ds(pl.multiple_of(ci * HEADS, HEADS), HEADS)
            take = lambda ref: ref[:, sl].reshape((nc,) + ref.shape[2:])
            zt = state_ref[...].reshape(nc, DH, DH)
            u = _mm(take(ta_ref), zt, BNT) + take(tpv_ref)
            o = _mm(take(rt_ref), zt, BNT) + take(qkv_ref) - _mm(take(qb_ref), u, BNN)
            zt = zt * take(el_ref) + take(ktv_ref) - _mm(u, take(bc_ref), BTN)
            state_ref[...] = zt.reshape(nbatch, HEADS, DH, DH)
            rows = pl.ds(pl.multiple_of(ci * CHUNK, CHUNK), CHUNK)
            for bb in range(nbatch):
                oraw_ref[bb, rows, :] = jnp.concatenate([o[bb * HEADS + h] for h in range(HEADS)], axis=1)
            return carry

        lax.fori_loop(0, nchunk, chunk_body, 0)
        o_all = oraw_ref[...].reshape(nbatch * tb, HW)
        mean = _seg_sum(o_all, seg) * (1.0 / DH)
        cen = o_all - mean
        var = _seg_sum(cen * cen, seg) * (1.0 / DH)
        o_n = cen * lax.rsqrt(var + RWKV_GN_EPS) * ln_w + ln_b
        post = post_ref[...].reshape(nbatch * tb, 2 * HW)
        o_ref[...] = ((o_n + post[:, 0:HW]) * post[:, HW:2 * HW]).reshape(nbatch, tb, HW).astype(o_ref.dtype)


RWKV_TB = 512


def _rwkv(slab, mu, vec, ww2, wa2, wg2, b, s):
    tb = min(RWKV_TB, s)
    nb = s // tb
    ng = (tb // CHUNK) * HEADS
    grp = lambda dt=F32: pltpu.VMEM((b, ng, CHUNK, DH), dt)
    full = lambda shape: pl.BlockSpec(shape, lambda j, i: (0,) * len(shape))
    out = pl.pallas_call(
        functools.partial(_rwkv_kernel, tb=tb, nbatch=b),
        grid=(nb, b),
        in_specs=[pl.BlockSpec((tb, 4 * HW), lambda j, i: (i * nb + j, OFF_RWKV // (4 * HW))),
                  full((1, 4 * HW)), full((8, HW)), full((64, HW)), full((64, HW)), full((128, HW))],
        out_specs=pl.BlockSpec((b, tb, HW), lambda j, i: (0, j, 0)),
        out_shape=jax.ShapeDtypeStruct((b, s, HW), BF16),
        scratch_shapes=[pltpu.VMEM((tb + 8, 4 * HW), F32), pltpu.VMEM((b, HEADS, DH, DH), F32),
                        grp(BF16), grp(), grp(BF16), grp(), grp(BF16), grp(BF16), grp(),
                        pltpu.VMEM((b, ng, 1, DH), F32), pltpu.VMEM((b, tb, HW), F32),
                        pltpu.VMEM((b, 8, 4 * HW), F32), pltpu.VMEM((b, tb, 2 * HW), F32)],
        compiler_params=_cparams(2),
        name="rwkv7",
    )(slab, mu, vec, ww2, wa2, wg2)
    return out.reshape(b * s, HW)


def _rope(x, cos_t, sin_lo, sin_hi, half):
    w = x.shape[-1]
    return x * cos_t + pltpu.roll(x, w - half, 1) * sin_lo + pltpu.roll(x, half, 1) * sin_hi


def _tile_lanes(t, n):
    return t if n == 1 else jnp.concatenate([t] * n, axis=1)


def _mla_prep_kernel(cq_ref, ckv_ref, kr_ref, cs_ref, e_ref, base_ref, qg_ref, kvg_ref, wq_ref, wk_ref,
                     wvt_ref, q_ref, k_ref, vt_ref):
    cq = cq_ref[...]
    hq = cq * lax.rsqrt(jnp.mean(cq * cq, axis=-1, keepdims=True) + NORM_EPS) * qg_ref[...]
    ckv = ckv_ref[...]
    hkv = ckv * lax.rsqrt(jnp.mean(ckv * ckv, axis=-1, keepdims=True) + NORM_EPS) * kvg_ref[...]
    cos_t, sin_lo, sin_hi = _rope_table(cs_ref, e_ref, base_ref)
    q = _mm(hq, wq_ref[...])
    q = _rope(q, _tile_lanes(cos_t, HEADS), _tile_lanes(sin_lo, HEADS), _tile_lanes(sin_hi, HEADS),
              MLA_ROPE // 2)
    q_ref[...] = (q * ((MLA_NOPE + MLA_ROPE) ** -0.5 * LOG2E)).astype(BF16)
    kr = _rope(kr_ref[...], cos_t, sin_lo, sin_hi, MLA_ROPE // 2)
    k = _mm(hkv, wk_ref[...]) + _tile_lanes(kr, HEADS)
    k_ref[...] = k.astype(BF16)
    vt_ref[...] = lax.dot_general(wvt_ref[...], hkv.astype(BF16), NT,
                                  preferred_element_type=F32).astype(BF16)


def _mla_prep(slab, cs, qg, kvg, wq, wk, wvt):
    t = slab.shape[0]
    tm = min(512, t)
    full = lambda shape: pl.BlockSpec(shape, lambda i: (0,) * len(shape))
    e, base = _rope_expansion(MLA_ROPE, MLA_NOPE, MLA_QK)
    return pl.pallas_call(
        _mla_prep_kernel,
        grid=(t // tm,),
        in_specs=[pl.BlockSpec((tm, 256), lambda i: (i, OFF_MLA_CQ // 256)),
                  pl.BlockSpec((tm, 128), lambda i: (i, OFF_MLA_CKV // 128)),
                  pl.BlockSpec((tm, 128), lambda i: (i, OFF_MLA_KR // 128)),
                  pl.BlockSpec((tm, MLA_ROPE), lambda i: (i, 0)), full(e.shape), full(base.shape),
                  full((1, 256)), full((1, 128)), full((256, 512)), full((128, 512)), full((256, 128))],
        out_specs=[pl.BlockSpec((tm, 512), lambda i: (i, 0)), pl.BlockSpec((tm, 512), lambda i: (i, 0)),
                   pl.BlockSpec((256, tm), lambda i: (0, i))],
        out_shape=[jax.ShapeDtypeStruct((t, 512), BF16), jax.ShapeDtypeStruct((t, 512), BF16),
                   jax.ShapeDtypeStruct((256, t), BF16)],
        compiler_params=_cparams(1),
        name="mla_prep",
    )(slab, slab, slab, cs, e, base, qg, kvg, wq, wk, wvt)


def _flash_kernel(q_ref, k_ref, vt_ref, o_ref, m_ref, l_ref, acc_ref, *, tq, tk):
    qi = pl.program_id(1)
    ki = pl.program_id(2)

    @pl.when(ki == 0)
    def _():
        m_ref[...] = jnp.full_like(m_ref, NEG_INF)
        l_ref[...] = jnp.zeros_like(l_ref)
        acc_ref[...] = jnp.zeros_like(acc_ref)

    def step(on_diagonal):
        if on_diagonal:
            causal = _iota2((tk, tq), 0) <= _iota2((tk, tq), 1)
        vt = vt_ref[...]
        ones = jnp.ones((16, tk), BF16)
        scores = [lax.dot_general(k_ref[:, h * MLA_QK:(h + 1) * MLA_QK], q_ref[:, h * MLA_QK:(h + 1) * MLA_QK],
                                  NT, preferred_element_type=F32) for h in range(HEADS)]
        for h in range(HEADS):
            s = scores[h]
            if on_diagonal:
                s = jnp.where(causal, s, NEG_INF)
            m_old = m_ref[h]
            m_new = jnp.maximum(m_old, jnp.max(s, axis=0, keepdims=True))
            alpha = jnp.exp2(m_old - m_new)
            p = jnp.exp2(s - m_new).astype(BF16)
            pv = jnp.dot(jnp.concatenate([vt[h * DH:(h + 1) * DH, :], ones], axis=0), p,
                         preferred_element_type=F32)
            l_ref[h] = alpha * l_ref[h] + pv[DH:DH + 1, :]
            acc_ref[h * DH:(h + 1) * DH, :] = alpha * acc_ref[h * DH:(h + 1) * DH, :] + pv[0:DH, :]
            m_ref[h] = m_new

    pl.when(ki < qi)(lambda: step(False))
    pl.when(ki == qi)(lambda: step(True))

    @pl.when(ki == pl.num_programs(2) - 1)
    def _():
        o_t = jnp.concatenate([acc_ref[h * DH:(h + 1) * DH, :] / l_ref[h] for h in range(HEADS)], axis=0)
        o_ref[...] = o_t.T.astype(o_ref.dtype)


def _flash(q, k, vt, b, s):
    tq = min(512, s)
    tk = tq
    nq = s // tq
    nk = s // tk
    return pl.pallas_call(
        functools.partial(_flash_kernel, tq=tq, tk=tk),
        grid=(b, nq, nk),
        in_specs=[pl.BlockSpec((tq, 512), lambda i, j, kk: (i * nq + j, 0)),
                  pl.BlockSpec((tk, 512), lambda i, j, kk: (i * nk + jnp.minimum(kk, j), 0)),
                  pl.BlockSpec((256, tk), lambda i, j, kk: (0, i * nk + jnp.minimum(kk, j)))],
        out_specs=pl.BlockSpec((tq, 256), lambda i, j, kk: (i * nq + j, 0)),
        out_shape=jax.ShapeDtypeStruct((b * s, 256), BF16),
        scratch_shapes=[pltpu.VMEM((HEADS, 1, tq), F32), pltpu.VMEM((HEADS, 1, tq), F32),
                        pltpu.VMEM((HW, tq), F32)],
        compiler_params=_cparams(3),
        name="mla_flash",
    )(q, k, vt)


DSA_TQ = 256
DSA_TK = 512


def _dsa_prep_kernel(iq_ref, ikw_ref, q_ref, k_ref, v_ref, cs_ref, e_ref, base_ref,
                     ik_ref, ko_ref, vt_ref, iqt_ref, qt_ref, iwt_ref):
    cos_t, sin_lo, sin_hi = _rope_table(cs_ref, e_ref, base_ref)
    half = DH // 8
    rope = lambda x, n: _rope(x, _tile_lanes(cos_t, n), _tile_lanes(sin_lo, n), _tile_lanes(sin_hi, n), half)
    ikw = ikw_ref[...]
    ik_ref[...] = rope(ikw, 1)[:, 0:DH].astype(BF16)
    kr = rope(k_ref[...], 2)
    for h in range(HEADS):
        ko_ref[h] = kr[:, h * DH:(h + 1) * DH].astype(BF16)
    vt_ref[0] = v_ref[...].T.astype(BF16)
    iqt_ref[...] = rope(iq_ref[...], 4).T.astype(BF16)
    qt_ref[...] = (rope(q_ref[...], 2) * (DH ** -0.5 * LOG2E)).T.astype(BF16)
    iwt_ref[...] = ikw.T[DH:DH + 8, :] * (IDX_HEADS ** -0.5 * DH ** -0.5)


def _dsa_prep(slab, cs):
    t = slab.shape[0]
    tm = min(DSA_TK, t)
    full = lambda shape: pl.BlockSpec(shape, lambda i: (0,) * len(shape))
    e, base = _rope_expansion(DH // 4, 0, DH)
    return pl.pallas_call(
        _dsa_prep_kernel,
        grid=(t // tm,),
        in_specs=[pl.BlockSpec((tm, IDX_HEADS * DH), lambda i: (i, OFF_DSA_IQ // (IDX_HEADS * DH))),
                  pl.BlockSpec((tm, 128), lambda i: (i, OFF_DSA_IKW // 128)),
                  pl.BlockSpec((tm, 256), lambda i: (i, OFF_DSA_Q // 256)),
                  pl.BlockSpec((tm, 256), lambda i: (i, OFF_DSA_K // 256)),
                  pl.BlockSpec((tm, 256), lambda i: (i, OFF_DSA_V // 256)),
                  pl.BlockSpec((tm, DH // 4), lambda i: (i, 0)), full(e.shape), full(base.shape)],
        out_specs=[pl.BlockSpec((tm, DH), lambda i: (i, 0)),
                   pl.BlockSpec((HEADS, tm, DH), lambda i: (0, i, 0)),
                   pl.BlockSpec((1, HW, tm), lambda i: (i, 0, 0)),
                   pl.BlockSpec((IDX_HEADS * DH, tm), lambda i: (0, i)),
                   pl.BlockSpec((HW, tm), lambda i: (0, i)),
                   pl.BlockSpec((8, tm), lambda i: (0, i))],
        out_shape=[jax.ShapeDtypeStruct((t, DH), BF16), jax.ShapeDtypeStruct((HEADS, t, DH), BF16),
                   jax.ShapeDtypeStruct((t // tm, HW, tm), BF16),
                   jax.ShapeDtypeStruct((IDX_HEADS * DH, t), BF16), jax.ShapeDtypeStruct((HW, t), BF16),
                   jax.ShapeDtypeStruct((8, t), F32)],
        compiler_params=_cparams(1),
        name="dsa_prep",
    )(slab, slab, slab, slab, slab, cs, e, base)


def _dsa_kernel(iqt_ref, iwt_ref, qt_ref, ik_ref, k_ref, vt_ref, kidx_ref, o_ref,
                hi_ref, lo_ref, lom_ref, bias_ref, acc_ref, s_ref, m_ref, l_ref,
                *, tq, tk, n_sel, idx_bits):
    qi = pl.program_id(1)
    nkt = (qi * tq + tq - 1) // tk + 1
    iqt = iqt_ref[...]
    iw = iwt_ref[...]
    q_pos = qi * tq + _iota2((tk, tq), 1)
    qt = qt_ref[...]

    def score_body(kt, carry):
        row0 = pl.multiple_of(kt * tk, tk)
        ik = ik_ref[pl.ds(row0, tk), :]
        sc = jnp.zeros((tk, tq), F32)
        for h in range(IDX_HEADS):
            lg = lax.dot_general(ik, iqt[h * DH:(h + 1) * DH, :], NN, preferred_element_type=F32)
            sc = sc + jnp.maximum(lg, 0.0) * iw[h:h + 1, :]
        bits = lax.bitcast_convert_type(sc, I32)
        key = bits ^ ((bits >> 31) & 0x7FFFFFFF)
        key = jnp.where(sc == 0.0, 0, key)
        k_pos = row0 + _iota2((tk, tq), 0)
        key = jnp.where(k_pos <= q_pos, key, INT_MIN)
        hi_ref[pl.ds(row0, tk), :] = (key >> 16).astype(I16)
        lo_ref[pl.ds(row0, tk), :] = ((key & 0xFFFF) - 32768).astype(I16)
        for h in range(HEADS):
            s_ref[h, pl.ds(row0, tk), :] = lax.dot_general(k_ref[h, pl.ds(row0, tk), :],
                                                           qt[h * DH:(h + 1) * DH, :], NN,
                                                           preferred_element_type=F32)
        return carry

    lax.fori_loop(0, nkt, score_body, 0)

    one16 = jnp.ones((tk, tq), I16)
    zero16 = jnp.zeros((tk, tq), I16)

    def count(*pred_fns):
        def body(kt, accs):
            sl = pl.ds(pl.multiple_of(kt * tk, tk), tk)
            args = (hi_ref[sl, :], lo_ref[sl, :], lom_ref[sl, :], kidx_ref[sl, :])
            out = []
            for acc, fn in zip(accs, pred_fns):
                ones = jnp.where(fn(*args), one16, zero16).reshape(tk // 64, 64, tq)
                for g in range(tk // 64):
                    acc = acc + ones[g]
                out.append(acc)
            return tuple(out)
        parts = lax.fori_loop(0, nkt, body, tuple(jnp.zeros((64, tq), I16) for _ in pred_fns))
        sums = tuple(jnp.sum(p.astype(I32).astype(F32), axis=0, keepdims=True) for p in parts)
        return sums[0] if len(sums) == 1 else sums

    def bisect16(pred_ge):
        def body(it, thr):
            bit = jnp.maximum(15 - it, 0)
            cand = jnp.where(it == 0, jnp.zeros_like(thr), thr | jnp.left_shift(jnp.int32(1), bit))
            c16 = cand.astype(I16)
            cnt = count(lambda hi, lo, lom, idx: pred_ge(hi, lo, lom, c16))
            return jnp.where(cnt >= float(n_sel), cand, thr)
        return lax.fori_loop(0, 16, body, jnp.full((1, tq), -32768, I32))

    t_hi = bisect16(lambda hi, lo, lom, c: hi >= c)
    t_hi16 = t_hi.astype(I16)

    def mask_lo_body(kt, carry):
        sl = pl.ds(pl.multiple_of(kt * tk, tk), tk)
        hi = hi_ref[sl, :]
        lom_ref[sl, :] = jnp.where(hi > t_hi16, jnp.int16(32767),
                                   jnp.where(hi == t_hi16, lo_ref[sl, :], jnp.int16(-32768)))
        return carry

    lax.fori_loop(0, nkt, mask_lo_body, 0)
    t_lo16 = bisect16(lambda hi, lo, lom, c: lom >= c).astype(I16)

    def is_gt(hi, lo):
        return (hi > t_hi16) | ((hi == t_hi16) & (lo > t_lo16))

    def is_eq(hi, lo):
        return (hi == t_hi16) & (lo == t_lo16)

    def write_bias(sl, hi, sel_eq, gt):
        sel = (gt | sel_eq) & (hi != jnp.int16(-32768))
        bias_ref[sl, :] = jnp.where(sel, jnp.zeros((tk, tq), BF16),
                                    jnp.full((tk, tq), NEG_INF, BF16)).astype(F32)

    def tally_body(kt, accs):
        sl = pl.ds(pl.multiple_of(kt * tk, tk), tk)
        hi, lo = hi_ref[sl, :], lo_ref[sl, :]
        gt, eq = is_gt(hi, lo), is_eq(hi, lo)
        write_bias(sl, hi, eq, gt)
        out = []
        for acc, m in zip(accs, (gt, eq)):
            ones = jnp.where(m, one16, zero16).reshape(tk // 64, 64, tq)
            for g in range(tk // 64):
                acc = acc + ones[g]
            out.append(acc)
        return tuple(out)

    parts = lax.fori_loop(0, nkt, tally_body, (jnp.zeros((64, tq), I16), jnp.zeros((64, tq), I16)))
    cnt_gt, cnt_eq = (jnp.sum(p.astype(I32).astype(F32), axis=0, keepdims=True) for p in parts)
    need = float(n_sel) - cnt_gt

    @pl.when(jnp.max(jnp.where(cnt_eq > need, 1.0, 0.0)) > 0.0)
    def _():
        def idx_body(it, j):
            cand = j + jnp.left_shift(jnp.int32(1), idx_bits - 1 - it)
            c16 = cand.astype(I16)
            cnt = count(lambda hi, lo, lom, idx: is_eq(hi, lo) & (idx < c16))
            return jnp.where(cnt < need, cand, j)

        j_cut16 = lax.fori_loop(0, idx_bits, idx_body, jnp.zeros((1, tq), I32)).astype(I16)

        def rewrite_body(kt, carry):
            sl = pl.ds(pl.multiple_of(kt * tk, tk), tk)
            hi, lo = hi_ref[sl, :], lo_ref[sl, :]
            write_bias(sl, hi, is_eq(hi, lo) & (kidx_ref[sl, :] <= j_cut16), is_gt(hi, lo))
            return carry

        lax.fori_loop(0, nkt, rewrite_body, 0)

    acc_ref[...] = jnp.zeros_like(acc_ref)

    def att_body(kt, carry):
        sl = pl.ds(pl.multiple_of(kt * tk, tk), tk)
        bias = bias_ref[sl, :]
        ones = jnp.ones((16, tk), BF16)
        for h in range(HEADS):
            hs = slice(h * DH, (h + 1) * DH)
            s = s_ref[h, sl, :] + bias
            m_old = m_ref[h]
            m_new = jnp.maximum(m_old, jnp.max(s, axis=0, keepdims=True))
            alpha = jnp.exp2(m_old - m_new)
            p = jnp.exp2(s - m_new).astype(BF16)
            pv = lax.dot_general(jnp.concatenate([vt_ref[kt, hs, :], ones], axis=0), p, NN,
                                 preferred_element_type=F32)
            acc_ref[hs, :] = alpha * acc_ref[hs, :] + pv[0:DH, :]
            m_ref[h] = m_new
            l_ref[h] = alpha * l_ref[h] + pv[DH:DH + 1, :]
        return carry

    m_ref[...] = jnp.full_like(m_ref, NEG_INF)
    l_ref[...] = jnp.zeros_like(l_ref)
    lax.fori_loop(0, nkt, att_body, 0)
    o_t = jnp.concatenate([acc_ref[h * DH:(h + 1) * DH, :] / l_ref[h] for h in range(HEADS)], axis=0)
    o_ref[...] = o_t.T.astype(o_ref.dtype)


def _dsa(iqt, iwt, qt, ik, kr, vt, b, s, n_sel):
    tq = min(DSA_TQ, s)
    tk = min(DSA_TK, s)
    nq = s // tq
    nkt = s // tk
    return pl.pallas_call(
        functools.partial(_dsa_kernel, tq=tq, tk=tk, n_sel=n_sel, idx_bits=int(math.log2(s))),
        grid=(b, nq),
        in_specs=[pl.BlockSpec((IDX_HEADS * DH, tq), lambda i, j: (0, i * nq + j)),
                  pl.BlockSpec((8, tq), lambda i, j: (0, i * nq + j)),
                  pl.BlockSpec((HW, tq), lambda i, j: (0, i * nq + j)),
                  pl.BlockSpec((s, DH), lambda i, j: (i, 0)),
                  pl.BlockSpec((HEADS, s, DH), lambda i, j: (0, i, 0)),
                  pl.BlockSpec((nkt, HW, tk), lambda i, j: (i, 0, 0)),
                  pl.BlockSpec((s, tq), lambda i, j: (0, 0))],
        out_specs=pl.BlockSpec((tq, HW), lambda i, j: (i * nq + j, 0)),
        out_shape=jax.ShapeDtypeStruct((b * s, HW), BF16),
        scratch_shapes=[pltpu.VMEM((s, tq), I16), pltpu.VMEM((s, tq), I16), pltpu.VMEM((s, tq), I16),
                        pltpu.VMEM((s, tq), F32), pltpu.VMEM((HW, tq), F32),
                        pltpu.VMEM((HEADS, s, tq), F32), pltpu.VMEM((HEADS, 1, tq), F32),
                        pltpu.VMEM((HEADS, 1, tq), F32)],
        compiler_params=_cparams(2),
        name="dsa",
    )(iqt, iwt, qt, ik, kr, vt,
      jnp.broadcast_to(jnp.arange(s, dtype=I16)[:, None], (s, tq)))


def _merge_kernel(x_ref, oa_ref, ob_ref, oc_ref, od_ref, ga_ref, gb_ref, gc_ref, gd_ref,
                  wa_ref, wb_ref, wc_ref, wd_ref, wo_ref, o_ref):
    def branch(o_r, g_r, w_r):
        y = jnp.dot(o_r[...].astype(BF16), w_r[...], preferred_element_type=F32)
        return _sigmoid(g_r[...].astype(F32)) * y

    merged = (branch(oa_ref, ga_ref, wa_ref) + branch(ob_ref, gb_ref, wb_ref)
              + branch(oc_ref, gc_ref, wc_ref) + branch(od_ref, gd_ref, wd_ref))
    o_ref[...] = x_ref[...] + jnp.dot(merged.astype(BF16), wo_ref[...], preferred_element_type=F32)


def _merge(x2d, oa, ob, oc, od, gates, wa, wb, wc, wd, wo):
    t, d = x2d.shape
    tm = min(512, t)
    row = lambda w: pl.BlockSpec((tm, w), lambda i: (i, 0))
    gate = lambda n: pl.BlockSpec((tm, d), lambda i, o=n: (i, o))
    full = lambda shape: pl.BlockSpec(shape, lambda i: (0,) * len(shape))
    return pl.pallas_call(
        _merge_kernel,
        grid=(t // tm,),
        in_specs=[row(d), row(HW), row(HW), row(HW), row(HW), gate(0), gate(1), gate(2), gate(3),
                  full((HW, d)), full((HW, d)), full((HW, d)), full((HW, d)), full((d, d))],
        out_specs=row(d),
        out_shape=jax.ShapeDtypeStruct((t, d), F32),
        compiler_params=_cparams(1),
        name="merge",
    )(x2d, oa, ob, oc, od, gates, gates, gates, gates, wa, wb, wc, wd, wo)


def _mlp_kernel(x_ref, g_ref, wu_ref, wd_ref, fg_ref, o_ref, h_ref, acc_ref, *, final_norm):
    fi = pl.program_id(1)

    @pl.when(fi == 0)
    def _():
        x = x_ref[...]
        ms = jnp.mean(x * x, axis=-1, keepdims=True)
        h_ref[...] = (x * lax.rsqrt(ms + NORM_EPS) * g_ref[...]).astype(BF16)
        acc_ref[...] = jnp.zeros_like(acc_ref)

    u = jnp.dot(h_ref[...], wu_ref[...], preferred_element_type=F32)
    u = jnp.square(jnp.maximum(u, 0.0))
    acc_ref[...] += jnp.dot(u.astype(BF16), wd_ref[...], preferred_element_type=F32)

    @pl.when(fi == pl.num_programs(1) - 1)
    def _():
        y = x_ref[...] + acc_ref[...]
        if final_norm:
            ms = jnp.mean(y * y, axis=-1, keepdims=True)
            y = y * lax.rsqrt(ms + NORM_EPS) * fg_ref[...]
        o_ref[...] = y


def _mlp(x2d, g, wu, wd, fg, final_norm):
    t, d = x2d.shape
    f = wu.shape[1]
    tm = min(1024, t)
    tf = 512
    return pl.pallas_call(
        functools.partial(_mlp_kernel, final_norm=final_norm),
        grid=(t // tm, f // tf),
        in_specs=[pl.BlockSpec((tm, d), lambda i, j: (i, 0)),
                  pl.BlockSpec((1, d), lambda i, j: (0, 0)),
                  pl.BlockSpec((d, tf), lambda i, j: (0, j)),
                  pl.BlockSpec((tf, d), lambda i, j: (j, 0)),
                  pl.BlockSpec((1, d), lambda i, j: (0, 0))],
        out_specs=pl.BlockSpec((tm, d), lambda i, j: (i, 0)),
        out_shape=jax.ShapeDtypeStruct((t, d), F32),
        scratch_shapes=[pltpu.VMEM((tm, d), BF16), pltpu.VMEM((tm, d), F32)],
        compiler_params=_cparams(2),
        name="mlp",
    )(x2d, g, wu, wd, fg)


def _slab_weight(w_in, dtype=BF16):
    d = w_in.shape[0]
    z = lambda n: jnp.zeros((d, n), w_in.dtype)
    gdn, mla, rwkv, dsa = 0, 1032, 1448, 2472
    gate = 3824
    cols = [
        w_in[:, rwkv:rwkv + 1024],
        w_in[:, gdn:gdn + 768],
        w_in[:, gdn + 776:gdn + 1032],
        w_in[:, mla:mla + 256],
        w_in[:, mla + 256:mla + 384],
        z(64), w_in[:, mla + 384:mla + 416], z(32),
        w_in[:, dsa:dsa + 512],
        w_in[:, dsa + 584:dsa + 1352],
        w_in[:, gdn + 768:gdn + 776], z(120),
        w_in[:, dsa + 512:dsa + 584], z(56),
        w_in[:, gate:gate + 4096],
    ]
    return jnp.concatenate(cols, axis=1).astype(dtype)


def _mla_weights(w_uq, w_ukv):
    wq = w_uq.reshape(MLA_Q_LORA, HEADS, MLA_NOPE + MLA_ROPE)
    wq = jnp.pad(wq, ((0, 0), (0, 0), (0, MLA_QK - MLA_NOPE - MLA_ROPE))).reshape(MLA_Q_LORA, HEADS * MLA_QK)
    wkv = w_ukv.reshape(MLA_KV_LORA, HEADS, MLA_NOPE + DH)
    wk = jnp.pad(wkv[:, :, :MLA_NOPE], ((0, 0), (0, 0), (0, MLA_QK - MLA_NOPE))).reshape(MLA_KV_LORA, HEADS * MLA_QK)
    wvt = wkv[:, :, MLA_NOPE:].reshape(MLA_KV_LORA, HW).T
    return wq.astype(BF16), wk.astype(BF16), wvt.astype(BF16)


def _rope_cos_sin(positions, rot_dims):
    inv_freq = ROPE_THETA ** (-jnp.arange(0, rot_dims, 2, dtype=F32) / rot_dims)
    ang = positions.astype(F32)[..., None] * inv_freq
    return jnp.concatenate([jnp.cos(ang), jnp.sin(ang)], axis=-1)


def _rope_expansion(rot_dims, offset, period):
    half = rot_dims // 2
    e = np.zeros((rot_dims, 3 * 128), np.float32)
    base = np.zeros((1, 3 * 128), np.float32)
    base[0, 0:128] = 1.0
    for rep in range(128 // period):
        lo = rep * period + offset
        for j in range(half):
            e[j, lo + j] = 1.0
            e[j, lo + half + j] = 1.0
            base[0, lo + j] = 0.0
            base[0, lo + half + j] = 0.0
            e[half + j, 128 + lo + j] = -1.0
            e[half + j, 256 + lo + half + j] = 1.0
    return jnp.asarray(e), jnp.asarray(base)


def _rope_table(cs_ref, e_ref, base_ref):
    tab = _mm_exact_rhs(cs_ref[...], e_ref[...]) + base_ref[...]
    return tab[:, 0:128], tab[:, 128:256], tab[:, 256:384]


def _layer(x2d, b, s, prm, tabs, final_g, is_last):
    (mix_g, w_slab, conv_w, gdn_pcol, gdn_prow, gdn_ng, mla_qg, mla_kvg, wq, wk, wv,
     rwkv_mu, rwkv_vec, ww2, wa2, wg2, wo_gdn, wo_mla, wo_rwkv, wo_dsa, w_o, mlp_g, w_up, w_down) = prm
    mla_cs, dsa_cs = tabs
    t = b * s
    slab = _in_proj(x2d, mix_g, w_slab[:, :OFF_GATE], F32)
    gates = _in_proj(x2d, mix_g, w_slab[:, OFF_GATE:], BF16)

    ab = slab[:, OFF_GDN_AB:OFF_GDN_AB + 8].reshape(b, s // CHUNK, CHUNK, 8)
    abr = jnp.swapaxes(ab, 2, 3)
    o_gdn = _gdn(slab, abr, conv_w, gdn_pcol, gdn_prow, gdn_ng, b, s)
    q, k, v = _mla_prep(slab, mla_cs, mla_qg, mla_kvg, wq, wk, wv)
    o_mla = _flash(q, k, v, b, s)
    o_rwkv = _rwkv(slab, rwkv_mu, rwkv_vec, ww2, wa2, wg2, b, s)
    ik, kr, vt, iqt, qt, iwt = _dsa_prep(slab, dsa_cs)
    o_dsa = _dsa(iqt, iwt, qt, ik, kr, vt, b, s, min(IDX_TOPK_MAX, s // 4))

    x2d = _merge(x2d, o_gdn, o_mla, o_rwkv, o_dsa, gates, wo_gdn, wo_mla, wo_rwkv, wo_dsa, w_o)
    return _mlp(x2d, mlp_g, w_up, w_down, final_g, is_last)


def kernel(x, positions, mix_norm_g, w_in, gdn_conv_w, gdn_a_log, gdn_dt_bias, gdn_norm_g, mla_q_norm_g, mla_kv_norm_g, mla_w_uq, mla_w_ukv, rwkv_mu, rwkv_w0, rwkv_w_w2, rwkv_a0, rwkv_w_a2, rwkv_w_g2, rwkv_k_k, rwkv_k_a, rwkv_r_k, rwkv_ln_w, rwkv_ln_b, w_out_gdn, w_out_mla, w_out_rwkv, w_out_dsa, w_o, mlp_norm_g, w_up, w_down, final_norm_g):
    b, s, d = x.shape
    depth = w_in.shape[0]
    tabs = (_rope_cos_sin(positions, MLA_ROPE).reshape(b * s, MLA_ROPE),
            _rope_cos_sin(positions, DH // 4).reshape(b * s, DH // 4))
    row = lambda a: a.reshape(1, -1).astype(F32)
    pad8 = lambda rows: jnp.concatenate(rows + [jnp.zeros((8 - len(rows), rows[0].shape[1]), F32)], axis=0)
    x2d = x.reshape(b * s, d)
    final_g = row(final_norm_g)
    for l in range(depth):
        lane4 = lambda a: jnp.pad(a.astype(F32), (0, 124)).reshape(1, 128)
        gdn_pcol = pad8([lane4(gdn_a_log[l]), lane4(gdn_dt_bias[l])])
        gdn_prow = jnp.pad(jnp.stack([gdn_a_log[l], gdn_dt_bias[l]], axis=1).astype(F32), ((0, 4), (0, 126)))
        wq, wk, wv = _mla_weights(mla_w_uq[l], mla_w_ukv[l])
        rwkv_vec = pad8([row(rwkv_w0[l]), row(rwkv_a0[l]), row(rwkv_k_k[l]), row(rwkv_k_a[l]),
                         row(rwkv_ln_w[l]), row(rwkv_ln_b[l]), row(rwkv_r_k[l])])
        prm = (row(mix_norm_g[l]), _slab_weight(w_in[l]), gdn_conv_w[l].astype(F32), gdn_pcol, gdn_prow,
               row(jnp.tile(gdn_norm_g[l], HEADS)), row(mla_q_norm_g[l]), row(mla_kv_norm_g[l]), wq, wk, wv,
               row(rwkv_mu[l]), rwkv_vec, rwkv_w_w2[l].astype(F32), rwkv_w_a2[l].astype(F32),
               rwkv_w_g2[l].astype(F32),
               w_out_gdn[l].astype(BF16), w_out_mla[l].astype(BF16), w_out_rwkv[l].astype(BF16),
               w_out_dsa[l].astype(BF16), w_o[l].astype(BF16), row(mlp_norm_g[l]),
               w_up[l].astype(BF16), w_down[l].astype(BF16))
        x2d = _layer(x2d, b, s, prm, tabs, final_g, l == depth - 1)
    return x2d.reshape(b, s, d)
```

```python
import functools
import math

import numpy as np
import jax
import jax.numpy as jnp
from jax import lax
from jax.experimental import pallas as pl
from jax.experimental.pallas import tpu as pltpu

F32 = jnp.float32
BF16 = jnp.bfloat16
I32 = jnp.int32
I16 = jnp.int16

D_MODEL = 1024
DEPTH = 4
ROPE_THETA = 500000.0
NORM_EPS = 1e-6
NEG_INF = -1e30
HEADS = 4
DH = 64
HW = HEADS * DH
GDN_CONV = 4
CHUNK = 64
MLA_Q_LORA = 256
MLA_KV_LORA = 128
MLA_NOPE = 64
MLA_ROPE = 32
MLA_QK = 128
RWKV_GN_EPS = 64e-5
IDX_HEADS = 8
IDX_TOPK_MAX = 256
D_FF = 4 * D_MODEL
INT_MIN = -2 ** 31
LOG2E = 1.4426950408889634

VMEM_LIMIT_BYTES = 56 * 1024 * 1024

OFF_RWKV = 0
OFF_GDN_Q = 1024
OFF_GDN_K = 1280
OFF_GDN_V = 1536
OFF_GDN_Z = 1792
OFF_MLA_CQ = 2048
OFF_MLA_CKV = 2304
OFF_MLA_KR = 2432
OFF_DSA_IQ = 2560
OFF_DSA_Q = 3072
OFF_DSA_K = 3328
OFF_DSA_V = 3584
OFF_GDN_AB = 3840
OFF_DSA_IKW = 3968
OFF_GATE = 4096
N_SLAB = 8192

NN = (((1,), (0,)), ((), ()))
NT = (((1,), (1,)), ((), ()))
TN = (((0,), (0,)), ((), ()))
BNN = (((2,), (1,)), ((0,), (0,)))
BNT = (((2,), (2,)), ((0,), (0,)))
BTN = (((1,), (1,)), ((0,), (0,)))


def _cparams(n_axes):
    return pltpu.CompilerParams(dimension_semantics=("arbitrary",) * n_axes,
                                vmem_limit_bytes=VMEM_LIMIT_BYTES)


def _mm(a, b, dims=NN):
    return lax.dot_general(a.astype(BF16), b.astype(BF16), dims, preferred_element_type=F32)


def _mm_exact_lhs(m, x, dims=NN):
    mb = m.astype(BF16)
    x0 = x.astype(BF16)
    r1 = x - x0.astype(F32)
    x1 = r1.astype(BF16)
    x2 = (r1 - x1.astype(F32)).astype(BF16)
    d = lambda y: lax.dot_general(mb, y, dims, preferred_element_type=F32)
    return d(x0) + (d(x1) + d(x2))


def _mm_exact_rhs(x, m, dims=NN):
    mb = m.astype(BF16)
    x0 = x.astype(BF16)
    r1 = x - x0.astype(F32)
    x1 = r1.astype(BF16)
    x2 = (r1 - x1.astype(F32)).astype(BF16)
    d = lambda y: lax.dot_general(y, mb, dims, preferred_element_type=F32)
    return d(x0) + (d(x1) + d(x2))


def _iota2(shape, dim):
    return lax.broadcasted_iota(I32, shape, dim)


def _seg_ones(width, seg):
    sh = int(math.log2(seg))
    r = _iota2((width, width), 0) >> sh
    c = _iota2((width, width), 1) >> sh
    return jnp.where(r == c, 1.0, 0.0).astype(F32)


def _seg_sum(x, seg_ones):
    return _mm_exact_rhs(x, seg_ones)


def _softplus(x):
    return jnp.maximum(x, 0.0) + jnp.log1p(jnp.exp(-jnp.abs(x)))


def _sigmoid(x):
    return 0.5 * jnp.tanh(0.5 * x) + 0.5


def _silu(x):
    return x * _sigmoid(x)


def _tri_masks(n):
    r = _iota2((n, n), 0)
    c = _iota2((n, n), 1)
    return r, c


def _unit_lower_inverse(m, r, c, dims=NN):
    eye = jnp.where(r == c, 1.0, 0.0).astype(F32)
    same16 = (r >> 4) == (c >> 4)
    same32 = (r >> 5) == (c >> 5)
    md = jnp.where(same16, m, 0.0)
    t = eye - md
    p = md
    for _ in range(3):
        p = _mm(p, p, dims)
        t = t + _mm(t, p, dims)
    c32 = jnp.where(same32 & jnp.logical_not(same16), m, 0.0)
    t = t - _mm(_mm(t, c32, dims), t, dims)
    c64 = jnp.where(same32, 0.0, m)
    t = t - _mm(_mm(t, c64, dims), t, dims)
    return t


IN_PROJ_TN = 512


def _in_proj_kernel(x_ref, g_ref, w_ref, o_ref):
    x = x_ref[...]
    ms = jnp.mean(x * x, axis=-1, keepdims=True)
    h = (x * lax.rsqrt(ms + NORM_EPS) * g_ref[...]).astype(BF16)
    for n0 in range(0, o_ref.shape[1], IN_PROJ_TN):
        o_ref[:, n0:n0 + IN_PROJ_TN] = jnp.dot(
            h, w_ref[:, n0:n0 + IN_PROJ_TN], preferred_element_type=F32).astype(o_ref.dtype)


def _in_proj(x2d, g, w, out_dtype):
    t, d = x2d.shape
    n = w.shape[1]
    tm = min(512, t)
    return pl.pallas_call(
        _in_proj_kernel,
        grid=(t // tm,),
        in_specs=[pl.BlockSpec((tm, d), lambda i: (i, 0)),
                  pl.BlockSpec((1, d), lambda i: (0, 0)),
                  pl.BlockSpec((d, n), lambda i: (0, 0))],
        out_specs=pl.BlockSpec((tm, n), lambda i: (i, 0)),
        out_shape=jax.ShapeDtypeStruct((t, n), out_dtype),
        compiler_params=_cparams(1),
        name="in_proj",
    )(x2d, g, w)


def _to_groups(x, nchunk):
    parts = [x[:, h * DH:(h + 1) * DH].reshape(nchunk, 1, CHUNK, DH) for h in range(HEADS)]
    return jnp.concatenate(parts, axis=1).reshape(nchunk * HEADS, CHUNK, DH)


def _col_groups(x, lane0, nchunk):
    parts = [x[:, lane0 + h:lane0 + h + 1].reshape(nchunk, 1, CHUNK, 1) for h in range(HEADS)]
    return jnp.concatenate(parts, axis=1).reshape(nchunk * HEADS, CHUNK, 1)


def _gdn_kernel(q_ref, k_ref, v_ref, z_ref, ab_ref, abr_ref, cw_ref, pcol_ref, prow_ref, ng_ref,
                o_ref, xbuf_ref, state_ref, u_ref, w_ref, a_ref, qd_ref, kd_ref, egl_ref, oraw_ref,
                carry_ref, *, tb, nbatch):
    nchunk = tb // CHUNK
    bi = pl.program_id(1)
    xbuf = xbuf_ref

    @pl.when(pl.program_id(0) == 0)
    def _():
        carry_ref[bi] = jnp.zeros((8, 3 * HW), F32)
        state_ref[bi] = jnp.zeros((HEADS, DH, DH), F32)

    xbuf[0:8, :] = carry_ref[bi]
    xbuf[8:8 + tb, 0:HW] = q_ref[...]
    xbuf[8:8 + tb, HW:2 * HW] = k_ref[...]
    xbuf[8:8 + tb, 2 * HW:3 * HW] = v_ref[...]
    cw = cw_ref[...]
    y = xbuf[8:8 + tb, :] * cw[3:4, :]
    y = y + xbuf[7:7 + tb, :] * cw[2:3, :]
    y = y + xbuf[6:6 + tb, :] * cw[1:2, :]
    y = y + xbuf[5:5 + tb, :] * cw[0:1, :]
    carry_ref[bi] = xbuf[tb:tb + 8, :]
    y = _silu(y)
    seg = _seg_ones(HW, DH)
    q = y[:, 0:HW]
    k = y[:, HW:2 * HW]
    q = q * lax.rsqrt(_seg_sum(q * q, seg) + 1e-6) * (DH ** -0.5)
    k = k * lax.rsqrt(_seg_sum(k * k, seg) + 1e-6)
    v = y[:, 2 * HW:3 * HW]

    ab = ab_ref[...]
    pcol = pcol_ref[...]
    g_col = -jnp.exp(pcol[0:1, :]) * _softplus(ab + pcol[1:2, :])
    beta_col = _sigmoid(ab)

    r, c = _tri_masks(CHUNK)
    incl = c <= r
    strict = c < r
    l_incl = jnp.where(incl, 1.0, 0.0).astype(F32)

    rb = _iota2((tb, tb), 0)
    cb = _iota2((tb, tb), 1)
    l_blk = jnp.where(((rb >> 6) == (cb >> 6)) & (cb <= rb), 1.0, 0.0).astype(F32)
    gc_col = _mm_exact_lhs(l_blk, g_col)
    prow = prow_ref[...]
    abr = abr_ref[0].reshape(nchunk * 8, CHUNK)
    prow_t = jnp.concatenate([prow] * nchunk, axis=0)
    g_row = -jnp.exp(prow_t[:, 0:1]) * _softplus(abr + prow_t[:, 1:2])
    gc_row = _mm_exact_rhs(g_row, l_incl, NT).reshape(nchunk, 8, CHUNK)

    qg, kg, vg = _to_groups(q, nchunk), _to_groups(k, nchunk), _to_groups(v, nchunk)
    beta = _col_groups(beta_col, 4, nchunk)
    gcol = _col_groups(gc_col, 0, nchunk)
    grow = jnp.concatenate([gc_row[:, h:h + 1, :].reshape(nchunk, 1, 1, CHUNK) for h in range(HEADS)],
                           axis=1).reshape(nchunk * HEADS, 1, CHUNK)
    decay = jnp.where(incl, jnp.exp(jnp.where(incl, gcol - grow, 0.0)), 0.0)
    kb = kg * beta
    m = jnp.where(strict, _mm(kb, kg, BNT) * decay, 0.0)
    t_inv = _unit_lower_inverse(m, r, c, BNN)
    egc = jnp.exp(gcol)
    glast = gcol[:, CHUNK - 1:CHUNK, :]
    uw = _mm(t_inv, jnp.concatenate([vg * beta, kb * egc], axis=2), BNN)
    u_ref[bi] = uw[:, :, 0:DH]
    w_ref[bi] = uw[:, :, DH:2 * DH].astype(BF16)
    a_ref[bi] = (_mm(qg, kg, BNT) * decay).astype(BF16)
    qd_ref[bi] = (qg * egc).astype(BF16)
    kd_ref[bi] = (kg * jnp.exp(glast - gcol)).astype(BF16)
    egl_ref[bi] = jnp.broadcast_to(jnp.exp(glast), (nchunk * HEADS, 1, DH))

    @pl.when(bi == nbatch - 1)
    def _():
        nc = nbatch * HEADS

        def chunk_body(ci, carry):
            sl = pl.ds(pl.multiple_of(ci * HEADS, HEADS), HEADS)
            take = lambda ref: ref[:, sl].reshape((nc,) + ref.shape[2:])
            s = state_ref[...].reshape(nc, DH, DH)
            v_new = take(u_ref) - _mm(take(w_ref), s, BNN)
            o = _mm(take(qd_ref), s, BNN) + _mm(take(a_ref), v_new, BNN)
            s = s * take(egl_ref) + _mm(take(kd_ref), v_new, BTN)
            state_ref[...] = s.reshape(nbatch, HEADS, DH, DH)
            rows = pl.ds(pl.multiple_of(ci * CHUNK, CHUNK), CHUNK)
            for bb in range(nbatch):
                oraw_ref[bb, rows, :] = jnp.concatenate([o[bb * HEADS + h] for h in range(HEADS)], axis=1)
            return carry

        lax.fori_loop(0, nchunk, chunk_body, 0)
        o_all = oraw_ref[...].reshape(nbatch * tb, HW)
        ms = _seg_sum(o_all * o_all, seg) * (1.0 / DH)
        o_n = o_all * lax.rsqrt(ms + NORM_EPS) * ng_ref[...] * _silu(z_ref[...].reshape(nbatch * tb, HW))
        o_ref[...] = o_n.reshape(nbatch, tb, HW).astype(o_ref.dtype)


GDN_TB = 512


def _gdn(slab, abr, conv_w, pcol, prow, norm_g, b, s):
    tb = min(GDN_TB, s)
    nb = s // tb
    nchunk = tb // CHUNK
    ng = nchunk * HEADS
    grp = lambda w, dt=BF16: pltpu.VMEM((b, ng, CHUNK, w), dt)
    col = lambda off, w: pl.BlockSpec((tb, w), lambda j, i, o=off // w: (i * nb + j, o))
    full = lambda shape: pl.BlockSpec(shape, lambda j, i: (0,) * len(shape))
    slab3 = slab.reshape(b, s, slab.shape[1])
    out = pl.pallas_call(
        functools.partial(_gdn_kernel, tb=tb, nbatch=b),
        grid=(nb, b),
        in_specs=[col(OFF_GDN_Q, HW), col(OFF_GDN_K, HW), col(OFF_GDN_V, HW),
                  pl.BlockSpec((b, tb, HW), lambda j, i: (0, j, OFF_GDN_Z // HW)),
                  col(OFF_GDN_AB, 128),
                  pl.BlockSpec((1, nchunk, 8, CHUNK), lambda j, i: (i, j, 0, 0)),
                  full((GDN_CONV, 3 * HW)), full((8, 128)), full((8, 128)), full((1, HW))],
        out_specs=pl.BlockSpec((b, tb, HW), lambda j, i: (0, j, 0)),
        out_shape=jax.ShapeDtypeStruct((b, s, HW), BF16),
        scratch_shapes=[pltpu.VMEM((tb + 8, 3 * HW), F32), pltpu.VMEM((b, HEADS, DH, DH), F32),
                        grp(DH, F32), grp(DH), grp(CHUNK), grp(DH), grp(DH),
                        pltpu.VMEM((b, ng, 1, DH), F32), pltpu.VMEM((b, tb, HW), F32),
                        pltpu.VMEM((b, 8, 3 * HW), F32)],
        compiler_params=_cparams(2),
        name="gdn",
    )(slab, slab, slab, slab3, slab, abr, conv_w, pcol, prow, norm_g)
    return out.reshape(b * s, HW)


def _rwkv_kernel(p_ref, mu_ref, vec_ref, ww2_ref, wa2_ref, wg2_ref, o_ref,
                 xbuf_ref, state_ref, ta_ref, tpv_ref, rt_ref, qkv_ref, qb_ref, bc_ref, ktv_ref,
                 el_ref, oraw_ref, carry_ref, post_ref, *, tb, nbatch):
    nchunk = tb // CHUNK
    bi = pl.program_id(1)

    @pl.when(pl.program_id(0) == 0)
    def _():
        carry_ref[bi] = jnp.zeros((8, 4 * HW), F32)
        state_ref[bi] = jnp.zeros((HEADS, DH, DH), F32)

    p = p_ref[...]
    xbuf_ref[0:8, :] = carry_ref[bi]
    xbuf_ref[8:8 + tb, :] = p
    prev = xbuf_ref[7:7 + tb, :]
    carry_ref[bi] = xbuf_ref[tb:tb + 8, :]
    pm = p + (prev - p) * mu_ref[...]
    vec = vec_ref[...]
    w0, a0, k_k, k_a = vec[0:1, :], vec[1:2, :], vec[2:3, :], vec[3:4, :]
    rr = pm[:, 0:HW]
    k = pm[:, HW:2 * HW]
    v = pm[:, 2 * HW:3 * HW]
    xw = pm[:, 768:832]
    xa = pm[:, 832:896]
    xg = pm[:, 896:1024]
    w_log = -_softplus(-(w0 + _mm(jnp.tanh(xw), ww2_ref[...]))) - 0.5
    lw = -jnp.exp(w_log)
    a = _sigmoid(a0 + _mm(xa, wa2_ref[...]))
    g = _mm(_sigmoid(xg), wg2_ref[...])
    seg = _seg_ones(HW, DH)
    kk = k * k_k
    kk = kk * lax.rsqrt(_seg_sum(kk * kk, seg) + 1e-6)
    k2 = k * (1.0 + (a - 1.0) * k_a)
    kka = kk * a

    r_i, c_i = _tri_masks(CHUNK)
    incl = c_i <= r_i
    strict = c_i < r_i
    rb = _iota2((tb, tb), 0)
    cb = _iota2((tb, tb), 1)
    l_blk = jnp.where(((rb >> 6) == (cb >> 6)) & (cb <= rb), 1.0, 0.0).astype(F32)
    lwc_all = _mm_exact_lhs(l_blk, lw)

    rg, kg, vg = _to_groups(rr, nchunk), _to_groups(k2, nchunk), _to_groups(v, nchunk)
    kkg, kkag = _to_groups(kk, nchunk), _to_groups(kka, nchunk)
    lwg, lwc = _to_groups(lw, nchunk), _to_groups(lwc_all, nchunk)
    lwl = lwc[:, CHUNK - 1:CHUNK, :]
    e_neg = jnp.exp(-lwc)
    at = kkg * jnp.exp(lwc - lwg)
    rt = rg * jnp.exp(lwc)
    y2 = jnp.concatenate([at, rt], axis=1)
    x2 = jnp.concatenate([kkag * e_neg, kg * e_neg], axis=1)
    gram = _mm(y2, x2, BNT)
    n_m = jnp.where(strict, gram[:, 0:CHUNK, 0:CHUNK], 0.0)
    p_m = jnp.where(strict, gram[:, 0:CHUNK, CHUNK:], 0.0)
    qb_m = jnp.where(incl, gram[:, CHUNK:, 0:CHUNK], 0.0)
    qk_m = jnp.where(incl, gram[:, CHUNK:, CHUNK:], 0.0)
    t_inv = _unit_lower_inverse(n_m, r_i, c_i, BNN)
    ta_ref[bi] = _mm(t_inv, at, BNN).astype(BF16)
    tpv_ref[bi] = _mm(t_inv, _mm(p_m, vg, BNN), BNN)
    rt_ref[bi] = rt.astype(BF16)
    qkv_ref[bi] = _mm(qk_m, vg, BNN)
    qb_ref[bi] = qb_m.astype(BF16)
    e_rel = jnp.exp(lwl - lwc)
    bc_ref[bi] = (kkag * e_rel).astype(BF16)
    ktv_ref[bi] = _mm(vg, kg * e_rel, BTN)
    el_ref[bi] = jnp.exp(lwl)
    ln_w, ln_b, r_k = vec[4:5, :], vec[5:6, :], vec[6:7, :]
    post_ref[bi, :, 0:HW] = _seg_sum(rr * k2 * r_k, seg) * v
    post_ref[bi, :, HW:2 * HW] = g

    @pl.when(bi == nbatch - 1)
    def _():
        nc = nbatch * HEADS

        def chunk_body(ci, carry):
            sl = pl.ds(pl.multiple_of(ci * HEADS, HEADS), HEADS)
            take = lambda ref: ref[:, sl].reshape((nc,) + ref.shape[2:])
            zt = state_ref[...].reshape(nc, DH, DH)
            u = _mm(take(ta_ref), zt, BNT) + take(tpv_ref)
            o = _mm(take(rt_ref), zt, BNT) + take(qkv_ref) - _mm(take(qb_ref), u, BNN)
            zt = zt * take(el_ref) + take(ktv_ref) - _mm(u, take(bc_ref), BTN)
            state_ref[...] = zt.reshape(nbatch, HEADS, DH, DH)
            rows = pl.ds(pl.multiple_of(ci * CHUNK, CHUNK), CHUNK)
            for bb in range(nbatch):
                oraw_ref[bb, rows, :] = jnp.concatenate([o[bb * HEADS + h] for h in range(HEADS)], axis=1)
            return carry

        lax.fori_loop(0, nchunk, chunk_body, 0)
        o_all = oraw_ref[...].reshape(nbatch * tb, HW)
        mean = _seg_sum(o_all, seg) * (1.0 / DH)
        cen = o_all - mean
        var = _seg_sum(cen * cen, seg) * (1.0 / DH)
        o_n = cen * lax.rsqrt(var + RWKV_GN_EPS) * ln_w + ln_b
        post = post_ref[...].reshape(nbatch * tb, 2 * HW)
        o_ref[...] = ((o_n + post[:, 0:HW]) * post[:, HW:2 * HW]).reshape(nbatch, tb, HW).astype(o_ref.dtype)


RWKV_TB = 512


def _rwkv(slab, mu, vec, ww2, wa2, wg2, b, s):
    tb = min(RWKV_TB, s)
    nb = s // tb
    ng = (tb // CHUNK) * HEADS
    grp = lambda dt=F32: pltpu.VMEM((b, ng, CHUNK, DH), dt)
    full = lambda shape: pl.BlockSpec(shape, lambda j, i: (0,) * len(shape))
    out = pl.pallas_call(
        functools.partial(_rwkv_kernel, tb=tb, nbatch=b),
        grid=(nb, b),
        in_specs=[pl.BlockSpec((tb, 4 * HW), lambda j, i: (i * nb + j, OFF_RWKV // (4 * HW))),
                  full((1, 4 * HW)), full((8, HW)), full((64, HW)), full((64, HW)), full((128, HW))],
        out_specs=pl.BlockSpec((b, tb, HW), lambda j, i: (0, j, 0)),
        out_shape=jax.ShapeDtypeStruct((b, s, HW), BF16),
        scratch_shapes=[pltpu.VMEM((tb + 8, 4 * HW), F32), pltpu.VMEM((b, HEADS, DH, DH), F32),
                        grp(BF16), grp(), grp(BF16), grp(), grp(BF16), grp(BF16), grp(),
                        pltpu.VMEM((b, ng, 1, DH), F32), pltpu.VMEM((b, tb, HW), F32),
                        pltpu.VMEM((b, 8, 4 * HW), F32), pltpu.VMEM((b, tb, 2 * HW), F32)],
        compiler_params=_cparams(2),
        name="rwkv7",
    )(slab, mu, vec, ww2, wa2, wg2)
    return out.reshape(b * s, HW)


def _rope(x, cos_t, sin_lo, sin_hi, half):
    w = x.shape[-1]
    return x * cos_t + pltpu.roll(x, w - half, 1) * sin_lo + pltpu.roll(x, half, 1) * sin_hi


def _tile_lanes(t, n):
    return t if n == 1 else jnp.concatenate([t] * n, axis=1)


def _mla_prep_kernel(cq_ref, ckv_ref, kr_ref, cs_ref, e_ref, base_ref, qg_ref, kvg_ref, wq_ref, wk_ref,
                     wvt_ref, q_ref, k_ref, vt_ref):
    cq = cq_ref[...]
    hq = cq * lax.rsqrt(jnp.mean(cq * cq, axis=-1, keepdims=True) + NORM_EPS) * qg_ref[...]
    ckv = ckv_ref[...]
    hkv = ckv * lax.rsqrt(jnp.mean(ckv * ckv, axis=-1, keepdims=True) + NORM_EPS) * kvg_ref[...]
    cos_t, sin_lo, sin_hi = _rope_table(cs_ref, e_ref, base_ref)
    q = _mm(hq, wq_ref[...])
    q = _rope(q, _tile_lanes(cos_t, HEADS), _tile_lanes(sin_lo, HEADS), _tile_lanes(sin_hi, HEADS),
              MLA_ROPE // 2)
    q_ref[...] = (q * ((MLA_NOPE + MLA_ROPE) ** -0.5 * LOG2E)).astype(BF16)
    kr = _rope(kr_ref[...], cos_t, sin_lo, sin_hi, MLA_ROPE // 2)
    k = _mm(hkv, wk_ref[...]) + _tile_lanes(kr, HEADS)
    k_ref[...] = k.astype(BF16)
    vt_ref[...] = lax.dot_general(wvt_ref[...], hkv.astype(BF16), NT,
                                  preferred_element_type=F32).astype(BF16)


def _mla_prep(slab, cs, qg, kvg, wq, wk, wvt):
    t = slab.shape[0]
    tm = min(512, t)
    full = lambda shape: pl.BlockSpec(shape, lambda i: (0,) * len(shape))
    e, base = _rope_expansion(MLA_ROPE, MLA_NOPE, MLA_QK)
    return pl.pallas_call(
        _mla_prep_kernel,
        grid=(t // tm,),
        in_specs=[pl.BlockSpec((tm, 256), lambda i: (i, OFF_MLA_CQ // 256)),
                  pl.BlockSpec((tm, 128), lambda i: (i, OFF_MLA_CKV // 128)),
                  pl.BlockSpec((tm, 128), lambda i: (i, OFF_MLA_KR // 128)),
                  pl.BlockSpec((tm, MLA_ROPE), lambda i: (i, 0)), full(e.shape), full(base.shape),
                  full((1, 256)), full((1, 128)), full((256, 512)), full((128, 512)), full((256, 128))],
        out_specs=[pl.BlockSpec((tm, 512), lambda i: (i, 0)), pl.BlockSpec((tm, 512), lambda i: (i, 0)),
                   pl.BlockSpec((256, tm), lambda i: (0, i))],
        out_shape=[jax.ShapeDtypeStruct((t, 512), BF16), jax.ShapeDtypeStruct((t, 512), BF16),
                   jax.ShapeDtypeStruct((256, t), BF16)],
        compiler_params=_cparams(1),
        name="mla_prep",
    )(slab, slab, slab, cs, e, base, qg, kvg, wq, wk, wvt)


def _flash_kernel(q_ref, k_ref, vt_ref, o_ref, m_ref, l_ref, acc_ref, *, tq, tk):
    qi = pl.program_id(1)
    ki = pl.program_id(2)

    @pl.when(ki == 0)
    def _():
        m_ref[...] = jnp.full_like(m_ref, NEG_INF)
        l_ref[...] = jnp.zeros_like(l_ref)
        acc_ref[...] = jnp.zeros_like(acc_ref)

    def step(on_diagonal):
        if on_diagonal:
            causal = _iota2((tk, tq), 0) <= _iota2((tk, tq), 1)
        vt = vt_ref[...]
        ones = jnp.ones((16, tk), BF16)
        scores = [lax.dot_general(k_ref[:, h * MLA_QK:(h + 1) * MLA_QK], q_ref[:, h * MLA_QK:(h + 1) * MLA_QK],
                                  NT, preferred_element_type=F32) for h in range(HEADS)]
        for h in range(HEADS):
            s = scores[h]
            if on_diagonal:
                s = jnp.where(causal, s, NEG_INF)
            m_old = m_ref[h]
            m_new = jnp.maximum(m_old, jnp.max(s, axis=0, keepdims=True))
            alpha = jnp.exp2(m_old - m_new)
            p = jnp.exp2(s - m_new).astype(BF16)
            pv = jnp.dot(jnp.concatenate([vt[h * DH:(h + 1) * DH, :], ones], axis=0), p,
                         preferred_element_type=F32)
            l_ref[h] = alpha * l_ref[h] + pv[DH:DH + 1, :]
            acc_ref[h * DH:(h + 1) * DH, :] = alpha * acc_ref[h * DH:(h + 1) * DH, :] + pv[0:DH, :]
            m_ref[h] = m_new

    pl.when(ki < qi)(lambda: step(False))
    pl.when(ki == qi)(lambda: step(True))

    @pl.when(ki == pl.num_programs(2) - 1)
    def _():
        o_t = jnp.concatenate([acc_ref[h * DH:(h + 1) * DH, :] / l_ref[h] for h in range(HEADS)], axis=0)
        o_ref[...] = o_t.T.astype(o_ref.dtype)


def _flash(q, k, vt, b, s):
    tq = min(512, s)
    tk = tq
    nq = s // tq
    nk = s // tk
    return pl.pallas_call(
        functools.partial(_flash_kernel, tq=tq, tk=tk),
        grid=(b, nq, nk),
        in_specs=[pl.BlockSpec((tq, 512), lambda i, j, kk: (i * nq + j, 0)),
                  pl.BlockSpec((tk, 512), lambda i, j, kk: (i * nk + jnp.minimum(kk, j), 0)),
                  pl.BlockSpec((256, tk), lambda i, j, kk: (0, i * nk + jnp.minimum(kk, j)))],
        out_specs=pl.BlockSpec((tq, 256), lambda i, j, kk: (i * nq + j, 0)),
        out_shape=jax.ShapeDtypeStruct((b * s, 256), BF16),
        scratch_shapes=[pltpu.VMEM((HEADS, 1, tq), F32), pltpu.VMEM((HEADS, 1, tq), F32),
                        pltpu.VMEM((HW, tq), F32)],
        compiler_params=_cparams(3),
        name="mla_flash",
    )(q, k, vt)


DSA_TQ = 256
DSA_TK = 512


def _dsa_prep_kernel(iq_ref, ikw_ref, q_ref, k_ref, v_ref, cs_ref, e_ref, base_ref,
                     ik_ref, ko_ref, vt_ref, iqt_ref, qt_ref, iwt_ref):
    cos_t, sin_lo, sin_hi = _rope_table(cs_ref, e_ref, base_ref)
    half = DH // 8
    rope = lambda x, n: _rope(x, _tile_lanes(cos_t, n), _tile_lanes(sin_lo, n), _tile_lanes(sin_hi, n), half)
    ikw = ikw_ref[...]
    ik_ref[...] = rope(ikw, 1)[:, 0:DH].astype(BF16)
    kr = rope(k_ref[...], 2)
    for h in range(HEADS):
        ko_ref[h] = kr[:, h * DH:(h + 1) * DH].astype(BF16)
    vt_ref[0] = v_ref[...].T.astype(BF16)
    iqt_ref[...] = rope(iq_ref[...], 4).T.astype(BF16)
    qt_ref[...] = (rope(q_ref[...], 2) * (DH ** -0.5 * LOG2E)).T.astype(BF16)
    iwt_ref[...] = ikw.T[DH:DH + 8, :] * (IDX_HEADS ** -0.5 * DH ** -0.5)


def _dsa_prep(slab, cs):
    t = slab.shape[0]
    tm = min(DSA_TK, t)
    full = lambda shape: pl.BlockSpec(shape, lambda i: (0,) * len(shape))
    e, base = _rope_expansion(DH // 4, 0, DH)
    return pl.pallas_call(
        _dsa_prep_kernel,
        grid=(t // tm,),
        in_specs=[pl.BlockSpec((tm, IDX_HEADS * DH), lambda i: (i, OFF_DSA_IQ // (IDX_HEADS * DH))),
                  pl.BlockSpec((tm, 128), lambda i: (i, OFF_DSA_IKW // 128)),
                  pl.BlockSpec((tm, 256), lambda i: (i, OFF_DSA_Q // 256)),
                  pl.BlockSpec((tm, 256), lambda i: (i, OFF_DSA_K // 256)),
                  pl.BlockSpec((tm, 256), lambda i: (i, OFF_DSA_V // 256)),
                  pl.BlockSpec((tm, DH // 4), lambda i: (i, 0)), full(e.shape), full(base.shape)],
        out_specs=[pl.BlockSpec((tm, DH), lambda i: (i, 0)),
                   pl.BlockSpec((HEADS, tm, DH), lambda i: (0, i, 0)),
                   pl.BlockSpec((1, HW, tm), lambda i: (i, 0, 0)),
                   pl.BlockSpec((IDX_HEADS * DH, tm), lambda i: (0, i)),
                   pl.BlockSpec((HW, tm), lambda i: (0, i)),
                   pl.BlockSpec((8, tm), lambda i: (0, i))],
        out_shape=[jax.ShapeDtypeStruct((t, DH), BF16), jax.ShapeDtypeStruct((HEADS, t, DH), BF16),
                   jax.ShapeDtypeStruct((t // tm, HW, tm), BF16),
                   jax.ShapeDtypeStruct((IDX_HEADS * DH, t), BF16), jax.ShapeDtypeStruct((HW, t), BF16),
                   jax.ShapeDtypeStruct((8, t), F32)],
        compiler_params=_cparams(1),
        name="dsa_prep",
    )(slab, slab, slab, slab, slab, cs, e, base)


def _dsa_kernel(iqt_ref, iwt_ref, qt_ref, ik_ref, k_ref, vt_ref, kidx_ref, o_ref,
                hi_ref, lo_ref, lom_ref, bias_ref, acc_ref, s_ref, m_ref, l_ref,
                *, tq, tk, n_sel, idx_bits):
    qi = pl.program_id(1)
    nkt = (qi * tq + tq - 1) // tk + 1
    iqt = iqt_ref[...]
    iw = iwt_ref[...]
    q_pos = qi * tq + _iota2((tk, tq), 1)
    qt = qt_ref[...]

    def score_body(kt, carry):
        row0 = pl.multiple_of(kt * tk, tk)
        ik = ik_ref[pl.ds(row0, tk), :]
        sc = jnp.zeros((tk, tq), F32)
        for h in range(IDX_HEADS):
            lg = lax.dot_general(ik, iqt[h * DH:(h + 1) * DH, :], NN, preferred_element_type=F32)
            sc = sc + jnp.maximum(lg, 0.0) * iw[h:h + 1, :]
        bits = lax.bitcast_convert_type(sc, I32)
        key = bits ^ ((bits >> 31) & 0x7FFFFFFF)
        key = jnp.where(sc == 0.0, 0, key)
        k_pos = row0 + _iota2((tk, tq), 0)
        key = jnp.where(k_pos <= q_pos, key, INT_MIN)
        hi_ref[pl.ds(row0, tk), :] = (key >> 16).astype(I16)
        lo_ref[pl.ds(row0, tk), :] = ((key & 0xFFFF) - 32768).astype(I16)
        for h in range(HEADS):
            s_ref[h, pl.ds(row0, tk), :] = lax.dot_general(k_ref[h, pl.ds(row0, tk), :],
                                                           qt[h * DH:(h + 1) * DH, :], NN,
                                                           preferred_element_type=F32)
        return carry

    lax.fori_loop(0, nkt, score_body, 0)

    one16 = jnp.ones((tk, tq), I16)
    zero16 = jnp.zeros((tk, tq), I16)

    def count(*pred_fns):
        def body(kt, accs):
            sl = pl.ds(pl.multiple_of(kt * tk, tk), tk)
            args = (hi_ref[sl, :], lo_ref[sl, :], lom_ref[sl, :], kidx_ref[sl, :])
            out = []
            for acc, fn in zip(accs, pred_fns):
                ones = jnp.where(fn(*args), one16, zero16).reshape(tk // 64, 64, tq)
                for g in range(tk // 64):
                    acc = acc + ones[g]
                out.append(acc)
            return tuple(out)
        parts = lax.fori_loop(0, nkt, body, tuple(jnp.zeros((64, tq), I16) for _ in pred_fns))
        sums = tuple(jnp.sum(p.astype(I32).astype(F32), axis=0, keepdims=True) for p in parts)
        return sums[0] if len(sums) == 1 else sums

    def bisect16(pred_ge):
        def body(it, thr):
            bit = jnp.maximum(15 - it, 0)
            cand = jnp.where(it == 0, jnp.zeros_like(thr), thr | jnp.left_shift(jnp.int32(1), bit))
            c16 = cand.astype(I16)
            cnt = count(lambda hi, lo, lom, idx: pred_ge(hi, lo, lom, c16))
            return jnp.where(cnt >= float(n_sel), cand, thr)
        return lax.fori_loop(0, 16, body, jnp.full((1, tq), -32768, I32))

    t_hi = bisect16(lambda hi, lo, lom, c: hi >= c)
    t_hi16 = t_hi.astype(I16)

    def mask_lo_body(kt, carry):
        sl = pl.ds(pl.multiple_of(kt * tk, tk), tk)
        hi = hi_ref[sl, :]
        lom_ref[sl, :] = jnp.where(hi > t_hi16, jnp.int16(32767),
                                   jnp.where(hi == t_hi16, lo_ref[sl, :], jnp.int16(-32768)))
        return carry

    lax.fori_loop(0, nkt, mask_lo_body, 0)
    t_lo16 = bisect16(lambda hi, lo, lom, c: lom >= c).astype(I16)

    def is_gt(hi, lo):
        return (hi > t_hi16) | ((hi == t_hi16) & (lo > t_lo16))

    def is_eq(hi, lo):
        return (hi == t_hi16) & (lo == t_lo16)

    def write_bias(sl, hi, sel_eq, gt):
        sel = (gt | sel_eq) & (hi != jnp.int16(-32768))
        bias_ref[sl, :] = jnp.where(sel, jnp.zeros((tk, tq), BF16),
                                    jnp.full((tk, tq), NEG_INF, BF16)).astype(F32)

    def tally_body(kt, accs):
        sl = pl.ds(pl.multiple_of(kt * tk, tk), tk)
        hi, lo = hi_ref[sl, :], lo_ref[sl, :]
        gt, eq = is_gt(hi, lo), is_eq(hi, lo)
        write_bias(sl, hi, eq, gt)
        out = []
        for acc, m in zip(accs, (gt, eq)):
            ones = jnp.where(m, one16, zero16).reshape(tk // 64, 64, tq)
            for g in range(tk // 64):
                acc = acc + ones[g]
            out.append(acc)
        return tuple(out)

    parts = lax.fori_loop(0, nkt, tally_body, (jnp.zeros((64, tq), I16), jnp.zeros((64, tq), I16)))
    cnt_gt, cnt_eq = (jnp.sum(p.astype(I32).astype(F32), axis=0, keepdims=True) for p in parts)
    need = float(n_sel) - cnt_gt

    @pl.when(jnp.max(jnp.where(cnt_eq > need, 1.0, 0.0)) > 0.0)
    def _():
        def idx_body(it, j):
            cand = j + jnp.left_shift(jnp.int32(1), idx_bits - 1 - it)
            c16 = cand.astype(I16)
            cnt = count(lambda hi, lo, lom, idx: is_eq(hi, lo) & (idx < c16))
            return jnp.where(cnt < need, cand, j)

        j_cut16 = lax.fori_loop(0, idx_bits, idx_body, jnp.zeros((1, tq), I32)).astype(I16)

        def rewrite_body(kt, carry):
            sl = pl.ds(pl.multiple_of(kt * tk, tk), tk)
            hi, lo = hi_ref[sl, :], lo_ref[sl, :]
            write_bias(sl, hi, is_eq(hi, lo) & (kidx_ref[sl, :] <= j_cut16), is_gt(hi, lo))
            return carry

        lax.fori_loop(0, nkt, rewrite_body, 0)

    acc_ref[...] = jnp.zeros_like(acc_ref)

    def att_body(kt, carry):
        sl = pl.ds(pl.multiple_of(kt * tk, tk), tk)
        bias = bias_ref[sl, :]
        ones = jnp.ones((16, tk), BF16)
        for h in range(HEADS):
            hs = slice(h * DH, (h + 1) * DH)
            s = s_ref[h, sl, :] + bias
            m_old = m_ref[h]
            m_new = jnp.maximum(m_old, jnp.max(s, axis=0, keepdims=True))
            alpha = jnp.exp2(m_old - m_new)
            p = jnp.exp2(s - m_new).astype(BF16)
            pv = lax.dot_general(jnp.concatenate([vt_ref[kt, hs, :], ones], axis=0), p, NN,
                                 preferred_element_type=F32)
            acc_ref[hs, :] = alpha * acc_ref[hs, :] + pv[0:DH, :]
            m_ref[h] = m_new
            l_ref[h] = alpha * l_ref[h] + pv[DH:DH + 1, :]
        return carry

    m_ref[...] = jnp.full_like(m_ref, NEG_INF)
    l_ref[...] = jnp.zeros_like(l_ref)
    lax.fori_loop(0, nkt, att_body, 0)
    o_t = jnp.concatenate([acc_ref[h * DH:(h + 1) * DH, :] / l_ref[h] for h in range(HEADS)], axis=0)
    o_ref[...] = o_t.T.astype(o_ref.dtype)


def _dsa(iqt, iwt, qt, ik, kr, vt, b, s, n_sel):
    tq = min(DSA_TQ, s)
    tk = min(DSA_TK, s)
    nq = s // tq
    nkt = s // tk
    return pl.pallas_call(
        functools.partial(_dsa_kernel, tq=tq, tk=tk, n_sel=n_sel, idx_bits=int(math.log2(s))),
        grid=(b, nq),
        in_specs=[pl.BlockSpec((IDX_HEADS * DH, tq), lambda i, j: (0, i * nq + j)),
                  pl.BlockSpec((8, tq), lambda i, j: (0, i * nq + j)),
                  pl.BlockSpec((HW, tq), lambda i, j: (0, i * nq + j)),
                  pl.BlockSpec((s, DH), lambda i, j: (i, 0)),
                  pl.BlockSpec((HEADS, s, DH), lambda i, j: (0, i, 0)),
                  pl.BlockSpec((nkt, HW, tk), lambda i, j: (i, 0, 0)),
                  pl.BlockSpec((s, tq), lambda i, j: (0, 0))],
        out_specs=pl.BlockSpec((tq, HW), lambda i, j: (i * nq + j, 0)),
        out_shape=jax.ShapeDtypeStruct((b * s, HW), BF16),
        scratch_shapes=[pltpu.VMEM((s, tq), I16), pltpu.VMEM((s, tq), I16), pltpu.VMEM((s, tq), I16),
                        pltpu.VMEM((s, tq), F32), pltpu.VMEM((HW, tq), F32),
                        pltpu.VMEM((HEADS, s, tq), F32), pltpu.VMEM((HEADS, 1, tq), F32),
                        pltpu.VMEM((HEADS, 1, tq), F32)],
        compiler_params=_cparams(2),
        name="dsa",
    )(iqt, iwt, qt, ik, kr, vt,
      jnp.broadcast_to(jnp.arange(s, dtype=I16)[:, None], (s, tq)))


def _merge_kernel(x_ref, oa_ref, ob_ref, oc_ref, od_ref, ga_ref, gb_ref, gc_ref, gd_ref,
                  wa_ref, wb_ref, wc_ref, wd_ref, wo_ref, o_ref):
    def branch(o_r, g_r, w_r):
        y = jnp.dot(o_r[...].astype(BF16), w_r[...], preferred_element_type=F32)
        return _sigmoid(g_r[...].astype(F32)) * y

    merged = (branch(oa_ref, ga_ref, wa_ref) + branch(ob_ref, gb_ref, wb_ref)
              + branch(oc_ref, gc_ref, wc_ref) + branch(od_ref, gd_ref, wd_ref))
    o_ref[...] = x_ref[...] + jnp.dot(merged.astype(BF16), wo_ref[...], preferred_element_type=F32)


def _merge(x2d, oa, ob, oc, od, gates, wa, wb, wc, wd, wo):
    t, d = x2d.shape
    tm = min(512, t)
    row = lambda w: pl.BlockSpec((tm, w), lambda i: (i, 0))
    gate = lambda n: pl.BlockSpec((tm, d), lambda i, o=n: (i, o))
    full = lambda shape: pl.BlockSpec(shape, lambda i: (0,) * len(shape))
    return pl.pallas_call(
        _merge_kernel,
        grid=(t // tm,),
        in_specs=[row(d), row(HW), row(HW), row(HW), row(HW), gate(0), gate(1), gate(2), gate(3),
                  full((HW, d)), full((HW, d)), full((HW, d)), full((HW, d)), full((d, d))],
        out_specs=row(d),
        out_shape=jax.ShapeDtypeStruct((t, d), F32),
        compiler_params=_cparams(1),
        name="merge",
    )(x2d, oa, ob, oc, od, gates, gates, gates, gates, wa, wb, wc, wd, wo)


def _mlp_kernel(x_ref, g_ref, wu_ref, wd_ref, fg_ref, o_ref, h_ref, acc_ref, *, final_norm):
    fi = pl.program_id(1)

    @pl.when(fi == 0)
    def _():
        x = x_ref[...]
        ms = jnp.mean(x * x, axis=-1, keepdims=True)
        h_ref[...] = (x * lax.rsqrt(ms + NORM_EPS) * g_ref[...]).astype(BF16)
        acc_ref[...] = jnp.zeros_like(acc_ref)

    u = jnp.dot(h_ref[...], wu_ref[...], preferred_element_type=F32)
    u = jnp.square(jnp.maximum(u, 0.0))
    acc_ref[...] += jnp.dot(u.astype(BF16), wd_ref[...], preferred_element_type=F32)

    @pl.when(fi == pl.num_programs(1) - 1)
    def _():
        y = x_ref[...] + acc_ref[...]
        if final_norm:
            ms = jnp.mean(y * y, axis=-1, keepdims=True)
            y = y * lax.rsqrt(ms + NORM_EPS) * fg_ref[...]
        o_ref[...] = y


def _mlp(x2d, g, wu, wd, fg, final_norm):
    t, d = x2d.shape
    f = wu.shape[1]
    tm = min(1024, t)
    tf = 1024
    return pl.pallas_call(
        functools.partial(_mlp_kernel, final_norm=final_norm),
        grid=(t // tm, f // tf),
        in_specs=[pl.BlockSpec((tm, d), lambda i, j: (i, 0)),
                  pl.BlockSpec((1, d), lambda i, j: (0, 0)),
                  pl.BlockSpec((d, tf), lambda i, j: (0, j)),
                  pl.BlockSpec((tf, d), lambda i, j: (j, 0)),
                  pl.BlockSpec((1, d), lambda i, j: (0, 0))],
        out_specs=pl.BlockSpec((tm, d), lambda i, j: (i, 0)),
        out_shape=jax.ShapeDtypeStruct((t, d), F32),
        scratch_shapes=[pltpu.VMEM((tm, d), BF16), pltpu.VMEM((tm, d), F32)],
        compiler_params=_cparams(2),
        name="mlp",
    )(x2d, g, wu, wd, fg)


def _slab_weight(w_in, dtype=BF16):
    d = w_in.shape[0]
    z = lambda n: jnp.zeros((d, n), w_in.dtype)
    gdn, mla, rwkv, dsa = 0, 1032, 1448, 2472
    gate = 3824
    cols = [
        w_in[:, rwkv:rwkv + 1024],
        w_in[:, gdn:gdn + 768],
        w_in[:, gdn + 776:gdn + 1032],
        w_in[:, mla:mla + 256],
        w_in[:, mla + 256:mla + 384],
        z(64), w_in[:, mla + 384:mla + 416], z(32),
        w_in[:, dsa:dsa + 512],
        w_in[:, dsa + 584:dsa + 1352],
        w_in[:, gdn + 768:gdn + 776], z(120),
        w_in[:, dsa + 512:dsa + 584], z(56),
        w_in[:, gate:gate + 4096],
    ]
    return jnp.concatenate(cols, axis=1).astype(dtype)


def _mla_weights(w_uq, w_ukv):
    wq = w_uq.reshape(MLA_Q_LORA, HEADS, MLA_NOPE + MLA_ROPE)
    wq = jnp.pad(wq, ((0, 0), (0, 0), (0, MLA_QK - MLA_NOPE - MLA_ROPE))).reshape(MLA_Q_LORA, HEADS * MLA_QK)
    wkv = w_ukv.reshape(MLA_KV_LORA, HEADS, MLA_NOPE + DH)
    wk = jnp.pad(wkv[:, :, :MLA_NOPE], ((0, 0), (0, 0), (0, MLA_QK - MLA_NOPE))).reshape(MLA_KV_LORA, HEADS * MLA_QK)
    wvt = wkv[:, :, MLA_NOPE:].reshape(MLA_KV_LORA, HW).T
    return wq.astype(BF16), wk.astype(BF16), wvt.astype(BF16)


def _rope_cos_sin(positions, rot_dims):
    inv_freq = ROPE_THETA ** (-jnp.arange(0, rot_dims, 2, dtype=F32) / rot_dims)
    ang = positions.astype(F32)[..., None] * inv_freq
    return jnp.concatenate([jnp.cos(ang), jnp.sin(ang)], axis=-1)


def _rope_expansion(rot_dims, offset, period):
    half = rot_dims // 2
    e = np.zeros((rot_dims, 3 * 128), np.float32)
    base = np.zeros((1, 3 * 128), np.float32)
    base[0, 0:128] = 1.0
    for rep in range(128 // period):
        lo = rep * period + offset
        for j in range(half):
            e[j, lo + j] = 1.0
            e[j, lo + half + j] = 1.0
            base[0, lo + j] = 0.0
            base[0, lo + half + j] = 0.0
            e[half + j, 128 + lo + j] = -1.0
            e[half + j, 256 + lo + half + j] = 1.0
    return jnp.asarray(e), jnp.asarray(base)


def _rope_table(cs_ref, e_ref, base_ref):
    tab = _mm_exact_rhs(cs_ref[...], e_ref[...]) + base_ref[...]
    return tab[:, 0:128], tab[:, 128:256], tab[:, 256:384]


def _layer(x2d, b, s, prm, tabs, final_g, is_last):
    (mix_g, w_slab, conv_w, gdn_pcol, gdn_prow, gdn_ng, mla_qg, mla_kvg, wq, wk, wv,
     rwkv_mu, rwkv_vec, ww2, wa2, wg2, wo_gdn, wo_mla, wo_rwkv, wo_dsa, w_o, mlp_g, w_up, w_down) = prm
    mla_cs, dsa_cs = tabs
    t = b * s
    slab = _in_proj(x2d, mix_g, w_slab[:, :OFF_GATE], F32)
    gates = _in_proj(x2d, mix_g, w_slab[:, OFF_GATE:], BF16)

    ab = slab[:, OFF_GDN_AB:OFF_GDN_AB + 8].reshape(b, s // CHUNK, CHUNK, 8)
    abr = jnp.swapaxes(ab, 2, 3)
    o_gdn = _gdn(slab, abr, conv_w, gdn_pcol, gdn_prow, gdn_ng, b, s)
    q, k, v = _mla_prep(slab, mla_cs, mla_qg, mla_kvg, wq, wk, wv)
    o_mla = _flash(q, k, v, b, s)
    o_rwkv = _rwkv(slab, rwkv_mu, rwkv_vec, ww2, wa2, wg2, b, s)
    ik, kr, vt, iqt, qt, iwt = _dsa_prep(slab, dsa_cs)
    o_dsa = _dsa(iqt, iwt, qt, ik, kr, vt, b, s, min(IDX_TOPK_MAX, s // 4))

    x2d = _merge(x2d, o_gdn, o_mla, o_rwkv, o_dsa, gates, wo_gdn, wo_mla, wo_rwkv, wo_dsa, w_o)
    return _mlp(x2d, mlp_g, w_up, w_down, final_g, is_last)


def kernel(x, positions, mix_norm_g, w_in, gdn_conv_w, gdn_a_log, gdn_dt_bias, gdn_norm_g, mla_q_norm_g, mla_kv_norm_g, mla_w_uq, mla_w_ukv, rwkv_mu, rwkv_w0, rwkv_w_w2, rwkv_a0, rwkv_w_a2, rwkv_w_g2, rwkv_k_k, rwkv_k_a, rwkv_r_k, rwkv_ln_w, rwkv_ln_b, w_out_gdn, w_out_mla, w_out_rwkv, w_out_dsa, w_o, mlp_norm_g, w_up, w_down, final_norm_g):
    b, s, d = x.shape
    depth = w_in.shape[0]
    tabs = (_rope_cos_sin(positions, MLA_ROPE).reshape(b * s, MLA_ROPE),
            _rope_cos_sin(positions, DH // 4).reshape(b * s, DH // 4))
    row = lambda a: a.reshape(1, -1).astype(F32)
    pad8 = lambda rows: jnp.concatenate(rows + [jnp.zeros((8 - len(rows), rows[0].shape[1]), F32)], axis=0)
    x2d = x.reshape(b * s, d)
    final_g = row(final_norm_g)
    for l in range(depth):
        lane4 = lambda a: jnp.pad(a.astype(F32), (0, 124)).reshape(1, 128)
        gdn_pcol = pad8([lane4(gdn_a_log[l]), lane4(gdn_dt_bias[l])])
        gdn_prow = jnp.pad(jnp.stack([gdn_a_log[l], gdn_dt_bias[l]], axis=1).astype(F32), ((0, 4), (0, 126)))
        wq, wk, wv = _mla_weights(mla_w_uq[l], mla_w_ukv[l])
        rwkv_vec = pad8([row(rwkv_w0[l]), row(rwkv_a0[l]), row(rwkv_k_k[l]), row(rwkv_k_a[l]),
                         row(rwkv_ln_w[l]), row(rwkv_ln_b[l]), row(rwkv_r_k[l])])
        prm = (row(mix_norm_g[l]), _slab_weight(w_in[l]), gdn_conv_w[l].astype(F32), gdn_pcol, gdn_prow,
               row(jnp.tile(gdn_norm_g[l], HEADS)), row(mla_q_norm_g[l]), row(mla_kv_norm_g[l]), wq, wk, wv,
               row(rwkv_mu[l]), rwkv_vec, rwkv_w_w2[l].astype(F32), rwkv_w_a2[l].astype(F32),
               rwkv_w_g2[l].astype(F32),
               w_out_gdn[l].astype(BF16), w_out_mla[l].astype(BF16), w_out_rwkv[l].astype(BF16),
               w_out_dsa[l].astype(BF16), w_o[l].astype(BF16), row(mlp_norm_g[l]),
               w_up[l].astype(BF16), w_down[l].astype(BF16))
        x2d = _layer(x2d, b, s, prm, tabs, final_g, l == depth - 1)
    return x2d.reshape(b, s, d)
```

```python
import functools
import math

import numpy as np
import jax
import jax.numpy as jnp
from jax import lax
from jax.experimental import pallas as pl
from jax.experimental.pallas import tpu as pltpu

F32 = jnp.float32
BF16 = jnp.bfloat16
I32 = jnp.int32
I16 = jnp.int16

ROPE_THETA = 500000.0
NORM_EPS = 1e-6
NEG_INF = -1e30
HEADS = 4
DH = 64
HW = HEADS * DH
GDN_CONV = 4
CHUNK = 64
MLA_Q_LORA = 256
MLA_KV_LORA = 128
MLA_NOPE = 64
MLA_ROPE = 32
MLA_QK = 128
RWKV_GN_EPS = 64e-5
IDX_HEADS = 8
IDX_TOPK_MAX = 256
INT_MIN = -2 ** 31
LOG2E = 1.4426950408889634

VMEM_LIMIT_BYTES = 56 * 1024 * 1024

OFF_RWKV = 0
OFF_GDN_Q = 1024
OFF_GDN_K = 1280
OFF_GDN_V = 1536
OFF_GDN_Z = 1792
OFF_MLA_CQ = 2048
OFF_MLA_CKV = 2304
OFF_MLA_KR = 2432
OFF_DSA_IQ = 2560
OFF_DSA_Q = 3072
OFF_DSA_K = 3328
OFF_DSA_V = 3584
OFF_GDN_AB = 3840
OFF_DSA_IKW = 3968

NN = (((1,), (0,)), ((), ()))
NT = (((1,), (1,)), ((), ()))
BNN = (((2,), (1,)), ((0,), (0,)))
BNT = (((2,), (2,)), ((0,), (0,)))
BTN = (((1,), (1,)), ((0,), (0,)))


def _cparams(n_axes):
    return pltpu.CompilerParams(dimension_semantics=("arbitrary",) * n_axes,
                                vmem_limit_bytes=VMEM_LIMIT_BYTES)


def _mm(a, b, dims=NN):
    return lax.dot_general(a.astype(BF16), b.astype(BF16), dims, preferred_element_type=F32)


def _mm_exact_lhs(m, x, dims=NN):
    mb = m.astype(BF16)
    x0 = x.astype(BF16)
    r1 = x - x0.astype(F32)
    x1 = r1.astype(BF16)
    x2 = (r1 - x1.astype(F32)).astype(BF16)
    d = lambda y: lax.dot_general(mb, y, dims, preferred_element_type=F32)
    return d(x0) + (d(x1) + d(x2))


def _mm_exact_rhs(x, m, dims=NN):
    mb = m.astype(BF16)
    x0 = x.astype(BF16)
    r1 = x - x0.astype(F32)
    x1 = r1.astype(BF16)
    x2 = (r1 - x1.astype(F32)).astype(BF16)
    d = lambda y: lax.dot_general(y, mb, dims, preferred_element_type=F32)
    return d(x0) + (d(x1) + d(x2))


def _iota2(shape, dim):
    return lax.broadcasted_iota(I32, shape, dim)


def _seg_ones(width, seg):
    sh = int(math.log2(seg))
    r = _iota2((width, width), 0) >> sh
    c = _iota2((width, width), 1) >> sh
    return jnp.where(r == c, 1.0, 0.0).astype(F32)


def _seg_sum(x, seg_ones):
    return _mm_exact_rhs(x, seg_ones)


def _softplus(x):
    return jnp.maximum(x, 0.0) + jnp.log1p(jnp.exp(-jnp.abs(x)))


def _sigmoid(x):
    return 0.5 * jnp.tanh(0.5 * x) + 0.5


def _silu(x):
    return x * _sigmoid(x)


def _tri_masks(n):
    r = _iota2((n, n), 0)
    c = _iota2((n, n), 1)
    return r, c


def _unit_lower_inverse(m, r, c, dims=NN):
    eye = jnp.where(r == c, 1.0, 0.0).astype(F32)
    same16 = (r >> 4) == (c >> 4)
    same32 = (r >> 5) == (c >> 5)
    md = jnp.where(same16, m, 0.0)
    t = eye - md
    p = md
    for _ in range(3):
        p = _mm(p, p, dims)
        t = t + _mm(t, p, dims)
    c32 = jnp.where(same32 & jnp.logical_not(same16), m, 0.0)
    t = t - _mm(_mm(t, c32, dims), t, dims)
    c64 = jnp.where(same32, 0.0, m)
    t = t - _mm(_mm(t, c64, dims), t, dims)
    return t


IN_PROJ_TN = 512


def _in_proj_kernel(x_ref, g_ref, w_ref, o_ref):
    x = x_ref[...]
    ms = jnp.mean(x * x, axis=-1, keepdims=True)
    h = (x * lax.rsqrt(ms + NORM_EPS) * g_ref[...]).astype(BF16)
    for n0 in range(0, o_ref.shape[1], IN_PROJ_TN):
        o_ref[:, n0:n0 + IN_PROJ_TN] = jnp.dot(
            h, w_ref[:, n0:n0 + IN_PROJ_TN], preferred_element_type=F32).astype(o_ref.dtype)


def _in_proj(x2d, g, w, out_dtype):
    t, d = x2d.shape
    n = w.shape[1]
    tm = min(512, t)
    return pl.pallas_call(
        _in_proj_kernel,
        grid=(t // tm,),
        in_specs=[pl.BlockSpec((tm, d), lambda i: (i, 0)),
                  pl.BlockSpec((1, d), lambda i: (0, 0)),
                  pl.BlockSpec((d, n), lambda i: (0, 0))],
        out_specs=pl.BlockSpec((tm, n), lambda i: (i, 0)),
        out_shape=jax.ShapeDtypeStruct((t, n), out_dtype),
        compiler_params=_cparams(1),
        name="in_proj",
    )(x2d, g, w)


def _to_groups(x, nchunk):
    parts = [x[:, h * DH:(h + 1) * DH].reshape(nchunk, 1, CHUNK, DH) for h in range(HEADS)]
    return jnp.concatenate(parts, axis=1).reshape(nchunk * HEADS, CHUNK, DH)


def _col_groups(x, lane0, nchunk):
    parts = [x[:, lane0 + h:lane0 + h + 1].reshape(nchunk, 1, CHUNK, 1) for h in range(HEADS)]
    return jnp.concatenate(parts, axis=1).reshape(nchunk * HEADS, CHUNK, 1)


def _gdn_kernel(q_ref, k_ref, v_ref, z_ref, ab_ref, abr_ref, cw_ref, pcol_ref, prow_ref, ng_ref,
                o_ref, xbuf_ref, state_ref, u_ref, w_ref, a_ref, qd_ref, kd_ref, egl_ref, oraw_ref,
                carry_ref, *, tb, nbatch):
    nchunk = tb // CHUNK
    bi = pl.program_id(1)
    xbuf = xbuf_ref

    @pl.when(pl.program_id(0) == 0)
    def _():
        carry_ref[bi] = jnp.zeros((8, 3 * HW), F32)
        state_ref[bi] = jnp.zeros((HEADS, DH, DH), F32)

    xbuf[0:8, :] = carry_ref[bi]
    xbuf[8:8 + tb, 0:HW] = q_ref[...]
    xbuf[8:8 + tb, HW:2 * HW] = k_ref[...]
    xbuf[8:8 + tb, 2 * HW:3 * HW] = v_ref[...]
    cw = cw_ref[...]
    y = xbuf[8:8 + tb, :] * cw[3:4, :]
    y = y + xbuf[7:7 + tb, :] * cw[2:3, :]
    y = y + xbuf[6:6 + tb, :] * cw[1:2, :]
    y = y + xbuf[5:5 + tb, :] * cw[0:1, :]
    carry_ref[bi] = xbuf[tb:tb + 8, :]
    y = _silu(y)
    seg = _seg_ones(HW, DH)
    q = y[:, 0:HW]
    k = y[:, HW:2 * HW]
    q = q * lax.rsqrt(_seg_sum(q * q, seg) + 1e-6) * (DH ** -0.5)
    k = k * lax.rsqrt(_seg_sum(k * k, seg) + 1e-6)
    v = y[:, 2 * HW:3 * HW]

    ab = ab_ref[...]
    pcol = pcol_ref[...]
    g_col = -jnp.exp(pcol[0:1, :]) * _softplus(ab + pcol[1:2, :])
    beta_col = _sigmoid(ab)

    r, c = _tri_masks(CHUNK)
    incl = c <= r
    strict = c < r
    l_incl = jnp.where(incl, 1.0, 0.0).astype(F32)

    rb = _iota2((tb, tb), 0)
    cb = _iota2((tb, tb), 1)
    l_blk = jnp.where(((rb >> 6) == (cb >> 6)) & (cb <= rb), 1.0, 0.0).astype(F32)
    gc_col = _mm_exact_lhs(l_blk, g_col)
    prow = prow_ref[...]
    abr = abr_ref[0].reshape(nchunk * 8, CHUNK)
    prow_t = jnp.concatenate([prow] * nchunk, axis=0)
    g_row = -jnp.exp(prow_t[:, 0:1]) * _softplus(abr + prow_t[:, 1:2])
    gc_row = _mm_exact_rhs(g_row, l_incl, NT).reshape(nchunk, 8, CHUNK)

    qg, kg, vg = _to_groups(q, nchunk), _to_groups(k, nchunk), _to_groups(v, nchunk)
    beta = _col_groups(beta_col, 4, nchunk)
    gcol = _col_groups(gc_col, 0, nchunk)
    grow = jnp.concatenate([gc_row[:, h:h + 1, :].reshape(nchunk, 1, 1, CHUNK) for h in range(HEADS)],
                           axis=1).reshape(nchunk * HEADS, 1, CHUNK)
    decay = jnp.where(incl, jnp.exp(jnp.where(incl, gcol - grow, 0.0)), 0.0)
    kb = kg * beta
    m = jnp.where(strict, _mm(kb, kg, BNT) * decay, 0.0)
    t_inv = _unit_lower_inverse(m, r, c, BNN)
    egc = jnp.exp(gcol)
    glast = gcol[:, CHUNK - 1:CHUNK, :]
    uw = _mm(t_inv, jnp.concatenate([vg * beta, kb * egc], axis=2), BNN)
    u_ref[bi] = uw[:, :, 0:DH]
    w_ref[bi] = uw[:, :, DH:2 * DH].astype(BF16)
    a_ref[bi] = (_mm(qg, kg, BNT) * decay).astype(BF16)
    qd_ref[bi] = (qg * egc).astype(BF16)
    kd_ref[bi] = (kg * jnp.exp(glast - gcol)).astype(BF16)
    egl_ref[bi] = jnp.broadcast_to(jnp.exp(glast), (nchunk * HEADS, 1, DH))

    @pl.when(bi == nbatch - 1)
    def _():
        nc = nbatch * HEADS

        def chunk_body(ci, carry):
            sl = pl.ds(pl.multiple_of(ci * HEADS, HEADS), HEADS)
            take = lambda ref: ref[:, sl].reshape((nc,) + ref.shape[2:])
            s = state_ref[...].reshape(nc, DH, DH)
            v_new = take(u_ref) - _mm(take(w_ref), s, BNN)
            o = _mm(take(qd_ref), s, BNN) + _mm(take(a_ref), v_new, BNN)
            s = s * take(egl_ref) + _mm(take(kd_ref), v_new, BTN)
            state_ref[...] = s.reshape(nbatch, HEADS, DH, DH)
            rows = pl.ds(pl.multiple_of(ci * CHUNK, CHUNK), CHUNK)
            for bb in range(nbatch):
                oraw_ref[bb, rows, :] = jnp.concatenate([o[bb * HEADS + h] for h in range(HEADS)], axis=1)
            return carry

        lax.fori_loop(0, nchunk, chunk_body, 0)
        o_all = oraw_ref[...].reshape(nbatch * tb, HW)
        ms = _seg_sum(o_all * o_all, seg) * (1.0 / DH)
        o_n = o_all * lax.rsqrt(ms + NORM_EPS) * ng_ref[...] * _silu(z_ref[...].reshape(nbatch * tb, HW))
        o_ref[...] = o_n.reshape(nbatch, tb, HW).astype(o_ref.dtype)


GDN_TB = 512


def _gdn(slab, abr, conv_w, pcol, prow, norm_g, b, s):
    tb = min(GDN_TB, s)
    nb = s // tb
    nchunk = tb // CHUNK
    ng = nchunk * HEADS
    grp = lambda w, dt=BF16: pltpu.VMEM((b, ng, CHUNK, w), dt)
    col = lambda off, w: pl.BlockSpec((tb, w), lambda j, i, o=off // w: (i * nb + j, o))
    full = lambda shape: pl.BlockSpec(shape, lambda j, i: (0,) * len(shape))
    slab3 = slab.reshape(b, s, slab.shape[1])
    out = pl.pallas_call(
        functools.partial(_gdn_kernel, tb=tb, nbatch=b),
        grid=(nb, b),
        in_specs=[col(OFF_GDN_Q, HW), col(OFF_GDN_K, HW), col(OFF_GDN_V, HW),
                  pl.BlockSpec((b, tb, HW), lambda j, i: (0, j, OFF_GDN_Z // HW)),
                  col(OFF_GDN_AB, 128),
                  pl.BlockSpec((1, nchunk, 8, CHUNK), lambda j, i: (i, j, 0, 0)),
                  full((GDN_CONV, 3 * HW)), full((8, 128)), full((8, 128)), full((1, HW))],
        out_specs=pl.BlockSpec((b, tb, HW), lambda j, i: (0, j, 0)),
        out_shape=jax.ShapeDtypeStruct((b, s, HW), BF16),
        scratch_shapes=[pltpu.VMEM((tb + 8, 3 * HW), F32), pltpu.VMEM((b, HEADS, DH, DH), F32),
                        grp(DH, F32), grp(DH), grp(CHUNK), grp(DH), grp(DH),
                        pltpu.VMEM((b, ng, 1, DH), F32), pltpu.VMEM((b, tb, HW), F32),
                        pltpu.VMEM((b, 8, 3 * HW), F32)],
        compiler_params=_cparams(2),
        name="gdn",
    )(slab, slab, slab, slab3, slab, abr, conv_w, pcol, prow, norm_g)
    return out.reshape(b * s, HW)


def _rwkv_kernel(p_ref, mu_ref, vec_ref, ww2_ref, wa2_ref, wg2_ref, o_ref,
                 xbuf_ref, state_ref, ta_ref, tpv_ref, rt_ref, qkv_ref, qb_ref, bc_ref, ktv_ref,
                 el_ref, oraw_ref, carry_ref, post_ref, *, tb, nbatch):
    nchunk = tb // CHUNK
    bi = pl.program_id(1)

    @pl.when(pl.program_id(0) == 0)
    def _():
        carry_ref[bi] = jnp.zeros((8, 4 * HW), F32)
        state_ref[bi] = jnp.zeros((HEADS, DH, DH), F32)

    p = p_ref[...]
    xbuf_ref[0:8, :] = carry_ref[bi]
    xbuf_ref[8:8 + tb, :] = p
    prev = xbuf_ref[7:7 + tb, :]
    carry_ref[bi] = xbuf_ref[tb:tb + 8, :]
    pm = p + (prev - p) * mu_ref[...]
    vec = vec_ref[...]
    w0, a0, k_k, k_a = vec[0:1, :], vec[1:2, :], vec[2:3, :], vec[3:4, :]
    rr = pm[:, 0:HW]
    k = pm[:, HW:2 * HW]
    v = pm[:, 2 * HW:3 * HW]
    xw = pm[:, 768:832]
    xa = pm[:, 832:896]
    xg = pm[:, 896:1024]
    w_log = -_softplus(-(w0 + _mm(jnp.tanh(xw), ww2_ref[...]))) - 0.5
    lw = -jnp.exp(w_log)
    a = _sigmoid(a0 + _mm(xa, wa2_ref[...]))
    g = _mm(_sigmoid(xg), wg2_ref[...])
    seg = _seg_ones(HW, DH)
    kk = k * k_k
    kk = kk * lax.rsqrt(_seg_sum(kk * kk, seg) + 1e-6)
    k2 = k * (1.0 + (a - 1.0) * k_a)
    kka = kk * a

    r_i, c_i = _tri_masks(CHUNK)
    incl = c_i <= r_i
    strict = c_i < r_i
    rb = _iota2((tb, tb), 0)
    cb = _iota2((tb, tb), 1)
    l_blk = jnp.where(((rb >> 6) == (cb >> 6)) & (cb <= rb), 1.0, 0.0).astype(F32)
    lwc_all = _mm_exact_lhs(l_blk, lw)

    rg, kg, vg = _to_groups(rr, nchunk), _to_groups(k2, nchunk), _to_groups(v, nchunk)
    kkg, kkag = _to_groups(kk, nchunk), _to_groups(kka, nchunk)
    lwg, lwc = _to_groups(lw, nchunk), _to_groups(lwc_all, nchunk)
    lwl = lwc[:, CHUNK - 1:CHUNK, :]
    e_neg = jnp.exp(-lwc)
    at = kkg * jnp.exp(lwc - lwg)
    rt = rg * jnp.exp(lwc)
    y2 = jnp.concatenate([at, rt], axis=1)
    x2 = jnp.concatenate([kkag * e_neg, kg * e_neg], axis=1)
    gram = _mm(y2, x2, BNT)
    n_m = jnp.where(strict, gram[:, 0:CHUNK, 0:CHUNK], 0.0)
    p_m = jnp.where(strict, gram[:, 0:CHUNK, CHUNK:], 0.0)
    qb_m = jnp.where(incl, gram[:, CHUNK:, 0:CHUNK], 0.0)
    qk_m = jnp.where(incl, gram[:, CHUNK:, CHUNK:], 0.0)
    t_inv = _unit_lower_inverse(n_m, r_i, c_i, BNN)
    ta_ref[bi] = _mm(t_inv, at, BNN).astype(BF16)
    tpv_ref[bi] = _mm(t_inv, _mm(p_m, vg, BNN), BNN)
    rt_ref[bi] = rt.astype(BF16)
    qkv_ref[bi] = _mm(qk_m, vg, BNN)
    qb_ref[bi] = qb_m.astype(BF16)
    e_rel = jnp.exp(lwl - lwc)
    bc_ref[bi] = (kkag * e_rel).astype(BF16)
    ktv_ref[bi] = _mm(vg, kg * e_rel, BTN)
    el_ref[bi] = jnp.exp(lwl)
    ln_w, ln_b, r_k = vec[4:5, :], vec[5:6, :], vec[6:7, :]
    post_ref[bi, :, 0:HW] = _seg_sum(rr * k2 * r_k, seg) * v
    post_ref[bi, :, HW:2 * HW] = g

    @pl.when(bi == nbatch - 1)
    def _():
        nc = nbatch * HEADS

        def chunk_body(ci, carry):
            sl = pl.ds(pl.multiple_of(ci * HEADS, HEADS), HEADS)
            take = lambda ref: ref[:, sl].reshape((nc,) + ref.shape[2:])
            zt = state_ref[...].reshape(nc, DH, DH)
            u = _mm(take(ta_ref), zt, BNT) + take(tpv_ref)
            o = _mm(take(rt_ref), zt, BNT) + take(qkv_ref) - _mm(take(qb_ref), u, BNN)
            zt = zt * take(el_ref) + take(ktv_ref) - _mm(u, take(bc_ref), BTN)
            state_ref[...] = zt.reshape(nbatch, HEADS, DH, DH)
            rows = pl.ds(pl.multiple_of(ci * CHUNK, CHUNK), CHUNK)
            for bb in range(nbatch):
                oraw_ref[bb, rows, :] = jnp.concatenate([o[bb * HEADS + h] for h in range(HEADS)], axis=1)
            return carry

        lax.fori_loop(0, nchunk, chunk_body, 0)
        o_all = oraw_ref[...].reshape(nbatch * tb, HW)
        mean = _seg_sum(o_all, seg) * (1.0 / DH)
        cen = o_all - mean
        var = _seg_sum(cen * cen, seg) * (1.0 / DH)
        o_n = cen * lax.rsqrt(var + RWKV_GN_EPS) * ln_w + ln_b
        post = post_ref[...].reshape(nbatch * tb, 2 * HW)
        o_ref[...] = ((o_n + post[:, 0:HW]) * post[:, HW:2 * HW]).reshape(nbatch, tb, HW).astype(o_ref.dtype)


RWKV_TB = 512


def _rwkv(slab, mu, vec, ww2, wa2, wg2, b, s):
    tb = min(RWKV_TB, s)
    nb = s // tb
    ng = (tb // CHUNK) * HEADS
    grp = lambda dt=F32: pltpu.VMEM((b, ng, CHUNK, DH), dt)
    full = lambda shape: pl.BlockSpec(shape, lambda j, i: (0,) * len(shape))
    out = pl.pallas_call(
        functools.partial(_rwkv_kernel, tb=tb, nbatch=b),
        grid=(nb, b),
        in_specs=[pl.BlockSpec((tb, 4 * HW), lambda j, i: (i * nb + j, OFF_RWKV // (4 * HW))),
                  full((1, 4 * HW)), full((8, HW)), full((64, HW)), full((64, HW)), full((128, HW))],
        out_specs=pl.BlockSpec((b, tb, HW), lambda j, i: (0, j, 0)),
        out_shape=jax.ShapeDtypeStruct((b, s, HW), BF16),
        scratch_shapes=[pltpu.VMEM((tb + 8, 4 * HW), F32), pltpu.VMEM((b, HEADS, DH, DH), F32),
                        grp(BF16), grp(), grp(BF16), grp(), grp(BF16), grp(BF16), grp(),
                        pltpu.VMEM((b, ng, 1, DH), F32), pltpu.VMEM((b, tb, HW), F32),
                        pltpu.VMEM((b, 8, 4 * HW), F32), pltpu.VMEM((b, tb, 2 * HW), F32)],
        compiler_params=_cparams(2),
        name="rwkv7",
    )(slab, mu, vec, ww2, wa2, wg2)
    return out.reshape(b * s, HW)


def _rope(x, cos_t, sin_lo, sin_hi, half):
    w = x.shape[-1]
    return x * cos_t + pltpu.roll(x, w - half, 1) * sin_lo + pltpu.roll(x, half, 1) * sin_hi


def _tile_lanes(t, n):
    return t if n == 1 else jnp.concatenate([t] * n, axis=1)


def _mla_prep_kernel(cq_ref, ckv_ref, kr_ref, cs_ref, e_ref, base_ref, qg_ref, kvg_ref, wq_ref, wk_ref,
                     wvt_ref, q_ref, k_ref, vt_ref):
    cq = cq_ref[...]
    hq = cq * lax.rsqrt(jnp.mean(cq * cq, axis=-1, keepdims=True) + NORM_EPS) * qg_ref[...]
    ckv = ckv_ref[...]
    hkv = ckv * lax.rsqrt(jnp.mean(ckv * ckv, axis=-1, keepdims=True) + NORM_EPS) * kvg_ref[...]
    cos_t, sin_lo, sin_hi = _rope_table(cs_ref, e_ref, base_ref)
    q = _mm(hq, wq_ref[...])
    q = _rope(q, _tile_lanes(cos_t, HEADS), _tile_lanes(sin_lo, HEADS), _tile_lanes(sin_hi, HEADS),
              MLA_ROPE // 2)
    q_ref[...] = (q * ((MLA_NOPE + MLA_ROPE) ** -0.5 * LOG2E)).astype(BF16)
    kr = _rope(kr_ref[...], cos_t, sin_lo, sin_hi, MLA_ROPE // 2)
    k = _mm(hkv, wk_ref[...]) + _tile_lanes(kr, HEADS)
    k_ref[...] = k.astype(BF16)
    vt_ref[...] = lax.dot_general(wvt_ref[...], hkv.astype(BF16), NT,
                                  preferred_element_type=F32).astype(BF16)


def _mla_prep(slab, cs, qg, kvg, wq, wk, wvt):
    t = slab.shape[0]
    tm = min(512, t)
    full = lambda shape: pl.BlockSpec(shape, lambda i: (0,) * len(shape))
    e, base = _rope_expansion(MLA_ROPE, MLA_NOPE, MLA_QK)
    return pl.pallas_call(
        _mla_prep_kernel,
        grid=(t // tm,),
        in_specs=[pl.BlockSpec((tm, 256), lambda i: (i, OFF_MLA_CQ // 256)),
                  pl.BlockSpec((tm, 128), lambda i: (i, OFF_MLA_CKV // 128)),
                  pl.BlockSpec((tm, 128), lambda i: (i, OFF_MLA_KR // 128)),
                  pl.BlockSpec((tm, MLA_ROPE), lambda i: (i, 0)), full(e.shape), full(base.shape),
                  full((1, 256)), full((1, 128)), full((256, 512)), full((128, 512)), full((256, 128))],
        out_specs=[pl.BlockSpec((tm, 512), lambda i: (i, 0)), pl.BlockSpec((tm, 512), lambda i: (i, 0)),
                   pl.BlockSpec((256, tm), lambda i: (0, i))],
        out_shape=[jax.ShapeDtypeStruct((t, 512), BF16), jax.ShapeDtypeStruct((t, 512), BF16),
                   jax.ShapeDtypeStruct((256, t), BF16)],
        compiler_params=_cparams(1),
        name="mla_prep",
    )(slab, slab, slab, cs, e, base, qg, kvg, wq, wk, wvt)


def _flash_kernel(qi_tab, ki_tab, q_ref, k_ref, vt_ref, o_ref, m_ref, l_ref, acc_ref, *, tq, tk):
    qi = qi_tab[pl.program_id(1)]
    ki = ki_tab[pl.program_id(1)]

    @pl.when(ki == 0)
    def _():
        m_ref[...] = jnp.full_like(m_ref, NEG_INF)
        l_ref[...] = jnp.zeros_like(l_ref)
        acc_ref[...] = jnp.zeros_like(acc_ref)

    def step(on_diagonal):
        if on_diagonal:
            causal = _iota2((tk, tq), 0) <= _iota2((tk, tq), 1)
        vt = vt_ref[...]
        ones = jnp.ones((16, tk), BF16)
        scores = [lax.dot_general(k_ref[:, h * MLA_QK:(h + 1) * MLA_QK], q_ref[:, h * MLA_QK:(h + 1) * MLA_QK],
                                  NT, preferred_element_type=F32) for h in range(HEADS)]
        for h in range(HEADS):
            s = scores[h]
            if on_diagonal:
                s = jnp.where(causal, s, NEG_INF)
            m_old = m_ref[h]
            m_new = jnp.maximum(m_old, jnp.max(s, axis=0, keepdims=True))
            alpha = jnp.exp2(m_old - m_new)
            p = jnp.exp2(s - m_new).astype(BF16)
            pv = jnp.dot(jnp.concatenate([vt[h * DH:(h + 1) * DH, :], ones], axis=0), p,
                         preferred_element_type=F32)
            l_ref[h] = alpha * l_ref[h] + pv[DH:DH + 1, :]
            acc_ref[h * DH:(h + 1) * DH, :] = alpha * acc_ref[h * DH:(h + 1) * DH, :] + pv[0:DH, :]
            m_ref[h] = m_new

    pl.when(ki < qi)(lambda: step(False))

    @pl.when(ki == qi)
    def _():
        step(True)
        o_t = jnp.concatenate([acc_ref[h * DH:(h + 1) * DH, :] / l_ref[h] for h in range(HEADS)], axis=0)
        o_ref[...] = o_t.T.astype(o_ref.dtype)


def _flash(q, k, vt, b, s):
    tq = min(512, s)
    tk = tq
    nq = s // tq
    pairs = [(qb, kb) for qb in range(nq) for kb in range(qb + 1)]
    qi_tab = jnp.asarray([p[0] for p in pairs], I32)
    ki_tab = jnp.asarray([p[1] for p in pairs], I32)
    return pl.pallas_call(
        functools.partial(_flash_kernel, tq=tq, tk=tk),
        grid_spec=pltpu.PrefetchScalarGridSpec(
            num_scalar_prefetch=2,
            grid=(b, len(pairs)),
            in_specs=[pl.BlockSpec((tq, 512), lambda i, p, qt, kt: (i * nq + qt[p], 0)),
                      pl.BlockSpec((tk, 512), lambda i, p, qt, kt: (i * nq + kt[p], 0)),
                      pl.BlockSpec((256, tk), lambda i, p, qt, kt: (0, i * nq + kt[p]))],
            out_specs=pl.BlockSpec((tq, 256), lambda i, p, qt, kt: (i * nq + qt[p], 0)),
            scratch_shapes=[pltpu.VMEM((HEADS, 1, tq), F32), pltpu.VMEM((HEADS, 1, tq), F32),
                            pltpu.VMEM((HW, tq), F32)]),
        out_shape=jax.ShapeDtypeStruct((b * s, 256), BF16),
        compiler_params=_cparams(2),
        name="mla_flash",
    )(qi_tab, ki_tab, q, k, vt)


DSA_TQ = 256
DSA_TK = 512


def _dsa_prep_kernel(iq_ref, ikw_ref, q_ref, k_ref, v_ref, cs_ref, e_ref, base_ref,
                     ik_ref, ko_ref, vt_ref, iqt_ref, qt_ref, iwt_ref):
    cos_t, sin_lo, sin_hi = _rope_table(cs_ref, e_ref, base_ref)
    half = DH // 8
    rope = lambda x, n: _rope(x, _tile_lanes(cos_t, n), _tile_lanes(sin_lo, n), _tile_lanes(sin_hi, n), half)
    ikw = ikw_ref[...]
    ik_ref[...] = rope(ikw, 1)[:, 0:DH].astype(BF16)
    kr = rope(k_ref[...], 2)
    for h in range(HEADS):
        ko_ref[h] = kr[:, h * DH:(h + 1) * DH].astype(BF16)
    vt_ref[0] = v_ref[...].T.astype(BF16)
    iqt_ref[...] = rope(iq_ref[...], 4).T.astype(BF16)
    qt_ref[...] = (rope(q_ref[...], 2) * (DH ** -0.5 * LOG2E)).T.astype(BF16)
    iwt_ref[...] = ikw.T[DH:DH + 8, :] * (IDX_HEADS ** -0.5 * DH ** -0.5)


def _dsa_prep(slab, cs):
    t = slab.shape[0]
    tm = min(DSA_TK, t)
    full = lambda shape: pl.BlockSpec(shape, lambda i: (0,) * len(shape))
    e, base = _rope_expansion(DH // 4, 0, DH)
    return pl.pallas_call(
        _dsa_prep_kernel,
        grid=(t // tm,),
        in_specs=[pl.BlockSpec((tm, IDX_HEADS * DH), lambda i: (i, OFF_DSA_IQ // (IDX_HEADS * DH))),
                  pl.BlockSpec((tm, 128), lambda i: (i, OFF_DSA_IKW // 128)),
                  pl.BlockSpec((tm, 256), lambda i: (i, OFF_DSA_Q // 256)),
                  pl.BlockSpec((tm, 256), lambda i: (i, OFF_DSA_K // 256)),
                  pl.BlockSpec((tm, 256), lambda i: (i, OFF_DSA_V // 256)),
                  pl.BlockSpec((tm, DH // 4), lambda i: (i, 0)), full(e.shape), full(base.shape)],
        out_specs=[pl.BlockSpec((tm, DH), lambda i: (i, 0)),
                   pl.BlockSpec((HEADS, tm, DH), lambda i: (0, i, 0)),
                   pl.BlockSpec((1, HW, tm), lambda i: (i, 0, 0)),
                   pl.BlockSpec((IDX_HEADS * DH, tm), lambda i: (0, i)),
                   pl.BlockSpec((HW, tm), lambda i: (0, i)),
                   pl.BlockSpec((8, tm), lambda i: (0, i))],
        out_shape=[jax.ShapeDtypeStruct((t, DH), BF16), jax.ShapeDtypeStruct((HEADS, t, DH), BF16),
                   jax.ShapeDtypeStruct((t // tm, HW, tm), BF16),
                   jax.ShapeDtypeStruct((IDX_HEADS * DH, t), BF16), jax.ShapeDtypeStruct((HW, t), BF16),
                   jax.ShapeDtypeStruct((8, t), F32)],
        compiler_params=_cparams(1),
        name="dsa_prep",
    )(slab, slab, slab, slab, slab, cs, e, base)


def _dsa_kernel(iqt_ref, iwt_ref, qt_ref, ik_ref, k_ref, vt_ref, kidx_ref, o_ref,
                hi_ref, lo_ref, lom_ref, bias_ref, acc_ref, s_ref, m_ref, l_ref,
                *, tq, tk, n_sel, idx_bits):
    qi = pl.program_id(1)
    nkt = (qi * tq + tq - 1) // tk + 1
    iqt = iqt_ref[...]
    iw = iwt_ref[...]
    q_pos = qi * tq + _iota2((tk, tq), 1)
    qt = qt_ref[...]

    def score_body(kt, carry):
        row0 = pl.multiple_of(kt * tk, tk)
        ik = ik_ref[pl.ds(row0, tk), :]
        sc = jnp.zeros((tk, tq), F32)
        for h in range(IDX_HEADS):
            lg = lax.dot_general(ik, iqt[h * DH:(h + 1) * DH, :], NN, preferred_element_type=F32)
            sc = sc + jnp.maximum(lg, 0.0) * iw[h:h + 1, :]
        bits = lax.bitcast_convert_type(sc, I32)
        key = bits ^ ((bits >> 31) & 0x7FFFFFFF)
        key = jnp.where(sc == 0.0, 0, key)
        k_pos = row0 + _iota2((tk, tq), 0)
        key = jnp.where(k_pos <= q_pos, key, INT_MIN)
        hi_ref[pl.ds(row0, tk), :] = (key >> 16).astype(I16)
        lo_ref[pl.ds(row0, tk), :] = ((key & 0xFFFF) - 32768).astype(I16)
        for h in range(HEADS):
            s_ref[h, pl.ds(row0, tk), :] = lax.dot_general(k_ref[h, pl.ds(row0, tk), :],
                                                           qt[h * DH:(h + 1) * DH, :], NN,
                                                           preferred_element_type=F32)
        return carry

    lax.fori_loop(0, nkt, score_body, 0)

    one16 = jnp.ones((tk, tq), I16)
    zero16 = jnp.zeros((tk, tq), I16)

    def count(*pred_fns):
        def body(kt, accs):
            sl = pl.ds(pl.multiple_of(kt * tk, tk), tk)
            args = (hi_ref[sl, :], lo_ref[sl, :], lom_ref[sl, :], kidx_ref[sl, :])
            out = []
            for acc, fn in zip(accs, pred_fns):
                ones = jnp.where(fn(*args), one16, zero16).reshape(tk // 64, 64, tq)
                for g in range(tk // 64):
                    acc = acc + ones[g]
                out.append(acc)
            return tuple(out)
        parts = lax.fori_loop(0, nkt, body, tuple(jnp.zeros((64, tq), I16) for _ in pred_fns))
        sums = tuple(jnp.sum(p.astype(I32).astype(F32), axis=0, keepdims=True) for p in parts)
        return sums[0] if len(sums) == 1 else sums

    def bisect16(pred_ge):
        def body(it, thr):
            bit = jnp.maximum(15 - it, 0)
            cand = jnp.where(it == 0, jnp.zeros_like(thr), thr | jnp.left_shift(jnp.int32(1), bit))
            c16 = cand.astype(I16)
            cnt = count(lambda hi, lo, lom, idx: pred_ge(hi, lo, lom, c16))
            return jnp.where(cnt >= float(n_sel), cand, thr)
        return lax.fori_loop(0, 16, body, jnp.full((1, tq), -32768, I32))

    t_hi = bisect16(lambda hi, lo, lom, c: hi >= c)
    t_hi16 = t_hi.astype(I16)

    def mask_lo_body(kt, carry):
        sl = pl.ds(pl.multiple_of(kt * tk, tk), tk)
        hi = hi_ref[sl, :]
        lom_ref[sl, :] = jnp.where(hi > t_hi16, jnp.int16(32767),
                                   jnp.where(hi == t_hi16, lo_ref[sl, :], jnp.int16(-32768)))
        return carry

    lax.fori_loop(0, nkt, mask_lo_body, 0)
    t_lo16 = bisect16(lambda hi, lo, lom, c: lom >= c).astype(I16)

    def is_gt(hi, lo):
        return (hi > t_hi16) | ((hi == t_hi16) & (lo > t_lo16))

    def is_eq(hi, lo):
        return (hi == t_hi16) & (lo == t_lo16)

    def write_bias(sl, hi, sel_eq, gt):
        sel = (gt | sel_eq) & (hi != jnp.int16(-32768))
        bias_ref[sl, :] = jnp.where(sel, jnp.zeros((tk, tq), BF16),
                                    jnp.full((tk, tq), NEG_INF, BF16)).astype(F32)

    def tally_body(kt, accs):
        sl = pl.ds(pl.multiple_of(kt * tk, tk), tk)
        hi, lo = hi_ref[sl, :], lo_ref[sl, :]
        gt, eq = is_gt(hi, lo), is_eq(hi, lo)
        write_bias(sl, hi, eq, gt)
        out = []
        for acc, m in zip(accs, (gt, eq)):
            ones = jnp.where(m, one16, zero16).reshape(tk // 64, 64, tq)
            for g in range(tk // 64):
                acc = acc + ones[g]
            out.append(acc)
        return tuple(out)

    parts = lax.fori_loop(0, nkt, tally_body, (jnp.zeros((64, tq), I16), jnp.zeros((64, tq), I16)))
    cnt_gt, cnt_eq = (jnp.sum(p.astype(I32).astype(F32), axis=0, keepdims=True) for p in parts)
    need = float(n_sel) - cnt_gt

    @pl.when(jnp.max(jnp.where(cnt_eq > need, 1.0, 0.0)) > 0.0)
    def _():
        def idx_body(it, j):
            cand = j + jnp.left_shift(jnp.int32(1), idx_bits - 1 - it)
            c16 = cand.astype(I16)
            cnt = count(lambda hi, lo, lom, idx: is_eq(hi, lo) & (idx < c16))
            return jnp.where(cnt < need, cand, j)

        j_cut16 = lax.fori_loop(0, idx_bits, idx_body, jnp.zeros((1, tq), I32)).astype(I16)

        def rewrite_body(kt, carry):
            sl = pl.ds(pl.multiple_of(kt * tk, tk), tk)
            hi, lo = hi_ref[sl, :], lo_ref[sl, :]
            write_bias(sl, hi, is_eq(hi, lo) & (kidx_ref[sl, :] <= j_cut16), is_gt(hi, lo))
            return carry

        lax.fori_loop(0, nkt, rewrite_body, 0)

    acc_ref[...] = jnp.zeros_like(acc_ref)

    def att_body(kt, carry):
        sl = pl.ds(pl.multiple_of(kt * tk, tk), tk)
        bias = bias_ref[sl, :]
        ones = jnp.ones((16, tk), BF16)
        for h in range(HEADS):
            hs = slice(h * DH, (h + 1) * DH)
            s = s_ref[h, sl, :] + bias
            m_old = m_ref[h]
            m_new = jnp.maximum(m_old, jnp.max(s, axis=0, keepdims=True))
            alpha = jnp.exp2(m_old - m_new)
            p = jnp.exp2(s - m_new).astype(BF16)
            pv = lax.dot_general(jnp.concatenate([vt_ref[kt, hs, :], ones], axis=0), p, NN,
                                 preferred_element_type=F32)
            acc_ref[hs, :] = alpha * acc_ref[hs, :] + pv[0:DH, :]
            m_ref[h] = m_new
            l_ref[h] = alpha * l_ref[h] + pv[DH:DH + 1, :]
        return carry

    m_ref[...] = jnp.full_like(m_ref, NEG_INF)
    l_ref[...] = jnp.zeros_like(l_ref)
    lax.fori_loop(0, nkt, att_body, 0)
    o_t = jnp.concatenate([acc_ref[h * DH:(h + 1) * DH, :] / l_ref[h] for h in range(HEADS)], axis=0)
    o_ref[...] = o_t.T.astype(o_ref.dtype)


def _dsa(iqt, iwt, qt, ik, kr, vt, b, s, n_sel):
    tq = min(DSA_TQ, s)
    tk = min(DSA_TK, s)
    nq = s // tq
    nkt = s // tk
    return pl.pallas_call(
        functools.partial(_dsa_kernel, tq=tq, tk=tk, n_sel=n_sel, idx_bits=int(math.log2(s))),
        grid=(b, nq),
        in_specs=[pl.BlockSpec((IDX_HEADS * DH, tq), lambda i, j: (0, i * nq + j)),
                  pl.BlockSpec((8, tq), lambda i, j: (0, i * nq + j)),
                  pl.BlockSpec((HW, tq), lambda i, j: (0, i * nq + j)),
                  pl.BlockSpec((s, DH), lambda i, j: (i, 0)),
                  pl.BlockSpec((HEADS, s, DH), lambda i, j: (0, i, 0)),
                  pl.BlockSpec((nkt, HW, tk), lambda i, j: (i, 0, 0)),
                  pl.BlockSpec((s, tq), lambda i, j: (0, 0))],
        out_specs=pl.BlockSpec((tq, HW), lambda i, j: (i * nq + j, 0)),
        out_shape=jax.ShapeDtypeStruct((b * s, HW), BF16),
        scratch_shapes=[pltpu.VMEM((s, tq), I16), pltpu.VMEM((s, tq), I16), pltpu.VMEM((s, tq), I16),
                        pltpu.VMEM((s, tq), F32), pltpu.VMEM((HW, tq), F32),
                        pltpu.VMEM((HEADS, s, tq), F32), pltpu.VMEM((HEADS, 1, tq), F32),
                        pltpu.VMEM((HEADS, 1, tq), F32)],
        compiler_params=_cparams(2),
        name="dsa",
    )(iqt, iwt, qt, ik, kr, vt,
      jnp.broadcast_to(jnp.arange(s, dtype=I16)[:, None], (s, tq)))


def _merge_kernel(x_ref, oa_ref, ob_ref, oc_ref, od_ref, ga_ref, gb_ref, gc_ref, gd_ref,
                  wa_ref, wb_ref, wc_ref, wd_ref, wo_ref, o_ref):
    def branch(o_r, g_r, w_r):
        y = jnp.dot(o_r[...].astype(BF16), w_r[...], preferred_element_type=F32)
        return _sigmoid(g_r[...].astype(F32)) * y

    merged = (branch(oa_ref, ga_ref, wa_ref) + branch(ob_ref, gb_ref, wb_ref)
              + branch(oc_ref, gc_ref, wc_ref) + branch(od_ref, gd_ref, wd_ref))
    o_ref[...] = x_ref[...] + jnp.dot(merged.astype(BF16), wo_ref[...], preferred_element_type=F32)


def _merge(x2d, oa, ob, oc, od, gates, wa, wb, wc, wd, wo):
    t, d = x2d.shape
    tm = min(512, t)
    row = lambda w: pl.BlockSpec((tm, w), lambda i: (i, 0))
    gate = lambda n: pl.BlockSpec((tm, d), lambda i, o=n: (i, o))
    full = lambda shape: pl.BlockSpec(shape, lambda i: (0,) * len(shape))
    return pl.pallas_call(
        _merge_kernel,
        grid=(t // tm,),
        in_specs=[row(d), row(HW), row(HW), row(HW), row(HW), gate(0), gate(1), gate(2), gate(3),
                  full((HW, d)), full((HW, d)), full((HW, d)), full((HW, d)), full((d, d))],
        out_specs=row(d),
        out_shape=jax.ShapeDtypeStruct((t, d), F32),
        compiler_params=_cparams(1),
        name="merge",
    )(x2d, oa, ob, oc, od, gates, gates, gates, gates, wa, wb, wc, wd, wo)


def _mlp_kernel(x_ref, g_ref, wu_ref, wd_ref, fg_ref, o_ref, h_ref, acc_ref, *, final_norm):
    fi = pl.program_id(1)

    @pl.when(fi == 0)
    def _():
        x = x_ref[...]
        ms = jnp.mean(x * x, axis=-1, keepdims=True)
        h_ref[...] = (x * lax.rsqrt(ms + NORM_EPS) * g_ref[...]).astype(BF16)
        acc_ref[...] = jnp.zeros_like(acc_ref)

    u = jnp.dot(h_ref[...], wu_ref[...], preferred_element_type=F32)
    u = jnp.square(jnp.maximum(u, 0.0))
    acc_ref[...] += jnp.dot(u.astype(BF16), wd_ref[...], preferred_element_type=F32)

    @pl.when(fi == pl.num_programs(1) - 1)
    def _():
        y = x_ref[...] + acc_ref[...]
        if final_norm:
            ms = jnp.mean(y * y, axis=-1, keepdims=True)
            y = y * lax.rsqrt(ms + NORM_EPS) * fg_ref[...]
        o_ref[...] = y


def _mlp(x2d, g, wu, wd, fg, final_norm):
    t, d = x2d.shape
    f = wu.shape[1]
    tm = min(1024, t)
    tf = 1024
    return pl.pallas_call(
        functools.partial(_mlp_kernel, final_norm=final_norm),
        grid=(t // tm, f // tf),
        in_specs=[pl.BlockSpec((tm, d), lambda i, j: (i, 0)),
                  pl.BlockSpec((1, d), lambda i, j: (0, 0)),
                  pl.BlockSpec((d, tf), lambda i, j: (0, j)),
                  pl.BlockSpec((tf, d), lambda i, j: (j, 0)),
                  pl.BlockSpec((1, d), lambda i, j: (0, 0))],
        out_specs=pl.BlockSpec((tm, d), lambda i, j: (i, 0)),
        out_shape=jax.ShapeDtypeStruct((t, d), F32),
        scratch_shapes=[pltpu.VMEM((tm, d), BF16), pltpu.VMEM((tm, d), F32)],
        compiler_params=_cparams(2),
        name="mlp",
    )(x2d, g, wu, wd, fg)


def _slab_weight(w_in, dtype=BF16):
    d = w_in.shape[0]
    z = lambda n: jnp.zeros((d, n), w_in.dtype)
    gdn, mla, rwkv, dsa = 0, 1032, 1448, 2472
    gate = 3824
    cols = [
        w_in[:, rwkv:rwkv + 1024],
        w_in[:, gdn:gdn + 768],
        w_in[:, gdn + 776:gdn + 1032],
        w_in[:, mla:mla + 256],
        w_in[:, mla + 256:mla + 384],
        z(64), w_in[:, mla + 384:mla + 416], z(32),
        w_in[:, dsa:dsa + 512],
        w_in[:, dsa + 584:dsa + 1352],
        w_in[:, gdn + 768:gdn + 776], z(120),
        w_in[:, dsa + 512:dsa + 584], z(56),
    ]
    return jnp.concatenate(cols, axis=1).astype(dtype), w_in[:, gate:gate + 4096].astype(dtype)


def _mla_weights(w_uq, w_ukv):
    wq = w_uq.reshape(MLA_Q_LORA, HEADS, MLA_NOPE + MLA_ROPE)
    wq = jnp.pad(wq, ((0, 0), (0, 0), (0, MLA_QK - MLA_NOPE - MLA_ROPE))).reshape(MLA_Q_LORA, HEADS * MLA_QK)
    wkv = w_ukv.reshape(MLA_KV_LORA, HEADS, MLA_NOPE + DH)
    wk = jnp.pad(wkv[:, :, :MLA_NOPE], ((0, 0), (0, 0), (0, MLA_QK - MLA_NOPE))).reshape(MLA_KV_LORA, HEADS * MLA_QK)
    wvt = wkv[:, :, MLA_NOPE:].reshape(MLA_KV_LORA, HW).T
    return wq.astype(BF16), wk.astype(BF16), wvt.astype(BF16)


def _rope_cos_sin(positions, rot_dims):
    inv_freq = ROPE_THETA ** (-jnp.arange(0, rot_dims, 2, dtype=F32) / rot_dims)
    ang = positions.astype(F32)[..., None] * inv_freq
    return jnp.concatenate([jnp.cos(ang), jnp.sin(ang)], axis=-1)


def _rope_expansion(rot_dims, offset, period):
    half = rot_dims // 2
    e = np.zeros((rot_dims, 3 * 128), np.float32)
    base = np.zeros((1, 3 * 128), np.float32)
    base[0, 0:128] = 1.0
    for rep in range(128 // period):
        lo = rep * period + offset
        for j in range(half):
            e[j, lo + j] = 1.0
            e[j, lo + half + j] = 1.0
            base[0, lo + j] = 0.0
            base[0, lo + half + j] = 0.0
            e[half + j, 128 + lo + j] = -1.0
            e[half + j, 256 + lo + half + j] = 1.0
    return jnp.asarray(e), jnp.asarray(base)


def _rope_table(cs_ref, e_ref, base_ref):
    tab = _mm_exact_rhs(cs_ref[...], e_ref[...]) + base_ref[...]
    return tab[:, 0:128], tab[:, 128:256], tab[:, 256:384]


def _layer(x2d, b, s, prm, tabs, final_g, is_last):
    (mix_g, w_slab, conv_w, gdn_pcol, gdn_prow, gdn_ng, mla_qg, mla_kvg, wq, wk, wv,
     rwkv_mu, rwkv_vec, ww2, wa2, wg2, wo_gdn, wo_mla, wo_rwkv, wo_dsa, w_o, mlp_g, w_up, w_down) = prm
    mla_cs, dsa_cs = tabs
    t = b * s
    slab = _in_proj(x2d, mix_g, w_slab[0], F32)
    gates = _in_proj(x2d, mix_g, w_slab[1], BF16)

    ab = slab[:, OFF_GDN_AB:OFF_GDN_AB + 8].reshape(b, s // CHUNK, CHUNK, 8)
    abr = jnp.swapaxes(ab, 2, 3)
    o_gdn = _gdn(slab, abr, conv_w, gdn_pcol, gdn_prow, gdn_ng, b, s)
    q, k, v = _mla_prep(slab, mla_cs, mla_qg, mla_kvg, wq, wk, wv)
    o_mla = _flash(q, k, v, b, s)
    o_rwkv = _rwkv(slab, rwkv_mu, rwkv_vec, ww2, wa2, wg2, b, s)
    ik, kr, vt, iqt, qt, iwt = _dsa_prep(slab, dsa_cs)
    o_dsa = _dsa(iqt, iwt, qt, ik, kr, vt, b, s, min(IDX_TOPK_MAX, s // 4))

    x2d = _merge(x2d, o_gdn, o_mla, o_rwkv, o_dsa, gates, wo_gdn, wo_mla, wo_rwkv, wo_dsa, w_o)
    return _mlp(x2d, mlp_g, w_up, w_down, final_g, is_last)


def kernel(x, positions, mix_norm_g, w_in, gdn_conv_w, gdn_a_log, gdn_dt_bias, gdn_norm_g, mla_q_norm_g, mla_kv_norm_g, mla_w_uq, mla_w_ukv, rwkv_mu, rwkv_w0, rwkv_w_w2, rwkv_a0, rwkv_w_a2, rwkv_w_g2, rwkv_k_k, rwkv_k_a, rwkv_r_k, rwkv_ln_w, rwkv_ln_b, w_out_gdn, w_out_mla, w_out_rwkv, w_out_dsa, w_o, mlp_norm_g, w_up, w_down, final_norm_g):
    b, s, d = x.shape
    depth = w_in.shape[0]
    tabs = (_rope_cos_sin(positions, MLA_ROPE).reshape(b * s, MLA_ROPE),
            _rope_cos_sin(positions, DH // 4).reshape(b * s, DH // 4))
    row = lambda a: a.reshape(1, -1).astype(F32)
    pad8 = lambda rows: jnp.concatenate(rows + [jnp.zeros((8 - len(rows), rows[0].shape[1]), F32)], axis=0)
    x2d = x.reshape(b * s, d)
    final_g = row(final_norm_g)
    for l in range(depth):
        lane4 = lambda a: jnp.pad(a.astype(F32), (0, 124)).reshape(1, 128)
        gdn_pcol = pad8([lane4(gdn_a_log[l]), lane4(gdn_dt_bias[l])])
        gdn_prow = jnp.pad(jnp.stack([gdn_a_log[l], gdn_dt_bias[l]], axis=1).astype(F32), ((0, 4), (0, 126)))
        wq, wk, wv = _mla_weights(mla_w_uq[l], mla_w_ukv[l])
        rwkv_vec = pad8([row(rwkv_w0[l]), row(rwkv_a0[l]), row(rwkv_k_k[l]), row(rwkv_k_a[l]),
                         row(rwkv_ln_w[l]), row(rwkv_ln_b[l]), row(rwkv_r_k[l])])
        prm = (row(mix_norm_g[l]), _slab_weight(w_in[l]), gdn_conv_w[l].astype(F32), gdn_pcol, gdn_prow,
               row(jnp.tile(gdn_norm_g[l], HEADS)), row(mla_q_norm_g[l]), row(mla_kv_norm_g[l]), wq, wk, wv,
               row(rwkv_mu[l]), rwkv_vec, rwkv_w_w2[l].astype(F32), rwkv_w_a2[l].astype(F32),
               rwkv_w_g2[l].astype(F32),
               w_out_gdn[l].astype(BF16), w_out_mla[l].astype(BF16), w_out_rwkv[l].astype(BF16),
               w_out_dsa[l].astype(BF16), w_o[l].astype(BF16), row(mlp_norm_g[l]),
               w_up[l].astype(BF16), w_down[l].astype(BF16))
        x2d = _layer(x2d, b, s, prm, tabs, final_g, l == depth - 1)
    return x2d.reshape(b, s, d)
```

```python
import functools
import math

import numpy as np
import jax
import jax.numpy as jnp
from jax import lax
from jax.experimental import pallas as pl
from jax.experimental.pallas import tpu as pltpu

F32 = jnp.float32
BF16 = jnp.bfloat16
I32 = jnp.int32
I16 = jnp.int16

ROPE_THETA = 500000.0
NORM_EPS = 1e-6
NEG_INF = -1e30
HEADS = 4
DH = 64
HW = HEADS * DH
GDN_CONV = 4
CHUNK = 64
MLA_Q_LORA = 256
MLA_KV_LORA = 128
MLA_NOPE = 64
MLA_ROPE = 32
MLA_QK = 128
RWKV_GN_EPS = 64e-5
IDX_HEADS = 8
IDX_TOPK_MAX = 256
INT_MIN = -2 ** 31
LOG2E = 1.4426950408889634

VMEM_LIMIT_BYTES = 56 * 1024 * 1024

OFF_RWKV = 0
OFF_GDN_Q = 1024
OFF_GDN_K = 1280
OFF_GDN_V = 1536
OFF_GDN_Z = 1792
OFF_MLA_CQ = 2048
OFF_MLA_CKV = 2304
OFF_MLA_KR = 2432
OFF_DSA_IQ = 2560
OFF_DSA_Q = 3072
OFF_DSA_K = 3328
OFF_DSA_V = 3584
OFF_GDN_AB = 3840
OFF_DSA_IKW = 3968

NN = (((1,), (0,)), ((), ()))
NT = (((1,), (1,)), ((), ()))
BNN = (((2,), (1,)), ((0,), (0,)))
BNT = (((2,), (2,)), ((0,), (0,)))
BTN = (((1,), (1,)), ((0,), (0,)))


def _cparams(n_axes):
    return pltpu.CompilerParams(dimension_semantics=("arbitrary",) * n_axes,
                                vmem_limit_bytes=VMEM_LIMIT_BYTES)


def _mm(a, b, dims=NN):
    return lax.dot_general(a.astype(BF16), b.astype(BF16), dims, preferred_element_type=F32)


def _mm_exact_lhs(m, x, dims=NN):
    mb = m.astype(BF16)
    x0 = x.astype(BF16)
    r1 = x - x0.astype(F32)
    x1 = r1.astype(BF16)
    x2 = (r1 - x1.astype(F32)).astype(BF16)
    d = lambda y: lax.dot_general(mb, y, dims, preferred_element_type=F32)
    return d(x0) + (d(x1) + d(x2))


def _mm_exact_rhs(x, m, dims=NN):
    mb = m.astype(BF16)
    x0 = x.astype(BF16)
    r1 = x - x0.astype(F32)
    x1 = r1.astype(BF16)
    x2 = (r1 - x1.astype(F32)).astype(BF16)
    d = lambda y: lax.dot_general(y, mb, dims, preferred_element_type=F32)
    return d(x0) + (d(x1) + d(x2))


def _iota2(shape, dim):
    return lax.broadcasted_iota(I32, shape, dim)


def _seg_ones(width, seg):
    sh = int(math.log2(seg))
    r = _iota2((width, width), 0) >> sh
    c = _iota2((width, width), 1) >> sh
    return jnp.where(r == c, 1.0, 0.0).astype(F32)


def _seg_sum(x, seg_ones):
    return _mm_exact_rhs(x, seg_ones)


def _softplus(x):
    return jnp.maximum(x, 0.0) + jnp.log1p(jnp.exp(-jnp.abs(x)))


def _sigmoid(x):
    return 0.5 * jnp.tanh(0.5 * x) + 0.5


def _silu(x):
    return x * _sigmoid(x)


def _tri_masks(n):
    r = _iota2((n, n), 0)
    c = _iota2((n, n), 1)
    return r, c


def _unit_lower_inverse(m, r, c, dims=NN):
    eye = jnp.where(r == c, 1.0, 0.0).astype(F32)
    same16 = (r >> 4) == (c >> 4)
    same32 = (r >> 5) == (c >> 5)
    md = jnp.where(same16, m, 0.0)
    t = eye - md
    p = md
    for _ in range(3):
        p = _mm(p, p, dims)
        t = t + _mm(t, p, dims)
    c32 = jnp.where(same32 & jnp.logical_not(same16), m, 0.0)
    t = t - _mm(_mm(t, c32, dims), t, dims)
    c64 = jnp.where(same32, 0.0, m)
    t = t - _mm(_mm(t, c64, dims), t, dims)
    return t


IN_PROJ_TN = 512


def _in_proj_kernel(x_ref, g_ref, w_ref, o_ref):
    x = x_ref[...]
    ms = jnp.mean(x * x, axis=-1, keepdims=True)
    h = (x * lax.rsqrt(ms + NORM_EPS) * g_ref[...]).astype(BF16)
    for n0 in range(0, o_ref.shape[1], IN_PROJ_TN):
        o_ref[:, n0:n0 + IN_PROJ_TN] = jnp.dot(
            h, w_ref[:, n0:n0 + IN_PROJ_TN], preferred_element_type=F32).astype(o_ref.dtype)


def _in_proj(x2d, g, w, out_dtype):
    t, d = x2d.shape
    n = w.shape[1]
    tm = min(512, t)
    return pl.pallas_call(
        _in_proj_kernel,
        grid=(t // tm,),
        in_specs=[pl.BlockSpec((tm, d), lambda i: (i, 0)),
                  pl.BlockSpec((1, d), lambda i: (0, 0)),
                  pl.BlockSpec((d, n), lambda i: (0, 0))],
        out_specs=pl.BlockSpec((tm, n), lambda i: (i, 0)),
        out_shape=jax.ShapeDtypeStruct((t, n), out_dtype),
        compiler_params=_cparams(1),
        name="in_proj",
    )(x2d, g, w)


def _to_groups(x, nchunk):
    parts = [x[:, h * DH:(h + 1) * DH].reshape(nchunk, 1, CHUNK, DH) for h in range(HEADS)]
    return jnp.concatenate(parts, axis=1).reshape(nchunk * HEADS, CHUNK, DH)


def _col_groups(x, lane0, nchunk):
    parts = [x[:, lane0 + h:lane0 + h + 1].reshape(nchunk, 1, CHUNK, 1) for h in range(HEADS)]
    return jnp.concatenate(parts, axis=1).reshape(nchunk * HEADS, CHUNK, 1)


def _gdn_kernel(q_ref, k_ref, v_ref, z_ref, ab_ref, abr_ref, cw_ref, pcol_ref, prow_ref, ng_ref,
                o_ref, xbuf_ref, state_ref, u_ref, w_ref, a_ref, qd_ref, kd_ref, egl_ref, oraw_ref,
                carry_ref, *, tb, nbatch):
    nchunk = tb // CHUNK
    bi = pl.program_id(1)
    xbuf = xbuf_ref

    @pl.when(pl.program_id(0) == 0)
    def _():
        carry_ref[bi] = jnp.zeros((8, 3 * HW), F32)
        state_ref[bi] = jnp.zeros((HEADS, DH, DH), F32)

    xbuf[0:8, :] = carry_ref[bi]
    xbuf[8:8 + tb, 0:HW] = q_ref[...]
    xbuf[8:8 + tb, HW:2 * HW] = k_ref[...]
    xbuf[8:8 + tb, 2 * HW:3 * HW] = v_ref[...]
    cw = cw_ref[...]
    y = xbuf[8:8 + tb, :] * cw[3:4, :]
    y = y + xbuf[7:7 + tb, :] * cw[2:3, :]
    y = y + xbuf[6:6 + tb, :] * cw[1:2, :]
    y = y + xbuf[5:5 + tb, :] * cw[0:1, :]
    carry_ref[bi] = xbuf[tb:tb + 8, :]
    y = _silu(y)
    seg = _seg_ones(HW, DH)
    q = y[:, 0:HW]
    k = y[:, HW:2 * HW]
    q = q * lax.rsqrt(_seg_sum(q * q, seg) + 1e-6) * (DH ** -0.5)
    k = k * lax.rsqrt(_seg_sum(k * k, seg) + 1e-6)
    v = y[:, 2 * HW:3 * HW]

    ab = ab_ref[...]
    pcol = pcol_ref[...]
    g_col = -jnp.exp(pcol[0:1, :]) * _softplus(ab + pcol[1:2, :])
    beta_col = _sigmoid(ab)

    r, c = _tri_masks(CHUNK)
    incl = c <= r
    strict = c < r
    l_incl = jnp.where(incl, 1.0, 0.0).astype(F32)

    rb = _iota2((tb, tb), 0)
    cb = _iota2((tb, tb), 1)
    l_blk = jnp.where(((rb >> 6) == (cb >> 6)) & (cb <= rb), 1.0, 0.0).astype(F32)
    gc_col = _mm_exact_lhs(l_blk, g_col)
    prow = prow_ref[...]
    abr = abr_ref[0].reshape(nchunk * 8, CHUNK)
    prow_t = jnp.concatenate([prow] * nchunk, axis=0)
    g_row = -jnp.exp(prow_t[:, 0:1]) * _softplus(abr + prow_t[:, 1:2])
    gc_row = _mm_exact_rhs(g_row, l_incl, NT).reshape(nchunk, 8, CHUNK)

    qg, kg, vg = _to_groups(q, nchunk), _to_groups(k, nchunk), _to_groups(v, nchunk)
    beta = _col_groups(beta_col, 4, nchunk)
    gcol = _col_groups(gc_col, 0, nchunk)
    grow = jnp.concatenate([gc_row[:, h:h + 1, :].reshape(nchunk, 1, 1, CHUNK) for h in range(HEADS)],
                           axis=1).reshape(nchunk * HEADS, 1, CHUNK)
    decay = jnp.where(incl, jnp.exp(jnp.where(incl, gcol - grow, 0.0)), 0.0)
    kb = kg * beta
    m = jnp.where(strict, _mm(kb, kg, BNT) * decay, 0.0)
    t_inv = _unit_lower_inverse(m, r, c, BNN)
    egc = jnp.exp(gcol)
    glast = gcol[:, CHUNK - 1:CHUNK, :]
    uw = _mm(t_inv, jnp.concatenate([vg * beta, kb * egc], axis=2), BNN)
    u_ref[bi] = uw[:, :, 0:DH]
    w_ref[bi] = uw[:, :, DH:2 * DH].astype(BF16)
    a_ref[bi] = (_mm(qg, kg, BNT) * decay).astype(BF16)
    qd_ref[bi] = (qg * egc).astype(BF16)
    kd_ref[bi] = (kg * jnp.exp(glast - gcol)).astype(BF16)
    egl_ref[bi] = jnp.broadcast_to(jnp.exp(glast), (nchunk * HEADS, 1, DH))

    @pl.when(bi == nbatch - 1)
    def _():
        nc = nbatch * HEADS

        def chunk_body(ci, carry):
            sl = pl.ds(pl.multiple_of(ci * HEADS, HEADS), HEADS)
            take = lambda ref: ref[:, sl].reshape((nc,) + ref.shape[2:])
            s = state_ref[...].reshape(nc, DH, DH)
            v_new = take(u_ref) - _mm(take(w_ref), s, BNN)
            o = _mm(take(qd_ref), s, BNN) + _mm(take(a_ref), v_new, BNN)
            s = s * take(egl_ref) + _mm(take(kd_ref), v_new, BTN)
            state_ref[...] = s.reshape(nbatch, HEADS, DH, DH)
            rows = pl.ds(pl.multiple_of(ci * CHUNK, CHUNK), CHUNK)
            for bb in range(nbatch):
                oraw_ref[bb, rows, :] = jnp.concatenate([o[bb * HEADS + h] for h in range(HEADS)], axis=1)
            return carry

        lax.fori_loop(0, nchunk, chunk_body, 0)
        o_all = oraw_ref[...].reshape(nbatch * tb, HW)
        ms = _seg_sum(o_all * o_all, seg) * (1.0 / DH)
        o_n = o_all * lax.rsqrt(ms + NORM_EPS) * ng_ref[...] * _silu(z_ref[...].reshape(nbatch * tb, HW))
        o_ref[...] = o_n.reshape(nbatch, tb, HW).astype(o_ref.dtype)


GDN_TB = 512


def _gdn(slab, abr, conv_w, pcol, prow, norm_g, b, s):
    tb = min(GDN_TB, s)
    nb = s // tb
    nchunk = tb // CHUNK
    ng = nchunk * HEADS
    grp = lambda w, dt=BF16: pltpu.VMEM((b, ng, CHUNK, w), dt)
    col = lambda off, w: pl.BlockSpec((tb, w), lambda j, i, o=off // w: (i * nb + j, o))
    full = lambda shape: pl.BlockSpec(shape, lambda j, i: (0,) * len(shape))
    slab3 = slab.reshape(b, s, slab.shape[1])
    out = pl.pallas_call(
        functools.partial(_gdn_kernel, tb=tb, nbatch=b),
        grid=(nb, b),
        in_specs=[col(OFF_GDN_Q, HW), col(OFF_GDN_K, HW), col(OFF_GDN_V, HW),
                  pl.BlockSpec((b, tb, HW), lambda j, i: (0, j, OFF_GDN_Z // HW)),
                  col(OFF_GDN_AB, 128),
                  pl.BlockSpec((1, nchunk, 8, CHUNK), lambda j, i: (i, j, 0, 0)),
                  full((GDN_CONV, 3 * HW)), full((8, 128)), full((8, 128)), full((1, HW))],
        out_specs=pl.BlockSpec((b, tb, HW), lambda j, i: (0, j, 0)),
        out_shape=jax.ShapeDtypeStruct((b, s, HW), BF16),
        scratch_shapes=[pltpu.VMEM((tb + 8, 3 * HW), F32), pltpu.VMEM((b, HEADS, DH, DH), F32),
                        grp(DH, F32), grp(DH), grp(CHUNK), grp(DH), grp(DH),
                        pltpu.VMEM((b, ng, 1, DH), F32), pltpu.VMEM((b, tb, HW), F32),
                        pltpu.VMEM((b, 8, 3 * HW), F32)],
        compiler_params=_cparams(2),
        name="gdn",
    )(slab, slab, slab, slab3, slab, abr, conv_w, pcol, prow, norm_g)
    return out.reshape(b * s, HW)


def _rwkv_kernel(p_ref, mu_ref, vec_ref, ww2_ref, wa2_ref, wg2_ref, o_ref,
                 xbuf_ref, state_ref, ta_ref, tpv_ref, rt_ref, qkv_ref, qb_ref, bc_ref, ktv_ref,
                 el_ref, oraw_ref, carry_ref, post_ref, *, tb, nbatch):
    nchunk = tb // CHUNK
    bi = pl.program_id(1)

    @pl.when(pl.program_id(0) == 0)
    def _():
        carry_ref[bi] = jnp.zeros((8, 4 * HW), F32)
        state_ref[bi] = jnp.zeros((HEADS, DH, DH), F32)

    p = p_ref[...]
    xbuf_ref[0:8, :] = carry_ref[bi]
    xbuf_ref[8:8 + tb, :] = p
    prev = xbuf_ref[7:7 + tb, :]
    carry_ref[bi] = xbuf_ref[tb:tb + 8, :]
    pm = p + (prev - p) * mu_ref[...]
    vec = vec_ref[...]
    w0, a0, k_k, k_a = vec[0:1, :], vec[1:2, :], vec[2:3, :], vec[3:4, :]
    rr = pm[:, 0:HW]
    k = pm[:, HW:2 * HW]
    v = pm[:, 2 * HW:3 * HW]
    xw = pm[:, 768:832]
    xa = pm[:, 832:896]
    xg = pm[:, 896:1024]
    w_log = -_softplus(-(w0 + _mm(jnp.tanh(xw), ww2_ref[...]))) - 0.5
    lw = -jnp.exp(w_log)
    a = _sigmoid(a0 + _mm(xa, wa2_ref[...]))
    g = _mm(_sigmoid(xg), wg2_ref[...])
    seg = _seg_ones(HW, DH)
    kk = k * k_k
    kk = kk * lax.rsqrt(_seg_sum(kk * kk, seg) + 1e-6)
    k2 = k * (1.0 + (a - 1.0) * k_a)
    kka = kk * a

    r_i, c_i = _tri_masks(CHUNK)
    incl = c_i <= r_i
    strict = c_i < r_i
    rb = _iota2((tb, tb), 0)
    cb = _iota2((tb, tb), 1)
    l_blk = jnp.where(((rb >> 6) == (cb >> 6)) & (cb <= rb), 1.0, 0.0).astype(F32)
    lwc_all = _mm_exact_lhs(l_blk, lw)

    rg, kg, vg = _to_groups(rr, nchunk), _to_groups(k2, nchunk), _to_groups(v, nchunk)
    kkg, kkag = _to_groups(kk, nchunk), _to_groups(kka, nchunk)
    lwg, lwc = _to_groups(lw, nchunk), _to_groups(lwc_all, nchunk)
    lwl = lwc[:, CHUNK - 1:CHUNK, :]
    e_neg = jnp.exp(-lwc)
    at = kkg * jnp.exp(lwc - lwg)
    rt = rg * jnp.exp(lwc)
    y2 = jnp.concatenate([at, rt], axis=1)
    x2 = jnp.concatenate([kkag * e_neg, kg * e_neg], axis=1)
    gram = _mm(y2, x2, BNT)
    n_m = jnp.where(strict, gram[:, 0:CHUNK, 0:CHUNK], 0.0)
    p_m = jnp.where(strict, gram[:, 0:CHUNK, CHUNK:], 0.0)
    qb_m = jnp.where(incl, gram[:, CHUNK:, 0:CHUNK], 0.0)
    qk_m = jnp.where(incl, gram[:, CHUNK:, CHUNK:], 0.0)
    t_inv = _unit_lower_inverse(n_m, r_i, c_i, BNN)
    ta_ref[bi] = _mm(t_inv, at, BNN).astype(BF16)
    tpv_ref[bi] = _mm(t_inv, _mm(p_m, vg, BNN), BNN)
    rt_ref[bi] = rt.astype(BF16)
    qkv_ref[bi] = _mm(qk_m, vg, BNN)
    qb_ref[bi] = qb_m.astype(BF16)
    e_rel = jnp.exp(lwl - lwc)
    bc_ref[bi] = (kkag * e_rel).astype(BF16)
    ktv_ref[bi] = _mm(vg, kg * e_rel, BTN)
    el_ref[bi] = jnp.exp(lwl)
    ln_w, ln_b, r_k = vec[4:5, :], vec[5:6, :], vec[6:7, :]
    post_ref[bi, :, 0:HW] = _seg_sum(rr * k2 * r_k, seg) * v
    post_ref[bi, :, HW:2 * HW] = g

    @pl.when(bi == nbatch - 1)
    def _():
        nc = nbatch * HEADS

        def chunk_body(ci, carry):
            sl = pl.ds(pl.multiple_of(ci * HEADS, HEADS), HEADS)
            take = lambda ref: ref[:, sl].reshape((nc,) + ref.shape[2:])
            zt = state_ref[...].reshape(nc, DH, DH)
            u = _mm(take(ta_ref), zt, BNT) + take(tpv_ref)
            o = _mm(take(rt_ref), zt, BNT) + take(qkv_ref) - _mm(take(qb_ref), u, BNN)
            zt = zt * take(el_ref) + take(ktv_ref) - _mm(u, take(bc_ref), BTN)
            state_ref[...] = zt.reshape(nbatch, HEADS, DH, DH)
            rows = pl.ds(pl.multiple_of(ci * CHUNK, CHUNK), CHUNK)
            for bb in range(nbatch):
                oraw_ref[bb, rows, :] = jnp.concatenate([o[bb * HEADS + h] for h in range(HEADS)], axis=1)
            return carry

        lax.fori_loop(0, nchunk, chunk_body, 0)
        o_all = oraw_ref[...].reshape(nbatch * tb, HW)
        mean = _seg_sum(o_all, seg) * (1.0 / DH)
        cen = o_all - mean
        var = _seg_sum(cen * cen, seg) * (1.0 / DH)
        o_n = cen * lax.rsqrt(var + RWKV_GN_EPS) * ln_w + ln_b
        post = post_ref[...].reshape(nbatch * tb, 2 * HW)
        o_ref[...] = ((o_n + post[:, 0:HW]) * post[:, HW:2 * HW]).reshape(nbatch, tb, HW).astype(o_ref.dtype)


RWKV_TB = 512


def _rwkv(slab, mu, vec, ww2, wa2, wg2, b, s):
    tb = min(RWKV_TB, s)
    nb = s // tb
    ng = (tb // CHUNK) * HEADS
    grp = lambda dt=F32: pltpu.VMEM((b, ng, CHUNK, DH), dt)
    full = lambda shape: pl.BlockSpec(shape, lambda j, i: (0,) * len(shape))
    out = pl.pallas_call(
        functools.partial(_rwkv_kernel, tb=tb, nbatch=b),
        grid=(nb, b),
        in_specs=[pl.BlockSpec((tb, 4 * HW), lambda j, i: (i * nb + j, OFF_RWKV // (4 * HW))),
                  full((1, 4 * HW)), full((8, HW)), full((64, HW)), full((64, HW)), full((128, HW))],
        out_specs=pl.BlockSpec((b, tb, HW), lambda j, i: (0, j, 0)),
        out_shape=jax.ShapeDtypeStruct((b, s, HW), BF16),
        scratch_shapes=[pltpu.VMEM((tb + 8, 4 * HW), F32), pltpu.VMEM((b, HEADS, DH, DH), F32),
                        grp(BF16), grp(), grp(BF16), grp(), grp(BF16), grp(BF16), grp(),
                        pltpu.VMEM((b, ng, 1, DH), F32), pltpu.VMEM((b, tb, HW), F32),
                        pltpu.VMEM((b, 8, 4 * HW), F32), pltpu.VMEM((b, tb, 2 * HW), F32)],
        compiler_params=_cparams(2),
        name="rwkv7",
    )(slab, mu, vec, ww2, wa2, wg2)
    return out.reshape(b * s, HW)


def _rope(x, cos_t, sin_lo, sin_hi, half):
    w = x.shape[-1]
    return x * cos_t + pltpu.roll(x, w - half, 1) * sin_lo + pltpu.roll(x, half, 1) * sin_hi


def _tile_lanes(t, n):
    return t if n == 1 else jnp.concatenate([t] * n, axis=1)


def _mla_prep_kernel(cq_ref, ckv_ref, kr_ref, cs_ref, e_ref, base_ref, qg_ref, kvg_ref, wq_ref, wk_ref,
                     wvt_ref, q_ref, k_ref, vt_ref):
    cq = cq_ref[...]
    hq = cq * lax.rsqrt(jnp.mean(cq * cq, axis=-1, keepdims=True) + NORM_EPS) * qg_ref[...]
    ckv = ckv_ref[...]
    hkv = ckv * lax.rsqrt(jnp.mean(ckv * ckv, axis=-1, keepdims=True) + NORM_EPS) * kvg_ref[...]
    cos_t, sin_lo, sin_hi = _rope_table(cs_ref, e_ref, base_ref)
    q = _mm(hq, wq_ref[...])
    q = _rope(q, _tile_lanes(cos_t, HEADS), _tile_lanes(sin_lo, HEADS), _tile_lanes(sin_hi, HEADS),
              MLA_ROPE // 2)
    q_ref[...] = (q * ((MLA_NOPE + MLA_ROPE) ** -0.5 * LOG2E)).astype(BF16)
    kr = _rope(kr_ref[...], cos_t, sin_lo, sin_hi, MLA_ROPE // 2)
    k = _mm(hkv, wk_ref[...]) + _tile_lanes(kr, HEADS)
    k_ref[...] = k.astype(BF16)
    vt_ref[...] = lax.dot_general(wvt_ref[...], hkv.astype(BF16), NT,
                                  preferred_element_type=F32).astype(BF16)


def _mla_prep(slab, cs, qg, kvg, wq, wk, wvt):
    t = slab.shape[0]
    tm = min(512, t)
    full = lambda shape: pl.BlockSpec(shape, lambda i: (0,) * len(shape))
    e, base = _rope_expansion(MLA_ROPE, MLA_NOPE, MLA_QK)
    return pl.pallas_call(
        _mla_prep_kernel,
        grid=(t // tm,),
        in_specs=[pl.BlockSpec((tm, 256), lambda i: (i, OFF_MLA_CQ // 256)),
                  pl.BlockSpec((tm, 128), lambda i: (i, OFF_MLA_CKV // 128)),
                  pl.BlockSpec((tm, 128), lambda i: (i, OFF_MLA_KR // 128)),
                  pl.BlockSpec((tm, MLA_ROPE), lambda i: (i, 0)), full(e.shape), full(base.shape),
                  full((1, 256)), full((1, 128)), full((256, 512)), full((128, 512)), full((256, 128))],
        out_specs=[pl.BlockSpec((tm, 512), lambda i: (i, 0)), pl.BlockSpec((tm, 512), lambda i: (i, 0)),
                   pl.BlockSpec((256, tm), lambda i: (0, i))],
        out_shape=[jax.ShapeDtypeStruct((t, 512), BF16), jax.ShapeDtypeStruct((t, 512), BF16),
                   jax.ShapeDtypeStruct((256, t), BF16)],
        compiler_params=_cparams(1),
        name="mla_prep",
    )(slab, slab, slab, cs, e, base, qg, kvg, wq, wk, wvt)


def _flash_kernel(qi_tab, ki_tab, q_ref, k_ref, vt_ref, o_ref, m_ref, l_ref, acc_ref, *, tq, tk):
    qi = qi_tab[pl.program_id(1)]
    ki = ki_tab[pl.program_id(1)]

    @pl.when(ki == 0)
    def _():
        m_ref[...] = jnp.full_like(m_ref, NEG_INF)
        l_ref[...] = jnp.zeros_like(l_ref)
        acc_ref[...] = jnp.zeros_like(acc_ref)

    def step(on_diagonal):
        if on_diagonal:
            causal = _iota2((tk, tq), 0) <= _iota2((tk, tq), 1)
        vt = vt_ref[...]
        ones = jnp.ones((16, tk), BF16)
        scores = [lax.dot_general(k_ref[:, h * MLA_QK:(h + 1) * MLA_QK], q_ref[:, h * MLA_QK:(h + 1) * MLA_QK],
                                  NT, preferred_element_type=F32) for h in range(HEADS)]
        for h in range(HEADS):
            s = scores[h]
            if on_diagonal:
                s = jnp.where(causal, s, NEG_INF)
            m_old = m_ref[h]
            m_new = jnp.maximum(m_old, jnp.max(s, axis=0, keepdims=True))
            alpha = jnp.exp2(m_old - m_new)
            p = jnp.exp2(s - m_new).astype(BF16)
            pv = jnp.dot(jnp.concatenate([vt[h * DH:(h + 1) * DH, :], ones], axis=0), p,
                         preferred_element_type=F32)
            l_ref[h] = alpha * l_ref[h] + pv[DH:DH + 1, :]
            acc_ref[h * DH:(h + 1) * DH, :] = alpha * acc_ref[h * DH:(h + 1) * DH, :] + pv[0:DH, :]
            m_ref[h] = m_new

    pl.when(ki < qi)(lambda: step(False))

    @pl.when(ki == qi)
    def _():
        step(True)
        o_t = jnp.concatenate([acc_ref[h * DH:(h + 1) * DH, :] / l_ref[h] for h in range(HEADS)], axis=0)
        o_ref[...] = o_t.T.astype(o_ref.dtype)


def _flash(q, k, vt, b, s):
    tq = min(512, s)
    tk = tq
    nq = s // tq
    pairs = [(qb, kb) for qb in range(nq) for kb in range(qb + 1)]
    qi_tab = jnp.asarray([p[0] for p in pairs], I32)
    ki_tab = jnp.asarray([p[1] for p in pairs], I32)
    return pl.pallas_call(
        functools.partial(_flash_kernel, tq=tq, tk=tk),
        grid_spec=pltpu.PrefetchScalarGridSpec(
            num_scalar_prefetch=2,
            grid=(b, len(pairs)),
            in_specs=[pl.BlockSpec((tq, 512), lambda i, p, qt, kt: (i * nq + qt[p], 0)),
                      pl.BlockSpec((tk, 512), lambda i, p, qt, kt: (i * nq + kt[p], 0)),
                      pl.BlockSpec((256, tk), lambda i, p, qt, kt: (0, i * nq + kt[p]))],
            out_specs=pl.BlockSpec((tq, 256), lambda i, p, qt, kt: (i * nq + qt[p], 0)),
            scratch_shapes=[pltpu.VMEM((HEADS, 1, tq), F32), pltpu.VMEM((HEADS, 1, tq), F32),
                            pltpu.VMEM((HW, tq), F32)]),
        out_shape=jax.ShapeDtypeStruct((b * s, 256), BF16),
        compiler_params=_cparams(2),
        name="mla_flash",
    )(qi_tab, ki_tab, q, k, vt)


DSA_TQ = 256
DSA_TK = 512


def _dsa_prep_kernel(iq_ref, ikw_ref, q_ref, k_ref, v_ref, cs_ref, e_ref, base_ref,
                     ik_ref, ko_ref, vt_ref, iqt_ref, qt_ref, iwt_ref):
    cos_t, sin_lo, sin_hi = _rope_table(cs_ref, e_ref, base_ref)
    half = DH // 8
    rope = lambda x, n: _rope(x, _tile_lanes(cos_t, n), _tile_lanes(sin_lo, n), _tile_lanes(sin_hi, n), half)
    ikw = ikw_ref[...]
    ik_ref[...] = rope(ikw, 1)[:, 0:DH].astype(BF16)
    kr = rope(k_ref[...], 2)
    for h in range(HEADS):
        ko_ref[h] = kr[:, h * DH:(h + 1) * DH].astype(BF16)
    vt_ref[0] = v_ref[...].T.astype(BF16)
    iqt_ref[...] = rope(iq_ref[...], 4).T.astype(BF16)
    qt_ref[...] = (rope(q_ref[...], 2) * (DH ** -0.5 * LOG2E)).T.astype(BF16)
    iwt_ref[...] = ikw.T[DH:DH + 8, :] * (IDX_HEADS ** -0.5 * DH ** -0.5)


def _dsa_prep(slab, cs):
    t = slab.shape[0]
    tm = min(DSA_TK, t)
    full = lambda shape: pl.BlockSpec(shape, lambda i: (0,) * len(shape))
    e, base = _rope_expansion(DH // 4, 0, DH)
    return pl.pallas_call(
        _dsa_prep_kernel,
        grid=(t // tm,),
        in_specs=[pl.BlockSpec((tm, IDX_HEADS * DH), lambda i: (i, OFF_DSA_IQ // (IDX_HEADS * DH))),
                  pl.BlockSpec((tm, 128), lambda i: (i, OFF_DSA_IKW // 128)),
                  pl.BlockSpec((tm, 256), lambda i: (i, OFF_DSA_Q // 256)),
                  pl.BlockSpec((tm, 256), lambda i: (i, OFF_DSA_K // 256)),
                  pl.BlockSpec((tm, 256), lambda i: (i, OFF_DSA_V // 256)),
                  pl.BlockSpec((tm, DH // 4), lambda i: (i, 0)), full(e.shape), full(base.shape)],
        out_specs=[pl.BlockSpec((tm, DH), lambda i: (i, 0)),
                   pl.BlockSpec((HEADS, tm, DH), lambda i: (0, i, 0)),
                   pl.BlockSpec((1, HW, tm), lambda i: (i, 0, 0)),
                   pl.BlockSpec((IDX_HEADS * DH, tm), lambda i: (0, i)),
                   pl.BlockSpec((HW, tm), lambda i: (0, i)),
                   pl.BlockSpec((8, tm), lambda i: (0, i))],
        out_shape=[jax.ShapeDtypeStruct((t, DH), BF16), jax.ShapeDtypeStruct((HEADS, t, DH), BF16),
                   jax.ShapeDtypeStruct((t // tm, HW, tm), BF16),
                   jax.ShapeDtypeStruct((IDX_HEADS * DH, t), BF16), jax.ShapeDtypeStruct((HW, t), BF16),
                   jax.ShapeDtypeStruct((8, t), F32)],
        compiler_params=_cparams(1),
        name="dsa_prep",
    )(slab, slab, slab, slab, slab, cs, e, base)


def _dsa_kernel(iqt_ref, iwt_ref, qt_ref, ik_ref, k_ref, vt_ref, kidx_ref, o_ref,
                hi_ref, lo_ref, lom_ref, bias_ref, acc_ref, s_ref, m_ref, l_ref,
                *, tq, tk, n_sel, idx_bits):
    qi = pl.program_id(1)
    nkt = (qi * tq + tq - 1) // tk + 1
    iqt = iqt_ref[...]
    iw = iwt_ref[...]
    q_pos = qi * tq + _iota2((tk, tq), 1)
    qt = qt_ref[...]

    def score_body(kt, carry):
        row0 = pl.multiple_of(kt * tk, tk)
        ik = ik_ref[pl.ds(row0, tk), :]
        sc = jnp.zeros((tk, tq), F32)
        for h in range(IDX_HEADS):
            lg = lax.dot_general(ik, iqt[h * DH:(h + 1) * DH, :], NN, preferred_element_type=F32)
            sc = sc + jnp.maximum(lg, 0.0) * iw[h:h + 1, :]
        bits = lax.bitcast_convert_type(sc, I32)
        key = bits ^ ((bits >> 31) & 0x7FFFFFFF)
        key = jnp.where(sc == 0.0, 0, key)
        k_pos = row0 + _iota2((tk, tq), 0)
        key = jnp.where(k_pos <= q_pos, key, INT_MIN)
        hi_ref[pl.ds(row0, tk), :] = (key >> 16).astype(I16)
        lo_ref[pl.ds(row0, tk), :] = ((key & 0xFFFF) - 32768).astype(I16)
        for h in range(HEADS):
            s_ref[h, pl.ds(row0, tk), :] = lax.dot_general(k_ref[h, pl.ds(row0, tk), :],
                                                           qt[h * DH:(h + 1) * DH, :], NN,
                                                           preferred_element_type=F32)
        return carry

    lax.fori_loop(0, nkt, score_body, 0)

    one16 = jnp.ones((tk, tq), I16)
    zero16 = jnp.zeros((tk, tq), I16)

    def count(*pred_fns):
        def body(kt, accs):
            sl = pl.ds(pl.multiple_of(kt * tk, tk), tk)
            args = (hi_ref[sl, :], lo_ref[sl, :], lom_ref[sl, :], kidx_ref[sl, :])
            out = []
            for acc, fn in zip(accs, pred_fns):
                ones = jnp.where(fn(*args), one16, zero16).reshape(tk // 64, 64, tq)
                for g in range(tk // 64):
                    acc = acc + ones[g]
                out.append(acc)
            return tuple(out)
        parts = lax.fori_loop(0, nkt, body, tuple(jnp.zeros((64, tq), I16) for _ in pred_fns))
        sums = tuple(jnp.sum(p.astype(I32).astype(F32), axis=0, keepdims=True) for p in parts)
        return sums[0] if len(sums) == 1 else sums

    def bisect16(pred_ge):
        def body(it, thr):
            bit = jnp.maximum(15 - it, 0)
            cand = jnp.where(it == 0, jnp.zeros_like(thr), thr | jnp.left_shift(jnp.int32(1), bit))
            c16 = cand.astype(I16)
            cnt = count(lambda hi, lo, lom, idx: pred_ge(hi, lo, lom, c16))
            return jnp.where(cnt >= float(n_sel), cand, thr)
        return lax.fori_loop(0, 16, body, jnp.full((1, tq), -32768, I32))

    t_hi = bisect16(lambda hi, lo, lom, c: hi >= c)
    t_hi16 = t_hi.astype(I16)

    def mask_lo_body(kt, carry):
        sl = pl.ds(pl.multiple_of(kt * tk, tk), tk)
        hi = hi_ref[sl, :]
        lom_ref[sl, :] = jnp.where(hi > t_hi16, jnp.int16(32767),
                                   jnp.where(hi == t_hi16, lo_ref[sl, :], jnp.int16(-32768)))
        return carry

    lax.fori_loop(0, nkt, mask_lo_body, 0)
    t_lo16 = bisect16(lambda hi, lo, lom, c: lom >= c).astype(I16)

    def is_gt(hi, lo):
        return (hi > t_hi16) | ((hi == t_hi16) & (lo > t_lo16))

    def is_eq(hi, lo):
        return (hi == t_hi16) & (lo == t_lo16)

    def write_bias(sl, hi, sel_eq, gt):
        sel = (gt | sel_eq) & (hi != jnp.int16(-32768))
        bias_ref[sl, :] = jnp.where(sel, jnp.zeros((tk, tq), BF16),
                                    jnp.full((tk, tq), NEG_INF, BF16)).astype(F32)

    def tally_body(kt, accs):
        sl = pl.ds(pl.multiple_of(kt * tk, tk), tk)
        hi, lo = hi_ref[sl, :], lo_ref[sl, :]
        gt, eq = is_gt(hi, lo), is_eq(hi, lo)
        write_bias(sl, hi, eq, gt)
        out = []
        for acc, m in zip(accs, (gt, eq)):
            ones = jnp.where(m, one16, zero16).reshape(tk // 64, 64, tq)
            for g in range(tk // 64):
                acc = acc + ones[g]
            out.append(acc)
        return tuple(out)

    parts = lax.fori_loop(0, nkt, tally_body, (jnp.zeros((64, tq), I16), jnp.zeros((64, tq), I16)))
    cnt_gt, cnt_eq = (jnp.sum(p.astype(I32).astype(F32), axis=0, keepdims=True) for p in parts)
    need = float(n_sel) - cnt_gt

    @pl.when(jnp.max(jnp.where(cnt_eq > need, 1.0, 0.0)) > 0.0)
    def _():
        def idx_body(it, j):
            cand = j + jnp.left_shift(jnp.int32(1), idx_bits - 1 - it)
            c16 = cand.astype(I16)
            cnt = count(lambda hi, lo, lom, idx: is_eq(hi, lo) & (idx < c16))
            return jnp.where(cnt < need, cand, j)

        j_cut16 = lax.fori_loop(0, idx_bits, idx_body, jnp.zeros((1, tq), I32)).astype(I16)

        def rewrite_body(kt, carry):
            sl = pl.ds(pl.multiple_of(kt * tk, tk), tk)
            hi, lo = hi_ref[sl, :], lo_ref[sl, :]
            write_bias(sl, hi, is_eq(hi, lo) & (kidx_ref[sl, :] <= j_cut16), is_gt(hi, lo))
            return carry

        lax.fori_loop(0, nkt, rewrite_body, 0)

    acc_ref[...] = jnp.zeros_like(acc_ref)

    def att_body(kt, carry):
        sl = pl.ds(pl.multiple_of(kt * tk, tk), tk)
        bias = bias_ref[sl, :]
        ones = jnp.ones((16, tk), BF16)
        for h in range(HEADS):
            hs = slice(h * DH, (h + 1) * DH)
            s = s_ref[h, sl, :] + bias
            m_old = m_ref[h]
            m_new = jnp.maximum(m_old, jnp.max(s, axis=0, keepdims=True))
            alpha = jnp.exp2(m_old - m_new)
            p = jnp.exp2(s - m_new).astype(BF16)
            pv = lax.dot_general(jnp.concatenate([vt_ref[kt, hs, :], ones], axis=0), p, NN,
                                 preferred_element_type=F32)
            acc_ref[hs, :] = alpha * acc_ref[hs, :] + pv[0:DH, :]
            m_ref[h] = m_new
            l_ref[h] = alpha * l_ref[h] + pv[DH:DH + 1, :]
        return carry

    m_ref[...] = jnp.full_like(m_ref, NEG_INF)
    l_ref[...] = jnp.zeros_like(l_ref)
    lax.fori_loop(0, nkt, att_body, 0)
    o_t = jnp.concatenate([acc_ref[h * DH:(h + 1) * DH, :] / l_ref[h] for h in range(HEADS)], axis=0)
    o_ref[...] = o_t.T.astype(o_ref.dtype)


def _dsa(iqt, iwt, qt, ik, kr, vt, b, s, n_sel):
    tq = min(DSA_TQ, s)
    tk = min(DSA_TK, s)
    nq = s // tq
    nkt = s // tk
    return pl.pallas_call(
        functools.partial(_dsa_kernel, tq=tq, tk=tk, n_sel=n_sel, idx_bits=int(math.log2(s))),
        grid=(b, nq),
        in_specs=[pl.BlockSpec((IDX_HEADS * DH, tq), lambda i, j: (0, i * nq + j)),
                  pl.BlockSpec((8, tq), lambda i, j: (0, i * nq + j)),
                  pl.BlockSpec((HW, tq), lambda i, j: (0, i * nq + j)),
                  pl.BlockSpec((s, DH), lambda i, j: (i, 0)),
                  pl.BlockSpec((HEADS, s, DH), lambda i, j: (0, i, 0)),
                  pl.BlockSpec((nkt, HW, tk), lambda i, j: (i, 0, 0)),
                  pl.BlockSpec((s, tq), lambda i, j: (0, 0))],
        out_specs=pl.BlockSpec((tq, HW), lambda i, j: (i * nq + j, 0)),
        out_shape=jax.ShapeDtypeStruct((b * s, HW), BF16),
        scratch_shapes=[pltpu.VMEM((s, tq), I16), pltpu.VMEM((s, tq), I16), pltpu.VMEM((s, tq), I16),
                        pltpu.VMEM((s, tq), F32), pltpu.VMEM((HW, tq), F32),
                        pltpu.VMEM((HEADS, s, tq), F32), pltpu.VMEM((HEADS, 1, tq), F32),
                        pltpu.VMEM((HEADS, 1, tq), F32)],
        compiler_params=_cparams(2),
        name="dsa",
    )(iqt, iwt, qt, ik, kr, vt,
      jnp.broadcast_to(jnp.arange(s, dtype=I16)[:, None], (s, tq)))


def _merge_kernel(x_ref, oa_ref, ob_ref, oc_ref, od_ref, ga_ref, gb_ref, gc_ref, gd_ref,
                  wa_ref, wb_ref, wc_ref, wd_ref, wo_ref, o_ref):
    def branch(o_r, g_r, w_r):
        y = jnp.dot(o_r[...].astype(BF16), w_r[...], preferred_element_type=F32)
        return _sigmoid(g_r[...]).astype(F32) * y

    merged = (branch(oa_ref, ga_ref, wa_ref) + branch(ob_ref, gb_ref, wb_ref)
              + branch(oc_ref, gc_ref, wc_ref) + branch(od_ref, gd_ref, wd_ref))
    o_ref[...] = x_ref[...] + jnp.dot(merged.astype(BF16), wo_ref[...], preferred_element_type=F32)


def _merge(x2d, oa, ob, oc, od, gates, wa, wb, wc, wd, wo):
    t, d = x2d.shape
    tm = min(512, t)
    row = lambda w: pl.BlockSpec((tm, w), lambda i: (i, 0))
    gate = lambda n: pl.BlockSpec((tm, d), lambda i, o=n: (i, o))
    full = lambda shape: pl.BlockSpec(shape, lambda i: (0,) * len(shape))
    return pl.pallas_call(
        _merge_kernel,
        grid=(t // tm,),
        in_specs=[row(d), row(HW), row(HW), row(HW), row(HW), gate(0), gate(1), gate(2), gate(3),
                  full((HW, d)), full((HW, d)), full((HW, d)), full((HW, d)), full((d, d))],
        out_specs=row(d),
        out_shape=jax.ShapeDtypeStruct((t, d), F32),
        compiler_params=_cparams(1),
        name="merge",
    )(x2d, oa, ob, oc, od, gates, gates, gates, gates, wa, wb, wc, wd, wo)


def _mlp_kernel(x_ref, g_ref, wu_ref, wd_ref, fg_ref, o_ref, h_ref, acc_ref, *, final_norm):
    fi = pl.program_id(1)

    @pl.when(fi == 0)
    def _():
        x = x_ref[...]
        ms = jnp.mean(x * x, axis=-1, keepdims=True)
        h_ref[...] = (x * lax.rsqrt(ms + NORM_EPS) * g_ref[...]).astype(BF16)
        acc_ref[...] = jnp.zeros_like(acc_ref)

    u = jnp.dot(h_ref[...], wu_ref[...], preferred_element_type=F32)
    u = jnp.square(jnp.maximum(u, 0.0))
    acc_ref[...] += jnp.dot(u.astype(BF16), wd_ref[...], preferred_element_type=F32)

    @pl.when(fi == pl.num_programs(1) - 1)
    def _():
        y = x_ref[...] + acc_ref[...]
        if final_norm:
            ms = jnp.mean(y * y, axis=-1, keepdims=True)
            y = y * lax.rsqrt(ms + NORM_EPS) * fg_ref[...]
        o_ref[...] = y


def _mlp(x2d, g, wu, wd, fg, final_norm):
    t, d = x2d.shape
    f = wu.shape[1]
    tm = min(1024, t)
    tf = 1024
    return pl.pallas_call(
        functools.partial(_mlp_kernel, final_norm=final_norm),
        grid=(t // tm, f // tf),
        in_specs=[pl.BlockSpec((tm, d), lambda i, j: (i, 0)),
                  pl.BlockSpec((1, d), lambda i, j: (0, 0)),
                  pl.BlockSpec((d, tf), lambda i, j: (0, j)),
                  pl.BlockSpec((tf, d), lambda i, j: (j, 0)),
                  pl.BlockSpec((1, d), lambda i, j: (0, 0))],
        out_specs=pl.BlockSpec((tm, d), lambda i, j: (i, 0)),
        out_shape=jax.ShapeDtypeStruct((t, d), F32),
        scratch_shapes=[pltpu.VMEM((tm, d), BF16), pltpu.VMEM((tm, d), F32)],
        compiler_params=_cparams(2),
        name="mlp",
    )(x2d, g, wu, wd, fg)


def _slab_weight(w_in, dtype=BF16):
    d = w_in.shape[0]
    z = lambda n: jnp.zeros((d, n), w_in.dtype)
    gdn, mla, rwkv, dsa = 0, 1032, 1448, 2472
    gate = 3824
    cols = [
        w_in[:, rwkv:rwkv + 1024],
        w_in[:, gdn:gdn + 768],
        w_in[:, gdn + 776:gdn + 1032],
        w_in[:, mla:mla + 256],
        w_in[:, mla + 256:mla + 384],
        z(64), w_in[:, mla + 384:mla + 416], z(32),
        w_in[:, dsa:dsa + 512],
        w_in[:, dsa + 584:dsa + 1352],
        w_in[:, gdn + 768:gdn + 776], z(120),
        w_in[:, dsa + 512:dsa + 584], z(56),
    ]
    return jnp.concatenate(cols, axis=1).astype(dtype), w_in[:, gate:gate + 4096].astype(dtype)


def _mla_weights(w_uq, w_ukv):
    wq = w_uq.reshape(MLA_Q_LORA, HEADS, MLA_NOPE + MLA_ROPE)
    wq = jnp.pad(wq, ((0, 0), (0, 0), (0, MLA_QK - MLA_NOPE - MLA_ROPE))).reshape(MLA_Q_LORA, HEADS * MLA_QK)
    wkv = w_ukv.reshape(MLA_KV_LORA, HEADS, MLA_NOPE + DH)
    wk = jnp.pad(wkv[:, :, :MLA_NOPE], ((0, 0), (0, 0), (0, MLA_QK - MLA_NOPE))).reshape(MLA_KV_LORA, HEADS * MLA_QK)
    wvt = wkv[:, :, MLA_NOPE:].reshape(MLA_KV_LORA, HW).T
    return wq.astype(BF16), wk.astype(BF16), wvt.astype(BF16)


def _rope_cos_sin(positions, rot_dims):
    inv_freq = ROPE_THETA ** (-jnp.arange(0, rot_dims, 2, dtype=F32) / rot_dims)
    ang = positions.astype(F32)[..., None] * inv_freq
    return jnp.concatenate([jnp.cos(ang), jnp.sin(ang)], axis=-1)


def _rope_expansion(rot_dims, offset, period):
    half = rot_dims // 2
    e = np.zeros((rot_dims, 3 * 128), np.float32)
    base = np.zeros((1, 3 * 128), np.float32)
    base[0, 0:128] = 1.0
    for rep in range(128 // period):
        lo = rep * period + offset
        for j in range(half):
            e[j, lo + j] = 1.0
            e[j, lo + half + j] = 1.0
            base[0, lo + j] = 0.0
            base[0, lo + half + j] = 0.0
            e[half + j, 128 + lo + j] = -1.0
            e[half + j, 256 + lo + half + j] = 1.0
    return jnp.asarray(e), jnp.asarray(base)


def _rope_table(cs_ref, e_ref, base_ref):
    tab = _mm_exact_rhs(cs_ref[...], e_ref[...]) + base_ref[...]
    return tab[:, 0:128], tab[:, 128:256], tab[:, 256:384]


def _layer(x2d, b, s, prm, tabs, final_g, is_last):
    (mix_g, w_slab, conv_w, gdn_pcol, gdn_prow, gdn_ng, mla_qg, mla_kvg, wq, wk, wv,
     rwkv_mu, rwkv_vec, ww2, wa2, wg2, wo_gdn, wo_mla, wo_rwkv, wo_dsa, w_o, mlp_g, w_up, w_down) = prm
    mla_cs, dsa_cs = tabs
    t = b * s
    slab = _in_proj(x2d, mix_g, w_slab[0], F32)
    gates = _in_proj(x2d, mix_g, w_slab[1], BF16)

    ab = slab[:, OFF_GDN_AB:OFF_GDN_AB + 8].reshape(b, s // CHUNK, CHUNK, 8)
    abr = jnp.swapaxes(ab, 2, 3)
    o_gdn = _gdn(slab, abr, conv_w, gdn_pcol, gdn_prow, gdn_ng, b, s)
    q, k, v = _mla_prep(slab, mla_cs, mla_qg, mla_kvg, wq, wk, wv)
    o_mla = _flash(q, k, v, b, s)
    o_rwkv = _rwkv(slab, rwkv_mu, rwkv_vec, ww2, wa2, wg2, b, s)
    ik, kr, vt, iqt, qt, iwt = _dsa_prep(slab, dsa_cs)
    o_dsa = _dsa(iqt, iwt, qt, ik, kr, vt, b, s, min(IDX_TOPK_MAX, s // 4))

    x2d = _merge(x2d, o_gdn, o_mla, o_rwkv, o_dsa, gates, wo_gdn, wo_mla, wo_rwkv, wo_dsa, w_o)
    return _mlp(x2d, mlp_g, w_up, w_down, final_g, is_last)


def kernel(x, positions, mix_norm_g, w_in, gdn_conv_w, gdn_a_log, gdn_dt_bias, gdn_norm_g, mla_q_norm_g, mla_kv_norm_g, mla_w_uq, mla_w_ukv, rwkv_mu, rwkv_w0, rwkv_w_w2, rwkv_a0, rwkv_w_a2, rwkv_w_g2, rwkv_k_k, rwkv_k_a, rwkv_r_k, rwkv_ln_w, rwkv_ln_b, w_out_gdn, w_out_mla, w_out_rwkv, w_out_dsa, w_o, mlp_norm_g, w_up, w_down, final_norm_g):
    b, s, d = x.shape
    depth = w_in.shape[0]
    tabs = (_rope_cos_sin(positions, MLA_ROPE).reshape(b * s, MLA_ROPE),
            _rope_cos_sin(positions, DH // 4).reshape(b * s, DH // 4))
    row = lambda a: a.reshape(1, -1).astype(F32)
    pad8 = lambda rows: jnp.concatenate(rows + [jnp.zeros((8 - len(rows), rows[0].shape[1]), F32)], axis=0)
    x2d = x.reshape(b * s, d)
    final_g = row(final_norm_g)
    for l in range(depth):
        lane4 = lambda a: jnp.pad(a.astype(F32), (0, 124)).reshape(1, 128)
        gdn_pcol = pad8([lane4(gdn_a_log[l]), lane4(gdn_dt_bias[l])])
        gdn_prow = jnp.pad(jnp.stack([gdn_a_log[l], gdn_dt_bias[l]], axis=1).astype(F32), ((0, 4), (0, 126)))
        wq, wk, wv = _mla_weights(mla_w_uq[l], mla_w_ukv[l])
        rwkv_vec = pad8([row(rwkv_w0[l]), row(rwkv_a0[l]), row(rwkv_k_k[l]), row(rwkv_k_a[l]),
                         row(rwkv_ln_w[l]), row(rwkv_ln_b[l]), row(rwkv_r_k[l])])
        prm = (row(mix_norm_g[l]), _slab_weight(w_in[l]), gdn_conv_w[l].astype(F32), gdn_pcol, gdn_prow,
               row(jnp.tile(gdn_norm_g[l], HEADS)), row(mla_q_norm_g[l]), row(mla_kv_norm_g[l]), wq, wk, wv,
               row(rwkv_mu[l]), rwkv_vec, rwkv_w_w2[l].astype(F32), rwkv_w_a2[l].astype(F32),
               rwkv_w_g2[l].astype(F32),
               w_out_gdn[l].astype(BF16), w_out_mla[l].astype(BF16), w_out_rwkv[l].astype(BF16),
               w_out_dsa[l].astype(BF16), w_o[l].astype(BF16), row(mlp_norm_g[l]),
               w_up[l].astype(BF16), w_down[l].astype(BF16))
        x2d = _layer(x2d, b, s, prm, tabs, final_g, l == depth - 1)
    return x2d.reshape(b, s, d)
```

```python
import functools
import math

import numpy as np
import jax
import jax.numpy as jnp
from jax import lax
from jax.experimental import pallas as pl
from jax.experimental.pallas import tpu as pltpu

F32 = jnp.float32
BF16 = jnp.bfloat16
I32 = jnp.int32
I16 = jnp.int16

ROPE_THETA = 500000.0
NORM_EPS = 1e-6
NEG_INF = -1e30
HEADS = 4
DH = 64
HW = HEADS * DH
GDN_CONV = 4
CHUNK = 64
MLA_Q_LORA = 256
MLA_KV_LORA = 128
MLA_NOPE = 64
MLA_ROPE = 32
MLA_QK = 128
RWKV_GN_EPS = 64e-5
IDX_HEADS = 8
IDX_TOPK_MAX = 256
INT_MIN = -2 ** 31
LOG2E = 1.4426950408889634

VMEM_LIMIT_BYTES = 56 * 1024 * 1024

OFF_RWKV = 0
OFF_GDN_Q = 1024
OFF_GDN_K = 1280
OFF_GDN_V = 1536
OFF_GDN_Z = 1792
OFF_MLA_CQ = 2048
OFF_MLA_CKV = 2304
OFF_MLA_KR = 2432
OFF_DSA_IQ = 2560
OFF_DSA_Q = 3072
OFF_DSA_K = 3328
OFF_DSA_V = 3584
OFF_GDN_AB = 3840
OFF_DSA_IKW = 3968

NN = (((1,), (0,)), ((), ()))
NT = (((1,), (1,)), ((), ()))
BNN = (((2,), (1,)), ((0,), (0,)))
BNT = (((2,), (2,)), ((0,), (0,)))
BTN = (((1,), (1,)), ((0,), (0,)))


def _cparams(n_axes):
    return pltpu.CompilerParams(dimension_semantics=("arbitrary",) * n_axes,
                                vmem_limit_bytes=VMEM_LIMIT_BYTES)


def _mm(a, b, dims=NN):
    return lax.dot_general(a.astype(BF16), b.astype(BF16), dims, preferred_element_type=F32)


def _mm_exact_lhs(m, x, dims=NN):
    mb = m.astype(BF16)
    x0 = x.astype(BF16)
    r1 = x - x0.astype(F32)
    x1 = r1.astype(BF16)
    x2 = (r1 - x1.astype(F32)).astype(BF16)
    d = lambda y: lax.dot_general(mb, y, dims, preferred_element_type=F32)
    return d(x0) + (d(x1) + d(x2))


def _mm_exact_rhs(x, m, dims=NN):
    mb = m.astype(BF16)
    x0 = x.astype(BF16)
    r1 = x - x0.astype(F32)
    x1 = r1.astype(BF16)
    x2 = (r1 - x1.astype(F32)).astype(BF16)
    d = lambda y: lax.dot_general(y, mb, dims, preferred_element_type=F32)
    return d(x0) + (d(x1) + d(x2))


def _iota2(shape, dim):
    return lax.broadcasted_iota(I32, shape, dim)


def _seg_ones(width, seg):
    sh = int(math.log2(seg))
    r = _iota2((width, width), 0) >> sh
    c = _iota2((width, width), 1) >> sh
    return jnp.where(r == c, 1.0, 0.0).astype(F32)


def _seg_sum(x, seg_ones):
    return _mm_exact_rhs(x, seg_ones)


def _softplus(x):
    return jnp.maximum(x, 0.0) + jnp.log1p(jnp.exp(-jnp.abs(x)))


def _sigmoid(x):
    return 0.5 * jnp.tanh(0.5 * x) + 0.5


def _silu(x):
    return x * _sigmoid(x)


def _tri_masks(n):
    r = _iota2((n, n), 0)
    c = _iota2((n, n), 1)
    return r, c


def _unit_lower_inverse(m, r, c, dims=NN):
    eye = jnp.where(r == c, 1.0, 0.0).astype(F32)
    same16 = (r >> 4) == (c >> 4)
    same32 = (r >> 5) == (c >> 5)
    md = jnp.where(same16, m, 0.0)
    t = eye - md
    p = md
    for _ in range(3):
        p = _mm(p, p, dims)
        t = t + _mm(t, p, dims)
    c32 = jnp.where(same32 & jnp.logical_not(same16), m, 0.0)
    t = t - _mm(_mm(t, c32, dims), t, dims)
    c64 = jnp.where(same32, 0.0, m)
    t = t - _mm(_mm(t, c64, dims), t, dims)
    return t


IN_PROJ_TN = 512


def _in_proj_kernel(x_ref, g_ref, w_ref, o_ref):
    x = x_ref[...]
    ms = jnp.mean(x * x, axis=-1, keepdims=True)
    h = (x * lax.rsqrt(ms + NORM_EPS) * g_ref[...]).astype(BF16)
    for n0 in range(0, o_ref.shape[1], IN_PROJ_TN):
        o_ref[:, n0:n0 + IN_PROJ_TN] = jnp.dot(
            h, w_ref[:, n0:n0 + IN_PROJ_TN], preferred_element_type=F32).astype(o_ref.dtype)


def _in_proj(x2d, g, w, out_dtype):
    t, d = x2d.shape
    n = w.shape[1]
    tm = min(512, t)
    return pl.pallas_call(
        _in_proj_kernel,
        grid=(t // tm,),
        in_specs=[pl.BlockSpec((tm, d), lambda i: (i, 0)),
                  pl.BlockSpec((1, d), lambda i: (0, 0)),
                  pl.BlockSpec((d, n), lambda i: (0, 0))],
        out_specs=pl.BlockSpec((tm, n), lambda i: (i, 0)),
        out_shape=jax.ShapeDtypeStruct((t, n), out_dtype),
        compiler_params=_cparams(1),
        name="in_proj",
    )(x2d, g, w)


def _to_groups(x, nchunk):
    parts = [x[:, h * DH:(h + 1) * DH].reshape(nchunk, 1, CHUNK, DH) for h in range(HEADS)]
    return jnp.concatenate(parts, axis=1).reshape(nchunk * HEADS, CHUNK, DH)


def _col_groups(x, lane0, nchunk):
    parts = [x[:, lane0 + h:lane0 + h + 1].reshape(nchunk, 1, CHUNK, 1) for h in range(HEADS)]
    return jnp.concatenate(parts, axis=1).reshape(nchunk * HEADS, CHUNK, 1)


def _gdn_kernel(q_ref, k_ref, v_ref, z_ref, ab_ref, abr_ref, cw_ref, pcol_ref, prow_ref, ng_ref,
                o_ref, xbuf_ref, state_ref, u_ref, w_ref, a_ref, qd_ref, kd_ref, egl_ref, oraw_ref,
                carry_ref, *, tb, nbatch):
    nchunk = tb // CHUNK
    bi = pl.program_id(1)
    xbuf = xbuf_ref

    @pl.when(pl.program_id(0) == 0)
    def _():
        carry_ref[bi] = jnp.zeros((8, 3 * HW), F32)
        state_ref[bi] = jnp.zeros((HEADS, DH, DH), F32)

    xbuf[0:8, :] = carry_ref[bi]
    xbuf[8:8 + tb, 0:HW] = q_ref[...]
    xbuf[8:8 + tb, HW:2 * HW] = k_ref[...]
    xbuf[8:8 + tb, 2 * HW:3 * HW] = v_ref[...]
    cw = cw_ref[...]
    y = xbuf[8:8 + tb, :] * cw[3:4, :]
    y = y + xbuf[7:7 + tb, :] * cw[2:3, :]
    y = y + xbuf[6:6 + tb, :] * cw[1:2, :]
    y = y + xbuf[5:5 + tb, :] * cw[0:1, :]
    carry_ref[bi] = xbuf[tb:tb + 8, :]
    y = _silu(y)
    seg = _seg_ones(HW, DH)
    q = y[:, 0:HW]
    k = y[:, HW:2 * HW]
    q = q * lax.rsqrt(_seg_sum(q * q, seg) + 1e-6) * (DH ** -0.5)
    k = k * lax.rsqrt(_seg_sum(k * k, seg) + 1e-6)
    v = y[:, 2 * HW:3 * HW]

    ab = ab_ref[...]
    pcol = pcol_ref[...]
    g_col = -jnp.exp(pcol[0:1, :]) * _softplus(ab + pcol[1:2, :])
    beta_col = _sigmoid(ab)

    r, c = _tri_masks(CHUNK)
    incl = c <= r
    strict = c < r
    l_incl = jnp.where(incl, 1.0, 0.0).astype(F32)

    rb = _iota2((tb, tb), 0)
    cb = _iota2((tb, tb), 1)
    l_blk = jnp.where(((rb >> 6) == (cb >> 6)) & (cb <= rb), 1.0, 0.0).astype(F32)
    gc_col = _mm_exact_lhs(l_blk, g_col)
    prow = prow_ref[...]
    abr = abr_ref[0].reshape(nchunk * 8, CHUNK)
    prow_t = jnp.concatenate([prow] * nchunk, axis=0)
    g_row = -jnp.exp(prow_t[:, 0:1]) * _softplus(abr + prow_t[:, 1:2])
    gc_row = _mm_exact_rhs(g_row, l_incl, NT).reshape(nchunk, 8, CHUNK)

    qg, kg, vg = _to_groups(q, nchunk), _to_groups(k, nchunk), _to_groups(v, nchunk)
    beta = _col_groups(beta_col, 4, nchunk)
    gcol = _col_groups(gc_col, 0, nchunk)
    grow = jnp.concatenate([gc_row[:, h:h + 1, :].reshape(nchunk, 1, 1, CHUNK) for h in range(HEADS)],
                           axis=1).reshape(nchunk * HEADS, 1, CHUNK)
    decay = jnp.where(incl, jnp.exp(jnp.where(incl, gcol - grow, 0.0)), 0.0)
    kb = kg * beta
    m = jnp.where(strict, _mm(kb, kg, BNT) * decay, 0.0)
    t_inv = _unit_lower_inverse(m, r, c, BNN)
    egc = jnp.exp(gcol)
    glast = gcol[:, CHUNK - 1:CHUNK, :]
    uw = _mm(t_inv, jnp.concatenate([vg * beta, kb * egc], axis=2), BNN)
    u_ref[bi] = uw[:, :, 0:DH]
    w_ref[bi] = uw[:, :, DH:2 * DH].astype(BF16)
    a_ref[bi] = (_mm(qg, kg, BNT) * decay).astype(BF16)
    qd_ref[bi] = (qg * egc).astype(BF16)
    kd_ref[bi] = (kg * jnp.exp(glast - gcol)).astype(BF16)
    egl_ref[bi] = jnp.broadcast_to(jnp.exp(glast), (nchunk * HEADS, 1, DH))

    @pl.when(bi == nbatch - 1)
    def _():
        nc = nbatch * HEADS

        def chunk_body(ci, carry):
            sl = pl.ds(pl.multiple_of(ci * HEADS, HEADS), HEADS)
            take = lambda ref: ref[:, sl].reshape((nc,) + ref.shape[2:])
            s = state_ref[...].reshape(nc, DH, DH)
            v_new = take(u_ref) - _mm(take(w_ref), s, BNN)
            o = _mm(take(qd_ref), s, BNN) + _mm(take(a_ref), v_new, BNN)
            s = s * take(egl_ref) + _mm(take(kd_ref), v_new, BTN)
            state_ref[...] = s.reshape(nbatch, HEADS, DH, DH)
            rows = pl.ds(pl.multiple_of(ci * CHUNK, CHUNK), CHUNK)
            for bb in range(nbatch):
                oraw_ref[bb, rows, :] = jnp.concatenate([o[bb * HEADS + h] for h in range(HEADS)], axis=1)
            return carry

        lax.fori_loop(0, nchunk, chunk_body, 0)
        o_all = oraw_ref[...].reshape(nbatch * tb, HW)
        ms = _seg_sum(o_all * o_all, seg) * (1.0 / DH)
        o_n = o_all * lax.rsqrt(ms + NORM_EPS) * ng_ref[...] * _silu(z_ref[...].reshape(nbatch * tb, HW))
        o_ref[...] = o_n.reshape(nbatch, tb, HW).astype(o_ref.dtype)


GDN_TB = 512


def _gdn(slab, abr, conv_w, pcol, prow, norm_g, b, s):
    tb = min(GDN_TB, s)
    nb = s // tb
    nchunk = tb // CHUNK
    ng = nchunk * HEADS
    grp = lambda w, dt=BF16: pltpu.VMEM((b, ng, CHUNK, w), dt)
    col = lambda off, w: pl.BlockSpec((tb, w), lambda j, i, o=off // w: (i * nb + j, o))
    full = lambda shape: pl.BlockSpec(shape, lambda j, i: (0,) * len(shape))
    slab3 = slab.reshape(b, s, slab.shape[1])
    out = pl.pallas_call(
        functools.partial(_gdn_kernel, tb=tb, nbatch=b),
        grid=(nb, b),
        in_specs=[col(OFF_GDN_Q, HW), col(OFF_GDN_K, HW), col(OFF_GDN_V, HW),
                  pl.BlockSpec((b, tb, HW), lambda j, i: (0, j, OFF_GDN_Z // HW)),
                  col(OFF_GDN_AB, 128),
                  pl.BlockSpec((1, nchunk, 8, CHUNK), lambda j, i: (i, j, 0, 0)),
                  full((GDN_CONV, 3 * HW)), full((8, 128)), full((8, 128)), full((1, HW))],
        out_specs=pl.BlockSpec((b, tb, HW), lambda j, i: (0, j, 0)),
        out_shape=jax.ShapeDtypeStruct((b, s, HW), BF16),
        scratch_shapes=[pltpu.VMEM((tb + 8, 3 * HW), F32), pltpu.VMEM((b, HEADS, DH, DH), F32),
                        grp(DH, F32), grp(DH), grp(CHUNK), grp(DH), grp(DH),
                        pltpu.VMEM((b, ng, 1, DH), F32), pltpu.VMEM((b, tb, HW), F32),
                        pltpu.VMEM((b, 8, 3 * HW), F32)],
        compiler_params=_cparams(2),
        name="gdn",
    )(slab, slab, slab, slab3, slab, abr, conv_w, pcol, prow, norm_g)
    return out.reshape(b * s, HW)


def _rwkv_kernel(p_ref, mu_ref, vec_ref, ww2_ref, wa2_ref, wg2_ref, o_ref,
                 xbuf_ref, state_ref, ta_ref, tpv_ref, rt_ref, qkv_ref, qb_ref, bc_ref, ktv_ref,
                 el_ref, oraw_ref, carry_ref, post_ref, *, tb, nbatch):
    nchunk = tb // CHUNK
    bi = pl.program_id(1)

    @pl.when(pl.program_id(0) == 0)
    def _():
        carry_ref[bi] = jnp.zeros((8, 4 * HW), F32)
        state_ref[bi] = jnp.zeros((HEADS, DH, DH), F32)

    p = p_ref[...]
    xbuf_ref[0:8, :] = carry_ref[bi]
    xbuf_ref[8:8 + tb, :] = p
    prev = xbuf_ref[7:7 + tb, :]
    carry_ref[bi] = xbuf_ref[tb:tb + 8, :]
    pm = p + (prev - p) * mu_ref[...]
    vec = vec_ref[...]
    w0, a0, k_k, k_a = vec[0:1, :], vec[1:2, :], vec[2:3, :], vec[3:4, :]
    rr = pm[:, 0:HW]
    k = pm[:, HW:2 * HW]
    v = pm[:, 2 * HW:3 * HW]
    xw = pm[:, 768:832]
    xa = pm[:, 832:896]
    xg = pm[:, 896:1024]
    w_log = -_softplus(-(w0 + _mm(jnp.tanh(xw), ww2_ref[...]))) - 0.5
    lw = -jnp.exp(w_log)
    a = _sigmoid(a0 + _mm(xa, wa2_ref[...]))
    g = _mm(_sigmoid(xg), wg2_ref[...])
    seg = _seg_ones(HW, DH)
    kk = k * k_k
    kk = kk * lax.rsqrt(_seg_sum(kk * kk, seg) + 1e-6)
    k2 = k * (1.0 + (a - 1.0) * k_a)
    kka = kk * a

    r_i, c_i = _tri_masks(CHUNK)
    incl = c_i <= r_i
    strict = c_i < r_i
    rb = _iota2((tb, tb), 0)
    cb = _iota2((tb, tb), 1)
    l_blk = jnp.where(((rb >> 6) == (cb >> 6)) & (cb <= rb), 1.0, 0.0).astype(F32)
    lwc_all = _mm_exact_lhs(l_blk, lw)

    rg, kg, vg = _to_groups(rr, nchunk), _to_groups(k2, nchunk), _to_groups(v, nchunk)
    kkg, kkag = _to_groups(kk, nchunk), _to_groups(kka, nchunk)
    lwg, lwc = _to_groups(lw, nchunk), _to_groups(lwc_all, nchunk)
    lwl = lwc[:, CHUNK - 1:CHUNK, :]
    e_neg = jnp.exp(-lwc)
    at = kkg * jnp.exp(lwc - lwg)
    rt = rg * jnp.exp(lwc)
    y2 = jnp.concatenate([at, rt], axis=1)
    x2 = jnp.concatenate([kkag * e_neg, kg * e_neg], axis=1)
    gram = _mm(y2, x2, BNT)
    n_m = jnp.where(strict, gram[:, 0:CHUNK, 0:CHUNK], 0.0)
    p_m = jnp.where(strict, gram[:, 0:CHUNK, CHUNK:], 0.0)
    qb_m = jnp.where(incl, gram[:, CHUNK:, 0:CHUNK], 0.0)
    qk_m = jnp.where(incl, gram[:, CHUNK:, CHUNK:], 0.0)
    t_inv = _unit_lower_inverse(n_m, r_i, c_i, BNN)
    ta_ref[bi] = _mm(t_inv, at, BNN).astype(BF16)
    tpv_ref[bi] = _mm(t_inv, _mm(p_m, vg, BNN), BNN)
    rt_ref[bi] = rt.astype(BF16)
    qkv_ref[bi] = _mm(qk_m, vg, BNN)
    qb_ref[bi] = qb_m.astype(BF16)
    e_rel = jnp.exp(lwl - lwc)
    bc_ref[bi] = (kkag * e_rel).astype(BF16)
    ktv_ref[bi] = _mm(vg, kg * e_rel, BTN)
    el_ref[bi] = jnp.exp(lwl)
    ln_w, ln_b, r_k = vec[4:5, :], vec[5:6, :], vec[6:7, :]
    post_ref[bi, :, 0:HW] = _seg_sum(rr * k2 * r_k, seg) * v
    post_ref[bi, :, HW:2 * HW] = g

    @pl.when(bi == nbatch - 1)
    def _():
        nc = nbatch * HEADS

        def chunk_body(ci, carry):
            sl = pl.ds(pl.multiple_of(ci * HEADS, HEADS), HEADS)
            take = lambda ref: ref[:, sl].reshape((nc,) + ref.shape[2:])
            zt = state_ref[...].reshape(nc, DH, DH)
            u = _mm(take(ta_ref), zt, BNT) + take(tpv_ref)
            o = _mm(take(rt_ref), zt, BNT) + take(qkv_ref) - _mm(take(qb_ref), u, BNN)
            zt = zt * take(el_ref) + take(ktv_ref) - _mm(u, take(bc_ref), BTN)
            state_ref[...] = zt.reshape(nbatch, HEADS, DH, DH)
            rows = pl.ds(pl.multiple_of(ci * CHUNK, CHUNK), CHUNK)
            for bb in range(nbatch):
                oraw_ref[bb, rows, :] = jnp.concatenate([o[bb * HEADS + h] for h in range(HEADS)], axis=1)
            return carry

        lax.fori_loop(0, nchunk, chunk_body, 0)
        o_all = oraw_ref[...].reshape(nbatch * tb, HW)
        mean = _seg_sum(o_all, seg) * (1.0 / DH)
        cen = o_all - mean
        var = _seg_sum(cen * cen, seg) * (1.0 / DH)
        o_n = cen * lax.rsqrt(var + RWKV_GN_EPS) * ln_w + ln_b
        post = post_ref[...].reshape(nbatch * tb, 2 * HW)
        o_ref[...] = ((o_n + post[:, 0:HW]) * post[:, HW:2 * HW]).reshape(nbatch, tb, HW).astype(o_ref.dtype)


RWKV_TB = 512


def _rwkv(slab, mu, vec, ww2, wa2, wg2, b, s):
    tb = min(RWKV_TB, s)
    nb = s // tb
    ng = (tb // CHUNK) * HEADS
    grp = lambda dt=F32: pltpu.VMEM((b, ng, CHUNK, DH), dt)
    full = lambda shape: pl.BlockSpec(shape, lambda j, i: (0,) * len(shape))
    out = pl.pallas_call(
        functools.partial(_rwkv_kernel, tb=tb, nbatch=b),
        grid=(nb, b),
        in_specs=[pl.BlockSpec((tb, 4 * HW), lambda j, i: (i * nb + j, OFF_RWKV // (4 * HW))),
                  full((1, 4 * HW)), full((8, HW)), full((64, HW)), full((64, HW)), full((128, HW))],
        out_specs=pl.BlockSpec((b, tb, HW), lambda j, i: (0, j, 0)),
        out_shape=jax.ShapeDtypeStruct((b, s, HW), BF16),
        scratch_shapes=[pltpu.VMEM((tb + 8, 4 * HW), F32), pltpu.VMEM((b, HEADS, DH, DH), F32),
                        grp(BF16), grp(), grp(BF16), grp(), grp(BF16), grp(BF16), grp(),
                        pltpu.VMEM((b, ng, 1, DH), F32), pltpu.VMEM((b, tb, HW), F32),
                        pltpu.VMEM((b, 8, 4 * HW), F32), pltpu.VMEM((b, tb, 2 * HW), F32)],
        compiler_params=_cparams(2),
        name="rwkv7",
    )(slab, mu, vec, ww2, wa2, wg2)
    return out.reshape(b * s, HW)


def _rope(x, cos_t, sin_lo, sin_hi, half):
    w = x.shape[-1]
    return x * cos_t + pltpu.roll(x, w - half, 1) * sin_lo + pltpu.roll(x, half, 1) * sin_hi


def _tile_lanes(t, n):
    return t if n == 1 else jnp.concatenate([t] * n, axis=1)


def _mla_prep_kernel(cq_ref, ckv_ref, kr_ref, cs_ref, e_ref, base_ref, qg_ref, kvg_ref, wq_ref, wk_ref,
                     wvt_ref, q_ref, k_ref, vt_ref):
    cq = cq_ref[...]
    hq = cq * lax.rsqrt(jnp.mean(cq * cq, axis=-1, keepdims=True) + NORM_EPS) * qg_ref[...]
    ckv = ckv_ref[...]
    hkv = ckv * lax.rsqrt(jnp.mean(ckv * ckv, axis=-1, keepdims=True) + NORM_EPS) * kvg_ref[...]
    cos_t, sin_lo, sin_hi = _rope_table(cs_ref, e_ref, base_ref)
    q = _mm(hq, wq_ref[...])
    q = _rope(q, _tile_lanes(cos_t, HEADS), _tile_lanes(sin_lo, HEADS), _tile_lanes(sin_hi, HEADS),
              MLA_ROPE // 2)
    q_ref[...] = (q * ((MLA_NOPE + MLA_ROPE) ** -0.5 * LOG2E)).astype(BF16)
    kr = _rope(kr_ref[...], cos_t, sin_lo, sin_hi, MLA_ROPE // 2)
    k = _mm(hkv, wk_ref[...]) + _tile_lanes(kr, HEADS)
    k_ref[...] = k.astype(BF16)
    vt_ref[...] = lax.dot_general(wvt_ref[...], hkv.astype(BF16), NT,
                                  preferred_element_type=F32).astype(BF16)


def _mla_prep(slab, cs, qg, kvg, wq, wk, wvt):
    t = slab.shape[0]
    tm = min(512, t)
    full = lambda shape: pl.BlockSpec(shape, lambda i: (0,) * len(shape))
    e, base = _rope_expansion(MLA_ROPE, MLA_NOPE, MLA_QK)
    return pl.pallas_call(
        _mla_prep_kernel,
        grid=(t // tm,),
        in_specs=[pl.BlockSpec((tm, 256), lambda i: (i, OFF_MLA_CQ // 256)),
                  pl.BlockSpec((tm, 128), lambda i: (i, OFF_MLA_CKV // 128)),
                  pl.BlockSpec((tm, 128), lambda i: (i, OFF_MLA_KR // 128)),
                  pl.BlockSpec((tm, MLA_ROPE), lambda i: (i, 0)), full(e.shape), full(base.shape),
                  full((1, 256)), full((1, 128)), full((256, 512)), full((128, 512)), full((256, 128))],
        out_specs=[pl.BlockSpec((tm, 512), lambda i: (i, 0)), pl.BlockSpec((tm, 512), lambda i: (i, 0)),
                   pl.BlockSpec((256, tm), lambda i: (0, i))],
        out_shape=[jax.ShapeDtypeStruct((t, 512), BF16), jax.ShapeDtypeStruct((t, 512), BF16),
                   jax.ShapeDtypeStruct((256, t), BF16)],
        compiler_params=_cparams(1),
        name="mla_prep",
    )(slab, slab, slab, cs, e, base, qg, kvg, wq, wk, wvt)


def _flash_kernel(qi_tab, ki_tab, q_ref, k_ref, vt_ref, o_ref, m_ref, l_ref, acc_ref, *, tq, tk):
    qi = qi_tab[pl.program_id(1)]
    ki = ki_tab[pl.program_id(1)]

    @pl.when(ki == 0)
    def _():
        m_ref[...] = jnp.full_like(m_ref, NEG_INF)
        l_ref[...] = jnp.zeros_like(l_ref)
        acc_ref[...] = jnp.zeros_like(acc_ref)

    def step(on_diagonal):
        if on_diagonal:
            causal = _iota2((tk, tq), 0) <= _iota2((tk, tq), 1)
        vt = vt_ref[...]
        ones = jnp.ones((16, tk), BF16)
        scores = [lax.dot_general(k_ref[:, h * MLA_QK:(h + 1) * MLA_QK], q_ref[:, h * MLA_QK:(h + 1) * MLA_QK],
                                  NT, preferred_element_type=F32) for h in range(HEADS)]
        for h in range(HEADS):
            s = scores[h]
            if on_diagonal:
                s = jnp.where(causal, s, NEG_INF)
            m_old = m_ref[h]
            m_new = jnp.maximum(m_old, jnp.max(s, axis=0, keepdims=True))
            alpha = jnp.exp2(m_old - m_new)
            p = jnp.exp2(s - m_new).astype(BF16)
            pv = jnp.dot(jnp.concatenate([vt[h * DH:(h + 1) * DH, :], ones], axis=0), p,
                         preferred_element_type=F32)
            l_ref[h] = alpha * l_ref[h] + pv[DH:DH + 1, :]
            acc_ref[h * DH:(h + 1) * DH, :] = alpha * acc_ref[h * DH:(h + 1) * DH, :] + pv[0:DH, :]
            m_ref[h] = m_new

    pl.when(ki < qi)(lambda: step(False))

    @pl.when(ki == qi)
    def _():
        step(True)
        o_t = jnp.concatenate([acc_ref[h * DH:(h + 1) * DH, :] / l_ref[h] for h in range(HEADS)], axis=0)
        o_ref[...] = o_t.T.astype(o_ref.dtype)


def _flash(q, k, vt, b, s):
    tq = min(512, s)
    tk = tq
    nq = s // tq
    pairs = [(qb, kb) for qb in range(nq) for kb in range(qb + 1)]
    qi_tab = jnp.asarray([p[0] for p in pairs], I32)
    ki_tab = jnp.asarray([p[1] for p in pairs], I32)
    return pl.pallas_call(
        functools.partial(_flash_kernel, tq=tq, tk=tk),
        grid_spec=pltpu.PrefetchScalarGridSpec(
            num_scalar_prefetch=2,
            grid=(b, len(pairs)),
            in_specs=[pl.BlockSpec((tq, 512), lambda i, p, qt, kt: (i * nq + qt[p], 0)),
                      pl.BlockSpec((tk, 512), lambda i, p, qt, kt: (i * nq + kt[p], 0)),
                      pl.BlockSpec((256, tk), lambda i, p, qt, kt: (0, i * nq + kt[p]))],
            out_specs=pl.BlockSpec((tq, 256), lambda i, p, qt, kt: (i * nq + qt[p], 0)),
            scratch_shapes=[pltpu.VMEM((HEADS, 1, tq), F32), pltpu.VMEM((HEADS, 1, tq), F32),
                            pltpu.VMEM((HW, tq), F32)]),
        out_shape=jax.ShapeDtypeStruct((b * s, 256), BF16),
        compiler_params=_cparams(2),
        name="mla_flash",
    )(qi_tab, ki_tab, q, k, vt)


DSA_TQ = 256
DSA_TK = 256


def _dsa_prep_kernel(iq_ref, ikw_ref, q_ref, k_ref, v_ref, cs_ref, e_ref, base_ref,
                     ik_ref, ko_ref, vt_ref, iqt_ref, qt_ref, iwt_ref):
    cos_t, sin_lo, sin_hi = _rope_table(cs_ref, e_ref, base_ref)
    half = DH // 8
    rope = lambda x, n: _rope(x, _tile_lanes(cos_t, n), _tile_lanes(sin_lo, n), _tile_lanes(sin_hi, n), half)
    ikw = ikw_ref[...]
    ik_ref[...] = rope(ikw, 1)[:, 0:DH].astype(BF16)
    kr = rope(k_ref[...], 2)
    for h in range(HEADS):
        ko_ref[h] = kr[:, h * DH:(h + 1) * DH].astype(BF16)
    vt_ref[0] = v_ref[...].T.astype(BF16)
    iqt_ref[...] = rope(iq_ref[...], 4).T.astype(BF16)
    qt_ref[...] = (rope(q_ref[...], 2) * (DH ** -0.5 * LOG2E)).T.astype(BF16)
    iwt_ref[...] = ikw.T[DH:DH + 8, :] * (IDX_HEADS ** -0.5 * DH ** -0.5)


def _dsa_prep(slab, cs):
    t = slab.shape[0]
    tm = min(DSA_TK, t)
    full = lambda shape: pl.BlockSpec(shape, lambda i: (0,) * len(shape))
    e, base = _rope_expansion(DH // 4, 0, DH)
    return pl.pallas_call(
        _dsa_prep_kernel,
        grid=(t // tm,),
        in_specs=[pl.BlockSpec((tm, IDX_HEADS * DH), lambda i: (i, OFF_DSA_IQ // (IDX_HEADS * DH))),
                  pl.BlockSpec((tm, 128), lambda i: (i, OFF_DSA_IKW // 128)),
                  pl.BlockSpec((tm, 256), lambda i: (i, OFF_DSA_Q // 256)),
                  pl.BlockSpec((tm, 256), lambda i: (i, OFF_DSA_K // 256)),
                  pl.BlockSpec((tm, 256), lambda i: (i, OFF_DSA_V // 256)),
                  pl.BlockSpec((tm, DH // 4), lambda i: (i, 0)), full(e.shape), full(base.shape)],
        out_specs=[pl.BlockSpec((tm, DH), lambda i: (i, 0)),
                   pl.BlockSpec((HEADS, tm, DH), lambda i: (0, i, 0)),
                   pl.BlockSpec((1, HW, tm), lambda i: (i, 0, 0)),
                   pl.BlockSpec((IDX_HEADS * DH, tm), lambda i: (0, i)),
                   pl.BlockSpec((HW, tm), lambda i: (0, i)),
                   pl.BlockSpec((8, tm), lambda i: (0, i))],
        out_shape=[jax.ShapeDtypeStruct((t, DH), BF16), jax.ShapeDtypeStruct((HEADS, t, DH), BF16),
                   jax.ShapeDtypeStruct((t // tm, HW, tm), BF16),
                   jax.ShapeDtypeStruct((IDX_HEADS * DH, t), BF16), jax.ShapeDtypeStruct((HW, t), BF16),
                   jax.ShapeDtypeStruct((8, t), F32)],
        compiler_params=_cparams(1),
        name="dsa_prep",
    )(slab, slab, slab, slab, slab, cs, e, base)


def _dsa_kernel(iqt_ref, iwt_ref, qt_ref, ik_ref, k_ref, vt_ref, kidx_ref, o_ref,
                hi_ref, lo_ref, lom_ref, bias_ref, acc_ref, s_ref, m_ref, l_ref,
                *, tq, tk, n_sel, idx_bits):
    qi = pl.program_id(1)
    nkt = (qi * tq + tq - 1) // tk + 1
    iqt = iqt_ref[...]
    iw = iwt_ref[...]
    q_pos = qi * tq + _iota2((tk, tq), 1)
    qt = qt_ref[...]

    def score_body(kt, carry):
        row0 = pl.multiple_of(kt * tk, tk)
        ik = ik_ref[pl.ds(row0, tk), :]
        sc = jnp.zeros((tk, tq), F32)
        for h in range(IDX_HEADS):
            lg = lax.dot_general(ik, iqt[h * DH:(h + 1) * DH, :], NN, preferred_element_type=F32)
            sc = sc + jnp.maximum(lg, 0.0) * iw[h:h + 1, :]
        bits = lax.bitcast_convert_type(sc, I32)
        key = bits ^ ((bits >> 31) & 0x7FFFFFFF)
        key = jnp.where(sc == 0.0, 0, key)
        k_pos = row0 + _iota2((tk, tq), 0)
        key = jnp.where(k_pos <= q_pos, key, INT_MIN)
        hi_ref[pl.ds(row0, tk), :] = (key >> 16).astype(I16)
        lo_ref[pl.ds(row0, tk), :] = ((key & 0xFFFF) - 32768).astype(I16)
        for h in range(HEADS):
            s_ref[h, pl.ds(row0, tk), :] = lax.dot_general(k_ref[h, pl.ds(row0, tk), :],
                                                           qt[h * DH:(h + 1) * DH, :], NN,
                                                           preferred_element_type=F32)
        return carry

    lax.fori_loop(0, nkt, score_body, 0)

    one16 = jnp.ones((tk, tq), I16)
    zero16 = jnp.zeros((tk, tq), I16)

    def count(*pred_fns):
        def body(kt, accs):
            sl = pl.ds(pl.multiple_of(kt * tk, tk), tk)
            args = (hi_ref[sl, :], lo_ref[sl, :], lom_ref[sl, :], kidx_ref[sl, :])
            out = []
            for acc, fn in zip(accs, pred_fns):
                ones = jnp.where(fn(*args), one16, zero16).reshape(tk // 64, 64, tq)
                for g in range(tk // 64):
                    acc = acc + ones[g]
                out.append(acc)
            return tuple(out)
        parts = lax.fori_loop(0, nkt, body, tuple(jnp.zeros((64, tq), I16) for _ in pred_fns))
        sums = tuple(jnp.sum(p.astype(I32).astype(F32), axis=0, keepdims=True) for p in parts)
        return sums[0] if len(sums) == 1 else sums

    def bisect16(pred_ge):
        def body(it, thr):
            bit = jnp.maximum(15 - it, 0)
            cand = jnp.where(it == 0, jnp.zeros_like(thr), thr | jnp.left_shift(jnp.int32(1), bit))
            c16 = cand.astype(I16)
            cnt = count(lambda hi, lo, lom, idx: pred_ge(hi, lo, lom, c16))
            return jnp.where(cnt >= float(n_sel), cand, thr)
        return lax.fori_loop(0, 16, body, jnp.full((1, tq), -32768, I32))

    t_hi = bisect16(lambda hi, lo, lom, c: hi >= c)
    t_hi16 = t_hi.astype(I16)

    def mask_lo_body(kt, carry):
        sl = pl.ds(pl.multiple_of(kt * tk, tk), tk)
        hi = hi_ref[sl, :]
        lom_ref[sl, :] = jnp.where(hi > t_hi16, jnp.int16(32767),
                                   jnp.where(hi == t_hi16, lo_ref[sl, :], jnp.int16(-32768)))
        return carry

    lax.fori_loop(0, nkt, mask_lo_body, 0)
    t_lo16 = bisect16(lambda hi, lo, lom, c: lom >= c).astype(I16)

    def is_gt(hi, lo):
        return (hi > t_hi16) | ((hi == t_hi16) & (lo > t_lo16))

    def is_eq(hi, lo):
        return (hi == t_hi16) & (lo == t_lo16)

    def write_bias(sl, hi, sel_eq, gt):
        sel = (gt | sel_eq) & (hi != jnp.int16(-32768))
        bias_ref[sl, :] = jnp.where(sel, jnp.zeros((tk, tq), BF16),
                                    jnp.full((tk, tq), NEG_INF, BF16)).astype(F32)

    def tally_body(kt, accs):
        sl = pl.ds(pl.multiple_of(kt * tk, tk), tk)
        hi, lo = hi_ref[sl, :], lo_ref[sl, :]
        gt, eq = is_gt(hi, lo), is_eq(hi, lo)
        write_bias(sl, hi, eq, gt)
        out = []
        for acc, m in zip(accs, (gt, eq)):
            ones = jnp.where(m, one16, zero16).reshape(tk // 64, 64, tq)
            for g in range(tk // 64):
                acc = acc + ones[g]
            out.append(acc)
        return tuple(out)

    parts = lax.fori_loop(0, nkt, tally_body, (jnp.zeros((64, tq), I16), jnp.zeros((64, tq), I16)))
    cnt_gt, cnt_eq = (jnp.sum(p.astype(I32).astype(F32), axis=0, keepdims=True) for p in parts)
    need = float(n_sel) - cnt_gt

    @pl.when(jnp.max(jnp.where(cnt_eq > need, 1.0, 0.0)) > 0.0)
    def _():
        def idx_body(it, j):
            cand = j + jnp.left_shift(jnp.int32(1), idx_bits - 1 - it)
            c16 = cand.astype(I16)
            cnt = count(lambda hi, lo, lom, idx: is_eq(hi, lo) & (idx < c16))
            return jnp.where(cnt < need, cand, j)

        j_cut16 = lax.fori_loop(0, idx_bits, idx_body, jnp.zeros((1, tq), I32)).astype(I16)

        def rewrite_body(kt, carry):
            sl = pl.ds(pl.multiple_of(kt * tk, tk), tk)
            hi, lo = hi_ref[sl, :], lo_ref[sl, :]
            write_bias(sl, hi, is_eq(hi, lo) & (kidx_ref[sl, :] <= j_cut16), is_gt(hi, lo))
            return carry

        lax.fori_loop(0, nkt, rewrite_body, 0)

    acc_ref[...] = jnp.zeros_like(acc_ref)

    def att_body(kt, carry):
        sl = pl.ds(pl.multiple_of(kt * tk, tk), tk)
        bias = bias_ref[sl, :]
        ones = jnp.ones((16, tk), BF16)
        for h in range(HEADS):
            hs = slice(h * DH, (h + 1) * DH)
            s = s_ref[h, sl, :] + bias
            m_old = m_ref[h]
            m_new = jnp.maximum(m_old, jnp.max(s, axis=0, keepdims=True))
            alpha = jnp.exp2(m_old - m_new)
            p = jnp.exp2(s - m_new).astype(BF16)
            pv = lax.dot_general(jnp.concatenate([vt_ref[kt, hs, :], ones], axis=0), p, NN,
                                 preferred_element_type=F32)
            acc_ref[hs, :] = alpha * acc_ref[hs, :] + pv[0:DH, :]
            m_ref[h] = m_new
            l_ref[h] = alpha * l_ref[h] + pv[DH:DH + 1, :]
        return carry

    m_ref[...] = jnp.full_like(m_ref, NEG_INF)
    l_ref[...] = jnp.zeros_like(l_ref)
    lax.fori_loop(0, nkt, att_body, 0)
    o_t = jnp.concatenate([acc_ref[h * DH:(h + 1) * DH, :] / l_ref[h] for h in range(HEADS)], axis=0)
    o_ref[...] = o_t.T.astype(o_ref.dtype)


def _dsa(iqt, iwt, qt, ik, kr, vt, b, s, n_sel):
    tq = min(DSA_TQ, s)
    tk = min(DSA_TK, s)
    nq = s // tq
    nkt = s // tk
    return pl.pallas_call(
        functools.partial(_dsa_kernel, tq=tq, tk=tk, n_sel=n_sel, idx_bits=int(math.log2(s))),
        grid=(b, nq),
        in_specs=[pl.BlockSpec((IDX_HEADS * DH, tq), lambda i, j: (0, i * nq + j)),
                  pl.BlockSpec((8, tq), lambda i, j: (0, i * nq + j)),
                  pl.BlockSpec((HW, tq), lambda i, j: (0, i * nq + j)),
                  pl.BlockSpec((s, DH), lambda i, j: (i, 0)),
                  pl.BlockSpec((HEADS, s, DH), lambda i, j: (0, i, 0)),
                  pl.BlockSpec((nkt, HW, tk), lambda i, j: (i, 0, 0)),
                  pl.BlockSpec((s, tq), lambda i, j: (0, 0))],
        out_specs=pl.BlockSpec((tq, HW), lambda i, j: (i * nq + j, 0)),
        out_shape=jax.ShapeDtypeStruct((b * s, HW), BF16),
        scratch_shapes=[pltpu.VMEM((s, tq), I16), pltpu.VMEM((s, tq), I16), pltpu.VMEM((s, tq), I16),
                        pltpu.VMEM((s, tq), F32), pltpu.VMEM((HW, tq), F32),
                        pltpu.VMEM((HEADS, s, tq), F32), pltpu.VMEM((HEADS, 1, tq), F32),
                        pltpu.VMEM((HEADS, 1, tq), F32)],
        compiler_params=_cparams(2),
        name="dsa",
    )(iqt, iwt, qt, ik, kr, vt,
      jnp.broadcast_to(jnp.arange(s, dtype=I16)[:, None], (s, tq)))


def _merge_kernel(x_ref, oa_ref, ob_ref, oc_ref, od_ref, ga_ref, gb_ref, gc_ref, gd_ref,
                  wa_ref, wb_ref, wc_ref, wd_ref, wo_ref, o_ref):
    def branch(o_r, g_r, w_r):
        y = jnp.dot(o_r[...].astype(BF16), w_r[...], preferred_element_type=F32)
        return _sigmoid(g_r[...]).astype(F32) * y

    merged = (branch(oa_ref, ga_ref, wa_ref) + branch(ob_ref, gb_ref, wb_ref)
              + branch(oc_ref, gc_ref, wc_ref) + branch(od_ref, gd_ref, wd_ref))
    o_ref[...] = x_ref[...] + jnp.dot(merged.astype(BF16), wo_ref[...], preferred_element_type=F32)


def _merge(x2d, oa, ob, oc, od, gates, wa, wb, wc, wd, wo):
    t, d = x2d.shape
    tm = min(512, t)
    row = lambda w: pl.BlockSpec((tm, w), lambda i: (i, 0))
    gate = lambda n: pl.BlockSpec((tm, d), lambda i, o=n: (i, o))
    full = lambda shape: pl.BlockSpec(shape, lambda i: (0,) * len(shape))
    return pl.pallas_call(
        _merge_kernel,
        grid=(t // tm,),
        in_specs=[row(d), row(HW), row(HW), row(HW), row(HW), gate(0), gate(1), gate(2), gate(3),
                  full((HW, d)), full((HW, d)), full((HW, d)), full((HW, d)), full((d, d))],
        out_specs=row(d),
        out_shape=jax.ShapeDtypeStruct((t, d), F32),
        compiler_params=_cparams(1),
        name="merge",
    )(x2d, oa, ob, oc, od, gates, gates, gates, gates, wa, wb, wc, wd, wo)


def _mlp_kernel(x_ref, g_ref, wu_ref, wd_ref, fg_ref, o_ref, h_ref, acc_ref, *, final_norm):
    fi = pl.program_id(1)

    @pl.when(fi == 0)
    def _():
        x = x_ref[...]
        ms = jnp.mean(x * x, axis=-1, keepdims=True)
        h_ref[...] = (x * lax.rsqrt(ms + NORM_EPS) * g_ref[...]).astype(BF16)
        acc_ref[...] = jnp.zeros_like(acc_ref)

    u = jnp.dot(h_ref[...], wu_ref[...], preferred_element_type=F32)
    u = jnp.square(jnp.maximum(u, 0.0))
    acc_ref[...] += jnp.dot(u.astype(BF16), wd_ref[...], preferred_element_type=F32)

    @pl.when(fi == pl.num_programs(1) - 1)
    def _():
        y = x_ref[...] + acc_ref[...]
        if final_norm:
            ms = jnp.mean(y * y, axis=-1, keepdims=True)
            y = y * lax.rsqrt(ms + NORM_EPS) * fg_ref[...]
        o_ref[...] = y


def _mlp(x2d, g, wu, wd, fg, final_norm):
    t, d = x2d.shape
    f = wu.shape[1]
    tm = min(1024, t)
    tf = 1024
    return pl.pallas_call(
        functools.partial(_mlp_kernel, final_norm=final_norm),
        grid=(t // tm, f // tf),
        in_specs=[pl.BlockSpec((tm, d), lambda i, j: (i, 0)),
                  pl.BlockSpec((1, d), lambda i, j: (0, 0)),
                  pl.BlockSpec((d, tf), lambda i, j: (0, j)),
                  pl.BlockSpec((tf, d), lambda i, j: (j, 0)),
                  pl.BlockSpec((1, d), lambda i, j: (0, 0))],
        out_specs=pl.BlockSpec((tm, d), lambda i, j: (i, 0)),
        out_shape=jax.ShapeDtypeStruct((t, d), F32),
        scratch_shapes=[pltpu.VMEM((tm, d), BF16), pltpu.VMEM((tm, d), F32)],
        compiler_params=_cparams(2),
        name="mlp",
    )(x2d, g, wu, wd, fg)


def _slab_weight(w_in, dtype=BF16):
    d = w_in.shape[0]
    z = lambda n: jnp.zeros((d, n), w_in.dtype)
    gdn, mla, rwkv, dsa = 0, 1032, 1448, 2472
    gate = 3824
    cols = [
        w_in[:, rwkv:rwkv + 1024],
        w_in[:, gdn:gdn + 768],
        w_in[:, gdn + 776:gdn + 1032],
        w_in[:, mla:mla + 256],
        w_in[:, mla + 256:mla + 384],
        z(64), w_in[:, mla + 384:mla + 416], z(32),
        w_in[:, dsa:dsa + 512],
        w_in[:, dsa + 584:dsa + 1352],
        w_in[:, gdn + 768:gdn + 776], z(120),
        w_in[:, dsa + 512:dsa + 584], z(56),
    ]
    return jnp.concatenate(cols, axis=1).astype(dtype), w_in[:, gate:gate + 4096].astype(dtype)


def _mla_weights(w_uq, w_ukv):
    wq = w_uq.reshape(MLA_Q_LORA, HEADS, MLA_NOPE + MLA_ROPE)
    wq = jnp.pad(wq, ((0, 0), (0, 0), (0, MLA_QK - MLA_NOPE - MLA_ROPE))).reshape(MLA_Q_LORA, HEADS * MLA_QK)
    wkv = w_ukv.reshape(MLA_KV_LORA, HEADS, MLA_NOPE + DH)
    wk = jnp.pad(wkv[:, :, :MLA_NOPE], ((0, 0), (0, 0), (0, MLA_QK - MLA_NOPE))).reshape(MLA_KV_LORA, HEADS * MLA_QK)
    wvt = wkv[:, :, MLA_NOPE:].reshape(MLA_KV_LORA, HW).T
    return wq.astype(BF16), wk.astype(BF16), wvt.astype(BF16)


def _rope_cos_sin(positions, rot_dims):
    inv_freq = ROPE_THETA ** (-jnp.arange(0, rot_dims, 2, dtype=F32) / rot_dims)
    ang = positions.astype(F32)[..., None] * inv_freq
    return jnp.concatenate([jnp.cos(ang), jnp.sin(ang)], axis=-1)


def _rope_expansion(rot_dims, offset, period):
    half = rot_dims // 2
    e = np.zeros((rot_dims, 3 * 128), np.float32)
    base = np.zeros((1, 3 * 128), np.float32)
    base[0, 0:128] = 1.0
    for rep in range(128 // period):
        lo = rep * period + offset
        for j in range(half):
            e[j, lo + j] = 1.0
            e[j, lo + half + j] = 1.0
            base[0, lo + j] = 0.0
            base[0, lo + half + j] = 0.0
            e[half + j, 128 + lo + j] = -1.0
            e[half + j, 256 + lo + half + j] = 1.0
    return jnp.asarray(e), jnp.asarray(base)


def _rope_table(cs_ref, e_ref, base_ref):
    tab = _mm_exact_rhs(cs_ref[...], e_ref[...]) + base_ref[...]
    return tab[:, 0:128], tab[:, 128:256], tab[:, 256:384]


def _layer(x2d, b, s, prm, tabs, final_g, is_last):
    (mix_g, w_slab, conv_w, gdn_pcol, gdn_prow, gdn_ng, mla_qg, mla_kvg, wq, wk, wv,
     rwkv_mu, rwkv_vec, ww2, wa2, wg2, wo_gdn, wo_mla, wo_rwkv, wo_dsa, w_o, mlp_g, w_up, w_down) = prm
    mla_cs, dsa_cs = tabs
    t = b * s
    slab = _in_proj(x2d, mix_g, w_slab[0], F32)
    gates = _in_proj(x2d, mix_g, w_slab[1], BF16)

    ab = slab[:, OFF_GDN_AB:OFF_GDN_AB + 8].reshape(b, s // CHUNK, CHUNK, 8)
    abr = jnp.swapaxes(ab, 2, 3)
    o_gdn = _gdn(slab, abr, conv_w, gdn_pcol, gdn_prow, gdn_ng, b, s)
    q, k, v = _mla_prep(slab, mla_cs, mla_qg, mla_kvg, wq, wk, wv)
    o_mla = _flash(q, k, v, b, s)
    o_rwkv = _rwkv(slab, rwkv_mu, rwkv_vec, ww2, wa2, wg2, b, s)
    ik, kr, vt, iqt, qt, iwt = _dsa_prep(slab, dsa_cs)
    o_dsa = _dsa(iqt, iwt, qt, ik, kr, vt, b, s, min(IDX_TOPK_MAX, s // 4))

    x2d = _merge(x2d, o_gdn, o_mla, o_rwkv, o_dsa, gates, wo_gdn, wo_mla, wo_rwkv, wo_dsa, w_o)
    return _mlp(x2d, mlp_g, w_up, w_down, final_g, is_last)


def kernel(x, positions, mix_norm_g, w_in, gdn_conv_w, gdn_a_log, gdn_dt_bias, gdn_norm_g, mla_q_norm_g, mla_kv_norm_g, mla_w_uq, mla_w_ukv, rwkv_mu, rwkv_w0, rwkv_w_w2, rwkv_a0, rwkv_w_a2, rwkv_w_g2, rwkv_k_k, rwkv_k_a, rwkv_r_k, rwkv_ln_w, rwkv_ln_b, w_out_gdn, w_out_mla, w_out_rwkv, w_out_dsa, w_o, mlp_norm_g, w_up, w_down, final_norm_g):
    b, s, d = x.shape
    depth = w_in.shape[0]
    tabs = (_rope_cos_sin(positions, MLA_ROPE).reshape(b * s, MLA_ROPE),
            _rope_cos_sin(positions, DH // 4).reshape(b * s, DH // 4))
    row = lambda a: a.reshape(1, -1).astype(F32)
    pad8 = lambda rows: jnp.concatenate(rows + [jnp.zeros((8 - len(rows), rows[0].shape[1]), F32)], axis=0)
    x2d = x.reshape(b * s, d)
    final_g = row(final_norm_g)
    for l in range(depth):
        lane4 = lambda a: jnp.pad(a.astype(F32), (0, 124)).reshape(1, 128)
        gdn_pcol = pad8([lane4(gdn_a_log[l]), lane4(gdn_dt_bias[l])])
        gdn_prow = jnp.pad(jnp.stack([gdn_a_log[l], gdn_dt_bias[l]], axis=1).astype(F32), ((0, 4), (0, 126)))
        wq, wk, wv = _mla_weights(mla_w_uq[l], mla_w_ukv[l])
        rwkv_vec = pad8([row(rwkv_w0[l]), row(rwkv_a0[l]), row(rwkv_k_k[l]), row(rwkv_k_a[l]),
                         row(rwkv_ln_w[l]), row(rwkv_ln_b[l]), row(rwkv_r_k[l])])
        prm = (row(mix_norm_g[l]), _slab_weight(w_in[l]), gdn_conv_w[l].astype(F32), gdn_pcol, gdn_prow,
               row(jnp.tile(gdn_norm_g[l], HEADS)), row(mla_q_norm_g[l]), row(mla_kv_norm_g[l]), wq, wk, wv,
               row(rwkv_mu[l]), rwkv_vec, rwkv_w_w2[l].astype(F32), rwkv_w_a2[l].astype(F32),
               rwkv_w_g2[l].astype(F32),
               w_out_gdn[l].astype(BF16), w_out_mla[l].astype(BF16), w_out_rwkv[l].astype(BF16),
               w_out_dsa[l].astype(BF16), w_o[l].astype(BF16), row(mlp_norm_g[l]),
               w_up[l].astype(BF16), w_down[l].astype(BF16))
        x2d = _layer(x2d, b, s, prm, tabs, final_g, l == depth - 1)
    return x2d.reshape(b, s, d)
```
